```python
import jax, jax.numpy as jnp
from jax import lax
import numpy as np

D_MODEL = 1024
BATCH = 16
SEQ = 4096
DEPTH = 2
DEC_BATCH = 16
DEC_SEQ = 32
PAST_LEN = 2048

CHUNK = 64
Q_BLOCK = 128
HEAD_DIM = 64
N_SB_HEADS = 4
N_RET_HEADS = 4
N_FOX_HEADS = 4
N_DSA_HEADS = 4
N_IDX_HEADS = 4
IDX_DIM = 64
DSA_TOP_K = 256
N_BRANCHES = 4
BRANCH_WIDTH = 4 * HEAD_DIM
FFN_HIDDEN = -(-8 * D_MODEL // (3 * 256)) * 256
ROPE_BASE = 10000.0
LN_EPS = 1e-5
FORGET_BIAS_INIT = 2.0
ALPHA = (2 * DEPTH) ** 0.25
BETA = (8 * DEPTH) ** -0.25

IN_LAYOUT = (
    ('sb_q', N_SB_HEADS * HEAD_DIM), ('sb_k', N_SB_HEADS * HEAD_DIM), ('sb_v', N_SB_HEADS * HEAD_DIM),
    ('ret_q', N_RET_HEADS * HEAD_DIM), ('ret_k', N_RET_HEADS * HEAD_DIM), ('ret_v', N_RET_HEADS * HEAD_DIM),
    ('ret_g', N_RET_HEADS * HEAD_DIM),
    ('fox_q', N_FOX_HEADS * HEAD_DIM), ('fox_k', N_FOX_HEADS * HEAD_DIM), ('fox_v', N_FOX_HEADS * HEAD_DIM),
    ('fox_f', N_FOX_HEADS),
    ('dsa_q', N_DSA_HEADS * HEAD_DIM), ('dsa_k', HEAD_DIM), ('dsa_v', HEAD_DIM),
    ('idx_q', N_IDX_HEADS * IDX_DIM), ('idx_k', IDX_DIM), ('idx_w', N_IDX_HEADS),
    ('merge_gate', N_BRANCHES * D_MODEL),
)
IN_WIDTH = sum(w for _, w in IN_LAYOUT)
DEEPNORM_V_COLS = ('sb_v', 'ret_v', 'fox_v', 'dsa_v')

kernel_name = 'hybrid_stickbreak_retention_fox_dsa_streaming_step'


def layer_norm(x, g, b):
    xf = x.astype(jnp.float32)
    xc = xf - jnp.mean(xf, -1, keepdims=True)
    var = jnp.mean(xc * xc, -1, keepdims=True)
    return (xc * lax.rsqrt(var + LN_EPS) * g + b).astype(x.dtype)


def head_norm(o):
    oc = o - jnp.mean(o, -1, keepdims=True)
    return oc * lax.rsqrt(jnp.mean(oc * oc, -1, keepdims=True) + LN_EPS)


def split_projection(p):
    offs = np.cumsum([w for _, w in IN_LAYOUT])[:-1].tolist()
    return dict(zip([n for n, _ in IN_LAYOUT], jnp.split(p, offs, axis=-1)))


def rotary(x, pos):
    half = x.shape[-1] // 2
    inv_freq = ROPE_BASE ** (-jnp.arange(half, dtype=jnp.float32) / half)
    ang = pos.astype(jnp.float32)[:, None] * inv_freq[None, :]
    cos = jnp.cos(ang)[None, :, None, :]
    sin = jnp.sin(ang)[None, :, None, :]
    xf = x.astype(jnp.float32)
    x1, x2 = xf[..., :half], xf[..., half:]
    return jnp.concatenate([x1 * cos - x2 * sin, x2 * cos + x1 * sin], -1)


def sweep_query_blocks(fn, qs, qpos):
    t = qpos.shape[0]
    blk = min(Q_BLOCK, t)
    nb = t // blk

    def to_blocks(a):
        return jnp.moveaxis(a.reshape(a.shape[0], nb, blk, *a.shape[2:]), 1, 0)

    out = lax.map(lambda xs: fn(*xs[0], xs[1]), (tuple(to_blocks(a) for a in qs), qpos.reshape(nb, blk)))
    out = jnp.moveaxis(out, 0, 1)
    return out.reshape(out.shape[0], t, *out.shape[3:])


def stick_breaking_block(q, qpos, k, v, kpos):
    z = jnp.einsum('bqhd,blhd->bhql', q, k).astype(jnp.float32) * HEAD_DIM ** -0.5
    earlier = kpos[None, :] < qpos[:, None]
    log_1mb = jnp.where(earlier, jax.nn.log_sigmoid(-z), 0.0)
    after = lax.cumsum(log_1mb, axis=3, reverse=True) - log_1mb
    w = jnp.where(earlier, jnp.exp(jax.nn.log_sigmoid(z) + after), 0.0)
    return jnp.einsum('bhql,blhd->bqhd', w, v.astype(jnp.float32)).astype(q.dtype)


def forgetting_block(q, cq, qpos, k, v, ck, kpos):
    logits = jnp.einsum('bqhd,blhd->bhql', q, k).astype(jnp.float32) * HEAD_DIM ** -0.5
    logits = logits + jnp.moveaxis(cq, -1, 1)[..., None] - jnp.moveaxis(ck, -1, 1)[:, :, None, :]
    logits = jnp.where((kpos[None, :] <= qpos[:, None])[None, None], logits, -jnp.inf)
    p = jax.nn.softmax(logits, axis=-1)
    return jnp.einsum('bhql,blhd->bqhd', p, v.astype(jnp.float32)).astype(q.dtype)


def dsa_block(q, qi, wi, qpos, k, v, ki, kpos, top_k):
    idx_logit = jnp.einsum('bqhe,ble->bqhl', qi, ki).astype(jnp.float32) * IDX_DIM ** -0.5
    score = jnp.einsum('bqh,bqhl->bql', wi.astype(jnp.float32) * N_IDX_HEADS ** -0.5, jax.nn.relu(idx_logit))
    admissible = (kpos[None, :] // CHUNK) <= (qpos[:, None] // CHUNK)
    score = jnp.where(admissible[None], score, -jnp.inf)
    _, sel = lax.top_k(score, top_k)
    valid = (kpos[sel] // CHUNK) <= (qpos // CHUNK)[None, :, None]
    gather = jax.vmap(lambda rows, idx: rows[idx])
    k_sel = gather(k, sel)
    v_sel = gather(v, sel)
    logits = jnp.einsum('bqhd,bqkd->bhqk', q, k_sel).astype(jnp.float32) * HEAD_DIM ** -0.5
    logits = jnp.where(valid[:, None], logits, -jnp.inf)
    p = jax.nn.softmax(logits, axis=-1)
    return jnp.einsum('bhqk,bqkd->bqhd', p, v_sel.astype(jnp.float32)).astype(q.dtype)


def retention_chunk(state, q, k, v):
    c = q.shape[1]
    log_gamma = jnp.log(1.0 - 2.0 ** (-5.0 - jnp.arange(N_RET_HEADS, dtype=jnp.float32)))
    n = jnp.arange(c, dtype=jnp.float32)
    rel = n[:, None] - n[None, :]
    decay = jnp.where(rel >= 0, jnp.exp(jnp.maximum(rel, 0.0)[None] * log_gamma[:, None, None]), 0.0)
    scores = jnp.einsum('bnhd,bmhd->bhnm', q, k) * decay[None]
    o = jnp.einsum('bhnm,bmhe->bnhe', scores, v)
    o = o + jnp.einsum('bnhd,bhde->bnhe', q, state) * jnp.exp((n[:, None] + 1.0) * log_gamma[None, :])[None, :, :, None]
    k_dec = k * jnp.exp((c - 1.0 - n)[:, None] * log_gamma[None, :])[None, :, :, None]
    state = jnp.exp(c * log_gamma)[None, :, None, None] * state + jnp.einsum('bmhd,bmhe->bhde', k_dec, v)
    return state, o


def retention(q, k, v, state):
    b, t = q.shape[:2]
    c = min(CHUNK, t)
    nc = t // c

    def to_chunks(a):
        return jnp.moveaxis(a.reshape(b, nc, c, *a.shape[2:]), 1, 0)

    state, o = lax.scan(lambda s, xs: retention_chunk(s, *xs), state, (to_chunks(q), to_chunks(k), to_chunks(v)))
    return jnp.moveaxis(o, 0, 1).reshape(b, t, *o.shape[3:]), state


def trunk_layer(h, past, ret_state, w_in, b_forget, w_branch, w_out, ln1_g, ln1_b,
                w_ffn_in, w_ffn_out, ln2_g, ln2_b):
    sb_k0, sb_v0, fox_k0, fox_v0, fox_lf0, dsa_k0, dsa_v0, dsa_ki0 = past
    b, t, _ = h.shape
    p_len = sb_k0.shape[1]
    qpos = p_len + jnp.arange(t, dtype=jnp.int32)
    kpos = jnp.arange(p_len + t, dtype=jnp.int32)
    pr = split_projection(h @ w_in)

    def heads(name, n):
        return pr[name].reshape(b, t, n, -1)

    sb_q, sb_k, sb_v = heads('sb_q', N_SB_HEADS), heads('sb_k', N_SB_HEADS), heads('sb_v', N_SB_HEADS)
    sb_k_all = jnp.concatenate([sb_k0, sb_k], 1)
    sb_v_all = jnp.concatenate([sb_v0, sb_v], 1)
    y_sb = sweep_query_blocks(lambda qb, pb: stick_breaking_block(qb, pb, sb_k_all, sb_v_all, kpos), (sb_q,), qpos)

    ret_q = rotary(heads('ret_q', N_RET_HEADS), qpos)
    ret_k = rotary(heads('ret_k', N_RET_HEADS), qpos) * HEAD_DIM ** -0.5
    ret_v = heads('ret_v', N_RET_HEADS).astype(jnp.float32)
    y_ret, ret_state_new = retention(ret_q, ret_k, ret_v, ret_state.astype(jnp.float32))
    y_ret = (head_norm(y_ret).reshape(b, t, -1) * jax.nn.silu(pr['ret_g'].astype(jnp.float32))).astype(h.dtype)

    fox_q, fox_k, fox_v = heads('fox_q', N_FOX_HEADS), heads('fox_k', N_FOX_HEADS), heads('fox_v', N_FOX_HEADS)
    fox_lf = jax.nn.log_sigmoid((pr['fox_f'] + b_forget).astype(jnp.float32))
    fox_k_all = jnp.concatenate([fox_k0, fox_k], 1)
    fox_v_all = jnp.concatenate([fox_v0, fox_v], 1)
    cum = jnp.cumsum(jnp.concatenate([fox_lf0.astype(jnp.float32), fox_lf], 1), axis=1)
    y_fox = sweep_query_blocks(
        lambda qb, cqb, pb: forgetting_block(qb, cqb, pb, fox_k_all, fox_v_all, cum, kpos),
        (fox_q, cum[:, p_len:]), qpos)

    dsa_q, dsa_k, dsa_v = heads('dsa_q', N_DSA_HEADS), pr['dsa_k'], pr['dsa_v']
    idx_q, idx_k, idx_w = heads('idx_q', N_IDX_HEADS), pr['idx_k'], pr['idx_w']
    dsa_k_all = jnp.concatenate([dsa_k0, dsa_k], 1)
    dsa_v_all = jnp.concatenate([dsa_v0, dsa_v], 1)
    dsa_ki_all = jnp.concatenate([dsa_ki0, idx_k], 1)
    top_k = min(DSA_TOP_K, (p_len + t) // 4)
    y_dsa = sweep_query_blocks(
        lambda qb, qib, wib, pb: dsa_block(qb, qib, wib, pb, dsa_k_all, dsa_v_all, dsa_ki_all, kpos, top_k),
        (dsa_q, idx_q, idx_w), qpos)

    gates = jax.nn.sigmoid(pr['merge_gate'].reshape(b, t, N_BRANCHES, D_MODEL))
    branches = (y_sb.reshape(b, t, -1), y_ret, y_fox.reshape(b, t, -1), y_dsa.reshape(b, t, -1))
    merged = gates[:, :, 0] * (branches[0] @ w_branch[0])
    for i in range(1, N_BRANCHES):
        merged = merged + gates[:, :, i] * (branches[i] @ w_branch[i])
    h = layer_norm(ALPHA * h + merged @ w_out, ln1_g, ln1_b)

    a, u = jnp.split(h @ w_ffn_in, 2, axis=-1)
    h = layer_norm(ALPHA * h + (jax.nn.silu(a) * u) @ w_ffn_out, ln2_g, ln2_b)
    new_state = (sb_k, sb_v, ret_state_new, fox_k, fox_v, fox_lf, dsa_k, dsa_v, idx_k)
    return h, new_state


def setup_inputs(seed: int = 0) -> dict:
    key = jax.random.key(seed)
    ks = jax.random.split(key, 24)

    def nrm(k, shape, scale=1.0):
        return scale * jax.random.normal(k, shape, jnp.float32)

    col_scale = jnp.asarray(np.concatenate(
        [np.full((w,), BETA if n in DEEPNORM_V_COLS else 1.0, np.float32) for n, w in IN_LAYOUT]))
    kv_shape = (DEPTH, DEC_BATCH, PAST_LEN, N_SB_HEADS, HEAD_DIM)
    fox_shape = (DEPTH, DEC_BATCH, PAST_LEN, N_FOX_HEADS, HEAD_DIM)
    return {
        'x_prompt': nrm(ks[0], (BATCH, SEQ, D_MODEL)),
        'x_sample': nrm(ks[1], (DEC_BATCH, DEC_SEQ, D_MODEL)),
        'cache_sb_k': nrm(ks[2], kv_shape),
        'cache_sb_v': nrm(ks[3], kv_shape, BETA),
        'state_ret': nrm(ks[4], (DEPTH, DEC_BATCH, N_RET_HEADS, HEAD_DIM, HEAD_DIM), 0.25),
        'cache_fox_k': nrm(ks[5], fox_shape),
        'cache_fox_v': nrm(ks[6], fox_shape, BETA),
        'cache_fox_logf': jax.nn.log_sigmoid(FORGET_BIAS_INIT + nrm(ks[7], (DEPTH, DEC_BATCH, PAST_LEN, N_FOX_HEADS))),
        'cache_dsa_k': nrm(ks[8], (DEPTH, DEC_BATCH, PAST_LEN, HEAD_DIM)),
        'cache_dsa_v': nrm(ks[9], (DEPTH, DEC_BATCH, PAST_LEN, HEAD_DIM), BETA),
        'cache_dsa_kidx': nrm(ks[10], (DEPTH, DEC_BATCH, PAST_LEN, IDX_DIM)),
        'w_in': nrm(ks[11], (DEPTH, D_MODEL, IN_WIDTH), D_MODEL ** -0.5) * col_scale,
        'b_forget': FORGET_BIAS_INIT + nrm(ks[12], (DEPTH, N_FOX_HEADS), 0.1),
        'w_branch': nrm(ks[13], (DEPTH, N_BRANCHES, BRANCH_WIDTH, D_MODEL), BETA * BRANCH_WIDTH ** -0.5),
        'w_out': nrm(ks[14], (DEPTH, D_MODEL, D_MODEL), BETA * D_MODEL ** -0.5),
        'ln1_g': 1.0 + nrm(ks[15], (DEPTH, D_MODEL), 0.02),
        'ln1_b': nrm(ks[16], (DEPTH, D_MODEL), 0.02),
        'w_ffn_in': nrm(ks[17], (DEPTH, D_MODEL, 2 * FFN_HIDDEN), D_MODEL ** -0.5),
        'w_ffn_out': nrm(ks[18], (DEPTH, FFN_HIDDEN, D_MODEL), BETA * FFN_HIDDEN ** -0.5),
        'ln2_g': 1.0 + nrm(ks[19], (DEPTH, D_MODEL), 0.02),
        'ln2_b': nrm(ks[20], (DEPTH, D_MODEL), 0.02),
    }


def reference(x_prompt, x_sample, cache_sb_k, cache_sb_v, state_ret, cache_fox_k, cache_fox_v,
              cache_fox_logf, cache_dsa_k, cache_dsa_v, cache_dsa_kidx, w_in, b_forget, w_branch,
              w_out, ln1_g, ln1_b, w_ffn_in, w_ffn_out, ln2_g, ln2_b):
    b = x_prompt.shape[0]
    dt = x_prompt.dtype
    empty_past = (
        jnp.zeros((b, 0, N_SB_HEADS, HEAD_DIM), dt), jnp.zeros((b, 0, N_SB_HEADS, HEAD_DIM), dt),
        jnp.zeros((b, 0, N_FOX_HEADS, HEAD_DIM), dt), jnp.zeros((b, 0, N_FOX_HEADS, HEAD_DIM), dt),
        jnp.zeros((b, 0, N_FOX_HEADS), jnp.float32),
        jnp.zeros((b, 0, HEAD_DIM), dt), jnp.zeros((b, 0, HEAD_DIM), dt), jnp.zeros((b, 0, IDX_DIM), dt),
    )
    ret_zero = jnp.zeros((b, N_RET_HEADS, HEAD_DIM, HEAD_DIM), jnp.float32)
    hp, hs = x_prompt, x_sample
    st_p, st_s = [], []
    for l in range(DEPTH):
        wl = (w_in[l], b_forget[l], w_branch[l], w_out[l], ln1_g[l], ln1_b[l],
              w_ffn_in[l], w_ffn_out[l], ln2_g[l], ln2_b[l])
        hp, sp = trunk_layer(hp, empty_past, ret_zero, *wl)
        past_l = (cache_sb_k[l], cache_sb_v[l], cache_fox_k[l], cache_fox_v[l], cache_fox_logf[l],
                  cache_dsa_k[l], cache_dsa_v[l], cache_dsa_kidx[l])
        hs, ss = trunk_layer(hs, past_l, state_ret[l], *wl)
        st_p.append(sp)
        st_s.append(ss)

    def stacked(states, i):
        return jnp.stack([s[i] for s in states])

    return (hp, hs,
            stacked(st_p, 0), stacked(st_p, 1), stacked(st_p, 2), stacked(st_p, 3), stacked(st_p, 4),
            stacked(st_p, 5), stacked(st_p, 6), stacked(st_p, 7), stacked(st_p, 8),
            stacked(st_s, 0), stacked(st_s, 1), stacked(st_s, 2), stacked(st_s, 3), stacked(st_s, 4),
            stacked(st_s, 5), stacked(st_s, 6), stacked(st_s, 7), stacked(st_s, 8))
```

```python
import functools

import numpy as np
import jax
import jax.numpy as jnp
from jax import lax
from jax.experimental import pallas as pl
from jax.experimental.pallas import tpu as pltpu

HEAD_DIM = 64
N_HEADS = 4
BRANCH_WIDTH = N_HEADS * HEAD_DIM
CHUNK = 64
CHUNK_SHIFT = 6
DSA_TOP_K = 256
ROPE_BASE = 10000.0
LN_EPS = 1e-5
QK_SCALE = HEAD_DIM ** -0.5
IDX_HEAD_SCALE = N_HEADS ** -0.5
MASK_VALUE = -1e30
INT_MIN = -2 ** 31

V7X_VMEM_LIMIT_BYTES = 56 * 1024 * 1024
LANES = 128

BF = jnp.bfloat16
F32 = jnp.float32


def _dot(a, b):
    return jnp.dot(a, b, preferred_element_type=F32)


def _dot_nt(a, b):
    return lax.dot_general(a, b, (((1,), (1,)), ((), ())), preferred_element_type=F32)


def _dot_tn(a, b):
    return lax.dot_general(a, b, (((0,), (0,)), ((), ())), preferred_element_type=F32)


def _params(*sem):
    return pltpu.CompilerParams(dimension_semantics=sem, vmem_limit_bytes=V7X_VMEM_LIMIT_BYTES)


def _const_spec(shape):
    nd = len(shape)
    return pl.BlockSpec(shape, lambda *_: (0,) * nd)


def _layer_norm(x, g, b):
    xc = x - jnp.mean(x, axis=-1, keepdims=True)
    var = jnp.mean(xc * xc, axis=-1, keepdims=True)
    return xc * lax.rsqrt(var + LN_EPS) * g + b


def _row_tile(m, want):
    t = min(m, want)
    assert m % t == 0
    return t


def _inproj_body(x_ref, w_ref, o_ref, *, n_chunk):
    xb = x_ref[...].astype(BF)
    n = w_ref.shape[1]
    for c in range(0, n, n_chunk):
        o_ref[:, c:c + n_chunk] = _dot(xb, w_ref[:, c:c + n_chunk])


def _inproj(x, w):
    m, d = x.shape
    n = w.shape[1]
    tm = _row_tile(m, 512)
    n_chunk = 256
    assert n % n_chunk == 0
    return pl.pallas_call(
        functools.partial(_inproj_body, n_chunk=n_chunk),
        out_shape=jax.ShapeDtypeStruct((m, n), F32),
        grid=(m // tm,),
        in_specs=[pl.BlockSpec((tm, d), lambda i: (i, 0)), _const_spec((d, n))],
        out_specs=pl.BlockSpec((tm, n), lambda i: (i, 0)),
        compiler_params=_params("parallel"),
        name="inproj",
    )(x, w)


def _block_counts(q0, tq, tk, last_key):
    n_full = lax.div(q0, tk)
    n_all = lax.div(last_key, tk) + 1
    return n_full, n_all


def _sb_body(q_ref, k_ref, v_ref, o_ref, *, p_len, tq, tk):
    q0 = p_len + pl.program_id(2) * tq
    q = q_ref[0, 0]
    qpos = q0 + lax.broadcasted_iota(jnp.int32, (tq, 1), 0)
    suffix = (lax.broadcasted_iota(jnp.int32, (tk, tk), 0)
              > lax.broadcasted_iota(jnp.int32, (tk, tk), 1)).astype(BF)
    n_full, n_all = _block_counts(q0, tq, tk, jnp.maximum(q0 + tq - 2, 0))

    def step(kb, carry, masked):
        later, acc = carry
        s0 = pl.multiple_of(kb * tk, tk)
        k = k_ref[0, 0, pl.ds(s0, tk), :]
        v = v_ref[0, 0, pl.ds(s0, tk), :]
        z = _dot_nt(q, k) * QK_SCALE
        t = jnp.log1p(jnp.exp(-jnp.abs(z)))
        log_1mb = jnp.minimum(-z, 0.0) - t
        log_b = jnp.minimum(z, 0.0) - t
        if masked:
            earlier = (s0 + lax.broadcasted_iota(jnp.int32, (1, tk), 1)) < qpos
            log_1mb = jnp.where(earlier, log_1mb, 0.0)
        hi = log_1mb.astype(BF)
        lo = (log_1mb - hi.astype(F32)).astype(BF)
        after = _dot(hi, suffix) + _dot(lo, suffix) + later
        w = jnp.exp(log_b + after)
        if masked:
            w = jnp.where(earlier, w, 0.0)
        acc = acc + _dot(w.astype(BF), v)
        later = later + jnp.sum(log_1mb, axis=-1, keepdims=True)
        return later, acc

    carry = (jnp.zeros((tq, 1), F32), jnp.zeros((tq, HEAD_DIM), F32))
    carry = lax.fori_loop(0, n_all - n_full, lambda i, c: step(n_all - 1 - i, c, True), carry)
    carry = lax.fori_loop(0, n_full, lambda i, c: step(n_full - 1 - i, c, False), carry)
    o_ref[0, 0] = carry[1].astype(o_ref.dtype)


def _sb_attention(q, k, v, p_len, tq, tk):
    b, h, t, _ = q.shape
    lp = k.shape[2]
    kv_spec = pl.BlockSpec((1, 1, lp, HEAD_DIM), lambda bi, hi, qi: (bi, hi, 0, 0))
    q_spec = pl.BlockSpec((1, 1, tq, HEAD_DIM), lambda bi, hi, qi: (bi, hi, qi, 0))
    return pl.pallas_call(
        functools.partial(_sb_body, p_len=p_len, tq=tq, tk=tk),
        out_shape=jax.ShapeDtypeStruct((b, h, t, HEAD_DIM), BF),
        grid=(b, h, t // tq),
        in_specs=[q_spec, kv_spec, kv_spec],
        out_specs=q_spec,
        compiler_params=_params("parallel", "parallel", "arbitrary"),
        name="sb_attention",
    )(q, k, v)


def _fox_body(q_ref, k_ref, v_ref, cq_ref, ck_ref, o_ref, *, p_len, tq, tk):
    q0 = p_len + pl.program_id(2) * tq
    q = q_ref[0, 0]
    cq = cq_ref[0, 0]
    qpos = q0 + lax.broadcasted_iota(jnp.int32, (tq, 1), 0)
    n_full, n_all = _block_counts(q0, tq, tk, q0 + tq - 1)

    def step(kb, carry, masked):
        m, l, acc = carry
        s0 = pl.multiple_of(kb * tk, tk)
        k = k_ref[0, 0, pl.ds(s0, tk), :]
        v = v_ref[0, 0, pl.ds(s0, tk), :]
        ck = ck_ref[0, 0, :, pl.ds(s0, tk)]
        logits = _dot_nt(q, k) * QK_SCALE + cq - ck
        if masked:
            visible = (s0 + lax.broadcasted_iota(jnp.int32, (1, tk), 1)) <= qpos
            logits = jnp.where(visible, logits, MASK_VALUE)
        m_new = jnp.maximum(m, jnp.max(logits, axis=-1, keepdims=True))
        alpha = jnp.exp(m - m_new)
        p = jnp.exp(logits - m_new)
        l = alpha * l + jnp.sum(p, axis=-1, keepdims=True)
        acc = alpha * acc + _dot(p.astype(BF), v)
        return m_new, l, acc

    carry = (jnp.full((tq, 1), -jnp.inf, F32), jnp.zeros((tq, 1), F32), jnp.zeros((tq, HEAD_DIM), F32))
    carry = lax.fori_loop(0, n_full, lambda i, c: step(i, c, False), carry)
    carry = lax.fori_loop(n_full, n_all, lambda i, c: step(i, c, True), carry)
    o_ref[0, 0] = (carry[2] / carry[1]).astype(o_ref.dtype)


def _fox_attention(q, k, v, cq, ck, p_len, tq, tk):
    b, h, t, _ = q.shape
    lp = k.shape[2]
    kv_spec = pl.BlockSpec((1, 1, lp, HEAD_DIM), lambda bi, hi, qi: (bi, hi, 0, 0))
    q_spec = pl.BlockSpec((1, 1, tq, HEAD_DIM), lambda bi, hi, qi: (bi, hi, qi, 0))
    return pl.pallas_call(
        functools.partial(_fox_body, p_len=p_len, tq=tq, tk=tk),
        out_shape=jax.ShapeDtypeStruct((b, h, t, HEAD_DIM), BF),
        grid=(b, h, t // tq),
        in_specs=[q_spec, kv_spec, kv_spec,
                  pl.BlockSpec((1, 1, tq, 1), lambda bi, hi, qi: (bi, hi, qi, 0)),
                  pl.BlockSpec((1, 1, 1, lp), lambda bi, hi, qi: (bi, hi, 0, 0))],
        out_specs=q_spec,
        compiler_params=_params("parallel", "parallel", "arbitrary"),
        name="fox_attention",
    )(q, k, v, cq, ck)


def _dsa_body(q_ref, qi_ref, wi_ref, k_ref, v_ref, ki_ref, o_ref, keys_ref, *, p_len, n_keys, tq, tk, top_k):
    q0 = p_len + pl.program_id(1) * tq
    qpos = q0 + lax.broadcasted_iota(jnp.int32, (tq, 1), 0)
    limit = jnp.minimum(((qpos >> CHUNK_SHIFT) + 1) << CHUNK_SHIFT, n_keys)
    last_limit = jnp.minimum((((q0 + tq - 1) >> CHUNK_SHIFT) + 1) << CHUNK_SHIFT, n_keys)
    n_blk = lax.div(last_limit - 1, tk) + 1

    def key_pos(s0):
        return s0 + lax.broadcasted_iota(jnp.int32, (1, tk), 1)

    wi = wi_ref[0] * IDX_HEAD_SCALE

    def score_step(kb, _):
        s0 = pl.multiple_of(kb * tk, tk)
        ki = ki_ref[0, pl.ds(s0, tk), :]
        score = jnp.zeros((tq, tk), F32)
        for h in range(N_HEADS):
            logit = _dot_nt(qi_ref[0, h], ki) * QK_SCALE
            score = score + wi[:, h:h + 1] * jnp.maximum(logit, 0.0)
        score = jnp.where(key_pos(s0) < limit, score, -jnp.inf)
        bits = pltpu.bitcast(score, jnp.int32)
        keys_ref[:, pl.ds(s0, tk)] = jnp.where(bits < 0, jnp.int32(INT_MIN) - bits, bits)
        return 0

    lax.fori_loop(0, n_blk, score_step, 0)

    def count(pred):
        def body(kb, acc):
            s0 = pl.multiple_of(kb * tk, tk)
            return acc + jnp.where(pred(keys_ref[:, pl.ds(s0, tk)]), 1.0, 0.0)
        acc = lax.fori_loop(0, n_blk, body, jnp.zeros((tq, tk), F32))
        return jnp.sum(acc, axis=-1, keepdims=True)

    kf = jnp.float32(top_k)
    zero = jnp.zeros((tq, 1), jnp.int32)
    thr = jnp.where(count(lambda key: key >= zero) >= kf, zero, jnp.int32(INT_MIN))

    def bit_step(it, thr):
        cand = thr + lax.shift_left(jnp.int32(1), 30 - it)
        return jnp.where(count(lambda key: key >= cand) >= kf, cand, thr)

    thr = lax.fori_loop(0, 31, bit_step, thr)
    n_tie_wanted = kf - count(lambda key: key > thr)

    before = (lax.broadcasted_iota(jnp.int32, (tk, tk), 0)
              < lax.broadcasted_iota(jnp.int32, (tk, tk), 1)).astype(BF)

    def attend_step(kb, carry):
        ties_seen, ms, ls, accs = carry
        s0 = pl.multiple_of(kb * tk, tk)
        key = keys_ref[:, pl.ds(s0, tk)]
        tie = jnp.where(key == thr, 1.0, 0.0)
        tie_rank = _dot(tie.astype(BF), before) + ties_seen
        take = jnp.where(key > thr, 1.0, jnp.where(tie_rank < n_tie_wanted, tie, 0.0))
        selected = jnp.where(key_pos(s0) < limit, take, 0.0) > 0.0
        k = k_ref[0, pl.ds(s0, tk), :]
        v = v_ref[0, pl.ds(s0, tk), :]
        ms_new, ls_new, accs_new = [], [], []
        for h in range(N_HEADS):
            logits = jnp.where(selected, _dot_nt(q_ref[0, h], k) * QK_SCALE, MASK_VALUE)
            m_new = jnp.maximum(ms[h], jnp.max(logits, axis=-1, keepdims=True))
            alpha = jnp.exp(ms[h] - m_new)
            p = jnp.exp(logits - m_new)
            ms_new.append(m_new)
            ls_new.append(alpha * ls[h] + jnp.sum(p, axis=-1, keepdims=True))
            accs_new.append(alpha * accs[h] + _dot(p.astype(BF), v))
        ties_seen = ties_seen + jnp.sum(tie, axis=-1, keepdims=True)
        return ties_seen, tuple(ms_new), tuple(ls_new), tuple(accs_new)

    carry = (jnp.zeros((tq, 1), F32),
             tuple(jnp.full((tq, 1), -jnp.inf, F32) for _ in range(N_HEADS)),
             tuple(jnp.zeros((tq, 1), F32) for _ in range(N_HEADS)),
             tuple(jnp.zeros((tq, HEAD_DIM), F32) for _ in range(N_HEADS)))
    _, _, ls, accs = lax.fori_loop(0, n_blk, attend_step, carry)
    for h in range(N_HEADS):
        o_ref[0, h] = (accs[h] / ls[h]).astype(o_ref.dtype)


def _dsa_attention(q, qi, wi, k, v, ki, p_len, n_keys, tq, tk, top_k):
    b, h, t, _ = q.shape
    lp = k.shape[1]
    q_spec = pl.BlockSpec((1, h, tq, HEAD_DIM), lambda bi, qi_: (bi, 0, qi_, 0))
    kv_spec = pl.BlockSpec((1, lp, HEAD_DIM), lambda bi, qi_: (bi, 0, 0))
    return pl.pallas_call(
        functools.partial(_dsa_body, p_len=p_len, n_keys=n_keys, tq=tq, tk=tk, top_k=top_k),
        out_shape=jax.ShapeDtypeStruct((b, h, t, HEAD_DIM), BF),
        grid=(b, t // tq),
        in_specs=[q_spec, q_spec, pl.BlockSpec((1, tq, N_HEADS), lambda bi, qi_: (bi, qi_, 0)),
                  kv_spec, kv_spec, kv_spec],
        out_specs=q_spec,
        scratch_shapes=[pltpu.VMEM((tq, lp), jnp.int32)],
        compiler_params=_params("parallel", "arbitrary"),
        name="dsa_attention",
    )(q, qi, wi, k, v, ki)


def _ret_body(q_ref, k_ref, v_ref, g_ref, s0_ref, dec_ref, qd_ref, kd_ref, sd_ref, o_ref, so_ref, state_ref):
    c = pl.program_id(2)

    @pl.when(c == 0)
    def _():
        state_ref[...] = s0_ref[0, 0]

    qb = q_ref[0, 0].astype(BF)
    k = k_ref[0, 0]
    vb = v_ref[0, 0].astype(BF)
    state = state_ref[...]
    scores = _dot_nt(qb, k.astype(BF)) * dec_ref[0]
    o = _dot(scores.astype(BF), vb) + _dot(qb, state.astype(BF)) * qd_ref[0]
    state = sd_ref[0] * state + _dot_tn((k * kd_ref[0]).astype(BF), vb)
    state_ref[...] = state
    oc = o - jnp.mean(o, axis=-1, keepdims=True)
    on = oc * lax.rsqrt(jnp.mean(oc * oc, axis=-1, keepdims=True) + LN_EPS)
    g = g_ref[0, 0]
    o_ref[0, 0] = (on * (g * jax.nn.sigmoid(g))).astype(o_ref.dtype)

    @pl.when(c == pl.num_programs(2) - 1)
    def _():
        so_ref[0, 0] = state


def _retention(q, k, v, g, state0, c):
    b, h, t, _ = q.shape
    log_gamma = np.log(1.0 - 2.0 ** (-5.0 - np.arange(h, dtype=np.float64)))
    n = np.arange(c, dtype=np.float64)
    rel = n[:, None] - n[None, :]
    decay = np.where(rel >= 0, np.exp(np.maximum(rel, 0.0)[None] * log_gamma[:, None, None]), 0.0)
    q_decay = np.exp((n[None, :] + 1.0) * log_gamma[:, None])[..., None]
    k_decay = np.exp((c - 1.0 - n)[None, :] * log_gamma[:, None])[..., None]
    s_decay = np.exp(c * log_gamma)[:, None, None]
    tables = [jnp.asarray(a, F32) for a in (decay, q_decay, k_decay, s_decay)]
    x_spec = pl.BlockSpec((1, 1, c, HEAD_DIM), lambda bi, hi, ci: (bi, hi, ci, 0))
    s_spec = pl.BlockSpec((1, 1, HEAD_DIM, HEAD_DIM), lambda bi, hi, ci: (bi, hi, 0, 0))

    def t_spec(a):
        return pl.BlockSpec((1,) + a.shape[1:], lambda bi, hi, ci: (hi, 0, 0))

    return pl.pallas_call(
        _ret_body,
        out_shape=(jax.ShapeDtypeStruct((b, h, t, HEAD_DIM), BF),
                   jax.ShapeDtypeStruct((b, h, HEAD_DIM, HEAD_DIM), F32)),
        grid=(b, h, t // c),
        in_specs=[x_spec, x_spec, x_spec, x_spec, s_spec] + [t_spec(a) for a in tables],
        out_specs=(x_spec, s_spec),
        scratch_shapes=[pltpu.VMEM((HEAD_DIM, HEAD_DIM), F32)],
        compiler_params=_params("parallel", "parallel", "arbitrary"),
        name="retention",
    )(q, k, v, g, state0, *tables)


def _merge_body(h_ref, y0_ref, y1_ref, y2_ref, y3_ref, wg_ref, wb_ref, wo_ref, g_ref, b_ref, o_ref, *, alpha):
    h = h_ref[...]
    hb = h.astype(BF)
    d = h.shape[1]
    merged = jnp.zeros(h.shape, F32)
    for i, y_ref in enumerate((y0_ref, y1_ref, y2_ref, y3_ref)):
        gate = jax.nn.sigmoid(_dot(hb, wg_ref[:, i * d:(i + 1) * d]))
        merged = merged + gate * _dot(y_ref[...], wb_ref[i])
    r = alpha * h + _dot(merged.astype(BF), wo_ref[...])
    o_ref[...] = _layer_norm(r, g_ref[...], b_ref[...])


def _merge(h, ys, w_gate, w_branch, w_out, ln_g, ln_b, alpha):
    m, d = h.shape
    tm = _row_tile(m, 256)
    row = lambda w: pl.BlockSpec((tm, w), lambda i: (i, 0))
    return pl.pallas_call(
        functools.partial(_merge_body, alpha=alpha),
        out_shape=jax.ShapeDtypeStruct((m, d), F32),
        grid=(m // tm,),
        in_specs=[row(d)] + [row(BRANCH_WIDTH)] * 4
                 + [_const_spec(w_gate.shape), _const_spec(w_branch.shape), _const_spec(w_out.shape),
                    _const_spec((1, d)), _const_spec((1, d))],
        out_specs=row(d),
        compiler_params=_params("parallel"),
        name="merge",
    )(h, *ys, w_gate, w_branch, w_out, ln_g, ln_b)


def _ffn_body(h_ref, wi_ref, wo_ref, g_ref, b_ref, o_ref, *, alpha, f_chunk):
    h = h_ref[...]
    hb = h.astype(BF)
    f = wo_ref.shape[0]
    acc = jnp.zeros(h.shape, F32)
    for c in range(0, f, f_chunk):
        a = _dot(hb, wi_ref[:, c:c + f_chunk])
        u = _dot(hb, wi_ref[:, f + c:f + c + f_chunk])
        acc = acc + _dot((a * jax.nn.sigmoid(a) * u).astype(BF), wo_ref[c:c + f_chunk, :])
    o_ref[...] = _layer_norm(alpha * h + acc, g_ref[...], b_ref[...])


def _ffn(h, w_in, w_out, ln_g, ln_b, alpha):
    m, d = h.shape
    f = w_out.shape[0]
    tm = _row_tile(m, 256)
    f_chunk = f // 2 if (f // 2) % LANES == 0 else f
    row = pl.BlockSpec((tm, d), lambda i: (i, 0))
    return pl.pallas_call(
        functools.partial(_ffn_body, alpha=alpha, f_chunk=f_chunk),
        out_shape=jax.ShapeDtypeStruct((m, d), F32),
        grid=(m // tm,),
        in_specs=[row, _const_spec(w_in.shape), _const_spec(w_out.shape),
                  _const_spec((1, d)), _const_spec((1, d))],
        out_specs=row,
        compiler_params=_params("parallel"),
        name="ffn",
    )(h, w_in, w_out, ln_g, ln_b)


def _in_layout(d):
    w = BRANCH_WIDTH
    return (('sb_q', w), ('sb_k', w), ('sb_v', w), ('ret_q', w), ('ret_k', w), ('ret_v', w), ('ret_g', w),
            ('fox_q', w), ('fox_k', w), ('fox_v', w), ('fox_f', N_HEADS),
            ('dsa_q', w), ('dsa_k', HEAD_DIM), ('dsa_v', HEAD_DIM),
            ('idx_q', w), ('idx_k', HEAD_DIM), ('idx_w', N_HEADS), ('merge_gate', 4 * d))


_WIDE = ('sb_q', 'sb_k', 'sb_v', 'ret_q', 'ret_k', 'ret_v', 'ret_g', 'fox_q', 'fox_k', 'fox_v', 'dsa_q', 'idx_q')
_NARROW = ('dsa_k', 'dsa_v', 'idx_k', 'fox_f', 'idx_w')


def _split_w_in(w_in):
    d = w_in.shape[0]
    cols, off = {}, 0
    for name, width in _in_layout(d):
        cols[name] = w_in[:, off:off + width]
        off += width
    assert off == w_in.shape[1]
    mix = jnp.concatenate([cols[n] for n in _WIDE + _NARROW], axis=1)
    pad = (-mix.shape[1]) % 256
    mix = jnp.pad(mix, ((0, 0), (0, pad)))
    return mix.astype(BF), cols['merge_gate'].astype(BF)


def _rotary(x, pos):
    half = x.shape[-1] // 2
    inv_freq = ROPE_BASE ** (-jnp.arange(half, dtype=F32) / half)
    ang = pos.astype(F32)[:, None] * inv_freq[None, :]
    cos = jnp.cos(ang)[None, :, None, :]
    sin = jnp.sin(ang)[None, :, None, :]
    x1, x2 = x[..., :half], x[..., half:]
    return jnp.concatenate([x1 * cos - x2 * sin, x2 * cos + x1 * sin], -1)


def _heads_major(a):
    return jnp.transpose(a, (0, 2, 1, 3))


def _pad_keys(a, lp, axis):
    pad = [(0, 0)] * a.ndim
    pad[axis] = (0, lp - a.shape[axis])
    return jnp.pad(a, pad)


def _layer(h, b, t, past, ret_state, w, alpha):
    m, d = h.shape
    p_len = 0 if past is None else past[0].shape[1]
    n_keys = p_len + t
    tq = min(t, 128)
    tk = 128
    lp = -(-n_keys // tk) * tk
    proj = _inproj(h, w['w_mix']).reshape(b, t, -1)

    def wide(name):
        i = _WIDE.index(name)
        return proj[..., i * BRANCH_WIDTH:(i + 1) * BRANCH_WIDTH].reshape(b, t, N_HEADS, HEAD_DIM)

    base = len(_WIDE) * BRANCH_WIDTH
    dsa_k = proj[..., base:base + HEAD_DIM]
    dsa_v = proj[..., base + HEAD_DIM:base + 2 * HEAD_DIM]
    idx_k = proj[..., base + 2 * HEAD_DIM:base + 3 * HEAD_DIM]
    fox_f = proj[..., base + 3 * HEAD_DIM:base + 3 * HEAD_DIM + N_HEADS]
    idx_w = proj[..., base + 3 * HEAD_DIM + N_HEADS:base + 3 * HEAD_DIM + 2 * N_HEADS]

    def with_past(new, old):
        return new if old is None else jnp.concatenate([old, new], axis=1)

    def keys_major(new, old):
        return _pad_keys(_heads_major(with_past(new, old)), lp, 2).astype(BF)

    old = (None,) * 8 if past is None else past
    sb_k0, sb_v0, fox_k0, fox_v0, fox_lf0, dsa_k0, dsa_v0, dsa_ki0 = old

    sb_q, sb_k, sb_v = wide('sb_q'), wide('sb_k'), wide('sb_v')
    y_sb = _sb_attention(_heads_major(sb_q).astype(BF), keys_major(sb_k, sb_k0), keys_major(sb_v, sb_v0),
                         p_len, tq, tk)

    pos = p_len + jnp.arange(t, dtype=jnp.int32)
    ret_q = _heads_major(_rotary(wide('ret_q'), pos))
    ret_k = _heads_major(_rotary(wide('ret_k'), pos) * QK_SCALE)
    y_ret, ret_state_new = _retention(ret_q, ret_k, _heads_major(wide('ret_v')), _heads_major(wide('ret_g')),
                                      ret_state, min(t, 256))

    fox_q, fox_k, fox_v = wide('fox_q'), wide('fox_k'), wide('fox_v')
    fox_lf = jax.nn.log_sigmoid(fox_f + w['b_forget'])
    cum = jnp.cumsum(with_past(fox_lf, fox_lf0), axis=1)
    cq = jnp.transpose(cum[:, p_len:], (0, 2, 1))[..., None]
    ck = _pad_keys(jnp.transpose(cum, (0, 2, 1)), lp, 2)[:, :, None, :]
    y_fox = _fox_attention(_heads_major(fox_q).astype(BF), keys_major(fox_k, fox_k0), keys_major(fox_v, fox_v0),
                           cq, ck, p_len, tq, tk)

    top_k = min(DSA_TOP_K, n_keys // 4)
    flat_keys = lambda new, old_: _pad_keys(with_past(new, old_), lp, 1).astype(BF)
    y_dsa = _dsa_attention(_heads_major(wide('dsa_q')).astype(BF), _heads_major(wide('idx_q')).astype(BF), idx_w,
                           flat_keys(dsa_k, dsa_k0), flat_keys(dsa_v, dsa_v0), flat_keys(idx_k, dsa_ki0),
                           p_len, n_keys, tq, tk, top_k)

    ys = [_heads_major(y).reshape(m, BRANCH_WIDTH) for y in (y_sb, y_ret, y_fox, y_dsa)]
    h = _merge(h, ys, w['w_gate'], w['w_branch'], w['w_out'], w['ln1_g'], w['ln1_b'], alpha)
    h = _ffn(h, w['w_ffn_in'], w['w_ffn_out'], w['ln2_g'], w['ln2_b'], alpha)
    return h, (sb_k, sb_v, ret_state_new, fox_k, fox_v, fox_lf, dsa_k, dsa_v, idx_k)


def kernel(x_prompt, x_sample, cache_sb_k, cache_sb_v, state_ret, cache_fox_k, cache_fox_v, cache_fox_logf,
           cache_dsa_k, cache_dsa_v, cache_dsa_kidx, w_in, b_forget, w_branch, w_out, ln1_g, ln1_b,
           w_ffn_in, w_ffn_out, ln2_g, ln2_b):
    depth = w_in.shape[0]
    alpha = float((2 * depth) ** 0.25)
    bp, tp, d = x_prompt.shape
    bs, ts, _ = x_sample.shape
    hp = x_prompt.reshape(bp * tp, d)
    hs = x_sample.reshape(bs * ts, d)
    ret_zero = jnp.zeros((bp, N_HEADS, HEAD_DIM, HEAD_DIM), F32)
    st_p, st_s = [], []
    for l in range(depth):
        w_mix, w_gate = _split_w_in(w_in[l])
        w = dict(w_mix=w_mix, w_gate=w_gate, b_forget=b_forget[l], w_branch=w_branch[l].astype(BF),
                 w_out=w_out[l].astype(BF), ln1_g=ln1_g[l][None], ln1_b=ln1_b[l][None],
                 w_ffn_in=w_ffn_in[l].astype(BF), w_ffn_out=w_ffn_out[l].astype(BF),
                 ln2_g=ln2_g[l][None], ln2_b=ln2_b[l][None])
        hp, sp = _layer(hp, bp, tp, None, ret_zero, w, alpha)
        past = (cache_sb_k[l], cache_sb_v[l], cache_fox_k[l], cache_fox_v[l], cache_fox_logf[l],
                cache_dsa_k[l], cache_dsa_v[l], cache_dsa_kidx[l])
        hs, ss = _layer(hs, bs, ts, past, state_ret[l], w, alpha)
        st_p.append(sp)
        st_s.append(ss)

    def stacked(states, i):
        return jnp.stack([s[i] for s in states])

    return ((hp.reshape(bp, tp, d), hs.reshape(bs, ts, d))
            + tuple(stacked(st_p, i) for i in range(9)) + tuple(stacked(st_s, i) for i in range(9)))
```

```python
import functools

import numpy as np
import jax
import jax.numpy as jnp
from jax import lax
from jax.experimental import pallas as pl
from jax.experimental.pallas import tpu as pltpu

HEAD_DIM = 64
N_HEADS = 4
BRANCH_WIDTH = N_HEADS * HEAD_DIM
CHUNK_SHIFT = 6
DSA_TOP_K = 256
ROPE_BASE = 10000.0
LN_EPS = 1e-5
QK_SCALE = HEAD_DIM ** -0.5
IDX_HEAD_SCALE = N_HEADS ** -0.5
MASK_VALUE = -1e30
INT_MIN = -2 ** 31

V7X_VMEM_LIMIT_BYTES = 56 * 1024 * 1024
LANES = 128
HEADS_PER_COL = LANES // HEAD_DIM
N_COLS = N_HEADS // HEADS_PER_COL

BF = jnp.bfloat16
F32 = jnp.float32


def _dot(a, b):
    return jnp.dot(a, b, preferred_element_type=F32)


def _dot_nt(a, b):
    return lax.dot_general(a, b, (((1,), (1,)), ((), ())), preferred_element_type=F32)


def _dot_tn(a, b):
    return lax.dot_general(a, b, (((0,), (0,)), ((), ())), preferred_element_type=F32)


def _params(*sem):
    return pltpu.CompilerParams(dimension_semantics=sem, vmem_limit_bytes=V7X_VMEM_LIMIT_BYTES)


def _const_spec(shape):
    nd = len(shape)
    return pl.BlockSpec(shape, lambda *_: (0,) * nd)


def _layer_norm(x, g, b):
    xc = x - jnp.mean(x, axis=-1, keepdims=True)
    var = jnp.mean(xc * xc, axis=-1, keepdims=True)
    return xc * lax.rsqrt(var + LN_EPS) * g + b


def _row_tile(m, want):
    t = min(m, want)
    assert m % t == 0
    return t


def _col(c):
    return slice(c * LANES, (c + 1) * LANES)


def _inproj_body(x_ref, w_ref, o_ref, *, n_chunk):
    xb = x_ref[...].astype(BF)
    n = w_ref.shape[1]
    for c in range(0, n, n_chunk):
        o_ref[:, c:c + n_chunk] = _dot(xb, w_ref[:, c:c + n_chunk])


def _inproj(x, w):
    m, d = x.shape
    n = w.shape[1]
    tm = _row_tile(m, 512)
    n_chunk = 256
    assert n % n_chunk == 0
    return pl.pallas_call(
        functools.partial(_inproj_body, n_chunk=n_chunk),
        out_shape=jax.ShapeDtypeStruct((m, n), F32),
        grid=(m // tm,),
        in_specs=[pl.BlockSpec((tm, d), lambda i: (i, 0)), _const_spec((d, n))],
        out_specs=pl.BlockSpec((tm, n), lambda i: (i, 0)),
        compiler_params=_params("parallel"),
        name="inproj",
    )(x, w)


def _block_counts(q0, tk, last_key):
    return lax.div(q0, tk), lax.div(last_key, tk) + 1


def _low_lanes():
    return lax.broadcasted_iota(jnp.int32, (1, LANES), 1) < HEAD_DIM


def _head_queries(q_ref):
    low = _low_lanes()
    out = []
    for h in range(N_HEADS):
        qc = q_ref[0, :, _col(h // HEADS_PER_COL)]
        keep = low if h % HEADS_PER_COL == 0 else jnp.logical_not(low)
        out.append(jnp.where(keep, qc, jnp.zeros_like(qc)))
    return out


def _merge_heads(first, second):
    return jnp.where(_low_lanes(), first, second)


def _sb_body(q_ref, kt_ref, v_ref, o_ref, *, p_len, tq, tk):
    q0 = p_len + pl.program_id(1) * tq
    qm = _head_queries(q_ref)
    qpos = q0 + lax.broadcasted_iota(jnp.int32, (tq, 1), 0)
    suffix = (lax.broadcasted_iota(jnp.int32, (tk, tk), 0)
              > lax.broadcasted_iota(jnp.int32, (tk, tk), 1)).astype(BF)
    n_full, n_all = _block_counts(q0, tk, jnp.maximum(q0 + tq - 2, 0))

    def step(kb, carry, masked):
        laters, accs = carry
        s0 = pl.multiple_of(kb * tk, tk)
        if masked:
            earlier = (s0 + lax.broadcasted_iota(jnp.int32, (1, tk), 1)) < qpos
        new_laters, outs = [], []
        for h in range(N_HEADS):
            c = h // HEADS_PER_COL
            z = _dot(qm[h], kt_ref[0, _col(c), pl.ds(s0, tk)]) * QK_SCALE
            t = jnp.log1p(jnp.exp(-jnp.abs(z)))
            log_1mb = jnp.minimum(-z, 0.0) - t
            log_b = jnp.minimum(z, 0.0) - t
            if masked:
                log_1mb = jnp.where(earlier, log_1mb, 0.0)
            hi = log_1mb.astype(BF)
            lo = (log_1mb - hi.astype(F32)).astype(BF)
            after = _dot(hi, suffix) + _dot(lo, suffix) + laters[h]
            w = jnp.exp(log_b + after)
            if masked:
                w = jnp.where(earlier, w, 0.0)
            outs.append(_dot(w.astype(BF), v_ref[0, pl.ds(s0, tk), _col(c)]))
            new_laters.append(laters[h] + jnp.sum(log_1mb, axis=-1, keepdims=True))
        accs = tuple(accs[c] + _merge_heads(outs[2 * c], outs[2 * c + 1]) for c in range(N_COLS))
        return tuple(new_laters), accs

    carry = (tuple(jnp.zeros((tq, 1), F32) for _ in range(N_HEADS)),
             tuple(jnp.zeros((tq, LANES), F32) for _ in range(N_COLS)))
    carry = lax.fori_loop(0, n_all - n_full, lambda i, c: step(n_all - 1 - i, c, True), carry)
    carry = lax.fori_loop(0, n_full, lambda i, c: step(n_full - 1 - i, c, False), carry)
    for c in range(N_COLS):
        o_ref[0, :, _col(c)] = carry[1][c].astype(o_ref.dtype)


def _sb_attention(q, kt, v, p_len, tq, tk):
    b, t, w = q.shape
    lp = v.shape[1]
    q_spec = pl.BlockSpec((1, tq, w), lambda bi, qi: (bi, qi, 0))
    return pl.pallas_call(
        functools.partial(_sb_body, p_len=p_len, tq=tq, tk=tk),
        out_shape=jax.ShapeDtypeStruct((b, t, w), BF),
        grid=(b, t // tq),
        in_specs=[q_spec, pl.BlockSpec((1, w, lp), lambda bi, qi: (bi, 0, 0)),
                  pl.BlockSpec((1, lp, w), lambda bi, qi: (bi, 0, 0))],
        out_specs=q_spec,
        compiler_params=_params("parallel", "arbitrary"),
        name="sb_attention",
    )(q, kt, v)


def _online_softmax_step(logits, m, l):
    m_new = jnp.maximum(m, jnp.max(logits, axis=-1, keepdims=True))
    alpha = jnp.exp(m - m_new)
    p = jnp.exp(logits - m_new)
    return m_new, alpha, alpha * l + jnp.sum(p, axis=-1, keepdims=True), p


def _fox_body(q_ref, kt_ref, v_ref, cq_ref, ck_ref, o_ref, *, p_len, tq, tk):
    q0 = p_len + pl.program_id(1) * tq
    qm = _head_queries(q_ref)
    cq = cq_ref[0]
    qpos = q0 + lax.broadcasted_iota(jnp.int32, (tq, 1), 0)
    n_full, n_all = _block_counts(q0, tk, q0 + tq - 1)

    def step(kb, carry, masked):
        ms, ls, accs = carry
        s0 = pl.multiple_of(kb * tk, tk)
        if masked:
            visible = (s0 + lax.broadcasted_iota(jnp.int32, (1, tk), 1)) <= qpos
        new_ms, new_ls, alphas, pvs = [], [], [], []
        for h in range(N_HEADS):
            c = h // HEADS_PER_COL
            logits = (_dot(qm[h], kt_ref[0, _col(c), pl.ds(s0, tk)]) * QK_SCALE
                      + cq[:, h:h + 1] - ck_ref[0, h:h + 1, pl.ds(s0, tk)])
            if masked:
                logits = jnp.where(visible, logits, MASK_VALUE)
            m_new, alpha, l_new, p = _online_softmax_step(logits, ms[h], ls[h])
            new_ms.append(m_new)
            new_ls.append(l_new)
            alphas.append(alpha)
            pvs.append(_dot(p.astype(BF), v_ref[0, pl.ds(s0, tk), _col(c)]))
        accs = tuple(_merge_heads(alphas[2 * c] * accs[c] + pvs[2 * c],
                                  alphas[2 * c + 1] * accs[c] + pvs[2 * c + 1]) for c in range(N_COLS))
        return tuple(new_ms), tuple(new_ls), accs

    carry = (tuple(jnp.full((tq, 1), -jnp.inf, F32) for _ in range(N_HEADS)),
             tuple(jnp.zeros((tq, 1), F32) for _ in range(N_HEADS)),
             tuple(jnp.zeros((tq, LANES), F32) for _ in range(N_COLS)))
    carry = lax.fori_loop(0, n_full, lambda i, c: step(i, c, False), carry)
    carry = lax.fori_loop(n_full, n_all, lambda i, c: step(i, c, True), carry)
    _, ls, accs = carry
    for c in range(N_COLS):
        o_ref[0, :, _col(c)] = (accs[c] / _merge_heads(ls[2 * c], ls[2 * c + 1])).astype(o_ref.dtype)


def _fox_attention(q, kt, v, cq, ck, p_len, tq, tk):
    b, t, w = q.shape
    lp = v.shape[1]
    q_spec = pl.BlockSpec((1, tq, w), lambda bi, qi: (bi, qi, 0))
    return pl.pallas_call(
        functools.partial(_fox_body, p_len=p_len, tq=tq, tk=tk),
        out_shape=jax.ShapeDtypeStruct((b, t, w), BF),
        grid=(b, t // tq),
        in_specs=[q_spec, pl.BlockSpec((1, w, lp), lambda bi, qi: (bi, 0, 0)),
                  pl.BlockSpec((1, lp, w), lambda bi, qi: (bi, 0, 0)),
                  pl.BlockSpec((1, tq, N_HEADS), lambda bi, qi: (bi, qi, 0)),
                  pl.BlockSpec((1, N_HEADS, lp), lambda bi, qi: (bi, 0, 0))],
        out_specs=q_spec,
        compiler_params=_params("parallel", "arbitrary"),
        name="fox_attention",
    )(q, kt, v, cq, ck)


def _dsa_body(q_ref, qi_ref, wi_ref, kt_ref, v_ref, kit_ref, o_ref, keys_ref, *, p_len, n_keys, tq, tk, top_k):
    q0 = p_len + pl.program_id(1) * tq
    qpos = q0 + lax.broadcasted_iota(jnp.int32, (tq, 1), 0)
    limit = jnp.minimum(((qpos >> CHUNK_SHIFT) + 1) << CHUNK_SHIFT, n_keys)
    last_limit = jnp.minimum((((q0 + tq - 1) >> CHUNK_SHIFT) + 1) << CHUNK_SHIFT, n_keys)
    n_blk = lax.div(last_limit - 1, tk) + 1

    def key_pos(s0):
        return s0 + lax.broadcasted_iota(jnp.int32, (1, tk), 1)

    wi = wi_ref[0] * IDX_HEAD_SCALE
    qim = _head_queries(qi_ref)

    def score_step(kb, _):
        s0 = pl.multiple_of(kb * tk, tk)
        kit = kit_ref[0, :, pl.ds(s0, tk)]
        score = jnp.zeros((tq, tk), F32)
        for h in range(N_HEADS):
            score = score + wi[:, h:h + 1] * jnp.maximum(_dot(qim[h], kit) * QK_SCALE, 0.0)
        score = jnp.where(key_pos(s0) < limit, score, -jnp.inf)
        bits = pltpu.bitcast(score, jnp.int32)
        keys_ref[:, pl.ds(s0, tk)] = jnp.where(bits < 0, jnp.int32(INT_MIN) - bits, bits)
        return 0

    lax.fori_loop(0, n_blk, score_step, 0)

    def count(pred):
        def body(kb, acc):
            s0 = pl.multiple_of(kb * tk, tk)
            return acc + jnp.where(pred(keys_ref[:, pl.ds(s0, tk)]), 1.0, 0.0)
        acc = lax.fori_loop(0, n_blk, body, jnp.zeros((tq, tk), F32))
        return jnp.sum(acc, axis=-1, keepdims=True)

    kf = jnp.float32(top_k)
    zero = jnp.zeros((tq, 1), jnp.int32)
    thr = jnp.where(count(lambda key: key >= zero) >= kf, zero, jnp.int32(INT_MIN))

    def bit_step(it, thr):
        cand = thr + lax.shift_left(jnp.int32(1), 30 - it)
        return jnp.where(count(lambda key: key >= cand) >= kf, cand, thr)

    thr = lax.fori_loop(0, 31, bit_step, thr)
    n_tie_wanted = kf - count(lambda key: key > thr)

    before = (lax.broadcasted_iota(jnp.int32, (tk, tk), 0)
              < lax.broadcasted_iota(jnp.int32, (tk, tk), 1)).astype(BF)
    qm = _head_queries(q_ref)

    def attend_step(kb, carry):
        ties_seen, ms, ls, accs = carry
        s0 = pl.multiple_of(kb * tk, tk)
        key = keys_ref[:, pl.ds(s0, tk)]
        tie = jnp.where(key == thr, 1.0, 0.0)
        tie_rank = _dot(tie.astype(BF), before) + ties_seen
        take = jnp.where(key > thr, 1.0, jnp.where(tie_rank < n_tie_wanted, tie, 0.0))
        selected = jnp.where(key_pos(s0) < limit, take, 0.0) > 0.0
        kt = kt_ref[0, :, pl.ds(s0, tk)]
        v = v_ref[0, pl.ds(s0, tk), :]
        new_ms, new_ls, alphas, pvs = [], [], [], []
        for h in range(N_HEADS):
            logits = jnp.where(selected, _dot(qm[h], kt) * QK_SCALE, MASK_VALUE)
            m_new, alpha, l_new, p = _online_softmax_step(logits, ms[h], ls[h])
            new_ms.append(m_new)
            new_ls.append(l_new)
            alphas.append(alpha)
            pvs.append(_dot(p.astype(BF), v))
        accs = tuple(_merge_heads(alphas[2 * c] * accs[c] + pvs[2 * c],
                                  alphas[2 * c + 1] * accs[c] + pvs[2 * c + 1]) for c in range(N_COLS))
        ties_seen = ties_seen + jnp.sum(tie, axis=-1, keepdims=True)
        return ties_seen, tuple(new_ms), tuple(new_ls), accs

    carry = (jnp.zeros((tq, 1), F32),
             tuple(jnp.full((tq, 1), -jnp.inf, F32) for _ in range(N_HEADS)),
             tuple(jnp.zeros((tq, 1), F32) for _ in range(N_HEADS)),
             tuple(jnp.zeros((tq, LANES), F32) for _ in range(N_COLS)))
    _, _, ls, accs = lax.fori_loop(0, n_blk, attend_step, carry)
    for c in range(N_COLS):
        o_ref[0, :, _col(c)] = (accs[c] / _merge_heads(ls[2 * c], ls[2 * c + 1])).astype(o_ref.dtype)


def _dsa_attention(q, qi, wi, kt, v, kit, p_len, n_keys, tq, tk, top_k):
    b, t, w = q.shape
    lp = v.shape[1]
    q_spec = pl.BlockSpec((1, tq, w), lambda bi, qi_: (bi, qi_, 0))
    kt_spec = pl.BlockSpec((1, LANES, lp), lambda bi, qi_: (bi, 0, 0))
    return pl.pallas_call(
        functools.partial(_dsa_body, p_len=p_len, n_keys=n_keys, tq=tq, tk=tk, top_k=top_k),
        out_shape=jax.ShapeDtypeStruct((b, t, w), BF),
        grid=(b, t // tq),
        in_specs=[q_spec, q_spec, pl.BlockSpec((1, tq, N_HEADS), lambda bi, qi_: (bi, qi_, 0)),
                  kt_spec, pl.BlockSpec((1, lp, LANES), lambda bi, qi_: (bi, 0, 0)), kt_spec],
        out_specs=q_spec,
        scratch_shapes=[pltpu.VMEM((tq, lp), jnp.int32)],
        compiler_params=_params("parallel", "arbitrary"),
        name="dsa_attention",
    )(q, qi, wi, kt, v, kit)


def _ret_body(q_ref, k_ref, v_ref, g_ref, s0_ref, dec_ref, qd_ref, kd_ref, sd_ref, o_ref, so_ref, state_ref):
    c = pl.program_id(2)

    @pl.when(c == 0)
    def _():
        state_ref[...] = s0_ref[0, 0]

    qb = q_ref[0, 0].astype(BF)
    k = k_ref[0, 0]
    vb = v_ref[0, 0].astype(BF)
    state = state_ref[...]
    scores = _dot_nt(qb, k.astype(BF)) * dec_ref[0]
    o = _dot(scores.astype(BF), vb) + _dot(qb, state.astype(BF)) * qd_ref[0]
    state = sd_ref[0] * state + _dot_tn((k * kd_ref[0]).astype(BF), vb)
    state_ref[...] = state
    oc = o - jnp.mean(o, axis=-1, keepdims=True)
    on = oc * lax.rsqrt(jnp.mean(oc * oc, axis=-1, keepdims=True) + LN_EPS)
    g = g_ref[0, 0]
    o_ref[0, 0] = (on * (g * jax.nn.sigmoid(g))).astype(o_ref.dtype)

    @pl.when(c == pl.num_programs(2) - 1)
    def _():
        so_ref[0, 0] = state


def _retention(q, k, v, g, state0, c):
    b, h, t, _ = q.shape
    log_gamma = np.log(1.0 - 2.0 ** (-5.0 - np.arange(h, dtype=np.float64)))
    n = np.arange(c, dtype=np.float64)
    rel = n[:, None] - n[None, :]
    decay = np.where(rel >= 0, np.exp(np.maximum(rel, 0.0)[None] * log_gamma[:, None, None]), 0.0)
    q_decay = np.exp((n[None, :] + 1.0) * log_gamma[:, None])[..., None]
    k_decay = np.exp((c - 1.0 - n)[None, :] * log_gamma[:, None])[..., None]
    s_decay = np.exp(c * log_gamma)[:, None, None]
    tables = [jnp.asarray(a, F32) for a in (decay, q_decay, k_decay, s_decay)]
    x_spec = pl.BlockSpec((1, 1, c, HEAD_DIM), lambda bi, hi, ci: (bi, hi, ci, 0))
    s_spec = pl.BlockSpec((1, 1, HEAD_DIM, HEAD_DIM), lambda bi, hi, ci: (bi, hi, 0, 0))

    def t_spec(a):
        return pl.BlockSpec((1,) + a.shape[1:], lambda bi, hi, ci: (hi, 0, 0))

    return pl.pallas_call(
        _ret_body,
        out_shape=(jax.ShapeDtypeStruct((b, h, t, HEAD_DIM), BF),
                   jax.ShapeDtypeStruct((b, h, HEAD_DIM, HEAD_DIM), F32)),
        grid=(b, h, t // c),
        in_specs=[x_spec, x_spec, x_spec, x_spec, s_spec] + [t_spec(a) for a in tables],
        out_specs=(x_spec, s_spec),
        scratch_shapes=[pltpu.VMEM((HEAD_DIM, HEAD_DIM), F32)],
        compiler_params=_params("parallel", "parallel", "arbitrary"),
        name="retention",
    )(q, k, v, g, state0, *tables)


def _merge_body(h_ref, y0_ref, y1_ref, y2_ref, y3_ref, wg_ref, wb_ref, wo_ref, g_ref, b_ref, o_ref, *, alpha):
    h = h_ref[...]
    hb = h.astype(BF)
    d = h.shape[1]
    merged = jnp.zeros(h.shape, F32)
    for i, y_ref in enumerate((y0_ref, y1_ref, y2_ref, y3_ref)):
        gate = jax.nn.sigmoid(_dot(hb, wg_ref[:, i * d:(i + 1) * d]))
        merged = merged + gate * _dot(y_ref[...], wb_ref[i])
    r = alpha * h + _dot(merged.astype(BF), wo_ref[...])
    o_ref[...] = _layer_norm(r, g_ref[...], b_ref[...])


def _merge(h, ys, w_gate, w_branch, w_out, ln_g, ln_b, alpha):
    m, d = h.shape
    tm = _row_tile(m, 256)
    row = lambda w: pl.BlockSpec((tm, w), lambda i: (i, 0))
    return pl.pallas_call(
        functools.partial(_merge_body, alpha=alpha),
        out_shape=jax.ShapeDtypeStruct((m, d), F32),
        grid=(m // tm,),
        in_specs=[row(d)] + [row(BRANCH_WIDTH)] * 4
                 + [_const_spec(w_gate.shape), _const_spec(w_branch.shape), _const_spec(w_out.shape),
                    _const_spec((1, d)), _const_spec((1, d))],
        out_specs=row(d),
        compiler_params=_params("parallel"),
        name="merge",
    )(h, *ys, w_gate, w_branch, w_out, ln_g, ln_b)


def _ffn_body(h_ref, wi_ref, wo_ref, g_ref, b_ref, o_ref, *, alpha, f_chunk):
    h = h_ref[...]
    hb = h.astype(BF)
    f = wo_ref.shape[0]
    acc = jnp.zeros(h.shape, F32)
    for c in range(0, f, f_chunk):
        a = _dot(hb, wi_ref[:, c:c + f_chunk])
        u = _dot(hb, wi_ref[:, f + c:f + c + f_chunk])
        acc = acc + _dot((a * jax.nn.sigmoid(a) * u).astype(BF), wo_ref[c:c + f_chunk, :])
    o_ref[...] = _layer_norm(alpha * h + acc, g_ref[...], b_ref[...])


def _ffn(h, w_in, w_out, ln_g, ln_b, alpha):
    m, d = h.shape
    f = w_out.shape[0]
    tm = _row_tile(m, 256)
    f_chunk = f // 2 if (f // 2) % LANES == 0 else f
    row = pl.BlockSpec((tm, d), lambda i: (i, 0))
    return pl.pallas_call(
        functools.partial(_ffn_body, alpha=alpha, f_chunk=f_chunk),
        out_shape=jax.ShapeDtypeStruct((m, d), F32),
        grid=(m // tm,),
        in_specs=[row, _const_spec(w_in.shape), _const_spec(w_out.shape),
                  _const_spec((1, d)), _const_spec((1, d))],
        out_specs=row,
        compiler_params=_params("parallel"),
        name="ffn",
    )(h, w_in, w_out, ln_g, ln_b)


def _in_layout(d):
    w = BRANCH_WIDTH
    return (('sb_q', w), ('sb_k', w), ('sb_v', w), ('ret_q', w), ('ret_k', w), ('ret_v', w), ('ret_g', w),
            ('fox_q', w), ('fox_k', w), ('fox_v', w), ('fox_f', N_HEADS),
            ('dsa_q', w), ('dsa_k', HEAD_DIM), ('dsa_v', HEAD_DIM),
            ('idx_q', w), ('idx_k', HEAD_DIM), ('idx_w', N_HEADS), ('merge_gate', 4 * d))


_WIDE = ('sb_q', 'sb_k', 'sb_v', 'ret_q', 'ret_k', 'ret_v', 'ret_g', 'fox_q', 'fox_k', 'fox_v', 'dsa_q', 'idx_q')
_NARROW = ('dsa_k', 'dsa_v', 'idx_k', 'fox_f', 'idx_w')


def _split_w_in(w_in):
    d = w_in.shape[0]
    cols, off = {}, 0
    for name, width in _in_layout(d):
        cols[name] = w_in[:, off:off + width]
        off += width
    assert off == w_in.shape[1]
    mix = jnp.concatenate([cols[n] for n in _WIDE + _NARROW], axis=1)
    pad = (-mix.shape[1]) % 256
    mix = jnp.pad(mix, ((0, 0), (0, pad)))
    return mix.astype(BF), cols['merge_gate'].astype(BF)


def _rotary(x, pos):
    half = x.shape[-1] // 2
    inv_freq = ROPE_BASE ** (-jnp.arange(half, dtype=F32) / half)
    ang = pos.astype(F32)[:, None] * inv_freq[None, :]
    cos = jnp.cos(ang)[None, :, None, :]
    sin = jnp.sin(ang)[None, :, None, :]
    x1, x2 = x[..., :half], x[..., half:]
    return jnp.concatenate([x1 * cos - x2 * sin, x2 * cos + x1 * sin], -1)


def _heads_major(a):
    return jnp.transpose(a, (0, 2, 1, 3))


def _pad_keys(a, lp):
    return jnp.pad(a, ((0, 0), (0, lp - a.shape[1]), (0, 0)))


def _key_tiles(t, n_keys):
    tq = min(t, 256)
    tiles = dict(sb=256, fox=512, dsa=512)
    padded = {name: -(-n_keys // tk) * tk for name, tk in tiles.items()}
    return tq, tiles, padded


def _layer(h, b, t, past, ret_state, w, alpha):
    m, d = h.shape
    p_len = 0 if past is None else past[0].shape[1]
    n_keys = p_len + t
    tq, tk, lp = _key_tiles(t, n_keys)
    proj = _inproj(h, w['w_mix']).reshape(b, t, -1)

    def wide(name):
        i = _WIDE.index(name)
        return proj[..., i * BRANCH_WIDTH:(i + 1) * BRANCH_WIDTH]

    def heads(a):
        return a.reshape(a.shape[0], a.shape[1], N_HEADS, HEAD_DIM)

    base = len(_WIDE) * BRANCH_WIDTH
    dsa_k = proj[..., base:base + HEAD_DIM]
    dsa_v = proj[..., base + HEAD_DIM:base + 2 * HEAD_DIM]
    idx_k = proj[..., base + 2 * HEAD_DIM:base + 3 * HEAD_DIM]
    fox_f = proj[..., base + 3 * HEAD_DIM:base + 3 * HEAD_DIM + N_HEADS]
    idx_w = proj[..., base + 3 * HEAD_DIM + N_HEADS:base + 3 * HEAD_DIM + 2 * N_HEADS]

    def with_past(new, old):
        if old is None:
            return new
        return jnp.concatenate([old.reshape(old.shape[0], old.shape[1], -1), new], axis=1)

    def keys_t(new, old, lp_, reps=1):
        kt = jnp.transpose(_pad_keys(with_past(new, old), lp_).astype(BF), (0, 2, 1))
        return kt if reps == 1 else jnp.concatenate([kt] * reps, axis=1)

    def values(new, old, lp_, reps=1):
        v = _pad_keys(with_past(new, old), lp_).astype(BF)
        return v if reps == 1 else jnp.concatenate([v] * reps, axis=2)

    old = (None,) * 8 if past is None else past
    sb_k0, sb_v0, fox_k0, fox_v0, fox_lf0, dsa_k0, dsa_v0, dsa_ki0 = old

    sb_k, sb_v = wide('sb_k'), wide('sb_v')
    y_sb = _sb_attention(wide('sb_q').astype(BF), keys_t(sb_k, sb_k0, lp['sb']), values(sb_v, sb_v0, lp['sb']),
                         p_len, tq, tk['sb'])

    pos = p_len + jnp.arange(t, dtype=jnp.int32)
    ret_q = _heads_major(_rotary(heads(wide('ret_q')), pos))
    ret_k = _heads_major(_rotary(heads(wide('ret_k')), pos) * QK_SCALE)
    y_ret, ret_state_new = _retention(ret_q, ret_k, _heads_major(heads(wide('ret_v'))),
                                      _heads_major(heads(wide('ret_g'))), ret_state, min(t, 256))
    y_ret = _heads_major(y_ret).reshape(b, t, BRANCH_WIDTH)

    fox_k, fox_v = wide('fox_k'), wide('fox_v')
    fox_lf = jax.nn.log_sigmoid(fox_f + w['b_forget'])
    cum = jnp.cumsum(with_past(fox_lf, fox_lf0), axis=1)
    ck = jnp.transpose(_pad_keys(cum, lp['fox']), (0, 2, 1))
    y_fox = _fox_attention(wide('fox_q').astype(BF), keys_t(fox_k, fox_k0, lp['fox']),
                           values(fox_v, fox_v0, lp['fox']), cum[:, p_len:], ck, p_len, tq, tk['fox'])

    top_k = min(DSA_TOP_K, n_keys // 4)
    y_dsa = _dsa_attention(wide('dsa_q').astype(BF), wide('idx_q').astype(BF), idx_w,
                           keys_t(dsa_k, dsa_k0, lp['dsa'], HEADS_PER_COL),
                           values(dsa_v, dsa_v0, lp['dsa'], HEADS_PER_COL),
                           keys_t(idx_k, dsa_ki0, lp['dsa'], HEADS_PER_COL),
                           p_len, n_keys, tq, tk['dsa'], top_k)

    ys = [y.reshape(m, BRANCH_WIDTH) for y in (y_sb, y_ret, y_fox, y_dsa)]
    h = _merge(h, ys, w['w_gate'], w['w_branch'], w['w_out'], w['ln1_g'], w['ln1_b'], alpha)
    h = _ffn(h, w['w_ffn_in'], w['w_ffn_out'], w['ln2_g'], w['ln2_b'], alpha)
    return h, (heads(sb_k), heads(sb_v), ret_state_new, heads(fox_k), heads(fox_v), fox_lf, dsa_k, dsa_v, idx_k)


def kernel(x_prompt, x_sample, cache_sb_k, cache_sb_v, state_ret, cache_fox_k, cache_fox_v, cache_fox_logf,
           cache_dsa_k, cache_dsa_v, cache_dsa_kidx, w_in, b_forget, w_branch, w_out, ln1_g, ln1_b,
           w_ffn_in, w_ffn_out, ln2_g, ln2_b):
    depth = w_in.shape[0]
    alpha = float((2 * depth) ** 0.25)
    bp, tp, d = x_prompt.shape
    bs, ts, _ = x_sample.shape
    hp = x_prompt.reshape(bp * tp, d)
    hs = x_sample.reshape(bs * ts, d)
    ret_zero = jnp.zeros((bp, N_HEADS, HEAD_DIM, HEAD_DIM), F32)
    st_p, st_s = [], []
    for l in range(depth):
        w_mix, w_gate = _split_w_in(w_in[l])
        w = dict(w_mix=w_mix, w_gate=w_gate, b_forget=b_forget[l], w_branch=w_branch[l].astype(BF),
                 w_out=w_out[l].astype(BF), ln1_g=ln1_g[l][None], ln1_b=ln1_b[l][None],
                 w_ffn_in=w_ffn_in[l].astype(BF), w_ffn_out=w_ffn_out[l].astype(BF),
                 ln2_g=ln2_g[l][None], ln2_b=ln2_b[l][None])
        hp, sp = _layer(hp, bp, tp, None, ret_zero, w, alpha)
        past = (cache_sb_k[l], cache_sb_v[l], cache_fox_k[l], cache_fox_v[l], cache_fox_logf[l],
                cache_dsa_k[l], cache_dsa_v[l], cache_dsa_kidx[l])
        hs, ss = _layer(hs, bs, ts, past, state_ret[l], w, alpha)
        st_p.append(sp)
        st_s.append(ss)

    def stacked(states, i):
        return jnp.stack([s[i] for s in states])

    return ((hp.reshape(bp, tp, d), hs.reshape(bs, ts, d))
            + tuple(stacked(st_p, i) for i in range(9)) + tuple(stacked(st_s, i) for i in range(9)))
```

```python
import functools

import numpy as np
import jax
import jax.numpy as jnp
from jax import lax
from jax.experimental import pallas as pl
from jax.experimental.pallas import tpu as pltpu

HEAD_DIM = 64
N_HEADS = 4
BRANCH_WIDTH = N_HEADS * HEAD_DIM
CHUNK_SHIFT = 6
DSA_TOP_K = 256
ROPE_BASE = 10000.0
LN_EPS = 1e-5
QK_SCALE = HEAD_DIM ** -0.5
IDX_HEAD_SCALE = N_HEADS ** -0.5
MASK_VALUE = -1e30
INT_MIN = -2 ** 31

V7X_VMEM_LIMIT_BYTES = 56 * 1024 * 1024
LANES = 128
HEADS_PER_COL = LANES // HEAD_DIM
COUNT_SLAB = 64

BF = jnp.bfloat16
F32 = jnp.float32


def _dot(a, b):
    return jnp.dot(a, b, preferred_element_type=F32)


def _dot_nt(a, b):
    return lax.dot_general(a, b, (((1,), (1,)), ((), ())), preferred_element_type=F32)


def _dot_tn(a, b):
    return lax.dot_general(a, b, (((0,), (0,)), ((), ())), preferred_element_type=F32)


def _params(*sem):
    return pltpu.CompilerParams(dimension_semantics=sem, vmem_limit_bytes=V7X_VMEM_LIMIT_BYTES)


def _const_spec(shape):
    nd = len(shape)
    return pl.BlockSpec(shape, lambda *_: (0,) * nd)


def _layer_norm(x, g, b):
    xc = x - jnp.mean(x, axis=-1, keepdims=True)
    var = jnp.mean(xc * xc, axis=-1, keepdims=True)
    return xc * lax.rsqrt(var + LN_EPS) * g + b


def _row_tile(m, want):
    t = min(m, want)
    assert m % t == 0
    return t


def _col(c):
    return slice(c * LANES, (c + 1) * LANES)


def _inproj_body(x_ref, w_ref, o_ref, *, n_chunk):
    xb = x_ref[...].astype(BF)
    n = w_ref.shape[1]
    for c in range(0, n, n_chunk):
        o_ref[:, c:c + n_chunk] = _dot(xb, w_ref[:, c:c + n_chunk])


def _inproj(x, w):
    m, d = x.shape
    n = w.shape[1]
    tm = _row_tile(m, 512)
    n_chunk = 256
    assert n % n_chunk == 0
    return pl.pallas_call(
        functools.partial(_inproj_body, n_chunk=n_chunk),
        out_shape=jax.ShapeDtypeStruct((m, n), F32),
        grid=(m // tm,),
        in_specs=[pl.BlockSpec((tm, d), lambda i: (i, 0)), _const_spec((d, n))],
        out_specs=pl.BlockSpec((tm, n), lambda i: (i, 0)),
        compiler_params=_params("parallel"),
        name="inproj",
    )(x, w)


def _block_counts(q0, tk, last_key):
    return lax.div(q0, tk), lax.div(last_key, tk) + 1


def _head_queries(qt_ref):
    low = lax.broadcasted_iota(jnp.int32, (LANES, 1), 0) < HEAD_DIM
    out = []
    for h in range(N_HEADS):
        qc = qt_ref[0, _col(h // HEADS_PER_COL), :]
        keep = low if h % HEADS_PER_COL == 0 else jnp.logical_not(low)
        out.append(jnp.where(keep, qc, jnp.zeros_like(qc)))
    return out


def _head_rows(h):
    return slice(h * HEAD_DIM, (h + 1) * HEAD_DIM)


def _key_minus_query(tk, tq):
    return (lax.broadcasted_iota(jnp.int32, (tk, tq), 0) - lax.broadcasted_iota(jnp.int32, (tk, tq), 1))


def _qt_spec(w, tq):
    return pl.BlockSpec((1, w, tq), lambda bi, qi: (bi, 0, qi))


def _whole_spec(rows, cols):
    return pl.BlockSpec((1, rows, cols), lambda bi, qi: (bi, 0, 0))


def _sb_body(qt_ref, k_ref, vt_ref, o_ref, *, p_len, tq, tk):
    q0 = p_len + pl.program_id(1) * tq
    qm = _head_queries(qt_ref)
    diff = _key_minus_query(tk, tq)
    later = (lax.broadcasted_iota(jnp.int32, (tk, 2 * tk), 1) & (tk - 1)) > lax.broadcasted_iota(
        jnp.int32, (tk, 2 * tk), 0)
    minus_later = jnp.where(later, -1.0, 0.0).astype(BF)
    n_full, n_all = _block_counts(q0, tk, jnp.maximum(q0 + tq - 2, 0))

    def step(kb, carry, masked):
        laters, accs = carry
        s0 = pl.multiple_of(kb * tk, tk)
        if masked:
            earlier = diff < (q0 - s0)
        zs = [_dot(k_ref[0, pl.ds(s0, tk), _col(h // HEADS_PER_COL)], qm[h]) for h in range(N_HEADS)]
        new_laters, log_bs, afters = [], [], []
        for h in range(N_HEADS):
            z = zs[h]
            minus_abs = pltpu.bitcast(pltpu.bitcast(z, jnp.int32) | jnp.int32(INT_MIN), F32)
            softplus = jnp.maximum(z, 0.0) + jnp.log(1.0 + jnp.exp(minus_abs))
            log_bs.append(z - softplus)
            if masked:
                softplus = jnp.where(earlier, softplus, 0.0)
            hi = softplus.astype(BF)
            lo = (softplus - hi.astype(F32)).astype(BF)
            afters.append(_dot(minus_later, jnp.concatenate([hi, lo], axis=0)) + laters[h])
            new_laters.append(laters[h] - jnp.sum(softplus, axis=0, keepdims=True))
        new_accs = []
        for h in range(N_HEADS):
            w = jnp.exp(log_bs[h] + afters[h])
            if masked:
                w = jnp.where(earlier, w, 0.0)
            new_accs.append(accs[h] + _dot(vt_ref[0, _head_rows(h), pl.ds(s0, tk)], w.astype(BF)))
        return tuple(new_laters), tuple(new_accs)

    carry = (tuple(jnp.zeros((1, tq), F32) for _ in range(N_HEADS)),
             tuple(jnp.zeros((HEAD_DIM, tq), F32) for _ in range(N_HEADS)))
    carry = lax.fori_loop(0, n_all - n_full, lambda i, c: step(n_all - 1 - i, c, True), carry)
    carry = lax.fori_loop(0, n_full, lambda i, c: step(n_full - 1 - i, c, False), carry)
    for h in range(N_HEADS):
        o_ref[0, _head_rows(h), :] = carry[1][h].astype(o_ref.dtype)


def _sb_attention(qt, k, vt, p_len, tq, tk):
    b, w, t = qt.shape
    lp = k.shape[1]
    assert tk & (tk - 1) == 0
    return pl.pallas_call(
        functools.partial(_sb_body, p_len=p_len, tq=tq, tk=tk),
        out_shape=jax.ShapeDtypeStruct((b, w, t), BF),
        grid=(b, t // tq),
        in_specs=[_qt_spec(w, tq), _whole_spec(lp, w), _whole_spec(w, lp)],
        out_specs=_qt_spec(w, tq),
        compiler_params=_params("parallel", "arbitrary"),
        name="sb_attention",
    )(qt, k, vt)


def _online_softmax_step(logits, m, l):
    m_new = jnp.maximum(m, jnp.max(logits, axis=0, keepdims=True))
    alpha = jnp.exp(m - m_new)
    p = jnp.exp(logits - m_new)
    return m_new, alpha, alpha * l + jnp.sum(p, axis=0, keepdims=True), p


def _fox_body(qx_ref, kx_ref, vt_ref, o_ref, *, p_len, tq, tk):
    q0 = p_len + pl.program_id(1) * tq
    diff = _key_minus_query(tk, tq)
    n_full, n_all = _block_counts(q0, tk, q0 + tq - 1)

    def step(kb, carry, masked):
        ms, ls, accs = carry
        s0 = pl.multiple_of(kb * tk, tk)
        if masked:
            visible = diff <= (q0 - s0)
        new_ms, new_ls, new_accs = [], [], []
        all_logits = [_dot(kx_ref[0, pl.ds(s0, tk), _col(h)], qx_ref[0, _col(h), :]) for h in range(N_HEADS)]
        for h in range(N_HEADS):
            logits = all_logits[h]
            if masked:
                logits = jnp.where(visible, logits, MASK_VALUE)
            m_new, alpha, l_new, p = _online_softmax_step(logits, ms[h], ls[h])
            new_ms.append(m_new)
            new_ls.append(l_new)
            new_accs.append(alpha * accs[h] + _dot(vt_ref[0, _head_rows(h), pl.ds(s0, tk)], p.astype(BF)))
        return tuple(new_ms), tuple(new_ls), tuple(new_accs)

    carry = (tuple(jnp.full((1, tq), -jnp.inf, F32) for _ in range(N_HEADS)),
             tuple(jnp.zeros((1, tq), F32) for _ in range(N_HEADS)),
             tuple(jnp.zeros((HEAD_DIM, tq), F32) for _ in range(N_HEADS)))
    carry = lax.fori_loop(0, n_full, lambda i, c: step(i, c, False), carry)
    carry = lax.fori_loop(n_full, n_all, lambda i, c: step(i, c, True), carry)
    _, ls, accs = carry
    for h in range(N_HEADS):
        o_ref[0, _head_rows(h), :] = (accs[h] / ls[h]).astype(o_ref.dtype)


def _fox_attention(qx, kx, vt, p_len, tq, tk):
    b, wx, t = qx.shape
    w, lp = vt.shape[1], vt.shape[2]
    return pl.pallas_call(
        functools.partial(_fox_body, p_len=p_len, tq=tq, tk=tk),
        out_shape=jax.ShapeDtypeStruct((b, w, t), BF),
        grid=(b, t // tq),
        in_specs=[_qt_spec(wx, tq), _whole_spec(lp, wx), _whole_spec(w, lp)],
        out_specs=_qt_spec(w, tq),
        compiler_params=_params("parallel", "arbitrary"),
        name="fox_attention",
    )(qx, kx, vt)


def _dsa_body(qt_ref, qit_ref, wit_ref, k_ref, vt_ref, ki_ref, o_ref, keys_ref, *, p_len, n_keys, tq, tk, top_k):
    q0 = p_len + pl.program_id(1) * tq
    qpos = q0 + lax.broadcasted_iota(jnp.int32, (1, tq), 1)
    limit = jnp.minimum(((qpos >> CHUNK_SHIFT) + 1) << CHUNK_SHIFT, n_keys)
    last_limit = jnp.minimum((((q0 + tq - 1) >> CHUNK_SHIFT) + 1) << CHUNK_SHIFT, n_keys)
    n_blk = lax.div(last_limit - 1, tk) + 1
    key_row = lax.broadcasted_iota(jnp.int32, (tk, tq), 0)

    wit = wit_ref[0]
    qim = _head_queries(qit_ref)

    def score_step(kb, _):
        s0 = pl.multiple_of(kb * tk, tk)
        ki = ki_ref[0, pl.ds(s0, tk), :]
        score = jnp.zeros((tk, tq), F32)
        for h in range(N_HEADS):
            score = score + wit[h:h + 1, :] * jnp.maximum(_dot(ki, qim[h]), 0.0)
        score = jnp.where(key_row < limit - s0, score, -jnp.inf)
        bits = pltpu.bitcast(score, jnp.int32)
        keys_ref[pl.ds(s0, tk), :] = jnp.where(bits < 0, jnp.int32(INT_MIN) - bits, bits)
        return 0

    lax.fori_loop(0, n_blk, score_step, 0)

    def count(pred):
        def body(kb, acc):
            for r in range(0, tk, COUNT_SLAB):
                s0 = pl.multiple_of(kb * tk + r, COUNT_SLAB)
                acc = acc + jnp.where(pred(keys_ref[pl.ds(s0, COUNT_SLAB), :]), 1.0, 0.0)
            return acc
        acc = lax.fori_loop(0, n_blk, body, jnp.zeros((COUNT_SLAB, tq), F32))
        return jnp.sum(acc, axis=0, keepdims=True)

    kf = jnp.float32(top_k)
    zero = jnp.zeros((1, tq), jnp.int32)
    thr = jnp.where(count(lambda key: key >= zero) >= kf, zero, jnp.int32(INT_MIN))

    def bit_step(it, thr):
        cand = thr + lax.shift_left(jnp.int32(1), 30 - it)
        return jnp.where(count(lambda key: key >= cand) >= kf, cand, thr)

    thr = lax.fori_loop(0, 31, bit_step, thr)
    n_tie_wanted = kf - count(lambda key: key > thr)

    earlier_keys = (lax.broadcasted_iota(jnp.int32, (tk, tk), 1)
                    < lax.broadcasted_iota(jnp.int32, (tk, tk), 0)).astype(BF)
    qm = _head_queries(qt_ref)

    def attend_step(kb, carry):
        ties_seen, ms, ls, accs = carry
        s0 = pl.multiple_of(kb * tk, tk)
        key = keys_ref[pl.ds(s0, tk), :]
        tie = jnp.where(key == thr, 1.0, 0.0)
        tie_rank = _dot(earlier_keys, tie.astype(BF)) + ties_seen
        take = jnp.where(key > thr, 1.0, jnp.where(tie_rank < n_tie_wanted, tie, 0.0))
        selected = jnp.where(key_row < limit - s0, take, 0.0) > 0.0
        k = k_ref[0, pl.ds(s0, tk), :]
        vt = vt_ref[0, :, pl.ds(s0, tk)]
        new_ms, new_ls, new_accs = [], [], []
        raw = [_dot(k, qm[h]) for h in range(N_HEADS)]
        for h in range(N_HEADS):
            logits = jnp.where(selected, raw[h], MASK_VALUE)
            m_new, alpha, l_new, p = _online_softmax_step(logits, ms[h], ls[h])
            new_ms.append(m_new)
            new_ls.append(l_new)
            new_accs.append(alpha * accs[h] + _dot(vt, p.astype(BF)))
        ties_seen = ties_seen + jnp.sum(tie, axis=0, keepdims=True)
        return ties_seen, tuple(new_ms), tuple(new_ls), tuple(new_accs)

    carry = (jnp.zeros((1, tq), F32),
             tuple(jnp.full((1, tq), -jnp.inf, F32) for _ in range(N_HEADS)),
             tuple(jnp.zeros((1, tq), F32) for _ in range(N_HEADS)),
             tuple(jnp.zeros((HEAD_DIM, tq), F32) for _ in range(N_HEADS)))
    _, _, ls, accs = lax.fori_loop(0, n_blk, attend_step, carry)
    for h in range(N_HEADS):
        o_ref[0, _head_rows(h), :] = (accs[h] / ls[h]).astype(o_ref.dtype)


def _dsa_attention(qt, qit, wit, k, vt, ki, p_len, n_keys, tq, tk, top_k):
    b, w, t = qt.shape
    lp = k.shape[1]
    return pl.pallas_call(
        functools.partial(_dsa_body, p_len=p_len, n_keys=n_keys, tq=tq, tk=tk, top_k=top_k),
        out_shape=jax.ShapeDtypeStruct((b, w, t), BF),
        grid=(b, t // tq),
        in_specs=[_qt_spec(w, tq), _qt_spec(w, tq), _qt_spec(N_HEADS, tq),
                  _whole_spec(lp, LANES), _whole_spec(HEAD_DIM, lp), _whole_spec(lp, LANES)],
        out_specs=_qt_spec(w, tq),
        scratch_shapes=[pltpu.VMEM((lp, tq), jnp.int32)],
        compiler_params=_params("parallel", "arbitrary"),
        name="dsa_attention",
    )(qt, qit, wit, k, vt, ki)


def _ret_body(q_ref, k_ref, v_ref, g_ref, s0_ref, dec_ref, qd_ref, kd_ref, sd_ref, o_ref, so_ref, state_ref):
    c = pl.program_id(2)

    @pl.when(c == 0)
    def _():
        state_ref[...] = s0_ref[0, 0]

    qb = q_ref[0, 0].astype(BF)
    k = k_ref[0, 0]
    vb = v_ref[0, 0].astype(BF)
    state = state_ref[...]
    scores = _dot_nt(qb, k.astype(BF)) * dec_ref[0]
    o = _dot(scores.astype(BF), vb) + _dot(qb, state.astype(BF)) * qd_ref[0]
    state = sd_ref[0] * state + _dot_tn((k * kd_ref[0]).astype(BF), vb)
    state_ref[...] = state
    oc = o - jnp.mean(o, axis=-1, keepdims=True)
    on = oc * lax.rsqrt(jnp.mean(oc * oc, axis=-1, keepdims=True) + LN_EPS)
    g = g_ref[0, 0]
    o_ref[0, 0] = (on * (g * jax.nn.sigmoid(g))).astype(o_ref.dtype)

    @pl.when(c == pl.num_programs(2) - 1)
    def _():
        so_ref[0, 0] = state


def _retention(q, k, v, g, state0, c):
    b, h, t, _ = q.shape
    log_gamma = np.log(1.0 - 2.0 ** (-5.0 - np.arange(h, dtype=np.float64)))
    n = np.arange(c, dtype=np.float64)
    rel = n[:, None] - n[None, :]
    decay = np.where(rel >= 0, np.exp(np.maximum(rel, 0.0)[None] * log_gamma[:, None, None]), 0.0)
    q_decay = np.exp((n[None, :] + 1.0) * log_gamma[:, None])[..., None]
    k_decay = np.exp((c - 1.0 - n)[None, :] * log_gamma[:, None])[..., None]
    s_decay = np.exp(c * log_gamma)[:, None, None]
    tables = [jnp.asarray(a, F32) for a in (decay, q_decay, k_decay, s_decay)]
    x_spec = pl.BlockSpec((1, 1, c, HEAD_DIM), lambda bi, hi, ci: (bi, hi, ci, 0))
    s_spec = pl.BlockSpec((1, 1, HEAD_DIM, HEAD_DIM), lambda bi, hi, ci: (bi, hi, 0, 0))

    def t_spec(a):
        return pl.BlockSpec((1,) + a.shape[1:], lambda bi, hi, ci: (hi, 0, 0))

    return pl.pallas_call(
        _ret_body,
        out_shape=(jax.ShapeDtypeStruct((b, h, t, HEAD_DIM), BF),
                   jax.ShapeDtypeStruct((b, h, HEAD_DIM, HEAD_DIM), F32)),
        grid=(b, h, t // c),
        in_specs=[x_spec, x_spec, x_spec, x_spec, s_spec] + [t_spec(a) for a in tables],
        out_specs=(x_spec, s_spec),
        scratch_shapes=[pltpu.VMEM((HEAD_DIM, HEAD_DIM), F32)],
        compiler_params=_params("parallel", "parallel", "arbitrary"),
        name="retention",
    )(q, k, v, g, state0, *tables)


def _merge_body(h_ref, y0_ref, y1_ref, y2_ref, y3_ref, wg_ref, wb_ref, wo_ref, g_ref, b_ref, o_ref, *, alpha):
    h = h_ref[...]
    hb = h.astype(BF)
    d = h.shape[1]
    merged = jnp.zeros(h.shape, F32)
    for i, y_ref in enumerate((y0_ref, y1_ref, y2_ref, y3_ref)):
        gate = jax.nn.sigmoid(_dot(hb, wg_ref[:, i * d:(i + 1) * d]))
        merged = merged + gate * _dot(y_ref[...], wb_ref[i])
    r = alpha * h + _dot(merged.astype(BF), wo_ref[...])
    o_ref[...] = _layer_norm(r, g_ref[...], b_ref[...])


def _merge(h, ys, w_gate, w_branch, w_out, ln_g, ln_b, alpha):
    m, d = h.shape
    tm = _row_tile(m, 256)
    row = lambda w: pl.BlockSpec((tm, w), lambda i: (i, 0))
    return pl.pallas_call(
        functools.partial(_merge_body, alpha=alpha),
        out_shape=jax.ShapeDtypeStruct((m, d), F32),
        grid=(m // tm,),
        in_specs=[row(d)] + [row(BRANCH_WIDTH)] * 4
                 + [_const_spec(w_gate.shape), _const_spec(w_branch.shape), _const_spec(w_out.shape),
                    _const_spec((1, d)), _const_spec((1, d))],
        out_specs=row(d),
        compiler_params=_params("parallel"),
        name="merge",
    )(h, *ys, w_gate, w_branch, w_out, ln_g, ln_b)


def _ffn_body(h_ref, wi_ref, wo_ref, g_ref, b_ref, o_ref, *, alpha, f_chunk):
    h = h_ref[...]
    hb = h.astype(BF)
    f = wo_ref.shape[0]
    acc = jnp.zeros(h.shape, F32)
    for c in range(0, f, f_chunk):
        a = _dot(hb, wi_ref[:, c:c + f_chunk])
        u = _dot(hb, wi_ref[:, f + c:f + c + f_chunk])
        acc = acc + _dot((a * jax.nn.sigmoid(a) * u).astype(BF), wo_ref[c:c + f_chunk, :])
    o_ref[...] = _layer_norm(alpha * h + acc, g_ref[...], b_ref[...])


def _ffn(h, w_in, w_out, ln_g, ln_b, alpha):
    m, d = h.shape
    f = w_out.shape[0]
    tm = _row_tile(m, 256)
    f_chunk = f // 2 if (f // 2) % LANES == 0 else f
    row = pl.BlockSpec((tm, d), lambda i: (i, 0))
    return pl.pallas_call(
        functools.partial(_ffn_body, alpha=alpha, f_chunk=f_chunk),
        out_shape=jax.ShapeDtypeStruct((m, d), F32),
        grid=(m // tm,),
        in_specs=[row, _const_spec(w_in.shape), _const_spec(w_out.shape),
                  _const_spec((1, d)), _const_spec((1, d))],
        out_specs=row,
        compiler_params=_params("parallel"),
        name="ffn",
    )(h, w_in, w_out, ln_g, ln_b)


def _in_layout(d):
    w = BRANCH_WIDTH
    return (('sb_q', w), ('sb_k', w), ('sb_v', w), ('ret_q', w), ('ret_k', w), ('ret_v', w), ('ret_g', w),
            ('fox_q', w), ('fox_k', w), ('fox_v', w), ('fox_f', N_HEADS),
            ('dsa_q', w), ('dsa_k', HEAD_DIM), ('dsa_v', HEAD_DIM),
            ('idx_q', w), ('idx_k', HEAD_DIM), ('idx_w', N_HEADS), ('merge_gate', 4 * d))


_WIDE = ('sb_q', 'sb_k', 'sb_v', 'ret_q', 'ret_k', 'ret_v', 'ret_g', 'fox_q', 'fox_k', 'fox_v', 'dsa_q', 'idx_q')
_NARROW = ('dsa_k', 'dsa_v', 'idx_k', 'fox_f', 'idx_w')
_FOLDED_SCALE = dict(sb_q=QK_SCALE, fox_q=QK_SCALE, dsa_q=QK_SCALE, idx_q=QK_SCALE, ret_k=QK_SCALE,
                     idx_w=IDX_HEAD_SCALE)


def _split_w_in(w_in):
    d = w_in.shape[0]
    cols, off = {}, 0
    for name, width in _in_layout(d):
        cols[name] = w_in[:, off:off + width] * _FOLDED_SCALE.get(name, 1.0)
        off += width
    assert off == w_in.shape[1]
    mix = jnp.concatenate([cols[n] for n in _WIDE + _NARROW], axis=1)
    pad = (-mix.shape[1]) % 256
    mix = jnp.pad(mix, ((0, 0), (0, pad)))
    return mix.astype(BF), cols['merge_gate'].astype(BF)


def _rotary(x, pos):
    half = x.shape[-1] // 2
    inv_freq = ROPE_BASE ** (-jnp.arange(half, dtype=F32) / half)
    ang = pos.astype(F32)[:, None] * inv_freq[None, :]
    cos = jnp.cos(ang)[None, :, None, :]
    sin = jnp.sin(ang)[None, :, None, :]
    x1, x2 = x[..., :half], x[..., half:]
    return jnp.concatenate([x1 * cos - x2 * sin, x2 * cos + x1 * sin], -1)


def _heads_major(a):
    return jnp.transpose(a, (0, 2, 1, 3))


def _swap(a):
    return jnp.transpose(a, (0, 2, 1))


def _pad_keys(a, lp):
    return jnp.pad(a, ((0, 0), (0, lp - a.shape[1]), (0, 0)))


def _bf16_terms(x, n):
    out = []
    for _ in range(n):
        t = lax.bitcast_convert_type(lax.bitcast_convert_type(x, jnp.int32) & jnp.int32(-65536), F32)
        out.append(t.astype(BF))
        x = x - t
    return out


def _key_tiles(t, n_keys):
    tq = min(t, 256)
    tiles = dict(sb=256, fox=512, dsa=512)
    padded = {name: -(-n_keys // tk) * tk for name, tk in tiles.items()}
    return tq, tiles, padded


def _layer(h, b, t, past, ret_state, w, alpha):
    m, d = h.shape
    p_len = 0 if past is None else past[0].shape[1]
    n_keys = p_len + t
    tq, tk, lp = _key_tiles(t, n_keys)
    proj = _inproj(h, w['w_mix']).reshape(b, t, -1)

    def wide(name):
        i = _WIDE.index(name)
        return proj[..., i * BRANCH_WIDTH:(i + 1) * BRANCH_WIDTH]

    def heads(a):
        return a.reshape(a.shape[0], a.shape[1], N_HEADS, HEAD_DIM)

    base = len(_WIDE) * BRANCH_WIDTH
    dsa_k = proj[..., base:base + HEAD_DIM]
    dsa_v = proj[..., base + HEAD_DIM:base + 2 * HEAD_DIM]
    idx_k = proj[..., base + 2 * HEAD_DIM:base + 3 * HEAD_DIM]
    fox_f = proj[..., base + 3 * HEAD_DIM:base + 3 * HEAD_DIM + N_HEADS]
    idx_w = proj[..., base + 3 * HEAD_DIM + N_HEADS:base + 3 * HEAD_DIM + 2 * N_HEADS]

    def with_past(new, old):
        if old is None:
            return new
        return jnp.concatenate([old.reshape(old.shape[0], old.shape[1], -1), new], axis=1)

    def keys(new, old, lp_, reps=1):
        k = _pad_keys(with_past(new, old), lp_).astype(BF)
        return k if reps == 1 else jnp.concatenate([k] * reps, axis=2)

    def values_t(new, old, lp_):
        return _swap(_pad_keys(with_past(new, old), lp_).astype(BF))

    def queries_t(name):
        return _swap(wide(name).astype(BF))

    old = (None,) * 8 if past is None else past
    sb_k0, sb_v0, fox_k0, fox_v0, fox_lf0, dsa_k0, dsa_v0, dsa_ki0 = old

    sb_k, sb_v = wide('sb_k'), wide('sb_v')
    y_sb = _sb_attention(queries_t('sb_q'), keys(sb_k, sb_k0, lp['sb']), values_t(sb_v, sb_v0, lp['sb']),
                         p_len, tq, tk['sb'])

    pos = p_len + jnp.arange(t, dtype=jnp.int32)
    ret_q = _heads_major(_rotary(heads(wide('ret_q')), pos))
    ret_k = _heads_major(_rotary(heads(wide('ret_k')), pos))
    y_ret, ret_state_new = _retention(ret_q, ret_k, _heads_major(heads(wide('ret_v'))),
                                      _heads_major(heads(wide('ret_g'))), ret_state, min(t, 256))
    y_ret = _heads_major(y_ret).reshape(b, t, BRANCH_WIDTH)

    fox_k, fox_v = wide('fox_k'), wide('fox_v')
    fox_lf = jax.nn.log_sigmoid(fox_f + w['b_forget'])
    cum = jnp.cumsum(with_past(fox_lf, fox_lf0), axis=1)
    n_terms = 3
    spare = LANES - HEAD_DIM - n_terms
    kx = jnp.concatenate([heads(with_past(fox_k, fox_k0)).astype(BF)]
                         + [term[..., None] for term in _bf16_terms(-cum, n_terms)]
                         + [jnp.zeros((b, n_keys, N_HEADS, spare), BF)], axis=-1)
    qx = jnp.concatenate([heads(wide('fox_q')).astype(BF), jnp.ones((b, t, N_HEADS, n_terms), BF),
                          jnp.zeros((b, t, N_HEADS, spare), BF)], axis=-1)
    y_fox = _fox_attention(_swap(qx.reshape(b, t, N_HEADS * LANES)),
                           _pad_keys(kx.reshape(b, n_keys, N_HEADS * LANES), lp['fox']),
                           values_t(fox_v, fox_v0, lp['fox']), p_len, tq, tk['fox'])

    top_k = min(DSA_TOP_K, n_keys // 4)
    y_dsa = _dsa_attention(queries_t('dsa_q'), queries_t('idx_q'), _swap(idx_w),
                           keys(dsa_k, dsa_k0, lp['dsa'], HEADS_PER_COL), values_t(dsa_v, dsa_v0, lp['dsa']),
                           keys(idx_k, dsa_ki0, lp['dsa'], HEADS_PER_COL),
                           p_len, n_keys, tq, tk['dsa'], top_k)

    ys = [_swap(y_sb).reshape(m, BRANCH_WIDTH), y_ret.reshape(m, BRANCH_WIDTH),
          _swap(y_fox).reshape(m, BRANCH_WIDTH), _swap(y_dsa).reshape(m, BRANCH_WIDTH)]
    h = _merge(h, ys, w['w_gate'], w['w_branch'], w['w_out'], w['ln1_g'], w['ln1_b'], alpha)
    h = _ffn(h, w['w_ffn_in'], w['w_ffn_out'], w['ln2_g'], w['ln2_b'], alpha)
    return h, (heads(sb_k), heads(sb_v), ret_state_new, heads(fox_k), heads(fox_v), fox_lf, dsa_k, dsa_v, idx_k)


def kernel(x_prompt, x_sample, cache_sb_k, cache_sb_v, state_ret, cache_fox_k, cache_fox_v, cache_fox_logf,
           cache_dsa_k, cache_dsa_v, cache_dsa_kidx, w_in, b_forget, w_branch, w_out, ln1_g, ln1_b,
           w_ffn_in, w_ffn_out, ln2_g, ln2_b):
    depth = w_in.shape[0]
    alpha = float((2 * depth) ** 0.25)
    bp, tp, d = x_prompt.shape
    bs, ts, _ = x_sample.shape
    hp = x_prompt.reshape(bp * tp, d)
    hs = x_sample.reshape(bs * ts, d)
    ret_zero = jnp.zeros((bp, N_HEADS, HEAD_DIM, HEAD_DIM), F32)
    st_p, st_s = [], []
    for l in range(depth):
        w_mix, w_gate = _split_w_in(w_in[l])
        w = dict(w_mix=w_mix, w_gate=w_gate, b_forget=b_forget[l], w_branch=w_branch[l].astype(BF),
                 w_out=w_out[l].astype(BF), ln1_g=ln1_g[l][None], ln1_b=ln1_b[l][None],
                 w_ffn_in=w_ffn_in[l].astype(BF), w_ffn_out=w_ffn_out[l].astype(BF),
                 ln2_g=ln2_g[l][None], ln2_b=ln2_b[l][None])
        hp, sp = _layer(hp, bp, tp, None, ret_zero, w, alpha)
        past = (cache_sb_k[l], cache_sb_v[l], cache_fox_k[l], cache_fox_v[l], cache_fox_logf[l],
                cache_dsa_k[l], cache_dsa_v[l], cache_dsa_kidx[l])
        hs, ss = _layer(hs, bs, ts, past, state_ret[l], w, alpha)
        st_p.append(sp)
        st_s.append(ss)

    def stacked(states, i):
        return jnp.stack([s[i] for s in states])

    return ((hp.reshape(bp, tp, d), hs.reshape(bs, ts, d))
            + tuple(stacked(st_p, i) for i in range(9)) + tuple(stacked(st_s, i) for i in range(9)))
```

```python
import functools

import numpy as np
import jax
import jax.numpy as jnp
from jax import lax
from jax.experimental import pallas as pl
from jax.experimental.pallas import tpu as pltpu

HEAD_DIM = 64
N_HEADS = 4
BRANCH_WIDTH = N_HEADS * HEAD_DIM
CHUNK_SHIFT = 6
DSA_TOP_K = 256
ROPE_BASE = 10000.0
LN_EPS = 1e-5
QK_SCALE = HEAD_DIM ** -0.5
IDX_HEAD_SCALE = N_HEADS ** -0.5
MASK_VALUE = -1e30
INT_MIN = -2 ** 31

V7X_VMEM_LIMIT_BYTES = 56 * 1024 * 1024
LANES = 128
HEADS_PER_COL = LANES // HEAD_DIM
COUNT_SLAB = 64

BF = jnp.bfloat16
F32 = jnp.float32


def _dot(a, b):
    return jnp.dot(a, b, preferred_element_type=F32)


def _dot_nt(a, b):
    return lax.dot_general(a, b, (((1,), (1,)), ((), ())), preferred_element_type=F32)


def _dot_tn(a, b):
    return lax.dot_general(a, b, (((0,), (0,)), ((), ())), preferred_element_type=F32)


def _params(*sem):
    return pltpu.CompilerParams(dimension_semantics=sem, vmem_limit_bytes=V7X_VMEM_LIMIT_BYTES)


def _const_spec(shape):
    nd = len(shape)
    return pl.BlockSpec(shape, lambda *_: (0,) * nd)


def _layer_norm(x, g, b):
    xc = x - jnp.mean(x, axis=-1, keepdims=True)
    var = jnp.mean(xc * xc, axis=-1, keepdims=True)
    return xc * lax.rsqrt(var + LN_EPS) * g + b


def _row_tile(m, want):
    t = min(m, want)
    assert m % t == 0
    return t


def _col(c):
    return slice(c * LANES, (c + 1) * LANES)


_ROW_OUTS = (
    ('sb_k', ('sb_k',), F32), ('sb_k_bf', ('sb_k',), BF), ('sb_v', ('sb_v',), F32),
    ('fox_k', ('fox_k',), F32), ('fox_k_bf', ('fox_k',), BF), ('fox_v', ('fox_v',), F32),
    ('narrow', ('dsa_k', 'dsa_v', 'idx_k', 'fox_f', 'idx_w'), F32),
    ('dsa_kk_bf', ('dsa_k', 'dsa_k', 'idx_k', 'idx_k'), BF),
)
_COL_OUTS = (
    ('sb_q_t', ('sb_q',), BF), ('fox_q_t', ('fox_q',), BF), ('dsa_q_t', ('dsa_q',), BF),
    ('idx_q_t', ('idx_q',), BF), ('sb_v_t', ('sb_v',), BF), ('fox_v_t', ('fox_v',), BF),
    ('ret_q_t', ('ret_q',), F32), ('ret_k_t', ('ret_k',), F32), ('ret_v_t', ('ret_v',), F32),
    ('ret_g_t', ('ret_g',), F32), ('dsa_v_t', ('dsa_v',), BF), ('idx_w_t', ('idx_w',), F32),
)
BF16_ROWS_PER_VREG = 16


def _inproj_plan(d):
    widths = dict(_in_layout(d))

    def spans(outs, multiple):
        plan, span_of, off = [], {}, 0
        for name, srcs, dtype in outs:
            if srcs not in span_of:
                w = -(-sum(widths[s] for s in srcs) // multiple) * multiple
                span_of[srcs] = (off, w)
                off += w
            plan.append((name, srcs, dtype) + span_of[srcs])
        return plan, span_of, off

    rows, row_spans, _ = spans(_ROW_OUTS, LANES)
    cols, col_spans, n_col = spans(_COL_OUTS, BF16_ROWS_PER_VREG)
    return rows, cols, row_spans, col_spans, -(-n_col // LANES) * LANES


def _inproj_weights(w_in):
    d = w_in.shape[0]
    pieces, off = {}, 0
    for name, width in _in_layout(d):
        pieces[name] = w_in[:, off:off + width] * _FOLDED_SCALE.get(name, 1.0)
        off += width
    assert off == w_in.shape[1]
    _, _, row_spans, col_spans, n_col = _inproj_plan(d)

    def block(srcs, width):
        w = jnp.concatenate([pieces[s] for s in srcs], axis=1)
        return jnp.pad(w, ((0, 0), (0, width - w.shape[1])))

    w_row = jnp.concatenate([block(srcs, w) for srcs, (_, w) in row_spans.items()], axis=1)
    w_col = jnp.concatenate([block(srcs, w) for srcs, (_, w) in col_spans.items()], axis=1)
    w_col = jnp.pad(w_col, ((0, 0), (0, n_col - w_col.shape[1])))
    return w_row.astype(BF), w_col.T.astype(BF), pieces['merge_gate'].astype(BF)


def _inproj_body(x_ref, wr_ref, wc_ref, *o_refs, rows, cols):
    xb = x_ref[0].astype(BF)
    done = {}
    for i, (_, srcs, _, off, w) in enumerate(rows):
        if srcs not in done:
            done[srcs] = _dot(xb, wr_ref[:, off:off + w])
        o_refs[i][0] = done[srcs].astype(o_refs[i].dtype)
    for i, (_, _, _, off, w) in enumerate(cols):
        o_ref = o_refs[len(rows) + i]
        o_ref[0] = _dot_nt(wc_ref[off:off + w, :], xb).astype(o_ref.dtype)


def _inproj(x, w_row, w_col):
    b, t, d = x.shape
    tm = _row_tile(t, 512)
    rows, cols, _, _, _ = _inproj_plan(d)
    out_shape = ([jax.ShapeDtypeStruct((b, t, w), dt) for _, _, dt, _, w in rows]
                 + [jax.ShapeDtypeStruct((b, w, t), dt) for _, _, dt, _, w in cols])
    out_specs = ([pl.BlockSpec((1, tm, w), lambda bi, i: (bi, i, 0)) for _, _, _, _, w in rows]
                 + [pl.BlockSpec((1, w, tm), lambda bi, i: (bi, 0, i)) for _, _, _, _, w in cols])
    outs = pl.pallas_call(
        functools.partial(_inproj_body, rows=rows, cols=cols),
        out_shape=out_shape,
        grid=(b, t // tm),
        in_specs=[pl.BlockSpec((1, tm, d), lambda bi, i: (bi, i, 0)),
                  _const_spec(w_row.shape), _const_spec(w_col.shape)],
        out_specs=out_specs,
        compiler_params=_params("parallel", "parallel"),
        name="inproj",
    )(x, w_row, w_col)
    return dict(zip([r[0] for r in rows] + [c[0] for c in cols], outs))


def _block_counts(q0, tk, last_key):
    return lax.div(q0, tk), lax.div(last_key, tk) + 1


def _head_queries(qt_ref):
    low = lax.broadcasted_iota(jnp.int32, (LANES, 1), 0) < HEAD_DIM
    out = []
    for h in range(N_HEADS):
        qc = qt_ref[0, _col(h // HEADS_PER_COL), :]
        keep = low if h % HEADS_PER_COL == 0 else jnp.logical_not(low)
        out.append(jnp.where(keep, qc, jnp.zeros_like(qc)))
    return out


def _head_rows(h):
    return slice(h * HEAD_DIM, (h + 1) * HEAD_DIM)


def _key_minus_query(tk, tq):
    return (lax.broadcasted_iota(jnp.int32, (tk, tq), 0) - lax.broadcasted_iota(jnp.int32, (tk, tq), 1))


def _qt_spec(w, tq):
    return pl.BlockSpec((1, w, tq), lambda bi, qi: (bi, 0, qi))


def _whole_spec(rows, cols):
    return pl.BlockSpec((1, rows, cols), lambda bi, qi: (bi, 0, 0))


def _sb_body(qt_ref, k_ref, vt_ref, o_ref, *, p_len, tq, tk):
    q0 = p_len + pl.program_id(1) * tq
    qm = _head_queries(qt_ref)
    diff = _key_minus_query(tk, tq)
    later = (lax.broadcasted_iota(jnp.int32, (tk, 2 * tk), 1) & (tk - 1)) > lax.broadcasted_iota(
        jnp.int32, (tk, 2 * tk), 0)
    minus_later = jnp.where(later, -1.0, 0.0).astype(BF)
    n_full, n_all = _block_counts(q0, tk, jnp.maximum(q0 + tq - 2, 0))

    def step(kb, carry, masked):
        laters, accs = carry
        s0 = pl.multiple_of(kb * tk, tk)
        if masked:
            earlier = diff < (q0 - s0)
        zs = [_dot(k_ref[0, pl.ds(s0, tk), _col(h // HEADS_PER_COL)], qm[h]) for h in range(N_HEADS)]
        new_laters, log_bs, afters = [], [], []
        for h in range(N_HEADS):
            z = zs[h]
            minus_abs = pltpu.bitcast(pltpu.bitcast(z, jnp.int32) | jnp.int32(INT_MIN), F32)
            softplus = jnp.maximum(z, 0.0) + jnp.log(1.0 + jnp.exp(minus_abs))
            log_bs.append(z - softplus)
            if masked:
                softplus = jnp.where(earlier, softplus, 0.0)
            hi = softplus.astype(BF)
            lo = (softplus - hi.astype(F32)).astype(BF)
            afters.append(_dot(minus_later, jnp.concatenate([hi, lo], axis=0)) + laters[h])
            new_laters.append(laters[h] - jnp.sum(softplus, axis=0, keepdims=True))
        new_accs = []
        for h in range(N_HEADS):
            w = jnp.exp(log_bs[h] + afters[h])
            if masked:
                w = jnp.where(earlier, w, 0.0)
            new_accs.append(accs[h] + _dot(vt_ref[0, _head_rows(h), pl.ds(s0, tk)], w.astype(BF)))
        return tuple(new_laters), tuple(new_accs)

    carry = (tuple(jnp.zeros((1, tq), F32) for _ in range(N_HEADS)),
             tuple(jnp.zeros((HEAD_DIM, tq), F32) for _ in range(N_HEADS)))
    carry = lax.fori_loop(0, n_all - n_full, lambda i, c: step(n_all - 1 - i, c, True), carry)
    carry = lax.fori_loop(0, n_full, lambda i, c: step(n_full - 1 - i, c, False), carry)
    for h in range(N_HEADS):
        o_ref[0, _head_rows(h), :] = carry[1][h].astype(o_ref.dtype)


def _sb_attention(qt, k, vt, p_len, tq, tk):
    b, w, t = qt.shape
    lp = k.shape[1]
    assert tk & (tk - 1) == 0
    return pl.pallas_call(
        functools.partial(_sb_body, p_len=p_len, tq=tq, tk=tk),
        out_shape=jax.ShapeDtypeStruct((b, w, t), BF),
        grid=(b, t // tq),
        in_specs=[_qt_spec(w, tq), _whole_spec(lp, w), _whole_spec(w, lp)],
        out_specs=_qt_spec(w, tq),
        compiler_params=_params("parallel", "arbitrary"),
        name="sb_attention",
    )(qt, k, vt)


def _online_softmax_step(logits, m, l):
    m_new = jnp.maximum(m, jnp.max(logits, axis=0, keepdims=True))
    alpha = jnp.exp(m - m_new)
    p = jnp.exp(logits - m_new)
    return m_new, alpha, alpha * l + jnp.sum(p, axis=0, keepdims=True), p


def _fox_body(qt_ref, k_ref, vt_ref, c_ref, o_ref, *, p_len, tq, tk):
    q0 = p_len + pl.program_id(1) * tq
    qm = _head_queries(qt_ref)
    diff = _key_minus_query(tk, tq)
    n_full, n_all = _block_counts(q0, tk, q0 + tq - 1)

    def key_bias(h, s0):
        c = c_ref[0, h, pl.ds(s0, tk), :]
        return c[:, :tq] if tq <= LANES else jnp.concatenate([c] * (tq // LANES), axis=1)

    def step(kb, carry, masked):
        ms, ls, accs = carry
        s0 = pl.multiple_of(kb * tk, tk)
        if masked:
            visible = diff <= (q0 - s0)
        new_ms, new_ls, new_accs = [], [], []
        raw = [_dot(k_ref[0, pl.ds(s0, tk), _col(h // HEADS_PER_COL)], qm[h]) for h in range(N_HEADS)]
        for h in range(N_HEADS):
            logits = raw[h] - key_bias(h, s0)
            if masked:
                logits = jnp.where(visible, logits, MASK_VALUE)
            m_new, alpha, l_new, p = _online_softmax_step(logits, ms[h], ls[h])
            new_ms.append(m_new)
            new_ls.append(l_new)
            new_accs.append(alpha * accs[h] + _dot(vt_ref[0, _head_rows(h), pl.ds(s0, tk)], p.astype(BF)))
        return tuple(new_ms), tuple(new_ls), tuple(new_accs)

    carry = (tuple(jnp.full((1, tq), -jnp.inf, F32) for _ in range(N_HEADS)),
             tuple(jnp.zeros((1, tq), F32) for _ in range(N_HEADS)),
             tuple(jnp.zeros((HEAD_DIM, tq), F32) for _ in range(N_HEADS)))
    carry = lax.fori_loop(0, n_full, lambda i, c: step(i, c, False), carry)
    carry = lax.fori_loop(n_full, n_all, lambda i, c: step(i, c, True), carry)
    _, ls, accs = carry
    for h in range(N_HEADS):
        o_ref[0, _head_rows(h), :] = (accs[h] / ls[h]).astype(o_ref.dtype)


def _fox_attention(qt, k, vt, c_lanes, p_len, tq, tk):
    b, w, t = qt.shape
    lp = k.shape[1]
    assert tq <= LANES or tq % LANES == 0
    return pl.pallas_call(
        functools.partial(_fox_body, p_len=p_len, tq=tq, tk=tk),
        out_shape=jax.ShapeDtypeStruct((b, w, t), BF),
        grid=(b, t // tq),
        in_specs=[_qt_spec(w, tq), _whole_spec(lp, w), _whole_spec(w, lp),
                  pl.BlockSpec((1, N_HEADS, lp, LANES), lambda bi, qi: (bi, 0, 0, 0))],
        out_specs=_qt_spec(w, tq),
        compiler_params=_params("parallel", "arbitrary"),
        name="fox_attention",
    )(qt, k, vt, c_lanes)


def _dsa_body(qt_ref, qit_ref, wit_ref, k_ref, vt_ref, ki_ref, o_ref, keys_ref, *, p_len, n_keys, tq, tk, top_k):
    q0 = p_len + pl.program_id(1) * tq
    qpos = q0 + lax.broadcasted_iota(jnp.int32, (1, tq), 1)
    limit = jnp.minimum(((qpos >> CHUNK_SHIFT) + 1) << CHUNK_SHIFT, n_keys)
    last_limit = jnp.minimum((((q0 + tq - 1) >> CHUNK_SHIFT) + 1) << CHUNK_SHIFT, n_keys)
    n_blk = lax.div(last_limit - 1, tk) + 1
    key_row = lax.broadcasted_iota(jnp.int32, (tk, tq), 0)

    wit = wit_ref[0]
    qim = _head_queries(qit_ref)

    def score_step(kb, _):
        s0 = pl.multiple_of(kb * tk, tk)
        ki = ki_ref[0, pl.ds(s0, tk), :]
        score = jnp.zeros((tk, tq), F32)
        for h in range(N_HEADS):
            score = score + wit[h:h + 1, :] * jnp.maximum(_dot(ki, qim[h]), 0.0)
        score = jnp.where(key_row < limit - s0, score, -jnp.inf)
        bits = pltpu.bitcast(score, jnp.int32)
        keys_ref[pl.ds(s0, tk), :] = jnp.where(bits < 0, jnp.int32(INT_MIN) - bits, bits)
        return 0

    lax.fori_loop(0, n_blk, score_step, 0)

    def count(pred):
        def body(kb, acc):
            for r in range(0, tk, COUNT_SLAB):
                s0 = pl.multiple_of(kb * tk + r, COUNT_SLAB)
                acc = acc + jnp.where(pred(keys_ref[pl.ds(s0, COUNT_SLAB), :]), 1.0, 0.0)
            return acc
        acc = lax.fori_loop(0, n_blk, body, jnp.zeros((COUNT_SLAB, tq), F32))
        return jnp.sum(acc, axis=0, keepdims=True)

    kf = jnp.float32(top_k)
    zero = jnp.zeros((1, tq), jnp.int32)
    thr = jnp.where(count(lambda key: key >= zero) >= kf, zero, jnp.int32(INT_MIN))

    def bit_step(it, thr):
        cand = thr + lax.shift_left(jnp.int32(1), 30 - it)
        return jnp.where(count(lambda key: key >= cand) >= kf, cand, thr)

    thr = lax.fori_loop(0, 31, bit_step, thr)
    n_tie_wanted = kf - count(lambda key: key > thr)

    earlier_keys = (lax.broadcasted_iota(jnp.int32, (tk, tk), 1)
                    < lax.broadcasted_iota(jnp.int32, (tk, tk), 0)).astype(BF)
    qm = _head_queries(qt_ref)

    def attend_step(kb, carry):
        ties_seen, ms, ls, accs = carry
        s0 = pl.multiple_of(kb * tk, tk)
        key = keys_ref[pl.ds(s0, tk), :]
        tie = jnp.where(key == thr, 1.0, 0.0)
        tie_rank = _dot(earlier_keys, tie.astype(BF)) + ties_seen
        take = jnp.where(key > thr, 1.0, jnp.where(tie_rank < n_tie_wanted, tie, 0.0))
        selected = jnp.where(key_row < limit - s0, take, 0.0) > 0.0
        k = k_ref[0, pl.ds(s0, tk), :]
        vt = vt_ref[0, :, pl.ds(s0, tk)]
        new_ms, new_ls, new_accs = [], [], []
        raw = [_dot(k, qm[h]) for h in range(N_HEADS)]
        for h in range(N_HEADS):
            logits = jnp.where(selected, raw[h], MASK_VALUE)
            m_new, alpha, l_new, p = _online_softmax_step(logits, ms[h], ls[h])
            new_ms.append(m_new)
            new_ls.append(l_new)
            new_accs.append(alpha * accs[h] + _dot(vt, p.astype(BF)))
        ties_seen = ties_seen + jnp.sum(tie, axis=0, keepdims=True)
        return ties_seen, tuple(new_ms), tuple(new_ls), tuple(new_accs)

    carry = (jnp.zeros((1, tq), F32),
             tuple(jnp.full((1, tq), -jnp.inf, F32) for _ in range(N_HEADS)),
             tuple(jnp.zeros((1, tq), F32) for _ in range(N_HEADS)),
             tuple(jnp.zeros((HEAD_DIM, tq), F32) for _ in range(N_HEADS)))
    _, _, ls, accs = lax.fori_loop(0, n_blk, attend_step, carry)
    for h in range(N_HEADS):
        o_ref[0, _head_rows(h), :] = (accs[h] / ls[h]).astype(o_ref.dtype)


def _dsa_attention(qt, qit, wit, kk, vt, p_len, n_keys, tq, tk, top_k):
    b, w, t = qt.shape
    lp = kk.shape[1]
    return pl.pallas_call(
        functools.partial(_dsa_body, p_len=p_len, n_keys=n_keys, tq=tq, tk=tk, top_k=top_k),
        out_shape=jax.ShapeDtypeStruct((b, w, t), BF),
        grid=(b, t // tq),
        in_specs=[_qt_spec(w, tq), _qt_spec(w, tq), _qt_spec(wit.shape[1], tq),
                  pl.BlockSpec((1, lp, LANES), lambda bi, qi: (bi, 0, 0)), _whole_spec(HEAD_DIM, lp),
                  pl.BlockSpec((1, lp, LANES), lambda bi, qi: (bi, 0, 1))],
        out_specs=_qt_spec(w, tq),
        scratch_shapes=[pltpu.VMEM((lp, tq), jnp.int32)],
        compiler_params=_params("parallel", "arbitrary"),
        name="dsa_attention",
    )(qt, qit, wit, kk, vt, kk)


def _ret_body(q_ref, k_ref, v_ref, g_ref, s0_ref, cos_ref, sin_ref, dec_ref, qd_ref, kd_ref, sd_ref,
              o_ref, so_ref, state_ref):
    c = pl.program_id(2)

    @pl.when(c == 0)
    def _():
        state_ref[...] = s0_ref[0, 0]

    cos, sin = cos_ref[...], sin_ref[...]
    half = HEAD_DIM // 2

    def rotary(x):
        x1, x2 = x[:half], x[half:]
        return jnp.concatenate([x1 * cos - x2 * sin, x2 * cos + x1 * sin], axis=0)

    qb = rotary(q_ref[0]).astype(BF)
    k = rotary(k_ref[0])
    vb = v_ref[0].astype(BF)
    state = state_ref[...]
    scores_t = _dot_tn(k.astype(BF), qb) * dec_ref[0]
    o = _dot(vb, scores_t.astype(BF)) + _dot(state.astype(BF), qb) * qd_ref[0]
    state = sd_ref[0] * state + _dot_nt(vb, (k * kd_ref[0]).astype(BF))
    state_ref[...] = state
    oc = o - jnp.mean(o, axis=0, keepdims=True)
    on = oc * lax.rsqrt(jnp.mean(oc * oc, axis=0, keepdims=True) + LN_EPS)
    g = g_ref[0]
    o_ref[0] = (on * (g * jax.nn.sigmoid(g))).astype(o_ref.dtype)

    @pl.when(c == pl.num_programs(2) - 1)
    def _():
        so_ref[0, 0] = state


def _retention(qt, kt, vt, gt, state0_t, pos, c):
    b, w, t = qt.shape
    h = w // HEAD_DIM
    half = HEAD_DIM // 2
    inv_freq = ROPE_BASE ** (-jnp.arange(half, dtype=F32) / half)
    ang = inv_freq[:, None] * pos.astype(F32)[None, :]
    log_gamma = np.log(1.0 - 2.0 ** (-5.0 - np.arange(h, dtype=np.float64)))
    n = np.arange(c, dtype=np.float64)
    rel = n[None, :] - n[:, None]
    decay_t = np.where(rel >= 0, np.exp(np.maximum(rel, 0.0)[None] * log_gamma[:, None, None]), 0.0)
    q_decay = np.exp((n[None, :] + 1.0) * log_gamma[:, None])[:, None, :]
    k_decay = np.exp((c - 1.0 - n)[None, :] * log_gamma[:, None])[:, None, :]
    s_decay = np.exp(c * log_gamma)[:, None, None]
    tables = [jnp.asarray(a, F32) for a in (decay_t, q_decay, k_decay, s_decay)]
    x_spec = pl.BlockSpec((1, HEAD_DIM, c), lambda bi, hi, ci: (bi, hi, ci))
    s_spec = pl.BlockSpec((1, 1, HEAD_DIM, HEAD_DIM), lambda bi, hi, ci: (bi, hi, 0, 0))
    rope_spec = pl.BlockSpec((half, c), lambda bi, hi, ci: (0, ci))

    def t_spec(a):
        return pl.BlockSpec((1,) + a.shape[1:], lambda bi, hi, ci: (hi, 0, 0))

    return pl.pallas_call(
        _ret_body,
        out_shape=(jax.ShapeDtypeStruct((b, w, t), BF),
                   jax.ShapeDtypeStruct((b, h, HEAD_DIM, HEAD_DIM), F32)),
        grid=(b, h, t // c),
        in_specs=[x_spec, x_spec, x_spec, x_spec, s_spec, rope_spec, rope_spec] + [t_spec(a) for a in tables],
        out_specs=(x_spec, s_spec),
        scratch_shapes=[pltpu.VMEM((HEAD_DIM, HEAD_DIM), F32)],
        compiler_params=_params("parallel", "parallel", "arbitrary"),
        name="retention",
    )(qt, kt, vt, gt, state0_t, jnp.cos(ang), jnp.sin(ang), *tables)


def _merge_body(h_ref, y0_ref, y1_ref, y2_ref, y3_ref, wg_ref, wb_ref, wo_ref, g_ref, b_ref, o_ref, *, alpha):
    h = h_ref[...]
    hb = h.astype(BF)
    d = h.shape[1]
    merged = jnp.zeros(h.shape, F32)
    for i, y_ref in enumerate((y0_ref, y1_ref, y2_ref, y3_ref)):
        gate = jax.nn.sigmoid(_dot(hb, wg_ref[:, i * d:(i + 1) * d]))
        merged = merged + gate * _dot(y_ref[...], wb_ref[i])
    r = alpha * h + _dot(merged.astype(BF), wo_ref[...])
    o_ref[...] = _layer_norm(r, g_ref[...], b_ref[...])


def _merge(h, ys, w_gate, w_branch, w_out, ln_g, ln_b, alpha):
    m, d = h.shape
    tm = _row_tile(m, 256)
    row = lambda w: pl.BlockSpec((tm, w), lambda i: (i, 0))
    return pl.pallas_call(
        functools.partial(_merge_body, alpha=alpha),
        out_shape=jax.ShapeDtypeStruct((m, d), F32),
        grid=(m // tm,),
        in_specs=[row(d)] + [row(BRANCH_WIDTH)] * 4
                 + [_const_spec(w_gate.shape), _const_spec(w_branch.shape), _const_spec(w_out.shape),
                    _const_spec((1, d)), _const_spec((1, d))],
        out_specs=row(d),
        compiler_params=_params("parallel"),
        name="merge",
    )(h, *ys, w_gate, w_branch, w_out, ln_g, ln_b)


def _ffn_body(h_ref, wi_ref, wo_ref, g_ref, b_ref, o_ref, *, alpha, f_chunk):
    h = h_ref[...]
    hb = h.astype(BF)
    f = wo_ref.shape[0]
    acc = jnp.zeros(h.shape, F32)
    for c in range(0, f, f_chunk):
        a = _dot(hb, wi_ref[:, c:c + f_chunk])
        u = _dot(hb, wi_ref[:, f + c:f + c + f_chunk])
        acc = acc + _dot((a * jax.nn.sigmoid(a) * u).astype(BF), wo_ref[c:c + f_chunk, :])
    o_ref[...] = _layer_norm(alpha * h + acc, g_ref[...], b_ref[...])


def _ffn(h, w_in, w_out, ln_g, ln_b, alpha):
    m, d = h.shape
    f = w_out.shape[0]
    tm = _row_tile(m, 256)
    f_chunk = f // 2 if (f // 2) % LANES == 0 else f
    row = pl.BlockSpec((tm, d), lambda i: (i, 0))
    return pl.pallas_call(
        functools.partial(_ffn_body, alpha=alpha, f_chunk=f_chunk),
        out_shape=jax.ShapeDtypeStruct((m, d), F32),
        grid=(m // tm,),
        in_specs=[row, _const_spec(w_in.shape), _const_spec(w_out.shape),
                  _const_spec((1, d)), _const_spec((1, d))],
        out_specs=row,
        compiler_params=_params("parallel"),
        name="ffn",
    )(h, w_in, w_out, ln_g, ln_b)


def _in_layout(d):
    w = BRANCH_WIDTH
    return (('sb_q', w), ('sb_k', w), ('sb_v', w), ('ret_q', w), ('ret_k', w), ('ret_v', w), ('ret_g', w),
            ('fox_q', w), ('fox_k', w), ('fox_v', w), ('fox_f', N_HEADS),
            ('dsa_q', w), ('dsa_k', HEAD_DIM), ('dsa_v', HEAD_DIM),
            ('idx_q', w), ('idx_k', HEAD_DIM), ('idx_w', N_HEADS), ('merge_gate', 4 * d))


_FOLDED_SCALE = dict(sb_q=QK_SCALE, fox_q=QK_SCALE, dsa_q=QK_SCALE, idx_q=QK_SCALE, ret_k=QK_SCALE,
                     idx_w=IDX_HEAD_SCALE)


def _swap(a):
    return jnp.swapaxes(a, -1, -2)


def _key_tiles(t, n_keys):
    tq = min(t, 256)
    tiles = dict(sb=256, fox=512, dsa=512)
    padded = {name: -(-n_keys // tk) * tk for name, tk in tiles.items()}
    return tq, tiles, padded


def _layer(h, b, t, past, ret_state, w, alpha):
    m, d = h.shape
    p_len = 0 if past is None else past[0].shape[1]
    n_keys = p_len + t
    tq, tk, lp = _key_tiles(t, n_keys)
    p = _inproj(h.reshape(b, t, d), w['w_row'], w['w_col'])

    def heads(a):
        return a.reshape(a.shape[0], a.shape[1], N_HEADS, HEAD_DIM)

    narrow = p['narrow']
    dsa_k, dsa_v, idx_k = (narrow[..., i * HEAD_DIM:(i + 1) * HEAD_DIM] for i in range(3))
    fox_f = narrow[..., 3 * HEAD_DIM:3 * HEAD_DIM + N_HEADS]

    old = (None,) * 8 if past is None else past
    sb_k0, sb_v0, fox_k0, fox_v0, fox_lf0, dsa_k0, dsa_v0, dsa_ki0 = old

    def rows_with_past(new_bf, olds, lp_):
        if past is not None:
            flat = [o.reshape(o.shape[0], o.shape[1], -1).astype(BF) for o in olds]
            new_bf = jnp.concatenate([jnp.concatenate(flat, axis=2), new_bf], axis=1)
        return jnp.pad(new_bf, ((0, 0), (0, lp_ - new_bf.shape[1]), (0, 0)))

    def cols_with_past(new_t, old, lp_):
        if old is not None:
            new_t = jnp.concatenate([_swap(old.reshape(old.shape[0], old.shape[1], -1).astype(BF)), new_t], axis=2)
        return jnp.pad(new_t, ((0, 0), (0, 0), (0, lp_ - new_t.shape[2])))

    y_sb = _sb_attention(p['sb_q_t'], rows_with_past(p['sb_k_bf'], [sb_k0], lp['sb']),
                         cols_with_past(p['sb_v_t'], sb_v0, lp['sb']), p_len, tq, tk['sb'])

    pos = p_len + jnp.arange(t, dtype=jnp.int32)
    y_ret, ret_state_t = _retention(p['ret_q_t'], p['ret_k_t'], p['ret_v_t'], p['ret_g_t'], _swap(ret_state),
                                    pos, min(t, 256))

    fox_lf = jax.nn.log_sigmoid(fox_f + w['b_forget'])
    lf_all = fox_lf if fox_lf0 is None else jnp.concatenate([fox_lf0, fox_lf], axis=1)
    cum = jnp.pad(jnp.cumsum(lf_all, axis=1), ((0, 0), (0, lp['fox'] - n_keys), (0, 0)))
    c_lanes = jnp.broadcast_to(_swap(cum)[..., None], (b, N_HEADS, lp['fox'], LANES))
    y_fox = _fox_attention(p['fox_q_t'], rows_with_past(p['fox_k_bf'], [fox_k0], lp['fox']),
                           cols_with_past(p['fox_v_t'], fox_v0, lp['fox']), c_lanes, p_len, tq, tk['fox'])

    top_k = min(DSA_TOP_K, n_keys // 4)
    kk_old = [dsa_k0, dsa_k0, dsa_ki0, dsa_ki0]
    y_dsa = _dsa_attention(p['dsa_q_t'], p['idx_q_t'], p['idx_w_t'], rows_with_past(p['dsa_kk_bf'], kk_old, lp['dsa']),
                           cols_with_past(p['dsa_v_t'], dsa_v0, lp['dsa']), p_len, n_keys, tq, tk['dsa'], top_k)

    ys = [_swap(y).reshape(m, BRANCH_WIDTH) for y in (y_sb, y_ret, y_fox, y_dsa)]
    h = _merge(h, ys, w['w_gate'], w['w_branch'], w['w_out'], w['ln1_g'], w['ln1_b'], alpha)
    h = _ffn(h, w['w_ffn_in'], w['w_ffn_out'], w['ln2_g'], w['ln2_b'], alpha)
    return h, (heads(p['sb_k']), heads(p['sb_v']), _swap(ret_state_t), heads(p['fox_k']), heads(p['fox_v']),
               fox_lf, dsa_k, dsa_v, idx_k)


def kernel(x_prompt, x_sample, cache_sb_k, cache_sb_v, state_ret, cache_fox_k, cache_fox_v, cache_fox_logf,
           cache_dsa_k, cache_dsa_v, cache_dsa_kidx, w_in, b_forget, w_branch, w_out, ln1_g, ln1_b,
           w_ffn_in, w_ffn_out, ln2_g, ln2_b):
    depth = w_in.shape[0]
    alpha = float((2 * depth) ** 0.25)
    bp, tp, d = x_prompt.shape
    bs, ts, _ = x_sample.shape
    hp = x_prompt.reshape(bp * tp, d)
    hs = x_sample.reshape(bs * ts, d)
    ret_zero = jnp.zeros((bp, N_HEADS, HEAD_DIM, HEAD_DIM), F32)
    st_p, st_s = [], []
    for l in range(depth):
        w_row, w_col, w_gate = _inproj_weights(w_in[l])
        w = dict(w_row=w_row, w_col=w_col, w_gate=w_gate, b_forget=b_forget[l], w_branch=w_branch[l].astype(BF),
                 w_out=w_out[l].astype(BF), ln1_g=ln1_g[l][None], ln1_b=ln1_b[l][None],
                 w_ffn_in=w_ffn_in[l].astype(BF), w_ffn_out=w_ffn_out[l].astype(BF),
                 ln2_g=ln2_g[l][None], ln2_b=ln2_b[l][None])
        hp, sp = _layer(hp, bp, tp, None, ret_zero, w, alpha)
        past = (cache_sb_k[l], cache_sb_v[l], cache_fox_k[l], cache_fox_v[l], cache_fox_logf[l],
                cache_dsa_k[l], cache_dsa_v[l], cache_dsa_kidx[l])
        hs, ss = _layer(hs, bs, ts, past, state_ret[l], w, alpha)
        st_p.append(sp)
        st_s.append(ss)

    def stacked(states, i):
        return jnp.stack([s[i] for s in states])

    return ((hp.reshape(bp, tp, d), hs.reshape(bs, ts, d))
            + tuple(stacked(st_p, i) for i in range(9)) + tuple(stacked(st_s, i) for i in range(9)))
```

```python
import functools
import math

import numpy as np
import jax
import jax.numpy as jnp
from jax import lax
from jax.experimental import pallas as pl
from jax.experimental.pallas import tpu as pltpu

HEAD_DIM = 64
N_HEADS = 4
BRANCH_WIDTH = N_HEADS * HEAD_DIM
CHUNK_SHIFT = 6
DSA_TOP_K = 256
ROPE_BASE = 10000.0
LN_EPS = 1e-5
QK_SCALE = HEAD_DIM ** -0.5
IDX_HEAD_SCALE = N_HEADS ** -0.5
MASK_VALUE = -1e30
INT_MIN = -2 ** 31
MINUS_INF_KEY = INT_MIN + 0x00800000

V7X_VMEM_LIMIT_BYTES = 56 * 1024 * 1024
LANES = 128
HEADS_PER_COL = LANES // HEAD_DIM
COUNT_SLAB = 64
MERGE_ROWS = 512
FFN_ROWS = 512

BF = jnp.bfloat16
F32 = jnp.float32


def _dot(a, b):
    return jnp.dot(a, b, preferred_element_type=F32)


def _dot_nt(a, b):
    return lax.dot_general(a, b, (((1,), (1,)), ((), ())), preferred_element_type=F32)


def _dot_tn(a, b):
    return lax.dot_general(a, b, (((0,), (0,)), ((), ())), preferred_element_type=F32)


def _params(*sem):
    return pltpu.CompilerParams(dimension_semantics=sem, vmem_limit_bytes=V7X_VMEM_LIMIT_BYTES)


def _const_spec(shape):
    nd = len(shape)
    return pl.BlockSpec(shape, lambda *_: (0,) * nd)


def _layer_norm(x, g, b):
    xc = x - jnp.mean(x, axis=-1, keepdims=True)
    var = jnp.mean(xc * xc, axis=-1, keepdims=True)
    return xc * lax.rsqrt(var + LN_EPS) * g + b


def _row_tile(m, want):
    t = min(m, want)
    assert m % t == 0
    return t


def _col(c):
    return slice(c * LANES, (c + 1) * LANES)


_ROW_OUTS = (
    ('sb_k', ('sb_k',), F32), ('sb_k_bf', ('sb_k',), BF), ('sb_v', ('sb_v',), F32),
    ('fox_k', ('fox_k',), F32), ('fox_k_bf', ('fox_k',), BF), ('fox_v', ('fox_v',), F32),
    ('narrow', ('dsa_k', 'dsa_v', 'idx_k', 'fox_f', 'idx_w'), F32),
    ('dsa_kk_bf', ('dsa_k', 'dsa_k', 'idx_k', 'idx_k'), BF),
)
_COL_OUTS = (
    ('sb_q_t', ('sb_q',), BF), ('fox_q_t', ('fox_q',), BF), ('dsa_q_t', ('dsa_q',), BF),
    ('idx_q_t', ('idx_q',), BF), ('sb_v_t', ('sb_v',), BF), ('fox_v_t', ('fox_v',), BF),
    ('ret_q_t', ('ret_q',), F32), ('ret_k_t', ('ret_k',), F32), ('ret_v_t', ('ret_v',), F32),
    ('ret_g_t', ('ret_g',), F32), ('dsa_v_t', ('dsa_v',), BF), ('idx_w_t', ('idx_w',), F32),
)
BF16_ROWS_PER_VREG = 16


def _inproj_plan(d):
    widths = dict(_in_layout(d))

    def spans(outs, multiple):
        plan, span_of, off = [], {}, 0
        for name, srcs, dtype in outs:
            if srcs not in span_of:
                w = -(-sum(widths[s] for s in srcs) // multiple) * multiple
                span_of[srcs] = (off, w)
                off += w
            plan.append((name, srcs, dtype) + span_of[srcs])
        return plan, span_of, off

    rows, row_spans, _ = spans(_ROW_OUTS, LANES)
    cols, col_spans, n_col = spans(_COL_OUTS, BF16_ROWS_PER_VREG)
    return rows, cols, row_spans, col_spans, -(-n_col // LANES) * LANES


def _inproj_weights(w_in):
    d = w_in.shape[0]
    pieces, off = {}, 0
    for name, width in _in_layout(d):
        pieces[name] = w_in[:, off:off + width] * _FOLDED_SCALE.get(name, 1.0)
        off += width
    assert off == w_in.shape[1]
    _, _, row_spans, col_spans, n_col = _inproj_plan(d)

    def block(srcs, width):
        w = jnp.concatenate([pieces[s] for s in srcs], axis=1)
        return jnp.pad(w, ((0, 0), (0, width - w.shape[1])))

    w_row = jnp.concatenate([block(srcs, w) for srcs, (_, w) in row_spans.items()], axis=1)
    w_col = jnp.concatenate([block(srcs, w) for srcs, (_, w) in col_spans.items()], axis=1)
    w_col = jnp.pad(w_col, ((0, 0), (0, n_col - w_col.shape[1])))
    return w_row.astype(BF), w_col.T.astype(BF), pieces['merge_gate'].astype(BF)


def _inproj_body(x_ref, wr_ref, wc_ref, *o_refs, rows, cols):
    xb = x_ref[0].astype(BF)
    done = {}
    for i, (_, srcs, _, off, w) in enumerate(rows):
        if srcs not in done:
            done[srcs] = _dot(xb, wr_ref[:, off:off + w])
        o_refs[i][0] = done[srcs].astype(o_refs[i].dtype)
    for i, (_, _, _, off, w) in enumerate(cols):
        o_ref = o_refs[len(rows) + i]
        o_ref[0] = _dot_nt(wc_ref[off:off + w, :], xb).astype(o_ref.dtype)


def _inproj(x, w_row, w_col):
    b, t, d = x.shape
    tm = _row_tile(t, 512)
    rows, cols, _, _, _ = _inproj_plan(d)
    out_shape = ([jax.ShapeDtypeStruct((b, t, w), dt) for _, _, dt, _, w in rows]
                 + [jax.ShapeDtypeStruct((b, w, t), dt) for _, _, dt, _, w in cols])
    out_specs = ([pl.BlockSpec((1, tm, w), lambda bi, i: (bi, i, 0)) for _, _, _, _, w in rows]
                 + [pl.BlockSpec((1, w, tm), lambda bi, i: (bi, 0, i)) for _, _, _, _, w in cols])
    outs = pl.pallas_call(
        functools.partial(_inproj_body, rows=rows, cols=cols),
        out_shape=out_shape,
        grid=(b, t // tm),
        in_specs=[pl.BlockSpec((1, tm, d), lambda bi, i: (bi, i, 0)),
                  _const_spec(w_row.shape), _const_spec(w_col.shape)],
        out_specs=out_specs,
        compiler_params=_params("parallel", "parallel"),
        name="inproj",
    )(x, w_row, w_col)
    return dict(zip([r[0] for r in rows] + [c[0] for c in cols], outs))


def _block_counts(q0, tk, last_key):
    return lax.div(q0, tk), lax.div(last_key, tk) + 1


def _head_queries(qt_ref):
    low = lax.broadcasted_iota(jnp.int32, (LANES, 1), 0) < HEAD_DIM
    out = []
    for h in range(N_HEADS):
        qc = qt_ref[0, _col(h // HEADS_PER_COL), :]
        keep = low if h % HEADS_PER_COL == 0 else jnp.logical_not(low)
        out.append(jnp.where(keep, qc, jnp.zeros_like(qc)))
    return out


def _head_rows(h):
    return slice(h * HEAD_DIM, (h + 1) * HEAD_DIM)


def _key_minus_query(tk, tq):
    return (lax.broadcasted_iota(jnp.int32, (tk, tq), 0) - lax.broadcasted_iota(jnp.int32, (tk, tq), 1))


def _qt_spec(w, tq):
    return pl.BlockSpec((1, w, tq), lambda bi, qi: (bi, 0, qi))


def _whole_spec(rows, cols):
    return pl.BlockSpec((1, rows, cols), lambda bi, qi: (bi, 0, 0))


def _sb_body(qt_ref, k_ref, vt_ref, o_ref, *, p_len, tq, tk):
    q0 = p_len + pl.program_id(1) * tq
    qm = _head_queries(qt_ref)
    diff = _key_minus_query(tk, tq)
    later = (lax.broadcasted_iota(jnp.int32, (tk, 2 * tk), 1) & (tk - 1)) > lax.broadcasted_iota(
        jnp.int32, (tk, 2 * tk), 0)
    minus_later = jnp.where(later, -1.0, 0.0).astype(BF)
    n_full, n_all = _block_counts(q0, tk, jnp.maximum(q0 + tq - 2, 0))

    def step(kb, carry, masked):
        laters, accs = carry
        s0 = pl.multiple_of(kb * tk, tk)
        if masked:
            earlier = diff < (q0 - s0)
        zs = [_dot(k_ref[0, pl.ds(s0, tk), _col(h // HEADS_PER_COL)], qm[h]) for h in range(N_HEADS)]
        new_laters, log_bs, afters = [], [], []
        for h in range(N_HEADS):
            z = zs[h]
            minus_abs = pltpu.bitcast(pltpu.bitcast(z, jnp.int32) | jnp.int32(INT_MIN), F32)
            softplus = jnp.maximum(z, 0.0) + jnp.log(1.0 + jnp.exp(minus_abs))
            log_bs.append(z - softplus)
            if masked:
                softplus = jnp.where(earlier, softplus, 0.0)
            hi = softplus.astype(BF)
            lo = (softplus - hi.astype(F32)).astype(BF)
            afters.append(_dot(minus_later, jnp.concatenate([hi, lo], axis=0)) + laters[h])
            new_laters.append(laters[h] - jnp.sum(softplus, axis=0, keepdims=True))
        new_accs = []
        for h in range(N_HEADS):
            w = jnp.exp(log_bs[h] + afters[h])
            if masked:
                w = jnp.where(earlier, w, 0.0)
            new_accs.append(accs[h] + _dot(vt_ref[0, _head_rows(h), pl.ds(s0, tk)], w.astype(BF)))
        return tuple(new_laters), tuple(new_accs)

    carry = (tuple(jnp.zeros((1, tq), F32) for _ in range(N_HEADS)),
             tuple(jnp.zeros((HEAD_DIM, tq), F32) for _ in range(N_HEADS)))
    carry = lax.fori_loop(0, n_all - n_full, lambda i, c: step(n_all - 1 - i, c, True), carry)
    carry = lax.fori_loop(0, n_full, lambda i, c: step(n_full - 1 - i, c, False), carry)
    for h in range(N_HEADS):
        o_ref[0, _head_rows(h), :] = carry[1][h].astype(o_ref.dtype)


def _sb_attention(qt, k, vt, p_len, tq, tk):
    b, w, t = qt.shape
    lp = k.shape[1]
    assert tk & (tk - 1) == 0
    return pl.pallas_call(
        functools.partial(_sb_body, p_len=p_len, tq=tq, tk=tk),
        out_shape=jax.ShapeDtypeStruct((b, w, t), BF),
        grid=(b, t // tq),
        in_specs=[_qt_spec(w, tq), _whole_spec(lp, w), _whole_spec(w, lp)],
        out_specs=_qt_spec(w, tq),
        compiler_params=_params("parallel", "arbitrary"),
        name="sb_attention",
    )(qt, k, vt)


def _online_softmax_step(logits, m, l):
    m_new = jnp.maximum(m, jnp.max(logits, axis=0, keepdims=True))
    alpha = jnp.exp(m - m_new)
    p = jnp.exp(logits - m_new)
    return m_new, alpha, alpha * l + jnp.sum(p, axis=0, keepdims=True), p


def _fox_body(qt_ref, k_ref, vt_ref, c_ref, o_ref, *, p_len, tq, tk):
    q0 = p_len + pl.program_id(1) * tq
    qm = _head_queries(qt_ref)
    diff = _key_minus_query(tk, tq)
    n_full, n_all = _block_counts(q0, tk, q0 + tq - 1)

    def key_bias(h, s0):
        c = c_ref[0, h, pl.ds(s0, tk), :]
        return c[:, :tq] if tq <= LANES else jnp.concatenate([c] * (tq // LANES), axis=1)

    def step(kb, carry, masked):
        ms, ls, accs = carry
        s0 = pl.multiple_of(kb * tk, tk)
        if masked:
            visible = diff <= (q0 - s0)
        new_ms, new_ls, new_accs = [], [], []
        raw = [_dot(k_ref[0, pl.ds(s0, tk), _col(h // HEADS_PER_COL)], qm[h]) for h in range(N_HEADS)]
        for h in range(N_HEADS):
            logits = raw[h] - key_bias(h, s0)
            if masked:
                logits = jnp.where(visible, logits, MASK_VALUE)
            m_new, alpha, l_new, p = _online_softmax_step(logits, ms[h], ls[h])
            new_ms.append(m_new)
            new_ls.append(l_new)
            new_accs.append(alpha * accs[h] + _dot(vt_ref[0, _head_rows(h), pl.ds(s0, tk)], p.astype(BF)))
        return tuple(new_ms), tuple(new_ls), tuple(new_accs)

    carry = (tuple(jnp.full((1, tq), -jnp.inf, F32) for _ in range(N_HEADS)),
             tuple(jnp.zeros((1, tq), F32) for _ in range(N_HEADS)),
             tuple(jnp.zeros((HEAD_DIM, tq), F32) for _ in range(N_HEADS)))
    carry = lax.fori_loop(0, n_full, lambda i, c: step(i, c, False), carry)
    carry = lax.fori_loop(n_full, n_all, lambda i, c: step(i, c, True), carry)
    _, ls, accs = carry
    for h in range(N_HEADS):
        o_ref[0, _head_rows(h), :] = (accs[h] / ls[h]).astype(o_ref.dtype)


def _fox_attention(qt, k, vt, c_lanes, p_len, tq, tk):
    b, w, t = qt.shape
    lp = k.shape[1]
    assert tq <= LANES or tq % LANES == 0
    return pl.pallas_call(
        functools.partial(_fox_body, p_len=p_len, tq=tq, tk=tk),
        out_shape=jax.ShapeDtypeStruct((b, w, t), BF),
        grid=(b, t // tq),
        in_specs=[_qt_spec(w, tq), _whole_spec(lp, w), _whole_spec(w, lp),
                  pl.BlockSpec((1, N_HEADS, lp, LANES), lambda bi, qi: (bi, 0, 0, 0))],
        out_specs=_qt_spec(w, tq),
        compiler_params=_params("parallel", "arbitrary"),
        name="fox_attention",
    )(qt, k, vt, c_lanes)


def _float_key(bits):
    return jnp.where(bits < 0, jnp.int32(INT_MIN) - bits, bits)


def _dsa_body(qt_ref, qit_ref, wit_ref, k_ref, vt_ref, ki_ref, o_ref, keys_ref, *, p_len, n_keys, tq, tk, top_k):
    q0 = p_len + pl.program_id(1) * tq
    qpos = q0 + lax.broadcasted_iota(jnp.int32, (1, tq), 1)
    limit = jnp.minimum(((qpos >> CHUNK_SHIFT) + 1) << CHUNK_SHIFT, n_keys)
    last_limit = jnp.minimum((((q0 + tq - 1) >> CHUNK_SHIFT) + 1) << CHUNK_SHIFT, n_keys)
    n_blk = lax.div(last_limit - 1, tk) + 1
    key_row = lax.broadcasted_iota(jnp.int32, (tk, tq), 0)

    wit = wit_ref[0]
    qim = _head_queries(qit_ref)

    def score_step(kb, _):
        s0 = pl.multiple_of(kb * tk, tk)
        ki = ki_ref[0, pl.ds(s0, tk), :]
        score = jnp.zeros((tk, tq), F32)
        for h in range(N_HEADS):
            score = score + wit[h:h + 1, :] * jnp.maximum(_dot(ki, qim[h]), 0.0)
        score = jnp.where(key_row < limit - s0, score, -jnp.inf)
        keys_ref[pl.ds(s0, tk), :] = _float_key(pltpu.bitcast(score, jnp.int32))
        return 0

    lax.fori_loop(0, n_blk, score_step, 0)

    def count(pred):
        def body(kb, acc):
            for r in range(0, tk, COUNT_SLAB):
                s0 = pl.multiple_of(kb * tk + r, COUNT_SLAB)
                acc = acc + jnp.where(pred(keys_ref[pl.ds(s0, COUNT_SLAB), :]), 1.0, 0.0)
            return acc
        acc = lax.fori_loop(0, n_blk, body, jnp.zeros((COUNT_SLAB, tq), F32))
        return jnp.sum(acc, axis=0, keepdims=True)

    kf = jnp.float32(top_k)
    zero = jnp.zeros((1, tq), jnp.int32)
    thr = jnp.where(count(lambda key: key >= zero) >= kf, zero, jnp.int32(INT_MIN))

    def bit_step(it, thr):
        cand = thr + lax.shift_left(jnp.int32(1), 30 - it)
        return jnp.where(count(lambda key: key >= cand) >= kf, cand, thr)

    thr = lax.fori_loop(0, 31, bit_step, thr)
    n_tie_wanted = kf - count(lambda key: key > thr)
    n_ties = count(lambda key: key == thr)
    needs_order = jnp.where((n_ties > n_tie_wanted) & (thr != MINUS_INF_KEY), 1.0, 0.0)
    ordered_ties = jnp.max(needs_order) > 0.0

    qm = _head_queries(qt_ref)

    def attend(with_order):
        if with_order:
            earlier_keys = (lax.broadcasted_iota(jnp.int32, (tk, tk), 1)
                            < lax.broadcasted_iota(jnp.int32, (tk, tk), 0)).astype(BF)

        def attend_step(kb, carry):
            ties_seen, ms, ls, accs = carry
            s0 = pl.multiple_of(kb * tk, tk)
            key = keys_ref[pl.ds(s0, tk), :]
            if with_order:
                tie = jnp.where(key == thr, 1.0, 0.0)
                tie_rank = _dot(earlier_keys, tie.astype(BF)) + ties_seen
                take = jnp.where(key > thr, 1.0, jnp.where(tie_rank < n_tie_wanted, tie, 0.0))
                ties_seen = ties_seen + jnp.sum(tie, axis=0, keepdims=True)
            else:
                take = jnp.where(key >= thr, 1.0, 0.0)
            selected = jnp.where(key_row < limit - s0, take, 0.0) > 0.0
            k = k_ref[0, pl.ds(s0, tk), :]
            vt = vt_ref[0, :, pl.ds(s0, tk)]
            new_ms, new_ls, new_accs = [], [], []
            raw = [_dot(k, qm[h]) for h in range(N_HEADS)]
            for h in range(N_HEADS):
                logits = jnp.where(selected, raw[h], MASK_VALUE)
                m_new, alpha, l_new, p = _online_softmax_step(logits, ms[h], ls[h])
                new_ms.append(m_new)
                new_ls.append(l_new)
                new_accs.append(alpha * accs[h] + _dot(vt, p.astype(BF)))
            return ties_seen, tuple(new_ms), tuple(new_ls), tuple(new_accs)

        carry = (jnp.zeros((1, tq), F32),
                 tuple(jnp.full((1, tq), -jnp.inf, F32) for _ in range(N_HEADS)),
                 tuple(jnp.zeros((1, tq), F32) for _ in range(N_HEADS)),
                 tuple(jnp.zeros((HEAD_DIM, tq), F32) for _ in range(N_HEADS)))
        _, _, ls, accs = lax.fori_loop(0, n_blk, attend_step, carry)
        return ls, accs

    ls, accs = lax.cond(ordered_ties, lambda: attend(True), lambda: attend(False))
    for h in range(N_HEADS):
        o_ref[0, _head_rows(h), :] = (accs[h] / ls[h]).astype(o_ref.dtype)


def _dsa_attention(qt, qit, wit, kk, vt, p_len, n_keys, tq, tk, top_k):
    b, w, t = qt.shape
    lp = kk.shape[1]
    return pl.pallas_call(
        functools.partial(_dsa_body, p_len=p_len, n_keys=n_keys, tq=tq, tk=tk, top_k=top_k),
        out_shape=jax.ShapeDtypeStruct((b, w, t), BF),
        grid=(b, t // tq),
        in_specs=[_qt_spec(w, tq), _qt_spec(w, tq), _qt_spec(wit.shape[1], tq),
                  pl.BlockSpec((1, lp, LANES), lambda bi, qi: (bi, 0, 0)), _whole_spec(HEAD_DIM, lp),
                  pl.BlockSpec((1, lp, LANES), lambda bi, qi: (bi, 0, 1))],
        out_specs=_qt_spec(w, tq),
        scratch_shapes=[pltpu.VMEM((lp, tq), jnp.int32)],
        compiler_params=_params("parallel", "arbitrary"),
        name="dsa_attention",
    )(qt, qit, wit, kk, vt, kk)


def _ret_body(q_ref, k_ref, v_ref, g_ref, s0_ref, cos_ref, sin_ref, dec_ref, qd_ref, kd_ref, sd_ref,
              o_ref, so_ref, state_ref):
    c = pl.program_id(2)

    @pl.when(c == 0)
    def _():
        state_ref[...] = s0_ref[0, 0]

    cos, sin = cos_ref[...], sin_ref[...]
    half = HEAD_DIM // 2

    def rotary(x):
        x1, x2 = x[:half], x[half:]
        return jnp.concatenate([x1 * cos - x2 * sin, x2 * cos + x1 * sin], axis=0)

    qb = rotary(q_ref[0]).astype(BF)
    k = rotary(k_ref[0])
    vb = v_ref[0].astype(BF)
    state = state_ref[...]
    scores_t = _dot_tn(k.astype(BF), qb) * dec_ref[0]
    o = _dot(vb, scores_t.astype(BF)) + _dot(state.astype(BF), qb) * qd_ref[0]
    state = sd_ref[0] * state + _dot_nt(vb, (k * kd_ref[0]).astype(BF))
    state_ref[...] = state
    oc = o - jnp.mean(o, axis=0, keepdims=True)
    on = oc * lax.rsqrt(jnp.mean(oc * oc, axis=0, keepdims=True) + LN_EPS)
    g = g_ref[0]
    o_ref[0] = (on * (g * jax.nn.sigmoid(g))).astype(o_ref.dtype)

    @pl.when(c == pl.num_programs(2) - 1)
    def _():
        so_ref[0, 0] = state


def _retention(qt, kt, vt, gt, state0_t, pos, c):
    b, w, t = qt.shape
    h = w // HEAD_DIM
    half = HEAD_DIM // 2
    inv_freq = ROPE_BASE ** (-jnp.arange(half, dtype=F32) / half)
    ang = inv_freq[:, None] * pos.astype(F32)[None, :]
    log_gamma = np.log(1.0 - 2.0 ** (-5.0 - np.arange(h, dtype=np.float64)))
    n = np.arange(c, dtype=np.float64)
    rel = n[None, :] - n[:, None]
    decay_t = np.where(rel >= 0, np.exp(np.maximum(rel, 0.0)[None] * log_gamma[:, None, None]), 0.0)
    q_decay = np.exp((n[None, :] + 1.0) * log_gamma[:, None])[:, None, :]
    k_decay = np.exp((c - 1.0 - n)[None, :] * log_gamma[:, None])[:, None, :]
    s_decay = np.exp(c * log_gamma)[:, None, None]
    tables = [jnp.asarray(a, F32) for a in (decay_t, q_decay, k_decay, s_decay)]
    x_spec = pl.BlockSpec((1, HEAD_DIM, c), lambda bi, hi, ci: (bi, hi, ci))
    s_spec = pl.BlockSpec((1, 1, HEAD_DIM, HEAD_DIM), lambda bi, hi, ci: (bi, hi, 0, 0))
    rope_spec = pl.BlockSpec((half, c), lambda bi, hi, ci: (0, ci))

    def t_spec(a):
        return pl.BlockSpec((1,) + a.shape[1:], lambda bi, hi, ci: (hi, 0, 0))

    return pl.pallas_call(
        _ret_body,
        out_shape=(jax.ShapeDtypeStruct((b, w, t), BF),
                   jax.ShapeDtypeStruct((b, h, HEAD_DIM, HEAD_DIM), F32)),
        grid=(b, h, t // c),
        in_specs=[x_spec, x_spec, x_spec, x_spec, s_spec, rope_spec, rope_spec] + [t_spec(a) for a in tables],
        out_specs=(x_spec, s_spec),
        scratch_shapes=[pltpu.VMEM((HEAD_DIM, HEAD_DIM), F32)],
        compiler_params=_params("parallel", "parallel", "arbitrary"),
        name="retention",
    )(qt, kt, vt, gt, state0_t, jnp.cos(ang), jnp.sin(ang), *tables)


def _merge_body(h_ref, y0_ref, y1_ref, y2_ref, y3_ref, wg_ref, wb_ref, wo_ref, g_ref, b_ref, o_ref, *, alpha):
    nb, tm, d = h_ref.shape
    h = h_ref[...].reshape(nb * tm, d)
    hb = h.astype(BF)
    merged = jnp.zeros(h.shape, F32)
    for i, y_ref in enumerate((y0_ref, y1_ref, y2_ref, y3_ref)):
        gate = jax.nn.sigmoid(_dot(hb, wg_ref[:, i * d:(i + 1) * d]))
        branch = jnp.concatenate([_dot_tn(y_ref[j], wb_ref[i]) for j in range(nb)], axis=0)
        merged = merged + gate * branch
    r = alpha * h + _dot(merged.astype(BF), wo_ref[...])
    o_ref[...] = _layer_norm(r, g_ref[...], b_ref[...]).reshape(nb, tm, d)


def _merge(h, ys_t, w_gate, w_branch, w_out, ln_g, ln_b, alpha):
    b, t, d = h.shape
    tm = _row_tile(t, MERGE_ROWS)
    nb = math.gcd(b, max(1, MERGE_ROWS // tm))
    row = pl.BlockSpec((nb, tm, d), lambda bi, i: (bi, i, 0))
    col = pl.BlockSpec((nb, BRANCH_WIDTH, tm), lambda bi, i: (bi, 0, i))
    return pl.pallas_call(
        functools.partial(_merge_body, alpha=alpha),
        out_shape=jax.ShapeDtypeStruct((b, t, d), F32),
        grid=(b // nb, t // tm),
        in_specs=[row] + [col] * 4
                 + [_const_spec(w_gate.shape), _const_spec(w_branch.shape), _const_spec(w_out.shape),
                    _const_spec((1, d)), _const_spec((1, d))],
        out_specs=row,
        compiler_params=_params("parallel", "parallel"),
        name="merge",
    )(h, *ys_t, w_gate, w_branch, w_out, ln_g, ln_b)


def _ffn_body(h_ref, wi_ref, wo_ref, g_ref, b_ref, o_ref, *, alpha, f_chunk):
    h = h_ref[...]
    hb = h.astype(BF)
    f = wo_ref.shape[0]
    acc = jnp.zeros(h.shape, F32)
    for c in range(0, f, f_chunk):
        a = _dot(hb, wi_ref[:, c:c + f_chunk])
        u = _dot(hb, wi_ref[:, f + c:f + c + f_chunk])
        acc = acc + _dot((a * jax.nn.sigmoid(a) * u).astype(BF), wo_ref[c:c + f_chunk, :])
    o_ref[...] = _layer_norm(alpha * h + acc, g_ref[...], b_ref[...])


def _ffn(h, w_in, w_out, ln_g, ln_b, alpha):
    m, d = h.shape
    f = w_out.shape[0]
    tm = _row_tile(m, FFN_ROWS)
    f_chunk = f // 2 if (f // 2) % LANES == 0 else f
    row = pl.BlockSpec((tm, d), lambda i: (i, 0))
    return pl.pallas_call(
        functools.partial(_ffn_body, alpha=alpha, f_chunk=f_chunk),
        out_shape=jax.ShapeDtypeStruct((m, d), F32),
        grid=(m // tm,),
        in_specs=[row, _const_spec(w_in.shape), _const_spec(w_out.shape),
                  _const_spec((1, d)), _const_spec((1, d))],
        out_specs=row,
        compiler_params=_params("parallel"),
        name="ffn",
    )(h, w_in, w_out, ln_g, ln_b)


def _in_layout(d):
    w = BRANCH_WIDTH
    return (('sb_q', w), ('sb_k', w), ('sb_v', w), ('ret_q', w), ('ret_k', w), ('ret_v', w), ('ret_g', w),
            ('fox_q', w), ('fox_k', w), ('fox_v', w), ('fox_f', N_HEADS),
            ('dsa_q', w), ('dsa_k', HEAD_DIM), ('dsa_v', HEAD_DIM),
            ('idx_q', w), ('idx_k', HEAD_DIM), ('idx_w', N_HEADS), ('merge_gate', 4 * d))


_FOLDED_SCALE = dict(sb_q=QK_SCALE, fox_q=QK_SCALE, dsa_q=QK_SCALE, idx_q=QK_SCALE, ret_k=QK_SCALE,
                     idx_w=IDX_HEAD_SCALE)


def _swap(a):
    return jnp.swapaxes(a, -1, -2)


def _key_tiles(t, n_keys):
    tq = min(t, 256)
    tiles = dict(sb=256, fox=512, dsa=512)
    padded = {name: -(-n_keys // tk) * tk for name, tk in tiles.items()}
    return tq, tiles, padded


def _layer(h, b, t, past, ret_state, w, alpha):
    m, d = h.shape
    p_len = 0 if past is None else past[0].shape[1]
    n_keys = p_len + t
    tq, tk, lp = _key_tiles(t, n_keys)
    p = _inproj(h.reshape(b, t, d), w['w_row'], w['w_col'])

    def heads(a):
        return a.reshape(a.shape[0], a.shape[1], N_HEADS, HEAD_DIM)

    narrow = p['narrow']
    dsa_k, dsa_v, idx_k = (narrow[..., i * HEAD_DIM:(i + 1) * HEAD_DIM] for i in range(3))
    fox_f = narrow[..., 3 * HEAD_DIM:3 * HEAD_DIM + N_HEADS]

    old = (None,) * 8 if past is None else past
    sb_k0, sb_v0, fox_k0, fox_v0, fox_lf0, dsa_k0, dsa_v0, dsa_ki0 = old

    def rows_with_past(new_bf, olds, lp_):
        if past is not None:
            flat = [o.reshape(o.shape[0], o.shape[1], -1).astype(BF) for o in olds]
            new_bf = jnp.concatenate([jnp.concatenate(flat, axis=2), new_bf], axis=1)
        return jnp.pad(new_bf, ((0, 0), (0, lp_ - new_bf.shape[1]), (0, 0)))

    def cols_with_past(new_t, old, lp_):
        if old is not None:
            new_t = jnp.concatenate([_swap(old.reshape(old.shape[0], old.shape[1], -1).astype(BF)), new_t], axis=2)
        return jnp.pad(new_t, ((0, 0), (0, 0), (0, lp_ - new_t.shape[2])))

    y_sb = _sb_attention(p['sb_q_t'], rows_with_past(p['sb_k_bf'], [sb_k0], lp['sb']),
                         cols_with_past(p['sb_v_t'], sb_v0, lp['sb']), p_len, tq, tk['sb'])

    pos = p_len + jnp.arange(t, dtype=jnp.int32)
    y_ret, ret_state_t = _retention(p['ret_q_t'], p['ret_k_t'], p['ret_v_t'], p['ret_g_t'], _swap(ret_state),
                                    pos, min(t, 256))

    fox_lf = jax.nn.log_sigmoid(fox_f + w['b_forget'])
    lf_all = fox_lf if fox_lf0 is None else jnp.concatenate([fox_lf0, fox_lf], axis=1)
    cum = jnp.pad(jnp.cumsum(lf_all, axis=1), ((0, 0), (0, lp['fox'] - n_keys), (0, 0)))
    c_lanes = jnp.broadcast_to(_swap(cum)[..., None], (b, N_HEADS, lp['fox'], LANES))
    y_fox = _fox_attention(p['fox_q_t'], rows_with_past(p['fox_k_bf'], [fox_k0], lp['fox']),
                           cols_with_past(p['fox_v_t'], fox_v0, lp['fox']), c_lanes, p_len, tq, tk['fox'])

    top_k = min(DSA_TOP_K, n_keys // 4)
    kk_old = [dsa_k0, dsa_k0, dsa_ki0, dsa_ki0]
    y_dsa = _dsa_attention(p['dsa_q_t'], p['idx_q_t'], p['idx_w_t'], rows_with_past(p['dsa_kk_bf'], kk_old, lp['dsa']),
                           cols_with_past(p['dsa_v_t'], dsa_v0, lp['dsa']), p_len, n_keys, tq, tk['dsa'], top_k)

    h = _merge(h.reshape(b, t, d), (y_sb, y_ret, y_fox, y_dsa), w['w_gate'], w['w_branch'], w['w_out'],
               w['ln1_g'], w['ln1_b'], alpha)
    h = _ffn(h.reshape(m, d), w['w_ffn_in'], w['w_ffn_out'], w['ln2_g'], w['ln2_b'], alpha)
    return h, (heads(p['sb_k']), heads(p['sb_v']), _swap(ret_state_t), heads(p['fox_k']), heads(p['fox_v']),
               fox_lf, dsa_k, dsa_v, idx_k)


def kernel(x_prompt, x_sample, cache_sb_k, cache_sb_v, state_ret, cache_fox_k, cache_fox_v, cache_fox_logf,
           cache_dsa_k, cache_dsa_v, cache_dsa_kidx, w_in, b_forget, w_branch, w_out, ln1_g, ln1_b,
           w_ffn_in, w_ffn_out, ln2_g, ln2_b):
    depth = w_in.shape[0]
    alpha = float((2 * depth) ** 0.25)
    bp, tp, d = x_prompt.shape
    bs, ts, _ = x_sample.shape
    hp = x_prompt.reshape(bp * tp, d)
    hs = x_sample.reshape(bs * ts, d)
    ret_zero = jnp.zeros((bp, N_HEADS, HEAD_DIM, HEAD_DIM), F32)
    st_p, st_s = [], []
    for l in range(depth):
        w_row, w_col, w_gate = _inproj_weights(w_in[l])
        w = dict(w_row=w_row, w_col=w_col, w_gate=w_gate, b_forget=b_forget[l], w_branch=w_branch[l].astype(BF),
                 w_out=w_out[l].astype(BF), ln1_g=ln1_g[l][None], ln1_b=ln1_b[l][None],
                 w_ffn_in=w_ffn_in[l].astype(BF), w_ffn_out=w_ffn_out[l].astype(BF),
                 ln2_g=ln2_g[l][None], ln2_b=ln2_b[l][None])
        hp, sp = _layer(hp, bp, tp, None, ret_zero, w, alpha)
        past = (cache_sb_k[l], cache_sb_v[l], cache_fox_k[l], cache_fox_v[l], cache_fox_logf[l],
                cache_dsa_k[l], cache_dsa_v[l], cache_dsa_kidx[l])
        hs, ss = _layer(hs, bs, ts, past, state_ret[l], w, alpha)
        st_p.append(sp)
        st_s.append(ss)

    def stacked(states, i):
        return jnp.stack([s[i] for s in states])

    return ((hp.reshape(bp, tp, d), hs.reshape(bs, ts, d))
            + tuple(stacked(st_p, i) for i in range(9)) + tuple(stacked(st_s, i) for i in range(9)))
```

```python
import functools
import math

import numpy as np
import jax
import jax.numpy as jnp
from jax import lax
from jax.experimental import pallas as pl
from jax.experimental.pallas import tpu as pltpu

HEAD_DIM = 64
N_HEADS = 4
BRANCH_WIDTH = N_HEADS * HEAD_DIM
CHUNK_SHIFT = 6
DSA_TOP_K = 256
ROPE_BASE = 10000.0
LN_EPS = 1e-5
QK_SCALE = HEAD_DIM ** -0.5
IDX_HEAD_SCALE = N_HEADS ** -0.5
MASK_VALUE = -1e30
INT_MIN = -2 ** 31

V7X_VMEM_LIMIT_BYTES = 56 * 1024 * 1024
LANES = 128
HEADS_PER_COL = LANES // HEAD_DIM
COUNT_SLAB = 64
MERGE_ROWS = 512
FFN_ROWS = 512

BF = jnp.bfloat16
F32 = jnp.float32


def _dot(a, b):
    return jnp.dot(a, b, preferred_element_type=F32)


def _dot_nt(a, b):
    return lax.dot_general(a, b, (((1,), (1,)), ((), ())), preferred_element_type=F32)


def _dot_tn(a, b):
    return lax.dot_general(a, b, (((0,), (0,)), ((), ())), preferred_element_type=F32)


def _params(*sem):
    return pltpu.CompilerParams(dimension_semantics=sem, vmem_limit_bytes=V7X_VMEM_LIMIT_BYTES)


def _const_spec(shape):
    nd = len(shape)
    return pl.BlockSpec(shape, lambda *_: (0,) * nd)


def _layer_norm(x, g, b):
    xc = x - jnp.mean(x, axis=-1, keepdims=True)
    var = jnp.mean(xc * xc, axis=-1, keepdims=True)
    return xc * lax.rsqrt(var + LN_EPS) * g + b


def _row_tile(m, want):
    t = min(m, want)
    assert m % t == 0
    return t


def _col(c):
    return slice(c * LANES, (c + 1) * LANES)


_NARROW_SRC = ('dsa_k', 'dsa_v', 'idx_k', 'fox_f')
_ROW_OUTS = (
    ('sb_k', ('sb_k',), F32, 0, BRANCH_WIDTH, True), ('sb_k_bf', ('sb_k',), BF, 0, BRANCH_WIDTH, False),
    ('sb_v', ('sb_v',), F32, 0, BRANCH_WIDTH, True),
    ('fox_k', ('fox_k',), F32, 0, BRANCH_WIDTH, True), ('fox_k_bf', ('fox_k',), BF, 0, BRANCH_WIDTH, False),
    ('fox_v', ('fox_v',), F32, 0, BRANCH_WIDTH, True),
    ('dsa_k', _NARROW_SRC, F32, 0, HEAD_DIM, True), ('dsa_v', _NARROW_SRC, F32, HEAD_DIM, HEAD_DIM, True),
    ('idx_k', _NARROW_SRC, F32, 2 * HEAD_DIM, HEAD_DIM, True), ('fox_f', _NARROW_SRC, F32, 3 * HEAD_DIM, N_HEADS, True),
    ('dsa_kk_bf', ('dsa_k', 'dsa_k', 'idx_k', 'idx_k'), BF, 0, 4 * HEAD_DIM, False),
)
_COL_OUTS = (
    ('sb_q_t', ('sb_q',), BF), ('fox_q_t', ('fox_q',), BF), ('dsa_q_t', ('dsa_q',), BF),
    ('idx_q_t', ('idx_q',), BF), ('sb_v_t', ('sb_v',), BF), ('fox_v_t', ('fox_v',), BF),
    ('ret_q_t', ('ret_q',), F32), ('ret_k_t', ('ret_k',), F32), ('ret_v_t', ('ret_v',), F32),
    ('ret_g_t', ('ret_g',), F32), ('dsa_v_t', ('dsa_v',), BF), ('idx_w_t', ('idx_w',), F32),
)
BF16_ROWS_PER_VREG = 16


def _inproj_plan(d):
    widths = dict(_in_layout(d))

    def spans(outs, multiple):
        span_of, off = {}, 0
        for out in outs:
            srcs = out[1]
            if srcs not in span_of:
                w = -(-sum(widths[s] for s in srcs) // multiple) * multiple
                span_of[srcs] = (off, w)
                off += w
        return span_of, off

    row_spans, _ = spans(_ROW_OUTS, LANES)
    col_spans, n_col = spans(_COL_OUTS, BF16_ROWS_PER_VREG)
    return row_spans, col_spans, -(-n_col // LANES) * LANES


def _inproj_weights(w_in):
    d = w_in.shape[0]
    pieces, off = {}, 0
    for name, width in _in_layout(d):
        pieces[name] = w_in[:, off:off + width] * _FOLDED_SCALE.get(name, 1.0)
        off += width
    assert off == w_in.shape[1]
    row_spans, col_spans, n_col = _inproj_plan(d)

    def block(srcs, width):
        w = jnp.concatenate([pieces[s] for s in srcs], axis=1)
        return jnp.pad(w, ((0, 0), (0, width - w.shape[1])))

    w_row = jnp.concatenate([block(srcs, w) for srcs, (_, w) in row_spans.items()], axis=1)
    w_col = jnp.concatenate([block(srcs, w) for srcs, (_, w) in col_spans.items()], axis=1)
    w_col = jnp.pad(w_col, ((0, 0), (0, n_col - w_col.shape[1])))
    return w_row.astype(BF), w_col.T.astype(BF), pieces['merge_gate'].astype(BF)


def _inproj_body(x_ref, wr_ref, wc_ref, *refs, row_spans, col_spans, n_alias):
    o_refs = refs[n_alias:]
    xb = x_ref[0].astype(BF)
    done = {}
    for o_ref, (_, srcs, _, lane, width, _) in zip(o_refs, _ROW_OUTS):
        if srcs not in done:
            off, w = row_spans[srcs]
            done[srcs] = _dot(xb, wr_ref[:, off:off + w])
        o_ref[...] = done[srcs][:, lane:lane + width].astype(o_ref.dtype).reshape(o_ref.shape)
    for o_ref, (_, srcs, _) in zip(o_refs[len(_ROW_OUTS):], _COL_OUTS):
        off, w = col_spans[srcs]
        o_ref[0] = _dot_nt(wc_ref[off:off + w, :], xb).astype(o_ref.dtype)


def _inproj(x, w_row, w_col, layer, depth, states):
    b, t, d = x.shape
    tm = _row_tile(t, 512)
    row_spans, col_spans, _ = _inproj_plan(d)
    out_shape, out_specs, state_names = [], [], []
    for name, _, dt, _, w, is_state in _ROW_OUTS:
        if is_state:
            state_names.append(name)
            out_shape.append(jax.ShapeDtypeStruct((depth, b, t, w), dt))
            out_specs.append(pl.BlockSpec((1, 1, tm, w), lambda bi, i: (layer, bi, i, 0)))
        else:
            out_shape.append(jax.ShapeDtypeStruct((b, t, w), dt))
            out_specs.append(pl.BlockSpec((1, tm, w), lambda bi, i: (bi, i, 0)))
    for _, srcs, dt in _COL_OUTS:
        w = col_spans[srcs][1]
        out_shape.append(jax.ShapeDtypeStruct((b, w, t), dt))
        out_specs.append(pl.BlockSpec((1, w, tm), lambda bi, i: (bi, 0, i)))
    prev = [] if states is None else [states[n] for n in state_names]
    names = [o[0] for o in _ROW_OUTS] + [o[0] for o in _COL_OUTS]
    aliases = {3 + j: names.index(n) for j, n in enumerate(state_names)} if prev else {}
    outs = pl.pallas_call(
        functools.partial(_inproj_body, row_spans=row_spans, col_spans=col_spans, n_alias=len(prev)),
        out_shape=out_shape,
        grid=(b, t // tm),
        in_specs=[pl.BlockSpec((1, tm, d), lambda bi, i: (bi, i, 0)),
                  _const_spec(w_row.shape), _const_spec(w_col.shape)]
                 + [pl.BlockSpec(memory_space=pl.ANY)] * len(prev),
        out_specs=out_specs,
        input_output_aliases=aliases,
        compiler_params=_params("parallel", "parallel"),
        name="inproj",
    )(x, w_row, w_col, *prev)
    return dict(zip(names, outs))


def _block_counts(q0, tk, last_key):
    return lax.div(q0, tk), lax.div(last_key, tk) + 1


def _head_queries(qt_ref):
    low = lax.broadcasted_iota(jnp.int32, (LANES, 1), 0) < HEAD_DIM
    out = []
    for h in range(N_HEADS):
        qc = qt_ref[0, _col(h // HEADS_PER_COL), :]
        keep = low if h % HEADS_PER_COL == 0 else jnp.logical_not(low)
        out.append(jnp.where(keep, qc, jnp.zeros_like(qc)))
    return out


def _head_rows(h):
    return slice(h * HEAD_DIM, (h + 1) * HEAD_DIM)


def _key_minus_query(tk, tq):
    return (lax.broadcasted_iota(jnp.int32, (tk, tq), 0) - lax.broadcasted_iota(jnp.int32, (tk, tq), 1))


def _qt_spec(w, tq):
    return pl.BlockSpec((1, w, tq), lambda bi, qi: (bi, 0, qi))


def _whole_spec(rows, cols):
    return pl.BlockSpec((1, rows, cols), lambda bi, qi: (bi, 0, 0))


def _sb_body(qt_ref, k_ref, vt_ref, o_ref, *, p_len, tq, tk):
    q0 = p_len + pl.program_id(1) * tq
    qm = _head_queries(qt_ref)
    diff = _key_minus_query(tk, tq)
    later = (lax.broadcasted_iota(jnp.int32, (tk, 2 * tk), 1) & (tk - 1)) > lax.broadcasted_iota(
        jnp.int32, (tk, 2 * tk), 0)
    minus_later = jnp.where(later, -1.0, 0.0).astype(BF)
    n_full, n_all = _block_counts(q0, tk, jnp.maximum(q0 + tq - 2, 0))

    def step(kb, carry, masked):
        laters, accs = carry
        s0 = pl.multiple_of(kb * tk, tk)
        if masked:
            earlier = diff < (q0 - s0)
        zs = [_dot(k_ref[0, pl.ds(s0, tk), _col(h // HEADS_PER_COL)], qm[h]) for h in range(N_HEADS)]
        new_laters, log_bs, afters = [], [], []
        for h in range(N_HEADS):
            z = zs[h]
            minus_abs = pltpu.bitcast(pltpu.bitcast(z, jnp.int32) | jnp.int32(INT_MIN), F32)
            softplus = jnp.maximum(z, 0.0) + jnp.log(1.0 + jnp.exp(minus_abs))
            log_bs.append(z - softplus)
            if masked:
                softplus = jnp.where(earlier, softplus, 0.0)
            hi = softplus.astype(BF)
            lo = (softplus - hi.astype(F32)).astype(BF)
            afters.append(_dot(minus_later, jnp.concatenate([hi, lo], axis=0)) + laters[h])
            new_laters.append(laters[h] - jnp.sum(softplus, axis=0, keepdims=True))
        new_accs = []
        for h in range(N_HEADS):
            w = jnp.exp(log_bs[h] + afters[h])
            if masked:
                w = jnp.where(earlier, w, 0.0)
            new_accs.append(accs[h] + _dot(vt_ref[0, _head_rows(h), pl.ds(s0, tk)], w.astype(BF)))
        return tuple(new_laters), tuple(new_accs)

    carry = (tuple(jnp.zeros((1, tq), F32) for _ in range(N_HEADS)),
             tuple(jnp.zeros((HEAD_DIM, tq), F32) for _ in range(N_HEADS)))
    carry = lax.fori_loop(0, n_all - n_full, lambda i, c: step(n_all - 1 - i, c, True), carry)
    carry = lax.fori_loop(0, n_full, lambda i, c: step(n_full - 1 - i, c, False), carry)
    for h in range(N_HEADS):
        o_ref[0, _head_rows(h), :] = carry[1][h].astype(o_ref.dtype)


def _sb_attention(qt, k, vt, p_len, tq, tk):
    b, w, t = qt.shape
    lp = k.shape[1]
    assert tk & (tk - 1) == 0
    return pl.pallas_call(
        functools.partial(_sb_body, p_len=p_len, tq=tq, tk=tk),
        out_shape=jax.ShapeDtypeStruct((b, w, t), BF),
        grid=(b, t // tq),
        in_specs=[_qt_spec(w, tq), _whole_spec(lp, w), _whole_spec(w, lp)],
        out_specs=_qt_spec(w, tq),
        compiler_params=_params("parallel", "arbitrary"),
        name="sb_attention",
    )(qt, k, vt)


def _online_softmax_step(logits, m, l):
    m_new = jnp.maximum(m, jnp.max(logits, axis=0, keepdims=True))
    alpha = jnp.exp(m - m_new)
    p = jnp.exp(logits - m_new)
    return m_new, alpha, alpha * l + jnp.sum(p, axis=0, keepdims=True), p


def _fox_body(qt_ref, k_ref, vt_ref, c_ref, o_ref, *, p_len, tq, tk):
    q0 = p_len + pl.program_id(1) * tq
    qm = _head_queries(qt_ref)
    diff = _key_minus_query(tk, tq)
    n_full, n_all = _block_counts(q0, tk, q0 + tq - 1)

    def key_bias(h, s0):
        c = c_ref[0, h, pl.ds(s0, tk), :]
        return c[:, :tq] if tq <= LANES else jnp.concatenate([c] * (tq // LANES), axis=1)

    def step(kb, carry, masked):
        ms, ls, accs = carry
        s0 = pl.multiple_of(kb * tk, tk)
        if masked:
            visible = diff <= (q0 - s0)
        new_ms, new_ls, new_accs = [], [], []
        raw = [_dot(k_ref[0, pl.ds(s0, tk), _col(h // HEADS_PER_COL)], qm[h]) for h in range(N_HEADS)]
        for h in range(N_HEADS):
            logits = raw[h] - key_bias(h, s0)
            if masked:
                logits = jnp.where(visible, logits, MASK_VALUE)
            m_new, alpha, l_new, p = _online_softmax_step(logits, ms[h], ls[h])
            new_ms.append(m_new)
            new_ls.append(l_new)
            new_accs.append(alpha * accs[h] + _dot(vt_ref[0, _head_rows(h), pl.ds(s0, tk)], p.astype(BF)))
        return tuple(new_ms), tuple(new_ls), tuple(new_accs)

    carry = (tuple(jnp.full((1, tq), -jnp.inf, F32) for _ in range(N_HEADS)),
             tuple(jnp.zeros((1, tq), F32) for _ in range(N_HEADS)),
             tuple(jnp.zeros((HEAD_DIM, tq), F32) for _ in range(N_HEADS)))
    carry = lax.fori_loop(0, n_full, lambda i, c: step(i, c, False), carry)
    carry = lax.fori_loop(n_full, n_all, lambda i, c: step(i, c, True), carry)
    _, ls, accs = carry
    for h in range(N_HEADS):
        o_ref[0, _head_rows(h), :] = (accs[h] / ls[h]).astype(o_ref.dtype)


def _fox_attention(qt, k, vt, c_lanes, p_len, tq, tk):
    b, w, t = qt.shape
    lp = k.shape[1]
    assert tq <= LANES or tq % LANES == 0
    return pl.pallas_call(
        functools.partial(_fox_body, p_len=p_len, tq=tq, tk=tk),
        out_shape=jax.ShapeDtypeStruct((b, w, t), BF),
        grid=(b, t // tq),
        in_specs=[_qt_spec(w, tq), _whole_spec(lp, w), _whole_spec(w, lp),
                  pl.BlockSpec((1, N_HEADS, lp, LANES), lambda bi, qi: (bi, 0, 0, 0))],
        out_specs=_qt_spec(w, tq),
        compiler_params=_params("parallel", "arbitrary"),
        name="fox_attention",
    )(qt, k, vt, c_lanes)


def _float_key(bits):
    return jnp.where(bits < 0, jnp.int32(INT_MIN) - bits, bits)


def _dsa_body(qt_ref, qit_ref, wit_ref, k_ref, vt_ref, ki_ref, o_ref, keys_ref, *, p_len, n_keys, tq, tk, top_k):
    q0 = p_len + pl.program_id(1) * tq
    qpos = q0 + lax.broadcasted_iota(jnp.int32, (1, tq), 1)
    limit = jnp.minimum(((qpos >> CHUNK_SHIFT) + 1) << CHUNK_SHIFT, n_keys)
    last_limit = jnp.minimum((((q0 + tq - 1) >> CHUNK_SHIFT) + 1) << CHUNK_SHIFT, n_keys)
    n_blk = lax.div(last_limit - 1, tk) + 1
    key_row = lax.broadcasted_iota(jnp.int32, (tk, tq), 0)

    wit = wit_ref[0]
    qim = _head_queries(qit_ref)

    def score_step(kb, _):
        s0 = pl.multiple_of(kb * tk, tk)
        ki = ki_ref[0, pl.ds(s0, tk), :]
        score = jnp.zeros((tk, tq), F32)
        for h in range(N_HEADS):
            score = score + wit[h:h + 1, :] * jnp.maximum(_dot(ki, qim[h]), 0.0)
        score = jnp.where(key_row < limit - s0, score, -jnp.inf)
        keys_ref[pl.ds(s0, tk), :] = _float_key(pltpu.bitcast(score, jnp.int32))
        return 0

    lax.fori_loop(0, n_blk, score_step, 0)

    def count(pred):
        def body(kb, acc):
            for r in range(0, tk, COUNT_SLAB):
                s0 = pl.multiple_of(kb * tk + r, COUNT_SLAB)
                acc = acc + jnp.where(pred(keys_ref[pl.ds(s0, COUNT_SLAB), :]), 1.0, 0.0)
            return acc
        acc = lax.fori_loop(0, n_blk, body, jnp.zeros((COUNT_SLAB, tq), F32))
        return jnp.sum(acc, axis=0, keepdims=True)

    kf = jnp.float32(top_k)
    zero = jnp.zeros((1, tq), jnp.int32)
    thr = jnp.where(count(lambda key: key >= zero) >= kf, zero, jnp.int32(INT_MIN))

    def bit_step(it, thr):
        cand = thr + lax.shift_left(jnp.int32(1), 30 - it)
        return jnp.where(count(lambda key: key >= cand) >= kf, cand, thr)

    thr = lax.fori_loop(0, 31, bit_step, thr)
    n_tie_wanted = kf - count(lambda key: key > thr)

    earlier_keys = (lax.broadcasted_iota(jnp.int32, (tk, tk), 1)
                    < lax.broadcasted_iota(jnp.int32, (tk, tk), 0)).astype(BF)
    qm = _head_queries(qt_ref)

    def attend_step(kb, carry):
        ties_seen, ms, ls, accs = carry
        s0 = pl.multiple_of(kb * tk, tk)
        key = keys_ref[pl.ds(s0, tk), :]
        tie = jnp.where(key == thr, 1.0, 0.0)
        tie_rank = _dot(earlier_keys, tie.astype(BF)) + ties_seen
        take = jnp.where(key > thr, 1.0, jnp.where(tie_rank < n_tie_wanted, tie, 0.0))
        selected = jnp.where(key_row < limit - s0, take, 0.0) > 0.0
        k = k_ref[0, pl.ds(s0, tk), :]
        vt = vt_ref[0, :, pl.ds(s0, tk)]
        new_ms, new_ls, new_accs = [], [], []
        raw = [_dot(k, qm[h]) for h in range(N_HEADS)]
        for h in range(N_HEADS):
            logits = jnp.where(selected, raw[h], MASK_VALUE)
            m_new, alpha, l_new, p = _online_softmax_step(logits, ms[h], ls[h])
            new_ms.append(m_new)
            new_ls.append(l_new)
            new_accs.append(alpha * accs[h] + _dot(vt, p.astype(BF)))
        ties_seen = ties_seen + jnp.sum(tie, axis=0, keepdims=True)
        return ties_seen, tuple(new_ms), tuple(new_ls), tuple(new_accs)

    carry = (jnp.zeros((1, tq), F32),
             tuple(jnp.full((1, tq), -jnp.inf, F32) for _ in range(N_HEADS)),
             tuple(jnp.zeros((1, tq), F32) for _ in range(N_HEADS)),
             tuple(jnp.zeros((HEAD_DIM, tq), F32) for _ in range(N_HEADS)))
    _, _, ls, accs = lax.fori_loop(0, n_blk, attend_step, carry)
    for h in range(N_HEADS):
        o_ref[0, _head_rows(h), :] = (accs[h] / ls[h]).astype(o_ref.dtype)


def _dsa_attention(qt, qit, wit, kk, vt, p_len, n_keys, tq, tk, top_k):
    b, w, t = qt.shape
    lp = kk.shape[1]
    return pl.pallas_call(
        functools.partial(_dsa_body, p_len=p_len, n_keys=n_keys, tq=tq, tk=tk, top_k=top_k),
        out_shape=jax.ShapeDtypeStruct((b, w, t), BF),
        grid=(b, t // tq),
        in_specs=[_qt_spec(w, tq), _qt_spec(w, tq), _qt_spec(wit.shape[1], tq),
                  pl.BlockSpec((1, lp, LANES), lambda bi, qi: (bi, 0, 0)), _whole_spec(HEAD_DIM, lp),
                  pl.BlockSpec((1, lp, LANES), lambda bi, qi: (bi, 0, 1))],
        out_specs=_qt_spec(w, tq),
        scratch_shapes=[pltpu.VMEM((lp, tq), jnp.int32)],
        compiler_params=_params("parallel", "arbitrary"),
        name="dsa_attention",
    )(qt, qit, wit, kk, vt, kk)


def _ret_body(q_ref, k_ref, v_ref, g_ref, s0_ref, cos_ref, sin_ref, dec_ref, qd_ref, kd_ref, sd_ref,
              o_ref, so_ref, state_ref):
    c = pl.program_id(2)

    @pl.when(c == 0)
    def _():
        state_ref[...] = s0_ref[0, 0]

    cos, sin = cos_ref[...], sin_ref[...]
    half = HEAD_DIM // 2

    def rotary(x):
        x1, x2 = x[:half], x[half:]
        return jnp.concatenate([x1 * cos - x2 * sin, x2 * cos + x1 * sin], axis=0)

    qb = rotary(q_ref[0]).astype(BF)
    k = rotary(k_ref[0])
    vb = v_ref[0].astype(BF)
    state = state_ref[...]
    scores_t = _dot_tn(k.astype(BF), qb) * dec_ref[0]
    o = _dot(vb, scores_t.astype(BF)) + _dot(state.astype(BF), qb) * qd_ref[0]
    state = sd_ref[0] * state + _dot_nt(vb, (k * kd_ref[0]).astype(BF))
    state_ref[...] = state
    oc = o - jnp.mean(o, axis=0, keepdims=True)
    on = oc * lax.rsqrt(jnp.mean(oc * oc, axis=0, keepdims=True) + LN_EPS)
    g = g_ref[0]
    o_ref[0] = (on * (g * jax.nn.sigmoid(g))).astype(o_ref.dtype)

    @pl.when(c == pl.num_programs(2) - 1)
    def _():
        so_ref[0, 0] = state


def _retention(qt, kt, vt, gt, state0_t, pos, c):
    b, w, t = qt.shape
    h = w // HEAD_DIM
    half = HEAD_DIM // 2
    inv_freq = ROPE_BASE ** (-jnp.arange(half, dtype=F32) / half)
    ang = inv_freq[:, None] * pos.astype(F32)[None, :]
    log_gamma = np.log(1.0 - 2.0 ** (-5.0 - np.arange(h, dtype=np.float64)))
    n = np.arange(c, dtype=np.float64)
    rel = n[None, :] - n[:, None]
    decay_t = np.where(rel >= 0, np.exp(np.maximum(rel, 0.0)[None] * log_gamma[:, None, None]), 0.0)
    q_decay = np.exp((n[None, :] + 1.0) * log_gamma[:, None])[:, None, :]
    k_decay = np.exp((c - 1.0 - n)[None, :] * log_gamma[:, None])[:, None, :]
    s_decay = np.exp(c * log_gamma)[:, None, None]
    tables = [jnp.asarray(a, F32) for a in (decay_t, q_decay, k_decay, s_decay)]
    x_spec = pl.BlockSpec((1, HEAD_DIM, c), lambda bi, hi, ci: (bi, hi, ci))
    s_spec = pl.BlockSpec((1, 1, HEAD_DIM, HEAD_DIM), lambda bi, hi, ci: (bi, hi, 0, 0))
    rope_spec = pl.BlockSpec((half, c), lambda bi, hi, ci: (0, ci))

    def t_spec(a):
        return pl.BlockSpec((1,) + a.shape[1:], lambda bi, hi, ci: (hi, 0, 0))

    return pl.pallas_call(
        _ret_body,
        out_shape=(jax.ShapeDtypeStruct((b, w, t), BF),
                   jax.ShapeDtypeStruct((b, h, HEAD_DIM, HEAD_DIM), F32)),
        grid=(b, h, t // c),
        in_specs=[x_spec, x_spec, x_spec, x_spec, s_spec, rope_spec, rope_spec] + [t_spec(a) for a in tables],
        out_specs=(x_spec, s_spec),
        scratch_shapes=[pltpu.VMEM((HEAD_DIM, HEAD_DIM), F32)],
        compiler_params=_params("parallel", "parallel", "arbitrary"),
        name="retention",
    )(qt, kt, vt, gt, state0_t, jnp.cos(ang), jnp.sin(ang), *tables)


def _merge_body(h_ref, y0_ref, y1_ref, y2_ref, y3_ref, wg_ref, wb_ref, wo_ref, g_ref, b_ref, o_ref, *, alpha):
    nb, tm, d = h_ref.shape
    h = h_ref[...].reshape(nb * tm, d)
    hb = h.astype(BF)
    merged = jnp.zeros(h.shape, F32)
    for i, y_ref in enumerate((y0_ref, y1_ref, y2_ref, y3_ref)):
        gate = jax.nn.sigmoid(_dot(hb, wg_ref[:, i * d:(i + 1) * d]))
        branch = jnp.concatenate([_dot_tn(y_ref[j], wb_ref[i]) for j in range(nb)], axis=0)
        merged = merged + gate * branch
    r = alpha * h + _dot(merged.astype(BF), wo_ref[...])
    o_ref[...] = _layer_norm(r, g_ref[...], b_ref[...]).reshape(nb, tm, d)


def _merge(h, ys_t, w_gate, w_branch, w_out, ln_g, ln_b, alpha):
    b, t, d = h.shape
    tm = _row_tile(t, MERGE_ROWS)
    nb = math.gcd(b, max(1, MERGE_ROWS // tm))
    row = pl.BlockSpec((nb, tm, d), lambda bi, i: (bi, i, 0))
    col = pl.BlockSpec((nb, BRANCH_WIDTH, tm), lambda bi, i: (bi, 0, i))
    return pl.pallas_call(
        functools.partial(_merge_body, alpha=alpha),
        out_shape=jax.ShapeDtypeStruct((b, t, d), F32),
        grid=(b // nb, t // tm),
        in_specs=[row] + [col] * 4
                 + [_const_spec(w_gate.shape), _const_spec(w_branch.shape), _const_spec(w_out.shape),
                    _const_spec((1, d)), _const_spec((1, d))],
        out_specs=row,
        compiler_params=_params("parallel", "parallel"),
        name="merge",
    )(h, *ys_t, w_gate, w_branch, w_out, ln_g, ln_b)


def _ffn_body(h_ref, wi_ref, wo_ref, g_ref, b_ref, o_ref, *, alpha, f_chunk):
    h = h_ref[...]
    hb = h.astype(BF)
    f = wo_ref.shape[0]
    acc = jnp.zeros(h.shape, F32)
    for c in range(0, f, f_chunk):
        a = _dot(hb, wi_ref[:, c:c + f_chunk])
        u = _dot(hb, wi_ref[:, f + c:f + c + f_chunk])
        acc = acc + _dot((a * jax.nn.sigmoid(a) * u).astype(BF), wo_ref[c:c + f_chunk, :])
    o_ref[...] = _layer_norm(alpha * h + acc, g_ref[...], b_ref[...])


def _ffn(h, w_in, w_out, ln_g, ln_b, alpha):
    m, d = h.shape
    f = w_out.shape[0]
    tm = _row_tile(m, FFN_ROWS)
    f_chunk = f // 2 if (f // 2) % LANES == 0 else f
    row = pl.BlockSpec((tm, d), lambda i: (i, 0))
    return pl.pallas_call(
        functools.partial(_ffn_body, alpha=alpha, f_chunk=f_chunk),
        out_shape=jax.ShapeDtypeStruct((m, d), F32),
        grid=(m // tm,),
        in_specs=[row, _const_spec(w_in.shape), _const_spec(w_out.shape),
                  _const_spec((1, d)), _const_spec((1, d))],
        out_specs=row,
        compiler_params=_params("parallel"),
        name="ffn",
    )(h, w_in, w_out, ln_g, ln_b)


def _in_layout(d):
    w = BRANCH_WIDTH
    return (('sb_q', w), ('sb_k', w), ('sb_v', w), ('ret_q', w), ('ret_k', w), ('ret_v', w), ('ret_g', w),
            ('fox_q', w), ('fox_k', w), ('fox_v', w), ('fox_f', N_HEADS),
            ('dsa_q', w), ('dsa_k', HEAD_DIM), ('dsa_v', HEAD_DIM),
            ('idx_q', w), ('idx_k', HEAD_DIM), ('idx_w', N_HEADS), ('merge_gate', 4 * d))


_FOLDED_SCALE = dict(sb_q=QK_SCALE, fox_q=QK_SCALE, dsa_q=QK_SCALE, idx_q=QK_SCALE, ret_k=QK_SCALE,
                     idx_w=IDX_HEAD_SCALE)


def _swap(a):
    return jnp.swapaxes(a, -1, -2)


def _key_tiles(t, n_keys):
    tq = min(t, 256)
    tiles = dict(sb=256, fox=512, dsa=512)
    padded = {name: -(-n_keys // tk) * tk for name, tk in tiles.items()}
    return tq, tiles, padded


def _layer(h, b, t, past, ret_state, w, alpha, layer, depth, states):
    m, d = h.shape
    p_len = 0 if past is None else past[0].shape[1]
    n_keys = p_len + t
    tq, tk, lp = _key_tiles(t, n_keys)
    p = _inproj(h.reshape(b, t, d), w['w_row'], w['w_col'], layer, depth, states)
    states = {name: p[name] for name, _, _, _, _, is_state in _ROW_OUTS if is_state}

    old = (None,) * 8 if past is None else past
    sb_k0, sb_v0, fox_k0, fox_v0, fox_lf0, dsa_k0, dsa_v0, dsa_ki0 = old

    def rows_with_past(new_bf, olds, lp_):
        if past is not None:
            flat = [o.reshape(o.shape[0], o.shape[1], -1).astype(BF) for o in olds]
            new_bf = jnp.concatenate([jnp.concatenate(flat, axis=2), new_bf], axis=1)
        return jnp.pad(new_bf, ((0, 0), (0, lp_ - new_bf.shape[1]), (0, 0)))

    def cols_with_past(new_t, old, lp_):
        if old is not None:
            new_t = jnp.concatenate([_swap(old.reshape(old.shape[0], old.shape[1], -1).astype(BF)), new_t], axis=2)
        return jnp.pad(new_t, ((0, 0), (0, 0), (0, lp_ - new_t.shape[2])))

    y_sb = _sb_attention(p['sb_q_t'], rows_with_past(p['sb_k_bf'], [sb_k0], lp['sb']),
                         cols_with_past(p['sb_v_t'], sb_v0, lp['sb']), p_len, tq, tk['sb'])

    pos = p_len + jnp.arange(t, dtype=jnp.int32)
    y_ret, ret_state_t = _retention(p['ret_q_t'], p['ret_k_t'], p['ret_v_t'], p['ret_g_t'], _swap(ret_state),
                                    pos, min(t, 256))

    fox_lf = jax.nn.log_sigmoid(states['fox_f'][layer] + w['b_forget'])
    lf_all = fox_lf if fox_lf0 is None else jnp.concatenate([fox_lf0, fox_lf], axis=1)
    cum = jnp.pad(jnp.cumsum(lf_all, axis=1), ((0, 0), (0, lp['fox'] - n_keys), (0, 0)))
    c_lanes = jnp.broadcast_to(_swap(cum)[..., None], (b, N_HEADS, lp['fox'], LANES))
    y_fox = _fox_attention(p['fox_q_t'], rows_with_past(p['fox_k_bf'], [fox_k0], lp['fox']),
                           cols_with_past(p['fox_v_t'], fox_v0, lp['fox']), c_lanes, p_len, tq, tk['fox'])

    top_k = min(DSA_TOP_K, n_keys // 4)
    kk_old = [dsa_k0, dsa_k0, dsa_ki0, dsa_ki0]
    y_dsa = _dsa_attention(p['dsa_q_t'], p['idx_q_t'], p['idx_w_t'], rows_with_past(p['dsa_kk_bf'], kk_old, lp['dsa']),
                           cols_with_past(p['dsa_v_t'], dsa_v0, lp['dsa']), p_len, n_keys, tq, tk['dsa'], top_k)

    h = _merge(h.reshape(b, t, d), (y_sb, y_ret, y_fox, y_dsa), w['w_gate'], w['w_branch'], w['w_out'],
               w['ln1_g'], w['ln1_b'], alpha)
    h = _ffn(h.reshape(m, d), w['w_ffn_in'], w['w_ffn_out'], w['ln2_g'], w['ln2_b'], alpha)
    return h, states, _swap(ret_state_t), fox_lf


def _group_outputs(states, ret_states, fox_lfs):
    def heads(a):
        return a.reshape(a.shape[:-1] + (N_HEADS, HEAD_DIM))

    return (heads(states['sb_k']), heads(states['sb_v']), jnp.stack(ret_states), heads(states['fox_k']),
            heads(states['fox_v']), jnp.stack(fox_lfs), states['dsa_k'], states['dsa_v'], states['idx_k'])


def kernel(x_prompt, x_sample, cache_sb_k, cache_sb_v, state_ret, cache_fox_k, cache_fox_v, cache_fox_logf,
           cache_dsa_k, cache_dsa_v, cache_dsa_kidx, w_in, b_forget, w_branch, w_out, ln1_g, ln1_b,
           w_ffn_in, w_ffn_out, ln2_g, ln2_b):
    depth = w_in.shape[0]
    alpha = float((2 * depth) ** 0.25)
    bp, tp, d = x_prompt.shape
    bs, ts, _ = x_sample.shape
    hp = x_prompt.reshape(bp * tp, d)
    hs = x_sample.reshape(bs * ts, d)
    ret_zero = jnp.zeros((bp, N_HEADS, HEAD_DIM, HEAD_DIM), F32)
    st_p, st_s, ret_p, ret_s, lf_p, lf_s = None, None, [], [], [], []
    for l in range(depth):
        w_row, w_col, w_gate = _inproj_weights(w_in[l])
        w = dict(w_row=w_row, w_col=w_col, w_gate=w_gate, b_forget=b_forget[l], w_branch=w_branch[l].astype(BF),
                 w_out=w_out[l].astype(BF), ln1_g=ln1_g[l][None], ln1_b=ln1_b[l][None],
                 w_ffn_in=w_ffn_in[l].astype(BF), w_ffn_out=w_ffn_out[l].astype(BF),
                 ln2_g=ln2_g[l][None], ln2_b=ln2_b[l][None])
        hp, st_p, ret, lf = _layer(hp, bp, tp, None, ret_zero, w, alpha, l, depth, st_p)
        ret_p.append(ret)
        lf_p.append(lf)
        past = (cache_sb_k[l], cache_sb_v[l], cache_fox_k[l], cache_fox_v[l], cache_fox_logf[l],
                cache_dsa_k[l], cache_dsa_v[l], cache_dsa_kidx[l])
        hs, st_s, ret, lf = _layer(hs, bs, ts, past, state_ret[l], w, alpha, l, depth, st_s)
        ret_s.append(ret)
        lf_s.append(lf)
    return ((hp.reshape(bp, tp, d), hs.reshape(bs, ts, d))
            + _group_outputs(st_p, ret_p, lf_p) + _group_outputs(st_s, ret_s, lf_s))
```

```python
import functools
import math

import numpy as np
import jax
import jax.numpy as jnp
from jax import lax
from jax.experimental import pallas as pl
from jax.experimental.pallas import tpu as pltpu

HEAD_DIM = 64
N_HEADS = 4
BRANCH_WIDTH = N_HEADS * HEAD_DIM
CHUNK_SHIFT = 6
DSA_TOP_K = 256
ROPE_BASE = 10000.0
LN_EPS = 1e-5
QK_SCALE = HEAD_DIM ** -0.5
IDX_HEAD_SCALE = N_HEADS ** -0.5
MASK_VALUE = -1e30
INT_MIN = -2 ** 31

V7X_VMEM_LIMIT_BYTES = 56 * 1024 * 1024
LANES = 128
HEADS_PER_COL = LANES // HEAD_DIM
COUNT_SLAB = 64
MERGE_ROWS = 512
FFN_ROWS = 512

BF = jnp.bfloat16
F32 = jnp.float32


def _dot(a, b):
    return jnp.dot(a, b, preferred_element_type=F32)


def _dot_nt(a, b):
    return lax.dot_general(a, b, (((1,), (1,)), ((), ())), preferred_element_type=F32)


def _dot_tn(a, b):
    return lax.dot_general(a, b, (((0,), (0,)), ((), ())), preferred_element_type=F32)


def _params(*sem):
    return pltpu.CompilerParams(dimension_semantics=sem, vmem_limit_bytes=V7X_VMEM_LIMIT_BYTES)


def _const_spec(shape):
    nd = len(shape)
    return pl.BlockSpec(shape, lambda *_: (0,) * nd)


def _layer_norm(x, g, b):
    xc = x - jnp.mean(x, axis=-1, keepdims=True)
    var = jnp.mean(xc * xc, axis=-1, keepdims=True)
    return xc * lax.rsqrt(var + LN_EPS) * g + b


def _row_tile(m, want):
    t = min(m, want)
    assert m % t == 0
    return t


def _col(c):
    return slice(c * LANES, (c + 1) * LANES)


_NARROW_SRC = ('dsa_k', 'dsa_v', 'idx_k', 'fox_f')
_ROW_OUTS = (
    ('sb_k', ('sb_k',), F32, 0, BRANCH_WIDTH, True), ('sb_k_bf', ('sb_k',), BF, 0, BRANCH_WIDTH, False),
    ('sb_v', ('sb_v',), F32, 0, BRANCH_WIDTH, True),
    ('fox_k', ('fox_k',), F32, 0, BRANCH_WIDTH, True), ('fox_k_bf', ('fox_k',), BF, 0, BRANCH_WIDTH, False),
    ('fox_v', ('fox_v',), F32, 0, BRANCH_WIDTH, True),
    ('dsa_k', _NARROW_SRC, F32, 0, HEAD_DIM, True), ('dsa_v', _NARROW_SRC, F32, HEAD_DIM, HEAD_DIM, True),
    ('idx_k', _NARROW_SRC, F32, 2 * HEAD_DIM, HEAD_DIM, True), ('fox_f', _NARROW_SRC, F32, 3 * HEAD_DIM, N_HEADS, True),
    ('dsa_kk_bf', ('dsa_k', 'dsa_k', 'idx_k', 'idx_k'), BF, 0, 4 * HEAD_DIM, False),
)
_COL_OUTS = (
    ('sb_q_t', ('sb_q',), BF), ('fox_q_t', ('fox_q',), BF), ('dsa_q_t', ('dsa_q',), BF),
    ('idx_q_t', ('idx_q',), BF), ('sb_v_t', ('sb_v',), BF), ('fox_v_t', ('fox_v',), BF),
    ('ret_q_t', ('ret_q',), F32), ('ret_k_t', ('ret_k',), F32), ('ret_v_t', ('ret_v',), F32),
    ('ret_g_t', ('ret_g',), F32), ('dsa_v_t', ('dsa_v',), BF), ('idx_w_t', ('idx_w',), F32),
)
BF16_ROWS_PER_VREG = 16


def _inproj_plan(d):
    widths = dict(_in_layout(d))

    def spans(outs, multiple):
        span_of, off = {}, 0
        for out in outs:
            srcs = out[1]
            if srcs not in span_of:
                w = -(-sum(widths[s] for s in srcs) // multiple) * multiple
                span_of[srcs] = (off, w)
                off += w
        return span_of, off

    row_spans, _ = spans(_ROW_OUTS, LANES)
    col_spans, n_col = spans(_COL_OUTS, BF16_ROWS_PER_VREG)
    return row_spans, col_spans, -(-n_col // LANES) * LANES


def _inproj_weights(w_in):
    d = w_in.shape[0]
    pieces, off = {}, 0
    for name, width in _in_layout(d):
        pieces[name] = w_in[:, off:off + width] * _FOLDED_SCALE.get(name, 1.0)
        off += width
    assert off == w_in.shape[1]
    row_spans, col_spans, n_col = _inproj_plan(d)

    def block(srcs, width):
        w = jnp.concatenate([pieces[s] for s in srcs], axis=1)
        return jnp.pad(w, ((0, 0), (0, width - w.shape[1])))

    w_row = jnp.concatenate([block(srcs, w) for srcs, (_, w) in row_spans.items()], axis=1)
    w_col = jnp.concatenate([block(srcs, w) for srcs, (_, w) in col_spans.items()], axis=1)
    w_col = jnp.pad(w_col, ((0, 0), (0, n_col - w_col.shape[1])))
    return w_row.astype(BF), w_col.T.astype(BF), pieces['merge_gate'].astype(BF)


def _inproj_body(x_ref, wr_ref, wc_ref, *refs, row_spans, col_spans, n_alias):
    o_refs = refs[n_alias:]
    xb = x_ref[0].astype(BF)
    done = {}
    for o_ref, (_, srcs, _, lane, width, _) in zip(o_refs, _ROW_OUTS):
        if srcs not in done:
            off, w = row_spans[srcs]
            done[srcs] = _dot(xb, wr_ref[:, off:off + w])
        o_ref[...] = done[srcs][:, lane:lane + width].astype(o_ref.dtype).reshape(o_ref.shape)
    for o_ref, (_, srcs, _) in zip(o_refs[len(_ROW_OUTS):], _COL_OUTS):
        off, w = col_spans[srcs]
        o_ref[0] = _dot_nt(wc_ref[off:off + w, :], xb).astype(o_ref.dtype)


def _inproj(x, w_row, w_col, layer, depth, states):
    b, t, d = x.shape
    tm = _row_tile(t, 512)
    row_spans, col_spans, _ = _inproj_plan(d)
    out_shape, out_specs, state_names = [], [], []
    for name, _, dt, _, w, is_state in _ROW_OUTS:
        if is_state:
            state_names.append(name)
            out_shape.append(jax.ShapeDtypeStruct((depth, b, t, w), dt))
            out_specs.append(pl.BlockSpec((1, 1, tm, w), lambda bi, i: (layer, bi, i, 0)))
        else:
            out_shape.append(jax.ShapeDtypeStruct((b, t, w), dt))
            out_specs.append(pl.BlockSpec((1, tm, w), lambda bi, i: (bi, i, 0)))
    for _, srcs, dt in _COL_OUTS:
        w = col_spans[srcs][1]
        out_shape.append(jax.ShapeDtypeStruct((b, w, t), dt))
        out_specs.append(pl.BlockSpec((1, w, tm), lambda bi, i: (bi, 0, i)))
    prev = [] if states is None else [states[n] for n in state_names]
    names = [o[0] for o in _ROW_OUTS] + [o[0] for o in _COL_OUTS]
    aliases = {3 + j: names.index(n) for j, n in enumerate(state_names)} if prev else {}
    outs = pl.pallas_call(
        functools.partial(_inproj_body, row_spans=row_spans, col_spans=col_spans, n_alias=len(prev)),
        out_shape=out_shape,
        grid=(b, t // tm),
        in_specs=[pl.BlockSpec((1, tm, d), lambda bi, i: (bi, i, 0)),
                  _const_spec(w_row.shape), _const_spec(w_col.shape)]
                 + [pl.BlockSpec(memory_space=pl.ANY)] * len(prev),
        out_specs=out_specs,
        input_output_aliases=aliases,
        compiler_params=_params("parallel", "parallel"),
        name="inproj",
    )(x, w_row, w_col, *prev)
    return dict(zip(names, outs))


def _block_counts(q0, tk, last_key):
    return lax.div(q0, tk), lax.div(last_key, tk) + 1


def _head_queries(qt_ref):
    low = lax.broadcasted_iota(jnp.int32, (LANES, 1), 0) < HEAD_DIM
    out = []
    for h in range(N_HEADS):
        qc = qt_ref[0, _col(h // HEADS_PER_COL), :]
        keep = low if h % HEADS_PER_COL == 0 else jnp.logical_not(low)
        out.append(jnp.where(keep, qc, jnp.zeros_like(qc)))
    return out


def _head_rows(h):
    return slice(h * HEAD_DIM, (h + 1) * HEAD_DIM)


def _key_minus_query(tk, tq):
    return (lax.broadcasted_iota(jnp.int32, (tk, tq), 0) - lax.broadcasted_iota(jnp.int32, (tk, tq), 1))


def _qt_spec(w, tq):
    return pl.BlockSpec((1, w, tq), lambda bi, qi: (bi, 0, qi))


def _whole_spec(rows, cols):
    return pl.BlockSpec((1, rows, cols), lambda bi, qi: (bi, 0, 0))


def _sb_body(qt_ref, k_ref, vt_ref, o_ref, *, p_len, tq, tk):
    q0 = p_len + pl.program_id(1) * tq
    qm = _head_queries(qt_ref)
    diff = _key_minus_query(tk, tq)
    later = (lax.broadcasted_iota(jnp.int32, (tk, 2 * tk), 1) & (tk - 1)) > lax.broadcasted_iota(
        jnp.int32, (tk, 2 * tk), 0)
    minus_later = jnp.where(later, -1.0, 0.0).astype(BF)
    n_full, n_all = _block_counts(q0, tk, jnp.maximum(q0 + tq - 2, 0))

    def step(kb, carry, masked):
        laters, accs = carry
        s0 = pl.multiple_of(kb * tk, tk)
        if masked:
            earlier = diff < (q0 - s0)
        zs = [_dot(k_ref[0, pl.ds(s0, tk), _col(h // HEADS_PER_COL)], qm[h]) for h in range(N_HEADS)]
        new_laters, log_bs, afters = [], [], []
        for h in range(N_HEADS):
            z = zs[h]
            minus_abs = pltpu.bitcast(pltpu.bitcast(z, jnp.int32) | jnp.int32(INT_MIN), F32)
            softplus = jnp.maximum(z, 0.0) + jnp.log(1.0 + jnp.exp(minus_abs))
            log_bs.append(z - softplus)
            if masked:
                softplus = jnp.where(earlier, softplus, 0.0)
            hi = softplus.astype(BF)
            lo = (softplus - hi.astype(F32)).astype(BF)
            afters.append(_dot(minus_later, jnp.concatenate([hi, lo], axis=0)) + laters[h])
            new_laters.append(laters[h] - jnp.sum(softplus, axis=0, keepdims=True))
        new_accs = []
        for h in range(N_HEADS):
            w = jnp.exp(log_bs[h] + afters[h])
            if masked:
                w = jnp.where(earlier, w, 0.0)
            new_accs.append(accs[h] + _dot(vt_ref[0, _head_rows(h), pl.ds(s0, tk)], w.astype(BF)))
        return tuple(new_laters), tuple(new_accs)

    carry = (tuple(jnp.zeros((1, tq), F32) for _ in range(N_HEADS)),
             tuple(jnp.zeros((HEAD_DIM, tq), F32) for _ in range(N_HEADS)))
    carry = lax.fori_loop(0, n_all - n_full, lambda i, c: step(n_all - 1 - i, c, True), carry)
    carry = lax.fori_loop(0, n_full, lambda i, c: step(n_full - 1 - i, c, False), carry)
    for h in range(N_HEADS):
        o_ref[0, _head_rows(h), :] = carry[1][h].astype(o_ref.dtype)


def _sb_attention(qt, k, vt, p_len, tq, tk):
    b, w, t = qt.shape
    lp = k.shape[1]
    assert tk & (tk - 1) == 0
    return pl.pallas_call(
        functools.partial(_sb_body, p_len=p_len, tq=tq, tk=tk),
        out_shape=jax.ShapeDtypeStruct((b, w, t), BF),
        grid=(b, t // tq),
        in_specs=[_qt_spec(w, tq), _whole_spec(lp, w), _whole_spec(w, lp)],
        out_specs=_qt_spec(w, tq),
        compiler_params=_params("parallel", "arbitrary"),
        name="sb_attention",
    )(qt, k, vt)


def _online_softmax_step(logits, m, l):
    m_new = jnp.maximum(m, jnp.max(logits, axis=0, keepdims=True))
    alpha = jnp.exp(m - m_new)
    p = jnp.exp(logits - m_new)
    return m_new, alpha, alpha * l + jnp.sum(p, axis=0, keepdims=True), p


def _fox_body(qt_ref, k_ref, vt_ref, c_ref, o_ref, *, p_len, tq, tk):
    q0 = p_len + pl.program_id(1) * tq
    qm = _head_queries(qt_ref)
    diff = _key_minus_query(tk, tq)
    n_full, n_all = _block_counts(q0, tk, q0 + tq - 1)

    def key_bias(h, s0):
        c = c_ref[0, h, pl.ds(s0, tk), :]
        return c[:, :tq] if tq <= LANES else jnp.concatenate([c] * (tq // LANES), axis=1)

    def step(kb, carry, masked):
        ms, ls, accs = carry
        s0 = pl.multiple_of(kb * tk, tk)
        if masked:
            visible = diff <= (q0 - s0)
        new_ms, new_ls, new_accs = [], [], []
        raw = [_dot(k_ref[0, pl.ds(s0, tk), _col(h // HEADS_PER_COL)], qm[h]) for h in range(N_HEADS)]
        for h in range(N_HEADS):
            logits = raw[h] - key_bias(h, s0)
            if masked:
                logits = jnp.where(visible, logits, MASK_VALUE)
            m_new, alpha, l_new, p = _online_softmax_step(logits, ms[h], ls[h])
            new_ms.append(m_new)
            new_ls.append(l_new)
            new_accs.append(alpha * accs[h] + _dot(vt_ref[0, _head_rows(h), pl.ds(s0, tk)], p.astype(BF)))
        return tuple(new_ms), tuple(new_ls), tuple(new_accs)

    carry = (tuple(jnp.full((1, tq), -jnp.inf, F32) for _ in range(N_HEADS)),
             tuple(jnp.zeros((1, tq), F32) for _ in range(N_HEADS)),
             tuple(jnp.zeros((HEAD_DIM, tq), F32) for _ in range(N_HEADS)))
    carry = lax.fori_loop(0, n_full, lambda i, c: step(i, c, False), carry)
    carry = lax.fori_loop(n_full, n_all, lambda i, c: step(i, c, True), carry)
    _, ls, accs = carry
    for h in range(N_HEADS):
        o_ref[0, _head_rows(h), :] = (accs[h] / ls[h]).astype(o_ref.dtype)


def _fox_attention(qt, k, vt, c_lanes, p_len, tq, tk):
    b, w, t = qt.shape
    lp = k.shape[1]
    assert tq <= LANES or tq % LANES == 0
    return pl.pallas_call(
        functools.partial(_fox_body, p_len=p_len, tq=tq, tk=tk),
        out_shape=jax.ShapeDtypeStruct((b, w, t), BF),
        grid=(b, t // tq),
        in_specs=[_qt_spec(w, tq), _whole_spec(lp, w), _whole_spec(w, lp),
                  pl.BlockSpec((1, N_HEADS, lp, LANES), lambda bi, qi: (bi, 0, 0, 0))],
        out_specs=_qt_spec(w, tq),
        compiler_params=_params("parallel", "arbitrary"),
        name="fox_attention",
    )(qt, k, vt, c_lanes)


def _float_key(bits):
    return jnp.where(bits < 0, jnp.int32(INT_MIN) - bits, bits)


def _dsa_body(qt_ref, qit_ref, wit_ref, k_ref, vt_ref, ki_ref, o_ref, keys_ref, *, p_len, n_keys, tq, tk, top_k):
    q0 = p_len + pl.program_id(1) * tq
    qpos = q0 + lax.broadcasted_iota(jnp.int32, (1, tq), 1)
    limit = jnp.minimum(((qpos >> CHUNK_SHIFT) + 1) << CHUNK_SHIFT, n_keys)
    last_limit = jnp.minimum((((q0 + tq - 1) >> CHUNK_SHIFT) + 1) << CHUNK_SHIFT, n_keys)
    n_blk = lax.div(last_limit - 1, tk) + 1
    key_row = lax.broadcasted_iota(jnp.int32, (tk, tq), 0)

    wit = wit_ref[0]
    qim = _head_queries(qit_ref)

    def score_step(kb, _):
        s0 = pl.multiple_of(kb * tk, tk)
        ki = ki_ref[0, pl.ds(s0, tk), :]
        score = jnp.zeros((tk, tq), F32)
        for h in range(N_HEADS):
            score = score + wit[h:h + 1, :] * jnp.maximum(_dot(ki, qim[h]), 0.0)
        score = jnp.where(key_row < limit - s0, score, -jnp.inf)
        keys_ref[pl.ds(s0, tk), :] = _float_key(pltpu.bitcast(score, jnp.int32))
        return 0

    lax.fori_loop(0, n_blk, score_step, 0)

    def count(pred):
        def body(kb, acc):
            for r in range(0, tk, COUNT_SLAB):
                s0 = pl.multiple_of(kb * tk + r, COUNT_SLAB)
                acc = acc + jnp.where(pred(keys_ref[pl.ds(s0, COUNT_SLAB), :]), 1.0, 0.0)
            return acc
        acc = lax.fori_loop(0, n_blk, body, jnp.zeros((COUNT_SLAB, tq), F32))
        return jnp.sum(acc, axis=0, keepdims=True)

    kf = jnp.float32(top_k)
    zero = jnp.zeros((1, tq), jnp.int32)
    thr = jnp.where(count(lambda key: key >= zero) >= kf, zero, jnp.int32(INT_MIN))

    def bit_step(it, thr):
        cand = thr + lax.shift_left(jnp.int32(1), 30 - it)
        return jnp.where(count(lambda key: key >= cand) >= kf, cand, thr)

    thr = lax.fori_loop(0, 31, bit_step, thr)
    n_tie_wanted = kf - count(lambda key: key > thr)

    earlier_keys = (lax.broadcasted_iota(jnp.int32, (tk, tk), 1)
                    < lax.broadcasted_iota(jnp.int32, (tk, tk), 0)).astype(BF)
    qm = _head_queries(qt_ref)

    def attend_step(kb, carry):
        ties_seen, ms, ls, accs = carry
        s0 = pl.multiple_of(kb * tk, tk)
        key = keys_ref[pl.ds(s0, tk), :]
        tie = jnp.where(key == thr, 1.0, 0.0)
        tie_rank = _dot(earlier_keys, tie.astype(BF)) + ties_seen
        take = jnp.where(key > thr, 1.0, jnp.where(tie_rank < n_tie_wanted, tie, 0.0))
        selected = jnp.where(key_row < limit - s0, take, 0.0) > 0.0
        k = k_ref[0, pl.ds(s0, tk), :]
        vt = vt_ref[0, :, pl.ds(s0, tk)]
        new_ms, new_ls, new_accs = [], [], []
        raw = [_dot(k, qm[h]) for h in range(N_HEADS)]
        for h in range(N_HEADS):
            logits = jnp.where(selected, raw[h], MASK_VALUE)
            m_new, alpha, l_new, p = _online_softmax_step(logits, ms[h], ls[h])
            new_ms.append(m_new)
            new_ls.append(l_new)
            new_accs.append(alpha * accs[h] + _dot(vt, p.astype(BF)))
        ties_seen = ties_seen + jnp.sum(tie, axis=0, keepdims=True)
        return ties_seen, tuple(new_ms), tuple(new_ls), tuple(new_accs)

    carry = (jnp.zeros((1, tq), F32),
             tuple(jnp.full((1, tq), -jnp.inf, F32) for _ in range(N_HEADS)),
             tuple(jnp.zeros((1, tq), F32) for _ in range(N_HEADS)),
             tuple(jnp.zeros((HEAD_DIM, tq), F32) for _ in range(N_HEADS)))
    _, _, ls, accs = lax.fori_loop(0, n_blk, attend_step, carry)
    for h in range(N_HEADS):
        o_ref[0, _head_rows(h), :] = (accs[h] / ls[h]).astype(o_ref.dtype)


def _dsa_attention(qt, qit, wit, kk, vt, p_len, n_keys, tq, tk, top_k):
    b, w, t = qt.shape
    lp = kk.shape[1]
    return pl.pallas_call(
        functools.partial(_dsa_body, p_len=p_len, n_keys=n_keys, tq=tq, tk=tk, top_k=top_k),
        out_shape=jax.ShapeDtypeStruct((b, w, t), BF),
        grid=(b, t // tq),
        in_specs=[_qt_spec(w, tq), _qt_spec(w, tq), _qt_spec(wit.shape[1], tq),
                  pl.BlockSpec((1, lp, LANES), lambda bi, qi: (bi, 0, 0)), _whole_spec(HEAD_DIM, lp),
                  pl.BlockSpec((1, lp, LANES), lambda bi, qi: (bi, 0, 1))],
        out_specs=_qt_spec(w, tq),
        scratch_shapes=[pltpu.VMEM((lp, tq), jnp.int32)],
        compiler_params=_params("parallel", "arbitrary"),
        name="dsa_attention",
    )(qt, qit, wit, kk, vt, kk)


def _ret_body(q_ref, k_ref, v_ref, g_ref, s0_ref, cos_ref, sin_ref, dec_ref, qd_ref, kd_ref, sd_ref,
              o_ref, so_ref, state_ref):
    c = pl.program_id(2)

    @pl.when(c == 0)
    def _():
        state_ref[...] = s0_ref[0, 0]

    cos, sin = cos_ref[...], sin_ref[...]
    half = HEAD_DIM // 2

    def rotary(x):
        x1, x2 = x[:half], x[half:]
        return jnp.concatenate([x1 * cos - x2 * sin, x2 * cos + x1 * sin], axis=0)

    qb = rotary(q_ref[0]).astype(BF)
    k = rotary(k_ref[0])
    vb = v_ref[0].astype(BF)
    state = state_ref[...]
    scores_t = _dot_tn(k.astype(BF), qb) * dec_ref[0]
    o = _dot(vb, scores_t.astype(BF)) + _dot(state.astype(BF), qb) * qd_ref[0]
    state = sd_ref[0] * state + _dot_nt(vb, (k * kd_ref[0]).astype(BF))
    state_ref[...] = state
    oc = o - jnp.mean(o, axis=0, keepdims=True)
    on = oc * lax.rsqrt(jnp.mean(oc * oc, axis=0, keepdims=True) + LN_EPS)
    g = g_ref[0]
    o_ref[0] = (on * (g * jax.nn.sigmoid(g))).astype(o_ref.dtype)

    @pl.when(c == pl.num_programs(2) - 1)
    def _():
        so_ref[0, 0] = state


def _retention(qt, kt, vt, gt, state0_t, pos, c):
    b, w, t = qt.shape
    h = w // HEAD_DIM
    half = HEAD_DIM // 2
    inv_freq = ROPE_BASE ** (-jnp.arange(half, dtype=F32) / half)
    ang = inv_freq[:, None] * pos.astype(F32)[None, :]
    log_gamma = np.log(1.0 - 2.0 ** (-5.0 - np.arange(h, dtype=np.float64)))
    n = np.arange(c, dtype=np.float64)
    rel = n[None, :] - n[:, None]
    decay_t = np.where(rel >= 0, np.exp(np.maximum(rel, 0.0)[None] * log_gamma[:, None, None]), 0.0)
    q_decay = np.exp((n[None, :] + 1.0) * log_gamma[:, None])[:, None, :]
    k_decay = np.exp((c - 1.0 - n)[None, :] * log_gamma[:, None])[:, None, :]
    s_decay = np.exp(c * log_gamma)[:, None, None]
    tables = [jnp.asarray(a, F32) for a in (decay_t, q_decay, k_decay, s_decay)]
    x_spec = pl.BlockSpec((1, HEAD_DIM, c), lambda bi, hi, ci: (bi, hi, ci))
    s_spec = pl.BlockSpec((1, 1, HEAD_DIM, HEAD_DIM), lambda bi, hi, ci: (bi, hi, 0, 0))
    rope_spec = pl.BlockSpec((half, c), lambda bi, hi, ci: (0, ci))

    def t_spec(a):
        return pl.BlockSpec((1,) + a.shape[1:], lambda bi, hi, ci: (hi, 0, 0))

    return pl.pallas_call(
        _ret_body,
        out_shape=(jax.ShapeDtypeStruct((b, w, t), BF),
                   jax.ShapeDtypeStruct((b, h, HEAD_DIM, HEAD_DIM), F32)),
        grid=(b, h, t // c),
        in_specs=[x_spec, x_spec, x_spec, x_spec, s_spec, rope_spec, rope_spec] + [t_spec(a) for a in tables],
        out_specs=(x_spec, s_spec),
        scratch_shapes=[pltpu.VMEM((HEAD_DIM, HEAD_DIM), F32)],
        compiler_params=_params("parallel", "parallel", "arbitrary"),
        name="retention",
    )(qt, kt, vt, gt, state0_t, jnp.cos(ang), jnp.sin(ang), *tables)


def _merge_body(h_ref, y0_ref, y1_ref, y2_ref, y3_ref, wg_ref, wb_ref, wo_ref, g_ref, b_ref, o_ref, *, alpha):
    nb, tm, d = h_ref.shape
    h = h_ref[...].reshape(nb * tm, d)
    hb = h.astype(BF)
    merged = jnp.zeros(h.shape, F32)
    for i, y_ref in enumerate((y0_ref, y1_ref, y2_ref, y3_ref)):
        gate = jax.nn.sigmoid(_dot(hb, wg_ref[:, i * d:(i + 1) * d]))
        branch = jnp.concatenate([_dot_tn(y_ref[j], wb_ref[i]) for j in range(nb)], axis=0)
        merged = merged + gate * branch
    r = alpha * h + _dot(merged.astype(BF), wo_ref[...])
    o_ref[...] = _layer_norm(r, g_ref[...], b_ref[...]).reshape(nb, tm, d)


def _merge(h, ys_t, w_gate, w_branch, w_out, ln_g, ln_b, alpha):
    b, t, d = h.shape
    tm = _row_tile(t, MERGE_ROWS)
    nb = math.gcd(b, max(1, MERGE_ROWS // tm))
    row = pl.BlockSpec((nb, tm, d), lambda bi, i: (bi, i, 0))
    col = pl.BlockSpec((nb, BRANCH_WIDTH, tm), lambda bi, i: (bi, 0, i))
    return pl.pallas_call(
        functools.partial(_merge_body, alpha=alpha),
        out_shape=jax.ShapeDtypeStruct((b, t, d), F32),
        grid=(b // nb, t // tm),
        in_specs=[row] + [col] * 4
                 + [_const_spec(w_gate.shape), _const_spec(w_branch.shape), _const_spec(w_out.shape),
                    _const_spec((1, d)), _const_spec((1, d))],
        out_specs=row,
        compiler_params=_params("parallel", "parallel"),
        name="merge",
    )(h, *ys_t, w_gate, w_branch, w_out, ln_g, ln_b)


def _ffn_body(h_ref, wi_ref, wo_ref, g_ref, b_ref, o_ref, *, alpha, f_chunk):
    h = h_ref[...]
    hb = h.astype(BF)
    f = wo_ref.shape[0]
    acc = jnp.zeros(h.shape, F32)
    for c in range(0, f, f_chunk):
        a = _dot(hb, wi_ref[:, c:c + f_chunk])
        u = _dot(hb, wi_ref[:, f + c:f + c + f_chunk])
        acc = acc + _dot((a * jax.nn.sigmoid(a) * u).astype(BF), wo_ref[c:c + f_chunk, :])
    o_ref[...] = _layer_norm(alpha * h + acc, g_ref[...], b_ref[...])


def _ffn(h, w_in, w_out, ln_g, ln_b, alpha):
    m, d = h.shape
    f = w_out.shape[0]
    tm = _row_tile(m, FFN_ROWS)
    f_chunk = f // 2 if (f // 2) % LANES == 0 else f
    row = pl.BlockSpec((tm, d), lambda i: (i, 0))
    return pl.pallas_call(
        functools.partial(_ffn_body, alpha=alpha, f_chunk=f_chunk),
        out_shape=jax.ShapeDtypeStruct((m, d), F32),
        grid=(m // tm,),
        in_specs=[row, _const_spec(w_in.shape), _const_spec(w_out.shape),
                  _const_spec((1, d)), _const_spec((1, d))],
        out_specs=row,
        compiler_params=_params("parallel"),
        name="ffn",
    )(h, w_in, w_out, ln_g, ln_b)


def _in_layout(d):
    w = BRANCH_WIDTH
    return (('sb_q', w), ('sb_k', w), ('sb_v', w), ('ret_q', w), ('ret_k', w), ('ret_v', w), ('ret_g', w),
            ('fox_q', w), ('fox_k', w), ('fox_v', w), ('fox_f', N_HEADS),
            ('dsa_q', w), ('dsa_k', HEAD_DIM), ('dsa_v', HEAD_DIM),
            ('idx_q', w), ('idx_k', HEAD_DIM), ('idx_w', N_HEADS), ('merge_gate', 4 * d))


_FOLDED_SCALE = dict(sb_q=QK_SCALE, fox_q=QK_SCALE, dsa_q=QK_SCALE, idx_q=QK_SCALE, ret_k=QK_SCALE,
                     idx_w=IDX_HEAD_SCALE)


def _swap(a):
    return jnp.swapaxes(a, -1, -2)


def _key_tiles(t, n_keys):
    tq = min(t, 512)
    tiles = dict(sb=256, fox=512, dsa=512)
    padded = {name: -(-n_keys // tk) * tk for name, tk in tiles.items()}
    return tq, tiles, padded


def _layer(h, b, t, past, ret_state, w, alpha, layer, depth, states):
    m, d = h.shape
    p_len = 0 if past is None else past[0].shape[1]
    n_keys = p_len + t
    tq, tk, lp = _key_tiles(t, n_keys)
    p = _inproj(h.reshape(b, t, d), w['w_row'], w['w_col'], layer, depth, states)
    states = {name: p[name] for name, _, _, _, _, is_state in _ROW_OUTS if is_state}

    old = (None,) * 8 if past is None else past
    sb_k0, sb_v0, fox_k0, fox_v0, fox_lf0, dsa_k0, dsa_v0, dsa_ki0 = old

    def rows_with_past(new_bf, olds, lp_):
        if past is not None:
            flat = [o.reshape(o.shape[0], o.shape[1], -1).astype(BF) for o in olds]
            new_bf = jnp.concatenate([jnp.concatenate(flat, axis=2), new_bf], axis=1)
        return jnp.pad(new_bf, ((0, 0), (0, lp_ - new_bf.shape[1]), (0, 0)))

    def cols_with_past(new_t, old, lp_):
        if old is not None:
            new_t = jnp.concatenate([_swap(old.reshape(old.shape[0], old.shape[1], -1).astype(BF)), new_t], axis=2)
        return jnp.pad(new_t, ((0, 0), (0, 0), (0, lp_ - new_t.shape[2])))

    y_sb = _sb_attention(p['sb_q_t'], rows_with_past(p['sb_k_bf'], [sb_k0], lp['sb']),
                         cols_with_past(p['sb_v_t'], sb_v0, lp['sb']), p_len, tq, tk['sb'])

    pos = p_len + jnp.arange(t, dtype=jnp.int32)
    y_ret, ret_state_t = _retention(p['ret_q_t'], p['ret_k_t'], p['ret_v_t'], p['ret_g_t'], _swap(ret_state),
                                    pos, min(t, 256))

    fox_lf = jax.nn.log_sigmoid(states['fox_f'][layer] + w['b_forget'])
    lf_all = fox_lf if fox_lf0 is None else jnp.concatenate([fox_lf0, fox_lf], axis=1)
    cum = jnp.pad(jnp.cumsum(lf_all, axis=1), ((0, 0), (0, lp['fox'] - n_keys), (0, 0)))
    c_lanes = jnp.broadcast_to(_swap(cum)[..., None], (b, N_HEADS, lp['fox'], LANES))
    y_fox = _fox_attention(p['fox_q_t'], rows_with_past(p['fox_k_bf'], [fox_k0], lp['fox']),
                           cols_with_past(p['fox_v_t'], fox_v0, lp['fox']), c_lanes, p_len, tq, tk['fox'])

    top_k = min(DSA_TOP_K, n_keys // 4)
    kk_old = [dsa_k0, dsa_k0, dsa_ki0, dsa_ki0]
    y_dsa = _dsa_attention(p['dsa_q_t'], p['idx_q_t'], p['idx_w_t'], rows_with_past(p['dsa_kk_bf'], kk_old, lp['dsa']),
                           cols_with_past(p['dsa_v_t'], dsa_v0, lp['dsa']), p_len, n_keys, tq, tk['dsa'], top_k)

    h = _merge(h.reshape(b, t, d), (y_sb, y_ret, y_fox, y_dsa), w['w_gate'], w['w_branch'], w['w_out'],
               w['ln1_g'], w['ln1_b'], alpha)
    h = _ffn(h.reshape(m, d), w['w_ffn_in'], w['w_ffn_out'], w['ln2_g'], w['ln2_b'], alpha)
    return h, states, _swap(ret_state_t), fox_lf


def _group_outputs(states, ret_states, fox_lfs):
    def heads(a):
        return a.reshape(a.shape[:-1] + (N_HEADS, HEAD_DIM))

    return (heads(states['sb_k']), heads(states['sb_v']), jnp.stack(ret_states), heads(states['fox_k']),
            heads(states['fox_v']), jnp.stack(fox_lfs), states['dsa_k'], states['dsa_v'], states['idx_k'])


def kernel(x_prompt, x_sample, cache_sb_k, cache_sb_v, state_ret, cache_fox_k, cache_fox_v, cache_fox_logf,
           cache_dsa_k, cache_dsa_v, cache_dsa_kidx, w_in, b_forget, w_branch, w_out, ln1_g, ln1_b,
           w_ffn_in, w_ffn_out, ln2_g, ln2_b):
    depth = w_in.shape[0]
    alpha = float((2 * depth) ** 0.25)
    bp, tp, d = x_prompt.shape
    bs, ts, _ = x_sample.shape
    hp = x_prompt.reshape(bp * tp, d)
    hs = x_sample.reshape(bs * ts, d)
    ret_zero = jnp.zeros((bp, N_HEADS, HEAD_DIM, HEAD_DIM), F32)
    st_p, st_s, ret_p, ret_s, lf_p, lf_s = None, None, [], [], [], []
    for l in range(depth):
        w_row, w_col, w_gate = _inproj_weights(w_in[l])
        w = dict(w_row=w_row, w_col=w_col, w_gate=w_gate, b_forget=b_forget[l], w_branch=w_branch[l].astype(BF),
                 w_out=w_out[l].astype(BF), ln1_g=ln1_g[l][None], ln1_b=ln1_b[l][None],
                 w_ffn_in=w_ffn_in[l].astype(BF), w_ffn_out=w_ffn_out[l].astype(BF),
                 ln2_g=ln2_g[l][None], ln2_b=ln2_b[l][None])
        hp, st_p, ret, lf = _layer(hp, bp, tp, None, ret_zero, w, alpha, l, depth, st_p)
        ret_p.append(ret)
        lf_p.append(lf)
        past = (cache_sb_k[l], cache_sb_v[l], cache_fox_k[l], cache_fox_v[l], cache_fox_logf[l],
                cache_dsa_k[l], cache_dsa_v[l], cache_dsa_kidx[l])
        hs, st_s, ret, lf = _layer(hs, bs, ts, past, state_ret[l], w, alpha, l, depth, st_s)
        ret_s.append(ret)
        lf_s.append(lf)
    return ((hp.reshape(bp, tp, d), hs.reshape(bs, ts, d))
            + _group_outputs(st_p, ret_p, lf_p) + _group_outputs(st_s, ret_s, lf_s))
```

```python
import functools
import math

import numpy as np
import jax
import jax.numpy as jnp
from jax import lax
from jax.experimental import pallas as pl
from jax.experimental.pallas import tpu as pltpu

HEAD_DIM = 64
N_HEADS = 4
BRANCH_WIDTH = N_HEADS * HEAD_DIM
CHUNK_SHIFT = 6
DSA_TOP_K = 256
ROPE_BASE = 10000.0
LN_EPS = 1e-5
QK_SCALE = HEAD_DIM ** -0.5
IDX_HEAD_SCALE = N_HEADS ** -0.5
MASK_VALUE = -1e30
INT_MIN = -2 ** 31
MINUS_INF_KEY = INT_MIN + 0x00800000

V7X_VMEM_LIMIT_BYTES = 56 * 1024 * 1024
LANES = 128
HEADS_PER_COL = LANES // HEAD_DIM
COUNT_SLAB = 64
MERGE_ROWS = 512
FFN_ROWS = 512

BF = jnp.bfloat16
F32 = jnp.float32


def _dot(a, b):
    return jnp.dot(a, b, preferred_element_type=F32)


def _dot_nt(a, b):
    return lax.dot_general(a, b, (((1,), (1,)), ((), ())), preferred_element_type=F32)


def _dot_tn(a, b):
    return lax.dot_general(a, b, (((0,), (0,)), ((), ())), preferred_element_type=F32)


def _params(*sem):
    return pltpu.CompilerParams(dimension_semantics=sem, vmem_limit_bytes=V7X_VMEM_LIMIT_BYTES)


def _const_spec(shape):
    nd = len(shape)
    return pl.BlockSpec(shape, lambda *_: (0,) * nd)


def _layer_norm(x, g, b):
    xc = x - jnp.mean(x, axis=-1, keepdims=True)
    var = jnp.mean(xc * xc, axis=-1, keepdims=True)
    return xc * lax.rsqrt(var + LN_EPS) * g + b


def _row_tile(m, want):
    t = min(m, want)
    assert m % t == 0
    return t


def _col(c):
    return slice(c * LANES, (c + 1) * LANES)


_NARROW_SRC = ('dsa_k', 'dsa_v', 'idx_k', 'fox_f')
_ROW_OUTS = (
    ('sb_k', ('sb_k',), F32, 0, BRANCH_WIDTH, True), ('sb_k_bf', ('sb_k',), BF, 0, BRANCH_WIDTH, False),
    ('sb_v', ('sb_v',), F32, 0, BRANCH_WIDTH, True),
    ('fox_k', ('fox_k',), F32, 0, BRANCH_WIDTH, True), ('fox_k_bf', ('fox_k',), BF, 0, BRANCH_WIDTH, False),
    ('fox_v', ('fox_v',), F32, 0, BRANCH_WIDTH, True),
    ('dsa_k', _NARROW_SRC, F32, 0, HEAD_DIM, True), ('dsa_v', _NARROW_SRC, F32, HEAD_DIM, HEAD_DIM, True),
    ('idx_k', _NARROW_SRC, F32, 2 * HEAD_DIM, HEAD_DIM, True), ('fox_f', _NARROW_SRC, F32, 3 * HEAD_DIM, N_HEADS, True),
    ('dsa_kk_bf', ('dsa_k', 'dsa_k', 'idx_k', 'idx_k'), BF, 0, 4 * HEAD_DIM, False),
)
_COL_OUTS = (
    ('sb_q_t', ('sb_q',), BF), ('fox_q_t', ('fox_q',), BF), ('dsa_q_t', ('dsa_q',), BF),
    ('idx_q_t', ('idx_q',), BF), ('sb_v_t', ('sb_v',), BF), ('fox_v_t', ('fox_v',), BF),
    ('ret_q_t', ('ret_q',), F32), ('ret_k_t', ('ret_k',), F32), ('ret_v_t', ('ret_v',), F32),
    ('ret_g_t', ('ret_g',), F32), ('dsa_v_t', ('dsa_v',), BF), ('idx_w_t', ('idx_w',), F32),
)
BF16_ROWS_PER_VREG = 16


def _inproj_plan(d):
    widths = dict(_in_layout(d))

    def spans(outs, multiple):
        span_of, off = {}, 0
        for out in outs:
            srcs = out[1]
            if srcs not in span_of:
                w = -(-sum(widths[s] for s in srcs) // multiple) * multiple
                span_of[srcs] = (off, w)
                off += w
        return span_of, off

    row_spans, _ = spans(_ROW_OUTS, LANES)
    col_spans, n_col = spans(_COL_OUTS, BF16_ROWS_PER_VREG)
    return row_spans, col_spans, -(-n_col // LANES) * LANES


def _inproj_weights(w_in):
    d = w_in.shape[0]
    pieces, off = {}, 0
    for name, width in _in_layout(d):
        pieces[name] = w_in[:, off:off + width] * _FOLDED_SCALE.get(name, 1.0)
        off += width
    assert off == w_in.shape[1]
    row_spans, col_spans, n_col = _inproj_plan(d)

    def block(srcs, width):
        w = jnp.concatenate([pieces[s] for s in srcs], axis=1)
        return jnp.pad(w, ((0, 0), (0, width - w.shape[1])))

    w_row = jnp.concatenate([block(srcs, w) for srcs, (_, w) in row_spans.items()], axis=1)
    w_col = jnp.concatenate([block(srcs, w) for srcs, (_, w) in col_spans.items()], axis=1)
    w_col = jnp.pad(w_col, ((0, 0), (0, n_col - w_col.shape[1])))
    return w_row.astype(BF), w_col.T.astype(BF), pieces['merge_gate'].astype(BF)


def _inproj_body(x_ref, wr_ref, wc_ref, *refs, row_spans, col_spans, n_alias):
    o_refs = refs[n_alias:]
    xb = x_ref[0].astype(BF)
    done = {}
    for o_ref, (_, srcs, _, lane, width, _) in zip(o_refs, _ROW_OUTS):
        if srcs not in done:
            off, w = row_spans[srcs]
            done[srcs] = _dot(xb, wr_ref[:, off:off + w])
        o_ref[...] = done[srcs][:, lane:lane + width].astype(o_ref.dtype).reshape(o_ref.shape)
    for o_ref, (_, srcs, _) in zip(o_refs[len(_ROW_OUTS):], _COL_OUTS):
        off, w = col_spans[srcs]
        o_ref[0] = _dot_nt(wc_ref[off:off + w, :], xb).astype(o_ref.dtype)


def _inproj(x, w_row, w_col, layer, depth, states):
    b, t, d = x.shape
    tm = _row_tile(t, 512)
    row_spans, col_spans, _ = _inproj_plan(d)
    out_shape, out_specs, state_names = [], [], []
    for name, _, dt, _, w, is_state in _ROW_OUTS:
        if is_state:
            state_names.append(name)
            out_shape.append(jax.ShapeDtypeStruct((depth, b, t, w), dt))
            out_specs.append(pl.BlockSpec((1, 1, tm, w), lambda bi, i: (layer, bi, i, 0)))
        else:
            out_shape.append(jax.ShapeDtypeStruct((b, t, w), dt))
            out_specs.append(pl.BlockSpec((1, tm, w), lambda bi, i: (bi, i, 0)))
    for _, srcs, dt in _COL_OUTS:
        w = col_spans[srcs][1]
        out_shape.append(jax.ShapeDtypeStruct((b, w, t), dt))
        out_specs.append(pl.BlockSpec((1, w, tm), lambda bi, i: (bi, 0, i)))
    prev = [] if states is None else [states[n] for n in state_names]
    names = [o[0] for o in _ROW_OUTS] + [o[0] for o in _COL_OUTS]
    aliases = {3 + j: names.index(n) for j, n in enumerate(state_names)} if prev else {}
    outs = pl.pallas_call(
        functools.partial(_inproj_body, row_spans=row_spans, col_spans=col_spans, n_alias=len(prev)),
        out_shape=out_shape,
        grid=(b, t // tm),
        in_specs=[pl.BlockSpec((1, tm, d), lambda bi, i: (bi, i, 0)),
                  _const_spec(w_row.shape), _const_spec(w_col.shape)]
                 + [pl.BlockSpec(memory_space=pl.ANY)] * len(prev),
        out_specs=out_specs,
        input_output_aliases=aliases,
        compiler_params=_params("parallel", "parallel"),
        name="inproj",
    )(x, w_row, w_col, *prev)
    return dict(zip(names, outs))


def _block_counts(q0, tk, last_key):
    return lax.div(q0, tk), lax.div(last_key, tk) + 1


def _head_queries(qt_ref):
    low = lax.broadcasted_iota(jnp.int32, (LANES, 1), 0) < HEAD_DIM
    out = []
    for h in range(N_HEADS):
        qc = qt_ref[0, _col(h // HEADS_PER_COL), :]
        keep = low if h % HEADS_PER_COL == 0 else jnp.logical_not(low)
        out.append(jnp.where(keep, qc, jnp.zeros_like(qc)))
    return out


def _head_rows(h):
    return slice(h * HEAD_DIM, (h + 1) * HEAD_DIM)


def _key_minus_query(tk, tq):
    return (lax.broadcasted_iota(jnp.int32, (tk, tq), 0) - lax.broadcasted_iota(jnp.int32, (tk, tq), 1))


def _qt_spec(w, tq):
    return pl.BlockSpec((1, w, tq), lambda bi, qi: (bi, 0, qi))


def _whole_spec(rows, cols):
    return pl.BlockSpec((1, rows, cols), lambda bi, qi: (bi, 0, 0))


def _sb_body(qt_ref, k_ref, vt_ref, o_ref, *, p_len, tq, tk):
    q0 = p_len + pl.program_id(1) * tq
    qm = _head_queries(qt_ref)
    diff = _key_minus_query(tk, tq)
    later = (lax.broadcasted_iota(jnp.int32, (tk, 2 * tk), 1) & (tk - 1)) > lax.broadcasted_iota(
        jnp.int32, (tk, 2 * tk), 0)
    minus_later = jnp.where(later, -1.0, 0.0).astype(BF)
    n_full, n_all = _block_counts(q0, tk, jnp.maximum(q0 + tq - 2, 0))

    def step(kb, carry, masked):
        laters, accs = carry
        s0 = pl.multiple_of(kb * tk, tk)
        if masked:
            earlier = diff < (q0 - s0)
        zs = [_dot(k_ref[0, pl.ds(s0, tk), _col(h // HEADS_PER_COL)], qm[h]) for h in range(N_HEADS)]
        new_laters, log_bs, afters = [], [], []
        for h in range(N_HEADS):
            z = zs[h]
            minus_abs = pltpu.bitcast(pltpu.bitcast(z, jnp.int32) | jnp.int32(INT_MIN), F32)
            softplus = jnp.maximum(z, 0.0) + jnp.log(1.0 + jnp.exp(minus_abs))
            log_bs.append(z - softplus)
            if masked:
                softplus = jnp.where(earlier, softplus, 0.0)
            hi = softplus.astype(BF)
            lo = (softplus - hi.astype(F32)).astype(BF)
            after = _dot(minus_later, jnp.concatenate([hi, lo], axis=0)) + laters[h]
            afters.append(after)
            new_laters.append(after[0:1, :] - softplus[0:1, :])
        new_accs = []
        for h in range(N_HEADS):
            w = jnp.exp(log_bs[h] + afters[h])
            if masked:
                w = jnp.where(earlier, w, 0.0)
            new_accs.append(accs[h] + _dot(vt_ref[0, _head_rows(h), pl.ds(s0, tk)], w.astype(BF)))
        return tuple(new_laters), tuple(new_accs)

    carry = (tuple(jnp.zeros((1, tq), F32) for _ in range(N_HEADS)),
             tuple(jnp.zeros((HEAD_DIM, tq), F32) for _ in range(N_HEADS)))
    carry = lax.fori_loop(0, n_all - n_full, lambda i, c: step(n_all - 1 - i, c, True), carry)
    carry = lax.fori_loop(0, n_full, lambda i, c: step(n_full - 1 - i, c, False), carry)
    for h in range(N_HEADS):
        o_ref[0, _head_rows(h), :] = carry[1][h].astype(o_ref.dtype)


def _sb_attention(qt, k, vt, p_len, tq, tk):
    b, w, t = qt.shape
    lp = k.shape[1]
    assert tk & (tk - 1) == 0
    return pl.pallas_call(
        functools.partial(_sb_body, p_len=p_len, tq=tq, tk=tk),
        out_shape=jax.ShapeDtypeStruct((b, w, t), BF),
        grid=(b, t // tq),
        in_specs=[_qt_spec(w, tq), _whole_spec(lp, w), _whole_spec(w, lp)],
        out_specs=_qt_spec(w, tq),
        compiler_params=_params("parallel", "arbitrary"),
        name="sb_attention",
    )(qt, k, vt)


def _online_softmax_step(logits, m, l):
    m_new = jnp.maximum(m, jnp.max(logits, axis=0, keepdims=True))
    alpha = jnp.exp(m - m_new)
    p = jnp.exp(logits - m_new)
    return m_new, alpha, alpha * l + jnp.sum(p, axis=0, keepdims=True), p


def _fox_body(qt_ref, k_ref, vt_ref, c_ref, o_ref, *, p_len, tq, tk):
    q0 = p_len + pl.program_id(1) * tq
    qm = _head_queries(qt_ref)
    diff = _key_minus_query(tk, tq)
    n_full, n_all = _block_counts(q0, tk, q0 + tq - 1)

    def key_bias(h, s0):
        c = c_ref[0, h, pl.ds(s0, tk), :]
        return c[:, :tq] if tq <= LANES else jnp.concatenate([c] * (tq // LANES), axis=1)

    def step(kb, carry, masked):
        ms, ls, accs = carry
        s0 = pl.multiple_of(kb * tk, tk)
        if masked:
            visible = diff <= (q0 - s0)
        new_ms, new_ls, new_accs = [], [], []
        raw = [_dot(k_ref[0, pl.ds(s0, tk), _col(h // HEADS_PER_COL)], qm[h]) for h in range(N_HEADS)]
        for h in range(N_HEADS):
            logits = raw[h] - key_bias(h, s0)
            if masked:
                logits = jnp.where(visible, logits, MASK_VALUE)
            m_new, alpha, l_new, p = _online_softmax_step(logits, ms[h], ls[h])
            new_ms.append(m_new)
            new_ls.append(l_new)
            new_accs.append(alpha * accs[h] + _dot(vt_ref[0, _head_rows(h), pl.ds(s0, tk)], p.astype(BF)))
        return tuple(new_ms), tuple(new_ls), tuple(new_accs)

    carry = (tuple(jnp.full((1, tq), -jnp.inf, F32) for _ in range(N_HEADS)),
             tuple(jnp.zeros((1, tq), F32) for _ in range(N_HEADS)),
             tuple(jnp.zeros((HEAD_DIM, tq), F32) for _ in range(N_HEADS)))
    carry = lax.fori_loop(0, n_full, lambda i, c: step(i, c, False), carry)
    carry = lax.fori_loop(n_full, n_all, lambda i, c: step(i, c, True), carry)
    _, ls, accs = carry
    for h in range(N_HEADS):
        o_ref[0, _head_rows(h), :] = (accs[h] / ls[h]).astype(o_ref.dtype)


def _fox_attention(qt, k, vt, c_lanes, p_len, tq, tk):
    b, w, t = qt.shape
    lp = k.shape[1]
    assert tq <= LANES or tq % LANES == 0
    return pl.pallas_call(
        functools.partial(_fox_body, p_len=p_len, tq=tq, tk=tk),
        out_shape=jax.ShapeDtypeStruct((b, w, t), BF),
        grid=(b, t // tq),
        in_specs=[_qt_spec(w, tq), _whole_spec(lp, w), _whole_spec(w, lp),
                  pl.BlockSpec((1, N_HEADS, lp, LANES), lambda bi, qi: (bi, 0, 0, 0))],
        out_specs=_qt_spec(w, tq),
        compiler_params=_params("parallel", "arbitrary"),
        name="fox_attention",
    )(qt, k, vt, c_lanes)


def _float_key(bits):
    return jnp.where(bits < 0, jnp.int32(INT_MIN) - bits, bits)


def _dsa_body(qt_ref, qit_ref, wit_ref, k_ref, vt_ref, ki_ref, o_ref, keys_ref, *, p_len, n_keys, tq, tk, top_k):
    q0 = p_len + pl.program_id(1) * tq
    qpos = q0 + lax.broadcasted_iota(jnp.int32, (1, tq), 1)
    limit = jnp.minimum(((qpos >> CHUNK_SHIFT) + 1) << CHUNK_SHIFT, n_keys)
    last_limit = jnp.minimum((((q0 + tq - 1) >> CHUNK_SHIFT) + 1) << CHUNK_SHIFT, n_keys)
    n_blk = lax.div(last_limit - 1, tk) + 1
    key_row = lax.broadcasted_iota(jnp.int32, (tk, tq), 0)

    wit = wit_ref[0]
    qim = _head_queries(qit_ref)

    def score_step(kb, _):
        s0 = pl.multiple_of(kb * tk, tk)
        ki = ki_ref[0, pl.ds(s0, tk), :]
        score = jnp.zeros((tk, tq), F32)
        for h in range(N_HEADS):
            score = score + wit[h:h + 1, :] * jnp.maximum(_dot(ki, qim[h]), 0.0)
        score = jnp.where(key_row < limit - s0, score, -jnp.inf)
        keys_ref[pl.ds(s0, tk), :] = _float_key(pltpu.bitcast(score, jnp.int32))
        return 0

    lax.fori_loop(0, n_blk, score_step, 0)

    def count(pred):
        def body(kb, acc):
            for r in range(0, tk, COUNT_SLAB):
                s0 = pl.multiple_of(kb * tk + r, COUNT_SLAB)
                acc = acc + jnp.where(pred(keys_ref[pl.ds(s0, COUNT_SLAB), :]), 1.0, 0.0)
            return acc
        acc = lax.fori_loop(0, n_blk, body, jnp.zeros((COUNT_SLAB, tq), F32))
        return jnp.sum(acc, axis=0, keepdims=True)

    kf = jnp.float32(top_k)
    zero = jnp.zeros((1, tq), jnp.int32)
    n_zero = count(lambda key: key >= zero)
    thr = jnp.where(n_zero >= kf, zero, jnp.int32(INT_MIN))
    n_at_thr = jnp.where(n_zero >= kf, n_zero, (n_blk * tk).astype(F32))

    def bit_step(it, carry):
        thr, n_at_thr = carry
        cand = thr + lax.shift_left(jnp.int32(1), 30 - it)
        n_cand = count(lambda key: key >= cand)
        return jnp.where(n_cand >= kf, cand, thr), jnp.where(n_cand >= kf, n_cand, n_at_thr)

    thr, n_at_thr = lax.fori_loop(0, 31, bit_step, (thr, n_at_thr))
    n_tie_wanted = kf - count(lambda key: key > thr)
    surplus = jnp.where((n_at_thr > kf) & (thr != MINUS_INF_KEY), 1.0, 0.0)
    ordered_ties = jnp.max(surplus) > 0.0

    qm = _head_queries(qt_ref)

    def attend(with_order):
        if with_order:
            earlier_keys = (lax.broadcasted_iota(jnp.int32, (tk, tk), 1)
                            < lax.broadcasted_iota(jnp.int32, (tk, tk), 0)).astype(BF)

        def attend_step(kb, carry):
            ties_seen, ms, ls, accs = carry
            s0 = pl.multiple_of(kb * tk, tk)
            key = keys_ref[pl.ds(s0, tk), :]
            if with_order:
                tie = jnp.where(key == thr, 1.0, 0.0)
                tie_rank = _dot(earlier_keys, tie.astype(BF)) + ties_seen
                take = jnp.where(key > thr, 1.0, jnp.where(tie_rank < n_tie_wanted, tie, 0.0))
                ties_seen = ties_seen + jnp.sum(tie, axis=0, keepdims=True)
            else:
                take = jnp.where(key >= thr, 1.0, 0.0)
            selected = jnp.where(key_row < limit - s0, take, 0.0) > 0.0
            k = k_ref[0, pl.ds(s0, tk), :]
            vt = vt_ref[0, :, pl.ds(s0, tk)]
            new_ms, new_ls, new_accs = [], [], []
            raw = [_dot(k, qm[h]) for h in range(N_HEADS)]
            for h in range(N_HEADS):
                logits = jnp.where(selected, raw[h], MASK_VALUE)
                m_new, alpha, l_new, p = _online_softmax_step(logits, ms[h], ls[h])
                new_ms.append(m_new)
                new_ls.append(l_new)
                new_accs.append(alpha * accs[h] + _dot(vt, p.astype(BF)))
            return ties_seen, tuple(new_ms), tuple(new_ls), tuple(new_accs)

        carry = (jnp.zeros((1, tq), F32),
                 tuple(jnp.full((1, tq), -jnp.inf, F32) for _ in range(N_HEADS)),
                 tuple(jnp.zeros((1, tq), F32) for _ in range(N_HEADS)),
                 tuple(jnp.zeros((HEAD_DIM, tq), F32) for _ in range(N_HEADS)))
        _, _, ls, accs = lax.fori_loop(0, n_blk, attend_step, carry)
        return ls, accs

    ls, accs = lax.cond(ordered_ties, lambda: attend(True), lambda: attend(False))
    for h in range(N_HEADS):
        o_ref[0, _head_rows(h), :] = (accs[h] / ls[h]).astype(o_ref.dtype)


def _dsa_attention(qt, qit, wit, kk, vt, p_len, n_keys, tq, tk, top_k):
    b, w, t = qt.shape
    lp = kk.shape[1]
    return pl.pallas_call(
        functools.partial(_dsa_body, p_len=p_len, n_keys=n_keys, tq=tq, tk=tk, top_k=top_k),
        out_shape=jax.ShapeDtypeStruct((b, w, t), BF),
        grid=(b, t // tq),
        in_specs=[_qt_spec(w, tq), _qt_spec(w, tq), _qt_spec(wit.shape[1], tq),
                  pl.BlockSpec((1, lp, LANES), lambda bi, qi: (bi, 0, 0)), _whole_spec(HEAD_DIM, lp),
                  pl.BlockSpec((1, lp, LANES), lambda bi, qi: (bi, 0, 1))],
        out_specs=_qt_spec(w, tq),
        scratch_shapes=[pltpu.VMEM((lp, tq), jnp.int32)],
        compiler_params=_params("parallel", "arbitrary"),
        name="dsa_attention",
    )(qt, qit, wit, kk, vt, kk)


def _ret_body(q_ref, k_ref, v_ref, g_ref, s0_ref, cos_ref, sin_ref, dec_ref, qd_ref, kd_ref, sd_ref,
              o_ref, so_ref, state_ref):
    c = pl.program_id(2)

    @pl.when(c == 0)
    def _():
        state_ref[...] = s0_ref[0, 0]

    cos, sin = cos_ref[...], sin_ref[...]
    half = HEAD_DIM // 2

    def rotary(x):
        x1, x2 = x[:half], x[half:]
        return jnp.concatenate([x1 * cos - x2 * sin, x2 * cos + x1 * sin], axis=0)

    qb = rotary(q_ref[0]).astype(BF)
    k = rotary(k_ref[0])
    vb = v_ref[0].astype(BF)
    state = state_ref[...]
    scores_t = _dot_tn(k.astype(BF), qb) * dec_ref[0]
    o = _dot(vb, scores_t.astype(BF)) + _dot(state.astype(BF), qb) * qd_ref[0]
    state = sd_ref[0] * state + _dot_nt(vb, (k * kd_ref[0]).astype(BF))
    state_ref[...] = state
    oc = o - jnp.mean(o, axis=0, keepdims=True)
    on = oc * lax.rsqrt(jnp.mean(oc * oc, axis=0, keepdims=True) + LN_EPS)
    g = g_ref[0]
    o_ref[0] = (on * (g * jax.nn.sigmoid(g))).astype(o_ref.dtype)

    @pl.when(c == pl.num_programs(2) - 1)
    def _():
        so_ref[0, 0] = state


def _retention(qt, kt, vt, gt, state0_t, pos, c):
    b, w, t = qt.shape
    h = w // HEAD_DIM
    half = HEAD_DIM // 2
    inv_freq = ROPE_BASE ** (-jnp.arange(half, dtype=F32) / half)
    ang = inv_freq[:, None] * pos.astype(F32)[None, :]
    log_gamma = np.log(1.0 - 2.0 ** (-5.0 - np.arange(h, dtype=np.float64)))
    n = np.arange(c, dtype=np.float64)
    rel = n[None, :] - n[:, None]
    decay_t = np.where(rel >= 0, np.exp(np.maximum(rel, 0.0)[None] * log_gamma[:, None, None]), 0.0)
    q_decay = np.exp((n[None, :] + 1.0) * log_gamma[:, None])[:, None, :]
    k_decay = np.exp((c - 1.0 - n)[None, :] * log_gamma[:, None])[:, None, :]
    s_decay = np.exp(c * log_gamma)[:, None, None]
    tables = [jnp.asarray(a, F32) for a in (decay_t, q_decay, k_decay, s_decay)]
    x_spec = pl.BlockSpec((1, HEAD_DIM, c), lambda bi, hi, ci: (bi, hi, ci))
    s_spec = pl.BlockSpec((1, 1, HEAD_DIM, HEAD_DIM), lambda bi, hi, ci: (bi, hi, 0, 0))
    rope_spec = pl.BlockSpec((half, c), lambda bi, hi, ci: (0, ci))

    def t_spec(a):
        return pl.BlockSpec((1,) + a.shape[1:], lambda bi, hi, ci: (hi, 0, 0))

    return pl.pallas_call(
        _ret_body,
        out_shape=(jax.ShapeDtypeStruct((b, w, t), BF),
                   jax.ShapeDtypeStruct((b, h, HEAD_DIM, HEAD_DIM), F32)),
        grid=(b, h, t // c),
        in_specs=[x_spec, x_spec, x_spec, x_spec, s_spec, rope_spec, rope_spec] + [t_spec(a) for a in tables],
        out_specs=(x_spec, s_spec),
        scratch_shapes=[pltpu.VMEM((HEAD_DIM, HEAD_DIM), F32)],
        compiler_params=_params("parallel", "parallel", "arbitrary"),
        name="retention",
    )(qt, kt, vt, gt, state0_t, jnp.cos(ang), jnp.sin(ang), *tables)


def _merge_body(h_ref, y0_ref, y1_ref, y2_ref, y3_ref, wg_ref, wb_ref, wo_ref, g_ref, b_ref, o_ref, *, alpha):
    nb, tm, d = h_ref.shape
    h = h_ref[...].reshape(nb * tm, d)
    hb = h.astype(BF)
    merged = jnp.zeros(h.shape, F32)
    for i, y_ref in enumerate((y0_ref, y1_ref, y2_ref, y3_ref)):
        gate = jax.nn.sigmoid(_dot(hb, wg_ref[:, i * d:(i + 1) * d]))
        branch = jnp.concatenate([_dot_tn(y_ref[j], wb_ref[i]) for j in range(nb)], axis=0)
        merged = merged + gate * branch
    r = alpha * h + _dot(merged.astype(BF), wo_ref[...])
    o_ref[...] = _layer_norm(r, g_ref[...], b_ref[...]).reshape(nb, tm, d)


def _merge(h, ys_t, w_gate, w_branch, w_out, ln_g, ln_b, alpha):
    b, t, d = h.shape
    tm = _row_tile(t, MERGE_ROWS)
    nb = math.gcd(b, max(1, MERGE_ROWS // tm))
    row = pl.BlockSpec((nb, tm, d), lambda bi, i: (bi, i, 0))
    col = pl.BlockSpec((nb, BRANCH_WIDTH, tm), lambda bi, i: (bi, 0, i))
    return pl.pallas_call(
        functools.partial(_merge_body, alpha=alpha),
        out_shape=jax.ShapeDtypeStruct((b, t, d), F32),
        grid=(b // nb, t // tm),
        in_specs=[row] + [col] * 4
                 + [_const_spec(w_gate.shape), _const_spec(w_branch.shape), _const_spec(w_out.shape),
                    _const_spec((1, d)), _const_spec((1, d))],
        out_specs=row,
        compiler_params=_params("parallel", "parallel"),
        name="merge",
    )(h, *ys_t, w_gate, w_branch, w_out, ln_g, ln_b)


def _ffn_body(h_ref, wi_ref, wo_ref, g_ref, b_ref, o_ref, *, alpha, f_chunk):
    h = h_ref[...]
    hb = h.astype(BF)
    f = wo_ref.shape[0]
    acc = jnp.zeros(h.shape, F32)
    for c in range(0, f, f_chunk):
        a = _dot(hb, wi_ref[:, c:c + f_chunk])
        u = _dot(hb, wi_ref[:, f + c:f + c + f_chunk])
        acc = acc + _dot((a * jax.nn.sigmoid(a) * u).astype(BF), wo_ref[c:c + f_chunk, :])
    o_ref[...] = _layer_norm(alpha * h + acc, g_ref[...], b_ref[...])


def _ffn(h, w_in, w_out, ln_g, ln_b, alpha):
    m, d = h.shape
    f = w_out.shape[0]
    tm = _row_tile(m, FFN_ROWS)
    f_chunk = f // 2 if (f // 2) % LANES == 0 else f
    row = pl.BlockSpec((tm, d), lambda i: (i, 0))
    return pl.pallas_call(
        functools.partial(_ffn_body, alpha=alpha, f_chunk=f_chunk),
        out_shape=jax.ShapeDtypeStruct((m, d), F32),
        grid=(m // tm,),
        in_specs=[row, _const_spec(w_in.shape), _const_spec(w_out.shape),
                  _const_spec((1, d)), _const_spec((1, d))],
        out_specs=row,
        compiler_params=_params("parallel"),
        name="ffn",
    )(h, w_in, w_out, ln_g, ln_b)


def _in_layout(d):
    w = BRANCH_WIDTH
    return (('sb_q', w), ('sb_k', w), ('sb_v', w), ('ret_q', w), ('ret_k', w), ('ret_v', w), ('ret_g', w),
            ('fox_q', w), ('fox_k', w), ('fox_v', w), ('fox_f', N_HEADS),
            ('dsa_q', w), ('dsa_k', HEAD_DIM), ('dsa_v', HEAD_DIM),
            ('idx_q', w), ('idx_k', HEAD_DIM), ('idx_w', N_HEADS), ('merge_gate', 4 * d))


_FOLDED_SCALE = dict(sb_q=QK_SCALE, fox_q=QK_SCALE, dsa_q=QK_SCALE, idx_q=QK_SCALE, ret_k=QK_SCALE,
                     idx_w=IDX_HEAD_SCALE)


def _swap(a):
    return jnp.swapaxes(a, -1, -2)


def _key_tiles(t, n_keys):
    tq = min(t, 512)
    tiles = dict(sb=256, fox=512, dsa=512)
    padded = {name: -(-n_keys // tk) * tk for name, tk in tiles.items()}
    return tq, tiles, padded


def _layer(h, b, t, past, ret_state, w, alpha, layer, depth, states):
    m, d = h.shape
    p_len = 0 if past is None else past[0].shape[1]
    n_keys = p_len + t
    tq, tk, lp = _key_tiles(t, n_keys)
    p = _inproj(h.reshape(b, t, d), w['w_row'], w['w_col'], layer, depth, states)
    states = {name: p[name] for name, _, _, _, _, is_state in _ROW_OUTS if is_state}

    old = (None,) * 8 if past is None else past
    sb_k0, sb_v0, fox_k0, fox_v0, fox_lf0, dsa_k0, dsa_v0, dsa_ki0 = old

    def rows_with_past(new_bf, olds, lp_):
        if past is not None:
            flat = [o.reshape(o.shape[0], o.shape[1], -1).astype(BF) for o in olds]
            new_bf = jnp.concatenate([jnp.concatenate(flat, axis=2), new_bf], axis=1)
        return jnp.pad(new_bf, ((0, 0), (0, lp_ - new_bf.shape[1]), (0, 0)))

    def cols_with_past(new_t, old, lp_):
        if old is not None:
            new_t = jnp.concatenate([_swap(old.reshape(old.shape[0], old.shape[1], -1).astype(BF)), new_t], axis=2)
        return jnp.pad(new_t, ((0, 0), (0, 0), (0, lp_ - new_t.shape[2])))

    y_sb = _sb_attention(p['sb_q_t'], rows_with_past(p['sb_k_bf'], [sb_k0], lp['sb']),
                         cols_with_past(p['sb_v_t'], sb_v0, lp['sb']), p_len, tq, tk['sb'])

    pos = p_len + jnp.arange(t, dtype=jnp.int32)
    y_ret, ret_state_t = _retention(p['ret_q_t'], p['ret_k_t'], p['ret_v_t'], p['ret_g_t'], _swap(ret_state),
                                    pos, min(t, 256))

    fox_lf = jax.nn.log_sigmoid(states['fox_f'][layer] + w['b_forget'])
    lf_all = fox_lf if fox_lf0 is None else jnp.concatenate([fox_lf0, fox_lf], axis=1)
    cum = jnp.pad(jnp.cumsum(lf_all, axis=1), ((0, 0), (0, lp['fox'] - n_keys), (0, 0)))
    c_lanes = jnp.broadcast_to(_swap(cum)[..., None], (b, N_HEADS, lp['fox'], LANES))
    y_fox = _fox_attention(p['fox_q_t'], rows_with_past(p['fox_k_bf'], [fox_k0], lp['fox']),
                           cols_with_past(p['fox_v_t'], fox_v0, lp['fox']), c_lanes, p_len, tq, tk['fox'])

    top_k = min(DSA_TOP_K, n_keys // 4)
    kk_old = [dsa_k0, dsa_k0, dsa_ki0, dsa_ki0]
    y_dsa = _dsa_attention(p['dsa_q_t'], p['idx_q_t'], p['idx_w_t'], rows_with_past(p['dsa_kk_bf'], kk_old, lp['dsa']),
                           cols_with_past(p['dsa_v_t'], dsa_v0, lp['dsa']), p_len, n_keys, tq, tk['dsa'], top_k)

    h = _merge(h.reshape(b, t, d), (y_sb, y_ret, y_fox, y_dsa), w['w_gate'], w['w_branch'], w['w_out'],
               w['ln1_g'], w['ln1_b'], alpha)
    h = _ffn(h.reshape(m, d), w['w_ffn_in'], w['w_ffn_out'], w['ln2_g'], w['ln2_b'], alpha)
    return h, states, _swap(ret_state_t), fox_lf


def _group_outputs(states, ret_states, fox_lfs):
    def heads(a):
        return a.reshape(a.shape[:-1] + (N_HEADS, HEAD_DIM))

    return (heads(states['sb_k']), heads(states['sb_v']), jnp.stack(ret_states), heads(states['fox_k']),
            heads(states['fox_v']), jnp.stack(fox_lfs), states['dsa_k'], states['dsa_v'], states['idx_k'])


def kernel(x_prompt, x_sample, cache_sb_k, cache_sb_v, state_ret, cache_fox_k, cache_fox_v, cache_fox_logf,
           cache_dsa_k, cache_dsa_v, cache_dsa_kidx, w_in, b_forget, w_branch, w_out, ln1_g, ln1_b,
           w_ffn_in, w_ffn_out, ln2_g, ln2_b):
    depth = w_in.shape[0]
    alpha = float((2 * depth) ** 0.25)
    bp, tp, d = x_prompt.shape
    bs, ts, _ = x_sample.shape
    hp = x_prompt.reshape(bp * tp, d)
    hs = x_sample.reshape(bs * ts, d)
    ret_zero = jnp.zeros((bp, N_HEADS, HEAD_DIM, HEAD_DIM), F32)
    st_p, st_s, ret_p, ret_s, lf_p, lf_s = None, None, [], [], [], []
    for l in range(depth):
        w_row, w_col, w_gate = _inproj_weights(w_in[l])
        w = dict(w_row=w_row, w_col=w_col, w_gate=w_gate, b_forget=b_forget[l], w_branch=w_branch[l].astype(BF),
                 w_out=w_out[l].astype(BF), ln1_g=ln1_g[l][None], ln1_b=ln1_b[l][None],
                 w_ffn_in=w_ffn_in[l].astype(BF), w_ffn_out=w_ffn_out[l].astype(BF),
                 ln2_g=ln2_g[l][None], ln2_b=ln2_b[l][None])
        hp, st_p, ret, lf = _layer(hp, bp, tp, None, ret_zero, w, alpha, l, depth, st_p)
        ret_p.append(ret)
        lf_p.append(lf)
        past = (cache_sb_k[l], cache_sb_v[l], cache_fox_k[l], cache_fox_v[l], cache_fox_logf[l],
                cache_dsa_k[l], cache_dsa_v[l], cache_dsa_kidx[l])
        hs, st_s, ret, lf = _layer(hs, bs, ts, past, state_ret[l], w, alpha, l, depth, st_s)
        ret_s.append(ret)
        lf_s.append(lf)
    return ((hp.reshape(bp, tp, d), hs.reshape(bs, ts, d))
            + _group_outputs(st_p, ret_p, lf_p) + _group_outputs(st_s, ret_s, lf_s))
```

```python
import functools
import math

import numpy as np
import jax
import jax.numpy as jnp
from jax import lax
from jax.experimental import pallas as pl
from jax.experimental.pallas import tpu as pltpu

HEAD_DIM = 64
N_HEADS = 4
BRANCH_WIDTH = N_HEADS * HEAD_DIM
CHUNK_SHIFT = 6
DSA_TOP_K = 256
ROPE_BASE = 10000.0
LN_EPS = 1e-5
QK_SCALE = HEAD_DIM ** -0.5
IDX_HEAD_SCALE = N_HEADS ** -0.5
MASK_VALUE = -1e30
INT_MIN = -2 ** 31

V7X_VMEM_LIMIT_BYTES = 56 * 1024 * 1024
LANES = 128
HEADS_PER_COL = LANES // HEAD_DIM
COUNT_SLAB = 64
RETENTION_CHUNK = 512
MERGE_ROWS = 512
FFN_ROWS = 512

BF = jnp.bfloat16
F32 = jnp.float32


def _dot(a, b):
    return jnp.dot(a, b, preferred_element_type=F32)


def _dot_nt(a, b):
    return lax.dot_general(a, b, (((1,), (1,)), ((), ())), preferred_element_type=F32)


def _dot_tn(a, b):
    return lax.dot_general(a, b, (((0,), (0,)), ((), ())), preferred_element_type=F32)


def _params(*sem):
    return pltpu.CompilerParams(dimension_semantics=sem, vmem_limit_bytes=V7X_VMEM_LIMIT_BYTES)


def _const_spec(shape):
    nd = len(shape)
    return pl.BlockSpec(shape, lambda *_: (0,) * nd)


def _layer_norm(x, g, b):
    xc = x - jnp.mean(x, axis=-1, keepdims=True)
    var = jnp.mean(xc * xc, axis=-1, keepdims=True)
    return xc * lax.rsqrt(var + LN_EPS) * g + b


def _row_tile(m, want):
    t = min(m, want)
    assert m % t == 0
    return t


def _col(c):
    return slice(c * LANES, (c + 1) * LANES)


_NARROW_SRC = ('dsa_k', 'dsa_v', 'idx_k', 'fox_f')
_ROW_OUTS = (
    ('sb_k', ('sb_k',), F32, 0, BRANCH_WIDTH, True), ('sb_k_bf', ('sb_k',), BF, 0, BRANCH_WIDTH, False),
    ('sb_v', ('sb_v',), F32, 0, BRANCH_WIDTH, True),
    ('fox_k', ('fox_k',), F32, 0, BRANCH_WIDTH, True), ('fox_k_bf', ('fox_k',), BF, 0, BRANCH_WIDTH, False),
    ('fox_v', ('fox_v',), F32, 0, BRANCH_WIDTH, True),
    ('dsa_k', _NARROW_SRC, F32, 0, HEAD_DIM, True), ('dsa_v', _NARROW_SRC, F32, HEAD_DIM, HEAD_DIM, True),
    ('idx_k', _NARROW_SRC, F32, 2 * HEAD_DIM, HEAD_DIM, True), ('fox_f', _NARROW_SRC, F32, 3 * HEAD_DIM, N_HEADS, True),
    ('dsa_kk_bf', ('dsa_k', 'dsa_k', 'idx_k', 'idx_k'), BF, 0, 4 * HEAD_DIM, False),
)
_COL_OUTS = (
    ('sb_q_t', ('sb_q',), BF), ('fox_q_t', ('fox_q',), BF), ('dsa_q_t', ('dsa_q',), BF),
    ('idx_q_t', ('idx_q',), BF), ('sb_v_t', ('sb_v',), BF), ('fox_v_t', ('fox_v',), BF),
    ('ret_q_t', ('ret_q',), F32), ('ret_k_t', ('ret_k',), F32), ('ret_v_t', ('ret_v',), F32),
    ('ret_g_t', ('ret_g',), F32), ('dsa_v_t', ('dsa_v',), BF), ('idx_w_t', ('idx_w',), F32),
)
BF16_ROWS_PER_VREG = 16


def _inproj_plan(d):
    widths = dict(_in_layout(d))

    def spans(outs, multiple):
        span_of, off = {}, 0
        for out in outs:
            srcs = out[1]
            if srcs not in span_of:
                w = -(-sum(widths[s] for s in srcs) // multiple) * multiple
                span_of[srcs] = (off, w)
                off += w
        return span_of, off

    row_spans, _ = spans(_ROW_OUTS, LANES)
    col_spans, n_col = spans(_COL_OUTS, BF16_ROWS_PER_VREG)
    return row_spans, col_spans, -(-n_col // LANES) * LANES


def _inproj_weights(w_in):
    d = w_in.shape[0]
    pieces, off = {}, 0
    for name, width in _in_layout(d):
        pieces[name] = w_in[:, off:off + width] * _FOLDED_SCALE.get(name, 1.0)
        off += width
    assert off == w_in.shape[1]
    row_spans, col_spans, n_col = _inproj_plan(d)

    def block(srcs, width):
        w = jnp.concatenate([pieces[s] for s in srcs], axis=1)
        return jnp.pad(w, ((0, 0), (0, width - w.shape[1])))

    w_row = jnp.concatenate([block(srcs, w) for srcs, (_, w) in row_spans.items()], axis=1)
    w_col = jnp.concatenate([block(srcs, w) for srcs, (_, w) in col_spans.items()], axis=1)
    w_col = jnp.pad(w_col, ((0, 0), (0, n_col - w_col.shape[1])))
    return w_row.astype(BF), w_col.T.astype(BF), pieces['merge_gate'].astype(BF)


def _inproj_body(x_ref, wr_ref, wc_ref, *refs, row_spans, col_spans, n_alias):
    o_refs = refs[n_alias:]
    xb = x_ref[0].astype(BF)
    done = {}
    for o_ref, (_, srcs, _, lane, width, _) in zip(o_refs, _ROW_OUTS):
        if srcs not in done:
            off, w = row_spans[srcs]
            done[srcs] = _dot(xb, wr_ref[:, off:off + w])
        o_ref[...] = done[srcs][:, lane:lane + width].astype(o_ref.dtype).reshape(o_ref.shape)
    for o_ref, (_, srcs, _) in zip(o_refs[len(_ROW_OUTS):], _COL_OUTS):
        off, w = col_spans[srcs]
        o_ref[0] = _dot_nt(wc_ref[off:off + w, :], xb).astype(o_ref.dtype)


def _inproj(x, w_row, w_col, layer, depth, states):
    b, t, d = x.shape
    tm = _row_tile(t, 512)
    row_spans, col_spans, _ = _inproj_plan(d)
    out_shape, out_specs, state_names = [], [], []
    for name, _, dt, _, w, is_state in _ROW_OUTS:
        if is_state:
            state_names.append(name)
            out_shape.append(jax.ShapeDtypeStruct((depth, b, t, w), dt))
            out_specs.append(pl.BlockSpec((1, 1, tm, w), lambda bi, i: (layer, bi, i, 0)))
        else:
            out_shape.append(jax.ShapeDtypeStruct((b, t, w), dt))
            out_specs.append(pl.BlockSpec((1, tm, w), lambda bi, i: (bi, i, 0)))
    for _, srcs, dt in _COL_OUTS:
        w = col_spans[srcs][1]
        out_shape.append(jax.ShapeDtypeStruct((b, w, t), dt))
        out_specs.append(pl.BlockSpec((1, w, tm), lambda bi, i: (bi, 0, i)))
    prev = [] if states is None else [states[n] for n in state_names]
    names = [o[0] for o in _ROW_OUTS] + [o[0] for o in _COL_OUTS]
    aliases = {3 + j: names.index(n) for j, n in enumerate(state_names)} if prev else {}
    outs = pl.pallas_call(
        functools.partial(_inproj_body, row_spans=row_spans, col_spans=col_spans, n_alias=len(prev)),
        out_shape=out_shape,
        grid=(b, t // tm),
        in_specs=[pl.BlockSpec((1, tm, d), lambda bi, i: (bi, i, 0)),
                  _const_spec(w_row.shape), _const_spec(w_col.shape)]
                 + [pl.BlockSpec(memory_space=pl.ANY)] * len(prev),
        out_specs=out_specs,
        input_output_aliases=aliases,
        compiler_params=_params("parallel", "parallel"),
        name="inproj",
    )(x, w_row, w_col, *prev)
    return dict(zip(names, outs))


def _block_counts(q0, tk, last_key):
    return lax.div(q0, tk), lax.div(last_key, tk) + 1


def _head_queries(qt_ref):
    low = lax.broadcasted_iota(jnp.int32, (LANES, 1), 0) < HEAD_DIM
    out = []
    for h in range(N_HEADS):
        qc = qt_ref[0, _col(h // HEADS_PER_COL), :]
        keep = low if h % HEADS_PER_COL == 0 else jnp.logical_not(low)
        out.append(jnp.where(keep, qc, jnp.zeros_like(qc)))
    return out


def _head_rows(h):
    return slice(h * HEAD_DIM, (h + 1) * HEAD_DIM)


def _key_minus_query(tk, tq):
    return (lax.broadcasted_iota(jnp.int32, (tk, tq), 0) - lax.broadcasted_iota(jnp.int32, (tk, tq), 1))


def _qt_spec(w, tq):
    return pl.BlockSpec((1, w, tq), lambda bi, qi: (bi, 0, qi))


def _whole_spec(rows, cols):
    return pl.BlockSpec((1, rows, cols), lambda bi, qi: (bi, 0, 0))


def _sb_body(qt_ref, k_ref, vt_ref, o_ref, *, p_len, tq, tk):
    q0 = p_len + pl.program_id(1) * tq
    qm = _head_queries(qt_ref)
    diff = _key_minus_query(tk, tq)
    later = (lax.broadcasted_iota(jnp.int32, (tk, 2 * tk), 1) & (tk - 1)) > lax.broadcasted_iota(
        jnp.int32, (tk, 2 * tk), 0)
    minus_later = jnp.where(later, -1.0, 0.0).astype(BF)
    n_full, n_all = _block_counts(q0, tk, jnp.maximum(q0 + tq - 2, 0))

    def step(kb, carry, masked):
        laters, accs = carry
        s0 = pl.multiple_of(kb * tk, tk)
        if masked:
            earlier = diff < (q0 - s0)
        zs = [_dot(k_ref[0, pl.ds(s0, tk), _col(h // HEADS_PER_COL)], qm[h]) for h in range(N_HEADS)]
        new_laters, log_bs, afters = [], [], []
        for h in range(N_HEADS):
            z = zs[h]
            minus_abs = pltpu.bitcast(pltpu.bitcast(z, jnp.int32) | jnp.int32(INT_MIN), F32)
            softplus = jnp.maximum(z, 0.0) + jnp.log(1.0 + jnp.exp(minus_abs))
            log_bs.append(z - softplus)
            if masked:
                softplus = jnp.where(earlier, softplus, 0.0)
            hi = softplus.astype(BF)
            lo = (softplus - hi.astype(F32)).astype(BF)
            after = _dot(minus_later, jnp.concatenate([hi, lo], axis=0)) + laters[h]
            afters.append(after)
            new_laters.append(after[0:1, :] - softplus[0:1, :])
        new_accs = []
        for h in range(N_HEADS):
            w = jnp.exp(log_bs[h] + afters[h])
            if masked:
                w = jnp.where(earlier, w, 0.0)
            new_accs.append(accs[h] + _dot(vt_ref[0, _head_rows(h), pl.ds(s0, tk)], w.astype(BF)))
        return tuple(new_laters), tuple(new_accs)

    carry = (tuple(jnp.zeros((1, tq), F32) for _ in range(N_HEADS)),
             tuple(jnp.zeros((HEAD_DIM, tq), F32) for _ in range(N_HEADS)))
    carry = lax.fori_loop(0, n_all - n_full, lambda i, c: step(n_all - 1 - i, c, True), carry)
    carry = lax.fori_loop(0, n_full, lambda i, c: step(n_full - 1 - i, c, False), carry)
    for h in range(N_HEADS):
        o_ref[0, _head_rows(h), :] = carry[1][h].astype(o_ref.dtype)


def _sb_attention(qt, k, vt, p_len, tq, tk):
    b, w, t = qt.shape
    lp = k.shape[1]
    assert tk & (tk - 1) == 0
    return pl.pallas_call(
        functools.partial(_sb_body, p_len=p_len, tq=tq, tk=tk),
        out_shape=jax.ShapeDtypeStruct((b, w, t), BF),
        grid=(b, t // tq),
        in_specs=[_qt_spec(w, tq), _whole_spec(lp, w), _whole_spec(w, lp)],
        out_specs=_qt_spec(w, tq),
        compiler_params=_params("parallel", "arbitrary"),
        name="sb_attention",
    )(qt, k, vt)


def _online_softmax_step(logits, m, l):
    m_new = jnp.maximum(m, jnp.max(logits, axis=0, keepdims=True))
    alpha = jnp.exp(m - m_new)
    p = jnp.exp(logits - m_new)
    return m_new, alpha, alpha * l + jnp.sum(p, axis=0, keepdims=True), p


def _fox_body(qt_ref, k_ref, vt_ref, c_ref, o_ref, *, p_len, tq, tk):
    q0 = p_len + pl.program_id(1) * tq
    qm = _head_queries(qt_ref)
    diff = _key_minus_query(tk, tq)
    n_full, n_all = _block_counts(q0, tk, q0 + tq - 1)

    def key_bias(h, s0):
        c = c_ref[0, h, pl.ds(s0, tk), :]
        return c[:, :tq] if tq <= LANES else jnp.concatenate([c] * (tq // LANES), axis=1)

    def step(kb, carry, masked):
        ms, ls, accs = carry
        s0 = pl.multiple_of(kb * tk, tk)
        if masked:
            visible = diff <= (q0 - s0)
        new_ms, new_ls, new_accs = [], [], []
        raw = [_dot(k_ref[0, pl.ds(s0, tk), _col(h // HEADS_PER_COL)], qm[h]) for h in range(N_HEADS)]
        for h in range(N_HEADS):
            logits = raw[h] - key_bias(h, s0)
            if masked:
                logits = jnp.where(visible, logits, MASK_VALUE)
            m_new, alpha, l_new, p = _online_softmax_step(logits, ms[h], ls[h])
            new_ms.append(m_new)
            new_ls.append(l_new)
            new_accs.append(alpha * accs[h] + _dot(vt_ref[0, _head_rows(h), pl.ds(s0, tk)], p.astype(BF)))
        return tuple(new_ms), tuple(new_ls), tuple(new_accs)

    carry = (tuple(jnp.full((1, tq), -jnp.inf, F32) for _ in range(N_HEADS)),
             tuple(jnp.zeros((1, tq), F32) for _ in range(N_HEADS)),
             tuple(jnp.zeros((HEAD_DIM, tq), F32) for _ in range(N_HEADS)))
    carry = lax.fori_loop(0, n_full, lambda i, c: step(i, c, False), carry)
    carry = lax.fori_loop(n_full, n_all, lambda i, c: step(i, c, True), carry)
    _, ls, accs = carry
    for h in range(N_HEADS):
        o_ref[0, _head_rows(h), :] = (accs[h] / ls[h]).astype(o_ref.dtype)


def _fox_attention(qt, k, vt, c_lanes, p_len, tq, tk):
    b, w, t = qt.shape
    lp = k.shape[1]
    assert tq <= LANES or tq % LANES == 0
    return pl.pallas_call(
        functools.partial(_fox_body, p_len=p_len, tq=tq, tk=tk),
        out_shape=jax.ShapeDtypeStruct((b, w, t), BF),
        grid=(b, t // tq),
        in_specs=[_qt_spec(w, tq), _whole_spec(lp, w), _whole_spec(w, lp),
                  pl.BlockSpec((1, N_HEADS, lp, LANES), lambda bi, qi: (bi, 0, 0, 0))],
        out_specs=_qt_spec(w, tq),
        compiler_params=_params("parallel", "arbitrary"),
        name="fox_attention",
    )(qt, k, vt, c_lanes)


def _float_key(bits):
    return jnp.where(bits < 0, jnp.int32(INT_MIN) - bits, bits)


def _dsa_body(qt_ref, qit_ref, wit_ref, k_ref, vt_ref, ki_ref, o_ref, keys_ref, *, p_len, n_keys, tq, tk, top_k):
    q0 = p_len + pl.program_id(1) * tq
    qpos = q0 + lax.broadcasted_iota(jnp.int32, (1, tq), 1)
    limit = jnp.minimum(((qpos >> CHUNK_SHIFT) + 1) << CHUNK_SHIFT, n_keys)
    last_limit = jnp.minimum((((q0 + tq - 1) >> CHUNK_SHIFT) + 1) << CHUNK_SHIFT, n_keys)
    n_blk = lax.div(last_limit - 1, tk) + 1
    key_row = lax.broadcasted_iota(jnp.int32, (tk, tq), 0)

    wit = wit_ref[0]
    qim = _head_queries(qit_ref)

    def score_step(kb, _):
        s0 = pl.multiple_of(kb * tk, tk)
        ki = ki_ref[0, pl.ds(s0, tk), :]
        score = jnp.zeros((tk, tq), F32)
        for h in range(N_HEADS):
            score = score + wit[h:h + 1, :] * jnp.maximum(_dot(ki, qim[h]), 0.0)
        score = jnp.where(key_row < limit - s0, score, -jnp.inf)
        keys_ref[pl.ds(s0, tk), :] = _float_key(pltpu.bitcast(score, jnp.int32))
        return 0

    lax.fori_loop(0, n_blk, score_step, 0)

    def count(pred):
        def body(kb, acc):
            for r in range(0, tk, COUNT_SLAB):
                s0 = pl.multiple_of(kb * tk + r, COUNT_SLAB)
                acc = acc + jnp.where(pred(keys_ref[pl.ds(s0, COUNT_SLAB), :]), 1.0, 0.0)
            return acc
        acc = lax.fori_loop(0, n_blk, body, jnp.zeros((COUNT_SLAB, tq), F32))
        return jnp.sum(acc, axis=0, keepdims=True)

    kf = jnp.float32(top_k)
    zero = jnp.zeros((1, tq), jnp.int32)
    thr = jnp.where(count(lambda key: key >= zero) >= kf, zero, jnp.int32(INT_MIN))

    def bit_step(it, thr):
        cand = thr + lax.shift_left(jnp.int32(1), 30 - it)
        return jnp.where(count(lambda key: key >= cand) >= kf, cand, thr)

    thr = lax.fori_loop(0, 31, bit_step, thr)
    n_tie_wanted = kf - count(lambda key: key > thr)

    earlier_keys = (lax.broadcasted_iota(jnp.int32, (tk, tk), 1)
                    < lax.broadcasted_iota(jnp.int32, (tk, tk), 0)).astype(BF)
    qm = _head_queries(qt_ref)

    def attend_step(kb, carry):
        ties_seen, ms, ls, accs = carry
        s0 = pl.multiple_of(kb * tk, tk)
        key = keys_ref[pl.ds(s0, tk), :]
        tie = jnp.where(key == thr, 1.0, 0.0)
        tie_rank = _dot(earlier_keys, tie.astype(BF)) + ties_seen
        take = jnp.where(key > thr, 1.0, jnp.where(tie_rank < n_tie_wanted, tie, 0.0))
        selected = jnp.where(key_row < limit - s0, take, 0.0) > 0.0
        k = k_ref[0, pl.ds(s0, tk), :]
        vt = vt_ref[0, :, pl.ds(s0, tk)]
        new_ms, new_ls, new_accs = [], [], []
        raw = [_dot(k, qm[h]) for h in range(N_HEADS)]
        for h in range(N_HEADS):
            logits = jnp.where(selected, raw[h], MASK_VALUE)
            m_new, alpha, l_new, p = _online_softmax_step(logits, ms[h], ls[h])
            new_ms.append(m_new)
            new_ls.append(l_new)
            new_accs.append(alpha * accs[h] + _dot(vt, p.astype(BF)))
        ties_seen = ties_seen + jnp.sum(tie, axis=0, keepdims=True)
        return ties_seen, tuple(new_ms), tuple(new_ls), tuple(new_accs)

    carry = (jnp.zeros((1, tq), F32),
             tuple(jnp.full((1, tq), -jnp.inf, F32) for _ in range(N_HEADS)),
             tuple(jnp.zeros((1, tq), F32) for _ in range(N_HEADS)),
             tuple(jnp.zeros((HEAD_DIM, tq), F32) for _ in range(N_HEADS)))
    _, _, ls, accs = lax.fori_loop(0, n_blk, attend_step, carry)
    for h in range(N_HEADS):
        o_ref[0, _head_rows(h), :] = (accs[h] / ls[h]).astype(o_ref.dtype)


def _dsa_attention(qt, qit, wit, kk, vt, p_len, n_keys, tq, tk, top_k):
    b, w, t = qt.shape
    lp = kk.shape[1]
    return pl.pallas_call(
        functools.partial(_dsa_body, p_len=p_len, n_keys=n_keys, tq=tq, tk=tk, top_k=top_k),
        out_shape=jax.ShapeDtypeStruct((b, w, t), BF),
        grid=(b, t // tq),
        in_specs=[_qt_spec(w, tq), _qt_spec(w, tq), _qt_spec(wit.shape[1], tq),
                  pl.BlockSpec((1, lp, LANES), lambda bi, qi: (bi, 0, 0)), _whole_spec(HEAD_DIM, lp),
                  pl.BlockSpec((1, lp, LANES), lambda bi, qi: (bi, 0, 1))],
        out_specs=_qt_spec(w, tq),
        scratch_shapes=[pltpu.VMEM((lp, tq), jnp.int32)],
        compiler_params=_params("parallel", "arbitrary"),
        name="dsa_attention",
    )(qt, qit, wit, kk, vt, kk)


def _ret_body(q_ref, k_ref, v_ref, g_ref, s0_ref, cos_ref, sin_ref, dec_ref, qd_ref, kd_ref, sd_ref,
              o_ref, so_ref, state_ref):
    c = pl.program_id(2)

    @pl.when(c == 0)
    def _():
        state_ref[...] = s0_ref[0, 0]

    cos, sin = cos_ref[...], sin_ref[...]
    half = HEAD_DIM // 2

    def rotary(x):
        x1, x2 = x[:half], x[half:]
        return jnp.concatenate([x1 * cos - x2 * sin, x2 * cos + x1 * sin], axis=0)

    qb = rotary(q_ref[0]).astype(BF)
    k = rotary(k_ref[0])
    vb = v_ref[0].astype(BF)
    state = state_ref[...]
    scores_t = _dot_tn(k.astype(BF), qb) * dec_ref[0]
    o = _dot(vb, scores_t.astype(BF)) + _dot(state.astype(BF), qb) * qd_ref[0]
    state = sd_ref[0] * state + _dot_nt(vb, (k * kd_ref[0]).astype(BF))
    state_ref[...] = state
    oc = o - jnp.mean(o, axis=0, keepdims=True)
    on = oc * lax.rsqrt(jnp.mean(oc * oc, axis=0, keepdims=True) + LN_EPS)
    g = g_ref[0]
    o_ref[0] = (on * (g * jax.nn.sigmoid(g))).astype(o_ref.dtype)

    @pl.when(c == pl.num_programs(2) - 1)
    def _():
        so_ref[0, 0] = state


def _retention(qt, kt, vt, gt, state0_t, pos, c):
    b, w, t = qt.shape
    h = w // HEAD_DIM
    half = HEAD_DIM // 2
    inv_freq = ROPE_BASE ** (-jnp.arange(half, dtype=F32) / half)
    ang = inv_freq[:, None] * pos.astype(F32)[None, :]
    log_gamma = np.log(1.0 - 2.0 ** (-5.0 - np.arange(h, dtype=np.float64)))
    n = np.arange(c, dtype=np.float64)
    rel = n[None, :] - n[:, None]
    decay_t = np.where(rel >= 0, np.exp(np.maximum(rel, 0.0)[None] * log_gamma[:, None, None]), 0.0)
    q_decay = np.exp((n[None, :] + 1.0) * log_gamma[:, None])[:, None, :]
    k_decay = np.exp((c - 1.0 - n)[None, :] * log_gamma[:, None])[:, None, :]
    s_decay = np.exp(c * log_gamma)[:, None, None]
    tables = [jnp.asarray(a, F32) for a in (decay_t, q_decay, k_decay, s_decay)]
    x_spec = pl.BlockSpec((1, HEAD_DIM, c), lambda bi, hi, ci: (bi, hi, ci))
    s_spec = pl.BlockSpec((1, 1, HEAD_DIM, HEAD_DIM), lambda bi, hi, ci: (bi, hi, 0, 0))
    rope_spec = pl.BlockSpec((half, c), lambda bi, hi, ci: (0, ci))

    def t_spec(a):
        return pl.BlockSpec((1,) + a.shape[1:], lambda bi, hi, ci: (hi, 0, 0))

    return pl.pallas_call(
        _ret_body,
        out_shape=(jax.ShapeDtypeStruct((b, w, t), BF),
                   jax.ShapeDtypeStruct((b, h, HEAD_DIM, HEAD_DIM), F32)),
        grid=(b, h, t // c),
        in_specs=[x_spec, x_spec, x_spec, x_spec, s_spec, rope_spec, rope_spec] + [t_spec(a) for a in tables],
        out_specs=(x_spec, s_spec),
        scratch_shapes=[pltpu.VMEM((HEAD_DIM, HEAD_DIM), F32)],
        compiler_params=_params("parallel", "parallel", "arbitrary"),
        name="retention",
    )(qt, kt, vt, gt, state0_t, jnp.cos(ang), jnp.sin(ang), *tables)


def _merge_body(h_ref, y0_ref, y1_ref, y2_ref, y3_ref, wg_ref, wb_ref, wo_ref, g_ref, b_ref, o_ref, *, alpha):
    nb, tm, d = h_ref.shape
    h = h_ref[...].reshape(nb * tm, d)
    hb = h.astype(BF)
    merged = jnp.zeros(h.shape, F32)
    for i, y_ref in enumerate((y0_ref, y1_ref, y2_ref, y3_ref)):
        gate = jax.nn.sigmoid(_dot(hb, wg_ref[:, i * d:(i + 1) * d]))
        branch = jnp.concatenate([_dot_tn(y_ref[j], wb_ref[i]) for j in range(nb)], axis=0)
        merged = merged + gate * branch
    r = alpha * h + _dot(merged.astype(BF), wo_ref[...])
    o_ref[...] = _layer_norm(r, g_ref[...], b_ref[...]).reshape(nb, tm, d)


def _merge(h, ys_t, w_gate, w_branch, w_out, ln_g, ln_b, alpha):
    b, t, d = h.shape
    tm = _row_tile(t, MERGE_ROWS)
    nb = math.gcd(b, max(1, MERGE_ROWS // tm))
    row = pl.BlockSpec((nb, tm, d), lambda bi, i: (bi, i, 0))
    col = pl.BlockSpec((nb, BRANCH_WIDTH, tm), lambda bi, i: (bi, 0, i))
    return pl.pallas_call(
        functools.partial(_merge_body, alpha=alpha),
        out_shape=jax.ShapeDtypeStruct((b, t, d), F32),
        grid=(b // nb, t // tm),
        in_specs=[row] + [col] * 4
                 + [_const_spec(w_gate.shape), _const_spec(w_branch.shape), _const_spec(w_out.shape),
                    _const_spec((1, d)), _const_spec((1, d))],
        out_specs=row,
        compiler_params=_params("parallel", "parallel"),
        name="merge",
    )(h, *ys_t, w_gate, w_branch, w_out, ln_g, ln_b)


def _ffn_body(h_ref, wi_ref, wo_ref, g_ref, b_ref, o_ref, *, alpha, f_chunk):
    h = h_ref[...]
    hb = h.astype(BF)
    f = wo_ref.shape[0]
    acc = jnp.zeros(h.shape, F32)
    for c in range(0, f, f_chunk):
        a = _dot(hb, wi_ref[:, c:c + f_chunk])
        u = _dot(hb, wi_ref[:, f + c:f + c + f_chunk])
        acc = acc + _dot((a * jax.nn.sigmoid(a) * u).astype(BF), wo_ref[c:c + f_chunk, :])
    o_ref[...] = _layer_norm(alpha * h + acc, g_ref[...], b_ref[...])


def _ffn(h, w_in, w_out, ln_g, ln_b, alpha):
    m, d = h.shape
    f = w_out.shape[0]
    tm = _row_tile(m, FFN_ROWS)
    f_chunk = f // 2 if (f // 2) % LANES == 0 else f
    row = pl.BlockSpec((tm, d), lambda i: (i, 0))
    return pl.pallas_call(
        functools.partial(_ffn_body, alpha=alpha, f_chunk=f_chunk),
        out_shape=jax.ShapeDtypeStruct((m, d), F32),
        grid=(m // tm,),
        in_specs=[row, _const_spec(w_in.shape), _const_spec(w_out.shape),
                  _const_spec((1, d)), _const_spec((1, d))],
        out_specs=row,
        compiler_params=_params("parallel"),
        name="ffn",
    )(h, w_in, w_out, ln_g, ln_b)


def _in_layout(d):
    w = BRANCH_WIDTH
    return (('sb_q', w), ('sb_k', w), ('sb_v', w), ('ret_q', w), ('ret_k', w), ('ret_v', w), ('ret_g', w),
            ('fox_q', w), ('fox_k', w), ('fox_v', w), ('fox_f', N_HEADS),
            ('dsa_q', w), ('dsa_k', HEAD_DIM), ('dsa_v', HEAD_DIM),
            ('idx_q', w), ('idx_k', HEAD_DIM), ('idx_w', N_HEADS), ('merge_gate', 4 * d))


_FOLDED_SCALE = dict(sb_q=QK_SCALE, fox_q=QK_SCALE, dsa_q=QK_SCALE, idx_q=QK_SCALE, ret_k=QK_SCALE,
                     idx_w=IDX_HEAD_SCALE)


def _swap(a):
    return jnp.swapaxes(a, -1, -2)


def _key_tiles(t, n_keys):
    tq = min(t, 512)
    tiles = dict(sb=256, fox=512, dsa=512)
    padded = {name: -(-n_keys // tk) * tk for name, tk in tiles.items()}
    return tq, tiles, padded


def _layer(h, b, t, past, ret_state, w, alpha, layer, depth, states):
    m, d = h.shape
    p_len = 0 if past is None else past[0].shape[1]
    n_keys = p_len + t
    tq, tk, lp = _key_tiles(t, n_keys)
    p = _inproj(h.reshape(b, t, d), w['w_row'], w['w_col'], layer, depth, states)
    states = {name: p[name] for name, _, _, _, _, is_state in _ROW_OUTS if is_state}

    old = (None,) * 8 if past is None else past
    sb_k0, sb_v0, fox_k0, fox_v0, fox_lf0, dsa_k0, dsa_v0, dsa_ki0 = old

    def rows_with_past(new_bf, olds, lp_):
        if past is not None:
            flat = [o.reshape(o.shape[0], o.shape[1], -1).astype(BF) for o in olds]
            new_bf = jnp.concatenate([jnp.concatenate(flat, axis=2), new_bf], axis=1)
        return jnp.pad(new_bf, ((0, 0), (0, lp_ - new_bf.shape[1]), (0, 0)))

    def cols_with_past(new_t, old, lp_):
        if old is not None:
            new_t = jnp.concatenate([_swap(old.reshape(old.shape[0], old.shape[1], -1).astype(BF)), new_t], axis=2)
        return jnp.pad(new_t, ((0, 0), (0, 0), (0, lp_ - new_t.shape[2])))

    y_sb = _sb_attention(p['sb_q_t'], rows_with_past(p['sb_k_bf'], [sb_k0], lp['sb']),
                         cols_with_past(p['sb_v_t'], sb_v0, lp['sb']), p_len, tq, tk['sb'])

    pos = p_len + jnp.arange(t, dtype=jnp.int32)
    y_ret, ret_state_t = _retention(p['ret_q_t'], p['ret_k_t'], p['ret_v_t'], p['ret_g_t'], _swap(ret_state),
                                    pos, min(t, RETENTION_CHUNK))

    fox_lf = jax.nn.log_sigmoid(states['fox_f'][layer] + w['b_forget'])
    lf_all = fox_lf if fox_lf0 is None else jnp.concatenate([fox_lf0, fox_lf], axis=1)
    cum = jnp.pad(jnp.cumsum(lf_all, axis=1), ((0, 0), (0, lp['fox'] - n_keys), (0, 0)))
    c_lanes = jnp.broadcast_to(_swap(cum)[..., None], (b, N_HEADS, lp['fox'], LANES))
    y_fox = _fox_attention(p['fox_q_t'], rows_with_past(p['fox_k_bf'], [fox_k0], lp['fox']),
                           cols_with_past(p['fox_v_t'], fox_v0, lp['fox']), c_lanes, p_len, tq, tk['fox'])

    top_k = min(DSA_TOP_K, n_keys // 4)
    kk_old = [dsa_k0, dsa_k0, dsa_ki0, dsa_ki0]
    y_dsa = _dsa_attention(p['dsa_q_t'], p['idx_q_t'], p['idx_w_t'], rows_with_past(p['dsa_kk_bf'], kk_old, lp['dsa']),
                           cols_with_past(p['dsa_v_t'], dsa_v0, lp['dsa']), p_len, n_keys, tq, tk['dsa'], top_k)

    h = _merge(h.reshape(b, t, d), (y_sb, y_ret, y_fox, y_dsa), w['w_gate'], w['w_branch'], w['w_out'],
               w['ln1_g'], w['ln1_b'], alpha)
    h = _ffn(h.reshape(m, d), w['w_ffn_in'], w['w_ffn_out'], w['ln2_g'], w['ln2_b'], alpha)
    return h, states, _swap(ret_state_t), fox_lf


def _group_outputs(states, ret_states, fox_lfs):
    def heads(a):
        return a.reshape(a.shape[:-1] + (N_HEADS, HEAD_DIM))

    return (heads(states['sb_k']), heads(states['sb_v']), jnp.stack(ret_states), heads(states['fox_k']),
            heads(states['fox_v']), jnp.stack(fox_lfs), states['dsa_k'], states['dsa_v'], states['idx_k'])


def kernel(x_prompt, x_sample, cache_sb_k, cache_sb_v, state_ret, cache_fox_k, cache_fox_v, cache_fox_logf,
           cache_dsa_k, cache_dsa_v, cache_dsa_kidx, w_in, b_forget, w_branch, w_out, ln1_g, ln1_b,
           w_ffn_in, w_ffn_out, ln2_g, ln2_b):
    depth = w_in.shape[0]
    alpha = float((2 * depth) ** 0.25)
    bp, tp, d = x_prompt.shape
    bs, ts, _ = x_sample.shape
    hp = x_prompt.reshape(bp * tp, d)
    hs = x_sample.reshape(bs * ts, d)
    ret_zero = jnp.zeros((bp, N_HEADS, HEAD_DIM, HEAD_DIM), F32)
    st_p, st_s, ret_p, ret_s, lf_p, lf_s = None, None, [], [], [], []
    for l in range(depth):
        w_row, w_col, w_gate = _inproj_weights(w_in[l])
        w = dict(w_row=w_row, w_col=w_col, w_gate=w_gate, b_forget=b_forget[l], w_branch=w_branch[l].astype(BF),
                 w_out=w_out[l].astype(BF), ln1_g=ln1_g[l][None], ln1_b=ln1_b[l][None],
                 w_ffn_in=w_ffn_in[l].astype(BF), w_ffn_out=w_ffn_out[l].astype(BF),
                 ln2_g=ln2_g[l][None], ln2_b=ln2_b[l][None])
        hp, st_p, ret, lf = _layer(hp, bp, tp, None, ret_zero, w, alpha, l, depth, st_p)
        ret_p.append(ret)
        lf_p.append(lf)
        past = (cache_sb_k[l], cache_sb_v[l], cache_fox_k[l], cache_fox_v[l], cache_fox_logf[l],
                cache_dsa_k[l], cache_dsa_v[l], cache_dsa_kidx[l])
        hs, st_s, ret, lf = _layer(hs, bs, ts, past, state_ret[l], w, alpha, l, depth, st_s)
        ret_s.append(ret)
        lf_s.append(lf)
    return ((hp.reshape(bp, tp, d), hs.reshape(bs, ts, d))
            + _group_outputs(st_p, ret_p, lf_p) + _group_outputs(st_s, ret_s, lf_s))
```

```python
import functools
import math

import numpy as np
import jax
import jax.numpy as jnp
from jax import lax
from jax.experimental import pallas as pl
from jax.experimental.pallas import tpu as pltpu

HEAD_DIM = 64
N_HEADS = 4
BRANCH_WIDTH = N_HEADS * HEAD_DIM
CHUNK_SHIFT = 6
DSA_TOP_K = 256
ROPE_BASE = 10000.0
LN_EPS = 1e-5
QK_SCALE = HEAD_DIM ** -0.5
IDX_HEAD_SCALE = N_HEADS ** -0.5
MASK_VALUE = -1e30
INT_MIN = -2 ** 31

V7X_VMEM_LIMIT_BYTES = 56 * 1024 * 1024
LANES = 128
HEADS_PER_COL = LANES // HEAD_DIM
COUNT_SLAB = 64
RETENTION_CHUNK = 512
MERGE_ROWS = 512
FFN_ROWS = 512

BF = jnp.bfloat16
F32 = jnp.float32


def _dot(a, b):
    return jnp.dot(a, b, preferred_element_type=F32)


def _dot_nt(a, b):
    return lax.dot_general(a, b, (((1,), (1,)), ((), ())), preferred_element_type=F32)


def _dot_tn(a, b):
    return lax.dot_general(a, b, (((0,), (0,)), ((), ())), preferred_element_type=F32)


def _params(*sem):
    return pltpu.CompilerParams(dimension_semantics=sem, vmem_limit_bytes=V7X_VMEM_LIMIT_BYTES)


def _const_spec(shape):
    nd = len(shape)
    return pl.BlockSpec(shape, lambda *_: (0,) * nd)


def _layer_norm(x, g, b):
    xc = x - jnp.mean(x, axis=-1, keepdims=True)
    var = jnp.mean(xc * xc, axis=-1, keepdims=True)
    return xc * lax.rsqrt(var + LN_EPS) * g + b


def _row_tile(m, want):
    t = min(m, want)
    assert m % t == 0
    return t


def _col(c):
    return slice(c * LANES, (c + 1) * LANES)


_NARROW_SRC = ('dsa_k', 'dsa_v', 'idx_k', 'fox_f')
_ROW_OUTS = (
    ('sb_k', ('sb_k',), F32, 0, BRANCH_WIDTH, True), ('sb_k_bf', ('sb_k',), BF, 0, BRANCH_WIDTH, False),
    ('sb_v', ('sb_v',), F32, 0, BRANCH_WIDTH, True),
    ('fox_k', ('fox_k',), F32, 0, BRANCH_WIDTH, True), ('fox_k_bf', ('fox_k',), BF, 0, BRANCH_WIDTH, False),
    ('fox_v', ('fox_v',), F32, 0, BRANCH_WIDTH, True),
    ('dsa_k', _NARROW_SRC, F32, 0, HEAD_DIM, True), ('dsa_v', _NARROW_SRC, F32, HEAD_DIM, HEAD_DIM, True),
    ('idx_k', _NARROW_SRC, F32, 2 * HEAD_DIM, HEAD_DIM, True), ('fox_f', _NARROW_SRC, F32, 3 * HEAD_DIM, N_HEADS, True),
    ('dsa_kk_bf', ('dsa_k', 'dsa_k', 'idx_k', 'idx_k'), BF, 0, 4 * HEAD_DIM, False),
)
_COL_OUTS = (
    ('sb_q_t', ('sb_q',), BF), ('fox_q_t', ('fox_q',), BF), ('dsa_q_t', ('dsa_q',), BF),
    ('idx_q_t', ('idx_q',), BF), ('sb_v_t', ('sb_v',), BF), ('fox_v_t', ('fox_v',), BF),
    ('ret_q_t', ('ret_q',), F32), ('ret_k_t', ('ret_k',), F32), ('ret_v_t', ('ret_v',), F32),
    ('ret_g_t', ('ret_g',), F32), ('dsa_v_t', ('dsa_v',), BF), ('idx_w_t', ('idx_w',), F32),
)
BF16_ROWS_PER_VREG = 16


def _inproj_plan(d):
    widths = dict(_in_layout(d))

    def spans(outs, multiple):
        span_of, off = {}, 0
        for out in outs:
            srcs = out[1]
            if srcs not in span_of:
                w = -(-sum(widths[s] for s in srcs) // multiple) * multiple
                span_of[srcs] = (off, w)
                off += w
        return span_of, off

    row_spans, _ = spans(_ROW_OUTS, LANES)
    col_spans, n_col = spans(_COL_OUTS, BF16_ROWS_PER_VREG)
    return row_spans, col_spans, -(-n_col // LANES) * LANES


def _inproj_weights(w_in):
    d = w_in.shape[0]
    w_bf = w_in.astype(BF)
    pieces, off = {}, 0
    for name, width in _in_layout(d):
        piece = w_bf[:, off:off + width]
        scale = _FOLDED_SCALE.get(name)
        pieces[name] = piece if scale is None else piece * jnp.asarray(scale, BF)
        off += width
    assert off == w_in.shape[1]
    row_spans, col_spans, n_col = _inproj_plan(d)

    def block(srcs, width):
        w = jnp.concatenate([pieces[s] for s in srcs], axis=1)
        return jnp.pad(w, ((0, 0), (0, width - w.shape[1])))

    w_row = jnp.concatenate([block(srcs, w) for srcs, (_, w) in row_spans.items()], axis=1)
    w_col = jnp.concatenate([block(srcs, w) for srcs, (_, w) in col_spans.items()], axis=1)
    w_col = jnp.pad(w_col, ((0, 0), (0, n_col - w_col.shape[1])))
    return w_row.astype(BF), w_col.T.astype(BF), pieces['merge_gate'].astype(BF)


def _inproj_body(x_ref, wr_ref, wc_ref, *refs, row_spans, col_spans, n_alias):
    o_refs = refs[n_alias:]
    xb = x_ref[0].astype(BF)
    done = {}
    for o_ref, (_, srcs, _, lane, width, _) in zip(o_refs, _ROW_OUTS):
        if srcs not in done:
            off, w = row_spans[srcs]
            done[srcs] = _dot(xb, wr_ref[:, off:off + w])
        o_ref[...] = done[srcs][:, lane:lane + width].astype(o_ref.dtype).reshape(o_ref.shape)
    for o_ref, (_, srcs, _) in zip(o_refs[len(_ROW_OUTS):], _COL_OUTS):
        off, w = col_spans[srcs]
        o_ref[0] = _dot_nt(wc_ref[off:off + w, :], xb).astype(o_ref.dtype)


def _inproj(x, w_row, w_col, layer, depth, states):
    b, t, d = x.shape
    tm = _row_tile(t, 512)
    row_spans, col_spans, _ = _inproj_plan(d)
    out_shape, out_specs, state_names = [], [], []
    for name, _, dt, _, w, is_state in _ROW_OUTS:
        if is_state:
            state_names.append(name)
            out_shape.append(jax.ShapeDtypeStruct((depth, b, t, w), dt))
            out_specs.append(pl.BlockSpec((1, 1, tm, w), lambda bi, i: (layer, bi, i, 0)))
        else:
            out_shape.append(jax.ShapeDtypeStruct((b, t, w), dt))
            out_specs.append(pl.BlockSpec((1, tm, w), lambda bi, i: (bi, i, 0)))
    for _, srcs, dt in _COL_OUTS:
        w = col_spans[srcs][1]
        out_shape.append(jax.ShapeDtypeStruct((b, w, t), dt))
        out_specs.append(pl.BlockSpec((1, w, tm), lambda bi, i: (bi, 0, i)))
    prev = [] if states is None else [states[n] for n in state_names]
    names = [o[0] for o in _ROW_OUTS] + [o[0] for o in _COL_OUTS]
    aliases = {3 + j: names.index(n) for j, n in enumerate(state_names)} if prev else {}
    outs = pl.pallas_call(
        functools.partial(_inproj_body, row_spans=row_spans, col_spans=col_spans, n_alias=len(prev)),
        out_shape=out_shape,
        grid=(b, t // tm),
        in_specs=[pl.BlockSpec((1, tm, d), lambda bi, i: (bi, i, 0)),
                  _const_spec(w_row.shape), _const_spec(w_col.shape)]
                 + [pl.BlockSpec(memory_space=pl.ANY)] * len(prev),
        out_specs=out_specs,
        input_output_aliases=aliases,
        compiler_params=_params("parallel", "parallel"),
        name="inproj",
    )(x, w_row, w_col, *prev)
    return dict(zip(names, outs))


def _block_counts(q0, tk, last_key):
    return lax.div(q0, tk), lax.div(last_key, tk) + 1


def _head_queries(qt_ref):
    low = lax.broadcasted_iota(jnp.int32, (LANES, 1), 0) < HEAD_DIM
    out = []
    for h in range(N_HEADS):
        qc = qt_ref[0, _col(h // HEADS_PER_COL), :]
        keep = low if h % HEADS_PER_COL == 0 else jnp.logical_not(low)
        out.append(jnp.where(keep, qc, jnp.zeros_like(qc)))
    return out


def _head_rows(h):
    return slice(h * HEAD_DIM, (h + 1) * HEAD_DIM)


def _key_minus_query(tk, tq):
    return (lax.broadcasted_iota(jnp.int32, (tk, tq), 0) - lax.broadcasted_iota(jnp.int32, (tk, tq), 1))


def _qt_spec(w, tq):
    return pl.BlockSpec((1, w, tq), lambda bi, qi: (bi, 0, qi))


def _whole_spec(rows, cols):
    return pl.BlockSpec((1, rows, cols), lambda bi, qi: (bi, 0, 0))


def _sb_body(qt_ref, k_ref, vt_ref, o_ref, *, p_len, tq, tk):
    q0 = p_len + pl.program_id(1) * tq
    qm = _head_queries(qt_ref)
    diff = _key_minus_query(tk, tq)
    later = (lax.broadcasted_iota(jnp.int32, (tk, 2 * tk), 1) & (tk - 1)) > lax.broadcasted_iota(
        jnp.int32, (tk, 2 * tk), 0)
    minus_later = jnp.where(later, -1.0, 0.0).astype(BF)
    n_full, n_all = _block_counts(q0, tk, jnp.maximum(q0 + tq - 2, 0))

    def step(kb, carry, masked):
        laters, accs = carry
        s0 = pl.multiple_of(kb * tk, tk)
        if masked:
            earlier = diff < (q0 - s0)
        zs = [_dot(k_ref[0, pl.ds(s0, tk), _col(h // HEADS_PER_COL)], qm[h]) for h in range(N_HEADS)]
        new_laters, log_bs, afters = [], [], []
        for h in range(N_HEADS):
            z = zs[h]
            minus_abs = pltpu.bitcast(pltpu.bitcast(z, jnp.int32) | jnp.int32(INT_MIN), F32)
            softplus = jnp.maximum(z, 0.0) + jnp.log(1.0 + jnp.exp(minus_abs))
            log_bs.append(z - softplus)
            if masked:
                softplus = jnp.where(earlier, softplus, 0.0)
            hi = softplus.astype(BF)
            lo = (softplus - hi.astype(F32)).astype(BF)
            after = _dot(minus_later, jnp.concatenate([hi, lo], axis=0)) + laters[h]
            afters.append(after)
            new_laters.append(after[0:1, :] - softplus[0:1, :])
        new_accs = []
        for h in range(N_HEADS):
            w = jnp.exp(log_bs[h] + afters[h])
            if masked:
                w = jnp.where(earlier, w, 0.0)
            new_accs.append(accs[h] + _dot(vt_ref[0, _head_rows(h), pl.ds(s0, tk)], w.astype(BF)))
        return tuple(new_laters), tuple(new_accs)

    carry = (tuple(jnp.zeros((1, tq), F32) for _ in range(N_HEADS)),
             tuple(jnp.zeros((HEAD_DIM, tq), F32) for _ in range(N_HEADS)))
    carry = lax.fori_loop(0, n_all - n_full, lambda i, c: step(n_all - 1 - i, c, True), carry)
    carry = lax.fori_loop(0, n_full, lambda i, c: step(n_full - 1 - i, c, False), carry)
    for h in range(N_HEADS):
        o_ref[0, _head_rows(h), :] = carry[1][h].astype(o_ref.dtype)


def _sb_attention(qt, k, vt, p_len, tq, tk):
    b, w, t = qt.shape
    lp = k.shape[1]
    assert tk & (tk - 1) == 0
    return pl.pallas_call(
        functools.partial(_sb_body, p_len=p_len, tq=tq, tk=tk),
        out_shape=jax.ShapeDtypeStruct((b, w, t), BF),
        grid=(b, t // tq),
        in_specs=[_qt_spec(w, tq), _whole_spec(lp, w), _whole_spec(w, lp)],
        out_specs=_qt_spec(w, tq),
        compiler_params=_params("parallel", "arbitrary"),
        name="sb_attention",
    )(qt, k, vt)


def _online_softmax_step(logits, m, l):
    m_new = jnp.maximum(m, jnp.max(logits, axis=0, keepdims=True))
    alpha = jnp.exp(m - m_new)
    p = jnp.exp(logits - m_new)
    return m_new, alpha, alpha * l + jnp.sum(p, axis=0, keepdims=True), p


def _fox_body(qt_ref, k_ref, vt_ref, c_ref, o_ref, *, p_len, tq, tk):
    q0 = p_len + pl.program_id(1) * tq
    qm = _head_queries(qt_ref)
    diff = _key_minus_query(tk, tq)
    n_full, n_all = _block_counts(q0, tk, q0 + tq - 1)

    def key_bias(h, s0):
        c = c_ref[0, h, pl.ds(s0, tk), :]
        return c[:, :tq] if tq <= LANES else jnp.concatenate([c] * (tq // LANES), axis=1)

    def step(kb, carry, masked):
        ms, ls, accs = carry
        s0 = pl.multiple_of(kb * tk, tk)
        if masked:
            visible = diff <= (q0 - s0)
        new_ms, new_ls, new_accs = [], [], []
        raw = [_dot(k_ref[0, pl.ds(s0, tk), _col(h // HEADS_PER_COL)], qm[h]) for h in range(N_HEADS)]
        for h in range(N_HEADS):
            logits = raw[h] - key_bias(h, s0)
            if masked:
                logits = jnp.where(visible, logits, MASK_VALUE)
            m_new, alpha, l_new, p = _online_softmax_step(logits, ms[h], ls[h])
            new_ms.append(m_new)
            new_ls.append(l_new)
            new_accs.append(alpha * accs[h] + _dot(vt_ref[0, _head_rows(h), pl.ds(s0, tk)], p.astype(BF)))
        return tuple(new_ms), tuple(new_ls), tuple(new_accs)

    carry = (tuple(jnp.full((1, tq), -jnp.inf, F32) for _ in range(N_HEADS)),
             tuple(jnp.zeros((1, tq), F32) for _ in range(N_HEADS)),
             tuple(jnp.zeros((HEAD_DIM, tq), F32) for _ in range(N_HEADS)))
    carry = lax.fori_loop(0, n_full, lambda i, c: step(i, c, False), carry)
    carry = lax.fori_loop(n_full, n_all, lambda i, c: step(i, c, True), carry)
    _, ls, accs = carry
    for h in range(N_HEADS):
        o_ref[0, _head_rows(h), :] = (accs[h] / ls[h]).astype(o_ref.dtype)


def _fox_attention(qt, k, vt, c_lanes, p_len, tq, tk):
    b, w, t = qt.shape
    lp = k.shape[1]
    assert tq <= LANES or tq % LANES == 0
    return pl.pallas_call(
        functools.partial(_fox_body, p_len=p_len, tq=tq, tk=tk),
        out_shape=jax.ShapeDtypeStruct((b, w, t), BF),
        grid=(b, t // tq),
        in_specs=[_qt_spec(w, tq), _whole_spec(lp, w), _whole_spec(w, lp),
                  pl.BlockSpec((1, N_HEADS, lp, LANES), lambda bi, qi: (bi, 0, 0, 0))],
        out_specs=_qt_spec(w, tq),
        compiler_params=_params("parallel", "arbitrary"),
        name="fox_attention",
    )(qt, k, vt, c_lanes)


def _float_key(bits):
    return jnp.where(bits < 0, jnp.int32(INT_MIN) - bits, bits)


def _dsa_body(qt_ref, qit_ref, wit_ref, k_ref, vt_ref, ki_ref, o_ref, keys_ref, *, p_len, n_keys, tq, tk, top_k):
    q0 = p_len + pl.program_id(1) * tq
    qpos = q0 + lax.broadcasted_iota(jnp.int32, (1, tq), 1)
    limit = jnp.minimum(((qpos >> CHUNK_SHIFT) + 1) << CHUNK_SHIFT, n_keys)
    last_limit = jnp.minimum((((q0 + tq - 1) >> CHUNK_SHIFT) + 1) << CHUNK_SHIFT, n_keys)
    n_blk = lax.div(last_limit - 1, tk) + 1
    key_row = lax.broadcasted_iota(jnp.int32, (tk, tq), 0)

    wit = wit_ref[0]
    qim = _head_queries(qit_ref)

    def score_step(kb, _):
        s0 = pl.multiple_of(kb * tk, tk)
        ki = ki_ref[0, pl.ds(s0, tk), :]
        score = jnp.zeros((tk, tq), F32)
        for h in range(N_HEADS):
            score = score + wit[h:h + 1, :] * jnp.maximum(_dot(ki, qim[h]), 0.0)
        score = jnp.where(key_row < limit - s0, score, -jnp.inf)
        keys_ref[pl.ds(s0, tk), :] = _float_key(pltpu.bitcast(score, jnp.int32))
        return 0

    lax.fori_loop(0, n_blk, score_step, 0)

    def count(pred):
        def body(kb, acc):
            for r in range(0, tk, COUNT_SLAB):
                s0 = pl.multiple_of(kb * tk + r, COUNT_SLAB)
                acc = acc + jnp.where(pred(keys_ref[pl.ds(s0, COUNT_SLAB), :]), 1.0, 0.0)
            return acc
        acc = lax.fori_loop(0, n_blk, body, jnp.zeros((COUNT_SLAB, tq), F32))
        return jnp.sum(acc, axis=0, keepdims=True)

    kf = jnp.float32(top_k)
    zero = jnp.zeros((1, tq), jnp.int32)
    thr = jnp.where(count(lambda key: key >= zero) >= kf, zero, jnp.int32(INT_MIN))

    def bit_step(it, thr):
        cand = thr + lax.shift_left(jnp.int32(1), 30 - it)
        return jnp.where(count(lambda key: key >= cand) >= kf, cand, thr)

    thr = lax.fori_loop(0, 31, bit_step, thr)
    n_tie_wanted = kf - count(lambda key: key > thr)

    earlier_keys = (lax.broadcasted_iota(jnp.int32, (tk, tk), 1)
                    < lax.broadcasted_iota(jnp.int32, (tk, tk), 0)).astype(BF)
    qm = _head_queries(qt_ref)

    def attend_step(kb, carry):
        ties_seen, ms, ls, accs = carry
        s0 = pl.multiple_of(kb * tk, tk)
        key = keys_ref[pl.ds(s0, tk), :]
        tie = jnp.where(key == thr, 1.0, 0.0)
        tie_rank = _dot(earlier_keys, tie.astype(BF)) + ties_seen
        take = jnp.where(key > thr, 1.0, jnp.where(tie_rank < n_tie_wanted, tie, 0.0))
        selected = jnp.where(key_row < limit - s0, take, 0.0) > 0.0
        k = k_ref[0, pl.ds(s0, tk), :]
        vt = vt_ref[0, :, pl.ds(s0, tk)]
        new_ms, new_ls, new_accs = [], [], []
        raw = [_dot(k, qm[h]) for h in range(N_HEADS)]
        for h in range(N_HEADS):
            logits = jnp.where(selected, raw[h], MASK_VALUE)
            m_new, alpha, l_new, p = _online_softmax_step(logits, ms[h], ls[h])
            new_ms.append(m_new)
            new_ls.append(l_new)
            new_accs.append(alpha * accs[h] + _dot(vt, p.astype(BF)))
        ties_seen = ties_seen + jnp.sum(tie, axis=0, keepdims=True)
        return ties_seen, tuple(new_ms), tuple(new_ls), tuple(new_accs)

    carry = (jnp.zeros((1, tq), F32),
             tuple(jnp.full((1, tq), -jnp.inf, F32) for _ in range(N_HEADS)),
             tuple(jnp.zeros((1, tq), F32) for _ in range(N_HEADS)),
             tuple(jnp.zeros((HEAD_DIM, tq), F32) for _ in range(N_HEADS)))
    _, _, ls, accs = lax.fori_loop(0, n_blk, attend_step, carry)
    for h in range(N_HEADS):
        o_ref[0, _head_rows(h), :] = (accs[h] / ls[h]).astype(o_ref.dtype)


def _dsa_attention(qt, qit, wit, kk, vt, p_len, n_keys, tq, tk, top_k):
    b, w, t = qt.shape
    lp = kk.shape[1]
    return pl.pallas_call(
        functools.partial(_dsa_body, p_len=p_len, n_keys=n_keys, tq=tq, tk=tk, top_k=top_k),
        out_shape=jax.ShapeDtypeStruct((b, w, t), BF),
        grid=(b, t // tq),
        in_specs=[_qt_spec(w, tq), _qt_spec(w, tq), _qt_spec(wit.shape[1], tq),
                  pl.BlockSpec((1, lp, LANES), lambda bi, qi: (bi, 0, 0)), _whole_spec(HEAD_DIM, lp),
                  pl.BlockSpec((1, lp, LANES), lambda bi, qi: (bi, 0, 1))],
        out_specs=_qt_spec(w, tq),
        scratch_shapes=[pltpu.VMEM((lp, tq), jnp.int32)],
        compiler_params=_params("parallel", "arbitrary"),
        name="dsa_attention",
    )(qt, qit, wit, kk, vt, kk)


def _ret_body(q_ref, k_ref, v_ref, g_ref, s0_ref, cos_ref, sin_ref, dec_ref, qd_ref, kd_ref, sd_ref,
              o_ref, so_ref, state_ref):
    c = pl.program_id(1)

    @pl.when(c == 0)
    def _():
        state_ref[...] = s0_ref[0]

    cos, sin = cos_ref[...], sin_ref[...]
    half = HEAD_DIM // 2

    def rotary(x):
        x1, x2 = x[:half], x[half:]
        return jnp.concatenate([x1 * cos - x2 * sin, x2 * cos + x1 * sin], axis=0)

    heads = range(N_HEADS)
    qb = [rotary(q_ref[0, _head_rows(h), :]).astype(BF) for h in heads]
    k = [rotary(k_ref[0, _head_rows(h), :]) for h in heads]
    vb = [v_ref[0, _head_rows(h), :].astype(BF) for h in heads]
    scores_t = [_dot_tn(k[h].astype(BF), qb[h]) * dec_ref[h] for h in heads]
    carried = [_dot(state_ref[h].astype(BF), qb[h]) * qd_ref[h] for h in heads]
    outs = [_dot(vb[h], scores_t[h].astype(BF)) + carried[h] for h in heads]
    for h in heads:
        state_ref[h] = sd_ref[h] * state_ref[h] + _dot_nt(vb[h], (k[h] * kd_ref[h]).astype(BF))
    for h in heads:
        o = outs[h]
        oc = o - jnp.mean(o, axis=0, keepdims=True)
        on = oc * lax.rsqrt(jnp.mean(oc * oc, axis=0, keepdims=True) + LN_EPS)
        g = g_ref[0, _head_rows(h), :]
        o_ref[0, _head_rows(h), :] = (on * (g * jax.nn.sigmoid(g))).astype(o_ref.dtype)

    @pl.when(c == pl.num_programs(1) - 1)
    def _():
        so_ref[0] = state_ref[...]


def _retention(qt, kt, vt, gt, state0_t, pos, c):
    b, w, t = qt.shape
    h = w // HEAD_DIM
    half = HEAD_DIM // 2
    inv_freq = ROPE_BASE ** (-jnp.arange(half, dtype=F32) / half)
    ang = inv_freq[:, None] * pos.astype(F32)[None, :]
    log_gamma = np.log(1.0 - 2.0 ** (-5.0 - np.arange(h, dtype=np.float64)))
    n = np.arange(c, dtype=np.float64)
    rel = n[None, :] - n[:, None]
    decay_t = np.where(rel >= 0, np.exp(np.maximum(rel, 0.0)[None] * log_gamma[:, None, None]), 0.0)
    q_decay = np.exp((n[None, :] + 1.0) * log_gamma[:, None])[:, None, :]
    k_decay = np.exp((c - 1.0 - n)[None, :] * log_gamma[:, None])[:, None, :]
    s_decay = np.exp(c * log_gamma)[:, None, None]
    tables = [jnp.asarray(a, F32) for a in (decay_t, q_decay, k_decay, s_decay)]
    assert h == N_HEADS
    x_spec = pl.BlockSpec((1, w, c), lambda bi, ci: (bi, 0, ci))
    s_spec = pl.BlockSpec((1, h, HEAD_DIM, HEAD_DIM), lambda bi, ci: (bi, 0, 0, 0))
    rope_spec = pl.BlockSpec((half, c), lambda bi, ci: (0, ci))
    return pl.pallas_call(
        _ret_body,
        out_shape=(jax.ShapeDtypeStruct((b, w, t), BF),
                   jax.ShapeDtypeStruct((b, h, HEAD_DIM, HEAD_DIM), F32)),
        grid=(b, t // c),
        in_specs=[x_spec, x_spec, x_spec, x_spec, s_spec, rope_spec, rope_spec] + [_const_spec(a.shape) for a in tables],
        out_specs=(x_spec, s_spec),
        scratch_shapes=[pltpu.VMEM((h, HEAD_DIM, HEAD_DIM), F32)],
        compiler_params=_params("parallel", "arbitrary"),
        name="retention",
    )(qt, kt, vt, gt, state0_t, jnp.cos(ang), jnp.sin(ang), *tables)


def _merge_body(h_ref, y0_ref, y1_ref, y2_ref, y3_ref, wg_ref, wb_ref, wo_ref, g_ref, b_ref, o_ref, *, alpha):
    nb, tm, d = h_ref.shape
    h = h_ref[...].reshape(nb * tm, d)
    hb = h.astype(BF)
    merged = jnp.zeros(h.shape, F32)
    for i, y_ref in enumerate((y0_ref, y1_ref, y2_ref, y3_ref)):
        gate = jax.nn.sigmoid(_dot(hb, wg_ref[:, i * d:(i + 1) * d]))
        branch = jnp.concatenate([_dot_tn(y_ref[j], wb_ref[i]) for j in range(nb)], axis=0)
        merged = merged + gate * branch
    r = alpha * h + _dot(merged.astype(BF), wo_ref[...])
    o_ref[...] = _layer_norm(r, g_ref[...], b_ref[...]).reshape(nb, tm, d)


def _merge(h, ys_t, w_gate, w_branch, w_out, ln_g, ln_b, alpha):
    b, t, d = h.shape
    tm = _row_tile(t, MERGE_ROWS)
    nb = math.gcd(b, max(1, MERGE_ROWS // tm))
    row = pl.BlockSpec((nb, tm, d), lambda bi, i: (bi, i, 0))
    col = pl.BlockSpec((nb, BRANCH_WIDTH, tm), lambda bi, i: (bi, 0, i))
    return pl.pallas_call(
        functools.partial(_merge_body, alpha=alpha),
        out_shape=jax.ShapeDtypeStruct((b, t, d), F32),
        grid=(b // nb, t // tm),
        in_specs=[row] + [col] * 4
                 + [_const_spec(w_gate.shape), _const_spec(w_branch.shape), _const_spec(w_out.shape),
                    _const_spec((1, d)), _const_spec((1, d))],
        out_specs=row,
        compiler_params=_params("parallel", "parallel"),
        name="merge",
    )(h, *ys_t, w_gate, w_branch, w_out, ln_g, ln_b)


def _ffn_body(h_ref, wi_ref, wo_ref, g_ref, b_ref, o_ref, *, alpha, f_chunk):
    h = h_ref[...]
    hb = h.astype(BF)
    f = wo_ref.shape[0]
    acc = jnp.zeros(h.shape, F32)
    for c in range(0, f, f_chunk):
        a = _dot(hb, wi_ref[:, c:c + f_chunk])
        u = _dot(hb, wi_ref[:, f + c:f + c + f_chunk])
        acc = acc + _dot((a * jax.nn.sigmoid(a) * u).astype(BF), wo_ref[c:c + f_chunk, :])
    o_ref[...] = _layer_norm(alpha * h + acc, g_ref[...], b_ref[...])


def _ffn(h, w_in, w_out, ln_g, ln_b, alpha):
    m, d = h.shape
    f = w_out.shape[0]
    tm = _row_tile(m, FFN_ROWS)
    f_chunk = f // 2 if (f // 2) % LANES == 0 else f
    row = pl.BlockSpec((tm, d), lambda i: (i, 0))
    return pl.pallas_call(
        functools.partial(_ffn_body, alpha=alpha, f_chunk=f_chunk),
        out_shape=jax.ShapeDtypeStruct((m, d), F32),
        grid=(m // tm,),
        in_specs=[row, _const_spec(w_in.shape), _const_spec(w_out.shape),
                  _const_spec((1, d)), _const_spec((1, d))],
        out_specs=row,
        compiler_params=_params("parallel"),
        name="ffn",
    )(h, w_in, w_out, ln_g, ln_b)


def _in_layout(d):
    w = BRANCH_WIDTH
    return (('sb_q', w), ('sb_k', w), ('sb_v', w), ('ret_q', w), ('ret_k', w), ('ret_v', w), ('ret_g', w),
            ('fox_q', w), ('fox_k', w), ('fox_v', w), ('fox_f', N_HEADS),
            ('dsa_q', w), ('dsa_k', HEAD_DIM), ('dsa_v', HEAD_DIM),
            ('idx_q', w), ('idx_k', HEAD_DIM), ('idx_w', N_HEADS), ('merge_gate', 4 * d))


_FOLDED_SCALE = dict(sb_q=QK_SCALE, fox_q=QK_SCALE, dsa_q=QK_SCALE, idx_q=QK_SCALE, ret_k=QK_SCALE,
                     idx_w=IDX_HEAD_SCALE)


def _swap(a):
    return jnp.swapaxes(a, -1, -2)


def _key_tiles(t, n_keys):
    tq = min(t, 512)
    tiles = dict(sb=256, fox=512, dsa=512)
    padded = {name: -(-n_keys // tk) * tk for name, tk in tiles.items()}
    return tq, tiles, padded


def _layer(h, b, t, past, ret_state, w, alpha, layer, depth, states):
    m, d = h.shape
    p_len = 0 if past is None else past[0].shape[1]
    n_keys = p_len + t
    tq, tk, lp = _key_tiles(t, n_keys)
    p = _inproj(h.reshape(b, t, d), w['w_row'], w['w_col'], layer, depth, states)
    states = {name: p[name] for name, _, _, _, _, is_state in _ROW_OUTS if is_state}

    old = (None,) * 8 if past is None else past
    sb_k0, sb_v0, fox_k0, fox_v0, fox_lf0, dsa_k0, dsa_v0, dsa_ki0 = old

    def rows_with_past(new_bf, olds, lp_):
        if past is not None:
            flat = [o.reshape(o.shape[0], o.shape[1], -1).astype(BF) for o in olds]
            new_bf = jnp.concatenate([jnp.concatenate(flat, axis=2), new_bf], axis=1)
        return jnp.pad(new_bf, ((0, 0), (0, lp_ - new_bf.shape[1]), (0, 0)))

    def cols_with_past(new_t, old, lp_):
        if old is not None:
            new_t = jnp.concatenate([_swap(old.reshape(old.shape[0], old.shape[1], -1).astype(BF)), new_t], axis=2)
        return jnp.pad(new_t, ((0, 0), (0, 0), (0, lp_ - new_t.shape[2])))

    y_sb = _sb_attention(p['sb_q_t'], rows_with_past(p['sb_k_bf'], [sb_k0], lp['sb']),
                         cols_with_past(p['sb_v_t'], sb_v0, lp['sb']), p_len, tq, tk['sb'])

    pos = p_len + jnp.arange(t, dtype=jnp.int32)
    y_ret, ret_state_t = _retention(p['ret_q_t'], p['ret_k_t'], p['ret_v_t'], p['ret_g_t'], _swap(ret_state),
                                    pos, min(t, RETENTION_CHUNK))

    fox_lf = jax.nn.log_sigmoid(states['fox_f'][layer] + w['b_forget'])
    lf_all = fox_lf if fox_lf0 is None else jnp.concatenate([fox_lf0, fox_lf], axis=1)
    cum = jnp.pad(jnp.cumsum(lf_all, axis=1), ((0, 0), (0, lp['fox'] - n_keys), (0, 0)))
    c_lanes = jnp.broadcast_to(_swap(cum)[..., None], (b, N_HEADS, lp['fox'], LANES))
    y_fox = _fox_attention(p['fox_q_t'], rows_with_past(p['fox_k_bf'], [fox_k0], lp['fox']),
                           cols_with_past(p['fox_v_t'], fox_v0, lp['fox']), c_lanes, p_len, tq, tk['fox'])

    top_k = min(DSA_TOP_K, n_keys // 4)
    kk_old = [dsa_k0, dsa_k0, dsa_ki0, dsa_ki0]
    y_dsa = _dsa_attention(p['dsa_q_t'], p['idx_q_t'], p['idx_w_t'], rows_with_past(p['dsa_kk_bf'], kk_old, lp['dsa']),
                           cols_with_past(p['dsa_v_t'], dsa_v0, lp['dsa']), p_len, n_keys, tq, tk['dsa'], top_k)

    h = _merge(h.reshape(b, t, d), (y_sb, y_ret, y_fox, y_dsa), w['w_gate'], w['w_branch'], w['w_out'],
               w['ln1_g'], w['ln1_b'], alpha)
    h = _ffn(h.reshape(m, d), w['w_ffn_in'], w['w_ffn_out'], w['ln2_g'], w['ln2_b'], alpha)
    return h, states, _swap(ret_state_t), fox_lf


def _group_outputs(states, ret_states, fox_lfs):
    def heads(a):
        return a.reshape(a.shape[:-1] + (N_HEADS, HEAD_DIM))

    return (heads(states['sb_k']), heads(states['sb_v']), jnp.stack(ret_states), heads(states['fox_k']),
            heads(states['fox_v']), jnp.stack(fox_lfs), states['dsa_k'], states['dsa_v'], states['idx_k'])


def kernel(x_prompt, x_sample, cache_sb_k, cache_sb_v, state_ret, cache_fox_k, cache_fox_v, cache_fox_logf,
           cache_dsa_k, cache_dsa_v, cache_dsa_kidx, w_in, b_forget, w_branch, w_out, ln1_g, ln1_b,
           w_ffn_in, w_ffn_out, ln2_g, ln2_b):
    depth = w_in.shape[0]
    alpha = float((2 * depth) ** 0.25)
    bp, tp, d = x_prompt.shape
    bs, ts, _ = x_sample.shape
    hp = x_prompt.reshape(bp * tp, d)
    hs = x_sample.reshape(bs * ts, d)
    ret_zero = jnp.zeros((bp, N_HEADS, HEAD_DIM, HEAD_DIM), F32)
    st_p, st_s, ret_p, ret_s, lf_p, lf_s = None, None, [], [], [], []
    for l in range(depth):
        w_row, w_col, w_gate = _inproj_weights(w_in[l])
        w = dict(w_row=w_row, w_col=w_col, w_gate=w_gate, b_forget=b_forget[l], w_branch=w_branch[l].astype(BF),
                 w_out=w_out[l].astype(BF), ln1_g=ln1_g[l][None], ln1_b=ln1_b[l][None],
                 w_ffn_in=w_ffn_in[l].astype(BF), w_ffn_out=w_ffn_out[l].astype(BF),
                 ln2_g=ln2_g[l][None], ln2_b=ln2_b[l][None])
        hp, st_p, ret, lf = _layer(hp, bp, tp, None, ret_zero, w, alpha, l, depth, st_p)
        ret_p.append(ret)
        lf_p.append(lf)
        past = (cache_sb_k[l], cache_sb_v[l], cache_fox_k[l], cache_fox_v[l], cache_fox_logf[l],
                cache_dsa_k[l], cache_dsa_v[l], cache_dsa_kidx[l])
        hs, st_s, ret, lf = _layer(hs, bs, ts, past, state_ret[l], w, alpha, l, depth, st_s)
        ret_s.append(ret)
        lf_s.append(lf)
    return ((hp.reshape(bp, tp, d), hs.reshape(bs, ts, d))
            + _group_outputs(st_p, ret_p, lf_p) + _group_outputs(st_s, ret_s, lf_s))
```

```python
import functools
import math

import numpy as np
import jax
import jax.numpy as jnp
from jax import lax
from jax.experimental import pallas as pl
from jax.experimental.pallas import tpu as pltpu

HEAD_DIM = 64
N_HEADS = 4
BRANCH_WIDTH = N_HEADS * HEAD_DIM
CHUNK_SHIFT = 6
DSA_TOP_K = 256
ROPE_BASE = 10000.0
LN_EPS = 1e-5
QK_SCALE = HEAD_DIM ** -0.5
IDX_HEAD_SCALE = N_HEADS ** -0.5
MASK_VALUE = -1e30
INT_MIN = -2 ** 31

V7X_VMEM_LIMIT_BYTES = 56 * 1024 * 1024
LANES = 128
HEADS_PER_COL = LANES // HEAD_DIM
SUBLANES = 8
WORD_BITS = 32
GROUP_KEYS = SUBLANES * WORD_BITS
RETENTION_CHUNK = 512
MERGE_ROWS = 512
FFN_ROWS = 512

BF = jnp.bfloat16
F32 = jnp.float32


def _dot(a, b):
    return jnp.dot(a, b, preferred_element_type=F32)


def _dot_nt(a, b):
    return lax.dot_general(a, b, (((1,), (1,)), ((), ())), preferred_element_type=F32)


def _dot_tn(a, b):
    return lax.dot_general(a, b, (((0,), (0,)), ((), ())), preferred_element_type=F32)


def _params(*sem):
    return pltpu.CompilerParams(dimension_semantics=sem, vmem_limit_bytes=V7X_VMEM_LIMIT_BYTES)


def _const_spec(shape):
    nd = len(shape)
    return pl.BlockSpec(shape, lambda *_: (0,) * nd)


def _layer_norm(x, g, b):
    xc = x - jnp.mean(x, axis=-1, keepdims=True)
    var = jnp.mean(xc * xc, axis=-1, keepdims=True)
    return xc * lax.rsqrt(var + LN_EPS) * g + b


def _row_tile(m, want):
    t = min(m, want)
    assert m % t == 0
    return t


def _col(c):
    return slice(c * LANES, (c + 1) * LANES)


_NARROW_SRC = ('dsa_k', 'dsa_v', 'idx_k', 'fox_f')
_ROW_OUTS = (
    ('sb_k', ('sb_k',), F32, 0, BRANCH_WIDTH, True), ('sb_k_bf', ('sb_k',), BF, 0, BRANCH_WIDTH, False),
    ('sb_v', ('sb_v',), F32, 0, BRANCH_WIDTH, True),
    ('fox_k', ('fox_k',), F32, 0, BRANCH_WIDTH, True), ('fox_k_bf', ('fox_k',), BF, 0, BRANCH_WIDTH, False),
    ('fox_v', ('fox_v',), F32, 0, BRANCH_WIDTH, True),
    ('dsa_k', _NARROW_SRC, F32, 0, HEAD_DIM, True), ('dsa_v', _NARROW_SRC, F32, HEAD_DIM, HEAD_DIM, True),
    ('idx_k', _NARROW_SRC, F32, 2 * HEAD_DIM, HEAD_DIM, True), ('fox_f', _NARROW_SRC, F32, 3 * HEAD_DIM, N_HEADS, True),
    ('dsa_kk_bf', ('dsa_k', 'dsa_k', 'idx_k', 'idx_k'), BF, 0, 4 * HEAD_DIM, False),
)
_COL_OUTS = (
    ('sb_q_t', ('sb_q',), BF), ('fox_q_t', ('fox_q',), BF), ('dsa_q_t', ('dsa_q',), BF),
    ('idx_q_t', ('idx_q',), BF), ('sb_v_t', ('sb_v',), BF), ('fox_v_t', ('fox_v',), BF),
    ('ret_q_t', ('ret_q',), F32), ('ret_k_t', ('ret_k',), F32), ('ret_v_t', ('ret_v',), F32),
    ('ret_g_t', ('ret_g',), F32), ('dsa_v_t', ('dsa_v',), BF), ('idx_w_t', ('idx_w',), F32),
)
BF16_ROWS_PER_VREG = 16


def _inproj_plan(d):
    widths = dict(_in_layout(d))

    def spans(outs, multiple):
        span_of, off = {}, 0
        for out in outs:
            srcs = out[1]
            if srcs not in span_of:
                w = -(-sum(widths[s] for s in srcs) // multiple) * multiple
                span_of[srcs] = (off, w)
                off += w
        return span_of, off

    row_spans, _ = spans(_ROW_OUTS, LANES)
    col_spans, n_col = spans(_COL_OUTS, BF16_ROWS_PER_VREG)
    return row_spans, col_spans, -(-n_col // LANES) * LANES


def _inproj_weights(w_in):
    d = w_in.shape[0]
    w_bf = w_in.astype(BF)
    pieces, off = {}, 0
    for name, width in _in_layout(d):
        piece = w_bf[:, off:off + width]
        scale = _FOLDED_SCALE.get(name)
        pieces[name] = piece if scale is None else piece * jnp.asarray(scale, BF)
        off += width
    assert off == w_in.shape[1]
    row_spans, col_spans, n_col = _inproj_plan(d)

    def block(srcs, width):
        w = jnp.concatenate([pieces[s] for s in srcs], axis=1)
        return jnp.pad(w, ((0, 0), (0, width - w.shape[1])))

    w_row = jnp.concatenate([block(srcs, w) for srcs, (_, w) in row_spans.items()], axis=1)
    w_col = jnp.concatenate([block(srcs, w) for srcs, (_, w) in col_spans.items()], axis=1)
    w_col = jnp.pad(w_col, ((0, 0), (0, n_col - w_col.shape[1])))
    return w_row.astype(BF), w_col.T.astype(BF), pieces['merge_gate'].astype(BF)


def _inproj_body(x_ref, wr_ref, wc_ref, *refs, row_spans, col_spans, n_alias):
    o_refs = refs[n_alias:]
    xb = x_ref[0].astype(BF)
    done = {}
    for o_ref, (_, srcs, _, lane, width, _) in zip(o_refs, _ROW_OUTS):
        if srcs not in done:
            off, w = row_spans[srcs]
            done[srcs] = _dot(xb, wr_ref[:, off:off + w])
        o_ref[...] = done[srcs][:, lane:lane + width].astype(o_ref.dtype).reshape(o_ref.shape)
    for o_ref, (_, srcs, _) in zip(o_refs[len(_ROW_OUTS):], _COL_OUTS):
        off, w = col_spans[srcs]
        o_ref[0] = _dot_nt(wc_ref[off:off + w, :], xb).astype(o_ref.dtype)


def _inproj(x, w_row, w_col, layer, depth, states):
    b, t, d = x.shape
    tm = _row_tile(t, 512)
    row_spans, col_spans, _ = _inproj_plan(d)
    out_shape, out_specs, state_names = [], [], []
    for name, _, dt, _, w, is_state in _ROW_OUTS:
        if is_state:
            state_names.append(name)
            out_shape.append(jax.ShapeDtypeStruct((depth, b, t, w), dt))
            out_specs.append(pl.BlockSpec((1, 1, tm, w), lambda bi, i: (layer, bi, i, 0)))
        else:
            out_shape.append(jax.ShapeDtypeStruct((b, t, w), dt))
            out_specs.append(pl.BlockSpec((1, tm, w), lambda bi, i: (bi, i, 0)))
    for _, srcs, dt in _COL_OUTS:
        w = col_spans[srcs][1]
        out_shape.append(jax.ShapeDtypeStruct((b, w, t), dt))
        out_specs.append(pl.BlockSpec((1, w, tm), lambda bi, i: (bi, 0, i)))
    prev = [] if states is None else [states[n] for n in state_names]
    names = [o[0] for o in _ROW_OUTS] + [o[0] for o in _COL_OUTS]
    aliases = {3 + j: names.index(n) for j, n in enumerate(state_names)} if prev else {}
    outs = pl.pallas_call(
        functools.partial(_inproj_body, row_spans=row_spans, col_spans=col_spans, n_alias=len(prev)),
        out_shape=out_shape,
        grid=(b, t // tm),
        in_specs=[pl.BlockSpec((1, tm, d), lambda bi, i: (bi, i, 0)),
                  _const_spec(w_row.shape), _const_spec(w_col.shape)]
                 + [pl.BlockSpec(memory_space=pl.ANY)] * len(prev),
        out_specs=out_specs,
        input_output_aliases=aliases,
        compiler_params=_params("parallel", "parallel"),
        name="inproj",
    )(x, w_row, w_col, *prev)
    return dict(zip(names, outs))


def _block_counts(q0, tk, last_key):
    return lax.div(q0, tk), lax.div(last_key, tk) + 1


def _head_queries(qt_ref):
    low = lax.broadcasted_iota(jnp.int32, (LANES, 1), 0) < HEAD_DIM
    out = []
    for h in range(N_HEADS):
        qc = qt_ref[0, _col(h // HEADS_PER_COL), :]
        keep = low if h % HEADS_PER_COL == 0 else jnp.logical_not(low)
        out.append(jnp.where(keep, qc, jnp.zeros_like(qc)))
    return out


def _head_rows(h):
    return slice(h * HEAD_DIM, (h + 1) * HEAD_DIM)


def _key_minus_query(tk, tq):
    return (lax.broadcasted_iota(jnp.int32, (tk, tq), 0) - lax.broadcasted_iota(jnp.int32, (tk, tq), 1))


def _qt_spec(w, tq):
    return pl.BlockSpec((1, w, tq), lambda bi, qi: (bi, 0, qi))


def _whole_spec(rows, cols):
    return pl.BlockSpec((1, rows, cols), lambda bi, qi: (bi, 0, 0))


def _sb_body(qt_ref, k_ref, vt_ref, o_ref, *, p_len, tq, tk):
    q0 = p_len + pl.program_id(1) * tq
    qm = _head_queries(qt_ref)
    diff = _key_minus_query(tk, tq)
    later = (lax.broadcasted_iota(jnp.int32, (tk, 2 * tk), 1) & (tk - 1)) > lax.broadcasted_iota(
        jnp.int32, (tk, 2 * tk), 0)
    minus_later = jnp.where(later, -1.0, 0.0).astype(BF)
    n_full, n_all = _block_counts(q0, tk, jnp.maximum(q0 + tq - 2, 0))

    def step(kb, carry, masked):
        laters, accs = carry
        s0 = pl.multiple_of(kb * tk, tk)
        if masked:
            earlier = diff < (q0 - s0)
        zs = [_dot(k_ref[0, pl.ds(s0, tk), _col(h // HEADS_PER_COL)], qm[h]) for h in range(N_HEADS)]
        new_laters, log_bs, afters = [], [], []
        for h in range(N_HEADS):
            z = zs[h]
            minus_abs = pltpu.bitcast(pltpu.bitcast(z, jnp.int32) | jnp.int32(INT_MIN), F32)
            softplus = jnp.maximum(z, 0.0) + jnp.log(1.0 + jnp.exp(minus_abs))
            log_bs.append(z - softplus)
            if masked:
                softplus = jnp.where(earlier, softplus, 0.0)
            hi = softplus.astype(BF)
            lo = (softplus - hi.astype(F32)).astype(BF)
            after = _dot(minus_later, jnp.concatenate([hi, lo], axis=0)) + laters[h]
            afters.append(after)
            new_laters.append(after[0:1, :] - softplus[0:1, :])
        new_accs = []
        for h in range(N_HEADS):
            w = jnp.exp(log_bs[h] + afters[h])
            if masked:
                w = jnp.where(earlier, w, 0.0)
            new_accs.append(accs[h] + _dot(vt_ref[0, _head_rows(h), pl.ds(s0, tk)], w.astype(BF)))
        return tuple(new_laters), tuple(new_accs)

    carry = (tuple(jnp.zeros((1, tq), F32) for _ in range(N_HEADS)),
             tuple(jnp.zeros((HEAD_DIM, tq), F32) for _ in range(N_HEADS)))
    carry = lax.fori_loop(0, n_all - n_full, lambda i, c: step(n_all - 1 - i, c, True), carry)
    carry = lax.fori_loop(0, n_full, lambda i, c: step(n_full - 1 - i, c, False), carry)
    for h in range(N_HEADS):
        o_ref[0, _head_rows(h), :] = carry[1][h].astype(o_ref.dtype)


def _sb_attention(qt, k, vt, p_len, tq, tk):
    b, w, t = qt.shape
    lp = k.shape[1]
    assert tk & (tk - 1) == 0
    return pl.pallas_call(
        functools.partial(_sb_body, p_len=p_len, tq=tq, tk=tk),
        out_shape=jax.ShapeDtypeStruct((b, w, t), BF),
        grid=(b, t // tq),
        in_specs=[_qt_spec(w, tq), _whole_spec(lp, w), _whole_spec(w, lp)],
        out_specs=_qt_spec(w, tq),
        compiler_params=_params("parallel", "arbitrary"),
        name="sb_attention",
    )(qt, k, vt)


def _online_softmax_step(logits, m, l):
    m_new = jnp.maximum(m, jnp.max(logits, axis=0, keepdims=True))
    alpha = jnp.exp(m - m_new)
    p = jnp.exp(logits - m_new)
    return m_new, alpha, alpha * l + jnp.sum(p, axis=0, keepdims=True), p


def _fox_body(qt_ref, k_ref, vt_ref, c_ref, o_ref, *, p_len, tq, tk):
    q0 = p_len + pl.program_id(1) * tq
    qm = _head_queries(qt_ref)
    diff = _key_minus_query(tk, tq)
    n_full, n_all = _block_counts(q0, tk, q0 + tq - 1)

    def key_bias(h, s0):
        c = c_ref[0, h, pl.ds(s0, tk), :]
        return c[:, :tq] if tq <= LANES else jnp.concatenate([c] * (tq // LANES), axis=1)

    def step(kb, carry, masked):
        ms, ls, accs = carry
        s0 = pl.multiple_of(kb * tk, tk)
        if masked:
            visible = diff <= (q0 - s0)
        new_ms, new_ls, new_accs = [], [], []
        raw = [_dot(k_ref[0, pl.ds(s0, tk), _col(h // HEADS_PER_COL)], qm[h]) for h in range(N_HEADS)]
        for h in range(N_HEADS):
            logits = raw[h] - key_bias(h, s0)
            if masked:
                logits = jnp.where(visible, logits, MASK_VALUE)
            m_new, alpha, l_new, p = _online_softmax_step(logits, ms[h], ls[h])
            new_ms.append(m_new)
            new_ls.append(l_new)
            new_accs.append(alpha * accs[h] + _dot(vt_ref[0, _head_rows(h), pl.ds(s0, tk)], p.astype(BF)))
        return tuple(new_ms), tuple(new_ls), tuple(new_accs)

    carry = (tuple(jnp.full((1, tq), -jnp.inf, F32) for _ in range(N_HEADS)),
             tuple(jnp.zeros((1, tq), F32) for _ in range(N_HEADS)),
             tuple(jnp.zeros((HEAD_DIM, tq), F32) for _ in range(N_HEADS)))
    carry = lax.fori_loop(0, n_full, lambda i, c: step(i, c, False), carry)
    carry = lax.fori_loop(n_full, n_all, lambda i, c: step(i, c, True), carry)
    _, ls, accs = carry
    for h in range(N_HEADS):
        o_ref[0, _head_rows(h), :] = (accs[h] / ls[h]).astype(o_ref.dtype)


def _fox_attention(qt, k, vt, c_lanes, p_len, tq, tk):
    b, w, t = qt.shape
    lp = k.shape[1]
    assert tq <= LANES or tq % LANES == 0
    return pl.pallas_call(
        functools.partial(_fox_body, p_len=p_len, tq=tq, tk=tk),
        out_shape=jax.ShapeDtypeStruct((b, w, t), BF),
        grid=(b, t // tq),
        in_specs=[_qt_spec(w, tq), _whole_spec(lp, w), _whole_spec(w, lp),
                  pl.BlockSpec((1, N_HEADS, lp, LANES), lambda bi, qi: (bi, 0, 0, 0))],
        out_specs=_qt_spec(w, tq),
        compiler_params=_params("parallel", "arbitrary"),
        name="fox_attention",
    )(qt, k, vt, c_lanes)


def _float_key(bits):
    return jnp.where(bits < 0, jnp.int32(INT_MIN) - bits, bits)


_BUTTERFLY = ((16, 0x0000FFFF), (8, 0x00FF00FF), (4, 0x0F0F0F0F), (2, 0x33333333), (1, 0x55555555))


def _bit_planes(words):
    a = list(words)
    for shift, mask in _BUTTERFLY:
        for k in range(WORD_BITS):
            if k & shift == 0:
                t = (a[k] ^ lax.shift_right_logical(a[k + shift], shift)) & mask
                a[k] = a[k] ^ t
                a[k + shift] = a[k + shift] ^ lax.shift_left(t, shift)
    return a


def _dsa_body(qt_ref, qit_ref, wit_ref, k_ref, vt_ref, ki_ref, o_ref, keys_ref, planes_ref, active_ref, *,
              p_len, n_keys, tq, tk, top_k):
    q0 = p_len + pl.program_id(1) * tq
    qpos = q0 + lax.broadcasted_iota(jnp.int32, (1, tq), 1)
    limit = jnp.minimum(((qpos >> CHUNK_SHIFT) + 1) << CHUNK_SHIFT, n_keys)
    last_limit = jnp.minimum((((q0 + tq - 1) >> CHUNK_SHIFT) + 1) << CHUNK_SHIFT, n_keys)
    n_blk = lax.div(last_limit - 1, tk) + 1
    key_row = lax.broadcasted_iota(jnp.int32, (tk, tq), 0)

    wit = wit_ref[0]
    qim = _head_queries(qit_ref)

    def score_step(kb, _):
        s0 = pl.multiple_of(kb * tk, tk)
        ki = ki_ref[0, pl.ds(s0, tk), :]
        score = jnp.zeros((tk, tq), F32)
        for h in range(N_HEADS):
            score = score + wit[h:h + 1, :] * jnp.maximum(_dot(ki, qim[h]), 0.0)
        score = jnp.where(key_row < limit - s0, score, -jnp.inf)
        key = _float_key(pltpu.bitcast(score, jnp.int32))
        keys_ref[pl.ds(s0, tk), :] = key
        unsigned = key ^ jnp.int32(INT_MIN)
        for g in range(tk // GROUP_KEYS):
            words = [unsigned[g * GROUP_KEYS + SUBLANES * j:g * GROUP_KEYS + SUBLANES * (j + 1), :]
                     for j in range(WORD_BITS)]
            r0 = pl.multiple_of(kb * plane_rows + g * SUBLANES, SUBLANES)
            for i, plane in enumerate(_bit_planes(words)):
                planes_ref[i, pl.ds(r0, SUBLANES), :] = plane
        r0 = pl.multiple_of(kb * plane_rows, plane_rows)
        active_ref[pl.ds(r0, plane_rows), :] = jnp.full((plane_rows, tq), -1, jnp.int32)
        return 0

    plane_rows = tk // WORD_BITS
    lax.fori_loop(0, n_blk, score_step, 0)

    def ones_in(i, narrow):
        def body(kb, acc):
            r0 = pl.multiple_of(kb * plane_rows, plane_rows)
            live = active_ref[pl.ds(r0, plane_rows), :]
            if narrow is not None:
                live = live & (planes_ref[i - 1, pl.ds(r0, plane_rows), :] ^ narrow)
                active_ref[pl.ds(r0, plane_rows), :] = live
            return acc + lax.population_count(live & planes_ref[i, pl.ds(r0, plane_rows), :])
        acc = lax.fori_loop(0, n_blk, body, jnp.zeros((plane_rows, tq), jnp.int32))
        return jnp.sum(acc.astype(F32), axis=0, keepdims=True)

    def choose(i, n_ones, wanted, thr_bits):
        take = n_ones >= wanted
        bit = lax.shift_left(jnp.int32(1), WORD_BITS - 1 - i)
        return (jnp.where(take, wanted, wanted - n_ones), thr_bits | jnp.where(take, bit, 0),
                jnp.where(take, 0, -1))

    wanted, thr_bits, narrow = choose(0, ones_in(0, None), jnp.full((1, tq), top_k, F32),
                                      jnp.zeros((1, tq), jnp.int32))

    def bit_step(i, carry):
        wanted, thr_bits, narrow = carry
        return choose(i, ones_in(i, narrow), wanted, thr_bits)

    wanted, thr_bits, _ = lax.fori_loop(1, WORD_BITS, bit_step, (wanted, thr_bits, narrow))
    thr = thr_bits ^ jnp.int32(INT_MIN)
    n_tie_wanted = wanted

    earlier_keys = (lax.broadcasted_iota(jnp.int32, (tk, tk), 1)
                    < lax.broadcasted_iota(jnp.int32, (tk, tk), 0)).astype(BF)
    qm = _head_queries(qt_ref)

    def attend_step(kb, carry):
        ties_seen, ms, ls, accs = carry
        s0 = pl.multiple_of(kb * tk, tk)
        key = keys_ref[pl.ds(s0, tk), :]
        tie = jnp.where(key == thr, 1.0, 0.0)
        tie_rank = _dot(earlier_keys, tie.astype(BF)) + ties_seen
        take = jnp.where(key > thr, 1.0, jnp.where(tie_rank < n_tie_wanted, tie, 0.0))
        selected = jnp.where(key_row < limit - s0, take, 0.0) > 0.0
        k = k_ref[0, pl.ds(s0, tk), :]
        vt = vt_ref[0, :, pl.ds(s0, tk)]
        new_ms, new_ls, new_accs = [], [], []
        raw = [_dot(k, qm[h]) for h in range(N_HEADS)]
        for h in range(N_HEADS):
            logits = jnp.where(selected, raw[h], MASK_VALUE)
            m_new, alpha, l_new, p = _online_softmax_step(logits, ms[h], ls[h])
            new_ms.append(m_new)
            new_ls.append(l_new)
            new_accs.append(alpha * accs[h] + _dot(vt, p.astype(BF)))
        ties_seen = ties_seen + jnp.sum(tie, axis=0, keepdims=True)
        return ties_seen, tuple(new_ms), tuple(new_ls), tuple(new_accs)

    carry = (jnp.zeros((1, tq), F32),
             tuple(jnp.full((1, tq), -jnp.inf, F32) for _ in range(N_HEADS)),
             tuple(jnp.zeros((1, tq), F32) for _ in range(N_HEADS)),
             tuple(jnp.zeros((HEAD_DIM, tq), F32) for _ in range(N_HEADS)))
    _, _, ls, accs = lax.fori_loop(0, n_blk, attend_step, carry)
    for h in range(N_HEADS):
        o_ref[0, _head_rows(h), :] = (accs[h] / ls[h]).astype(o_ref.dtype)


def _dsa_attention(qt, qit, wit, kk, vt, p_len, n_keys, tq, tk, top_k):
    b, w, t = qt.shape
    lp = kk.shape[1]
    assert tk % GROUP_KEYS == 0
    return pl.pallas_call(
        functools.partial(_dsa_body, p_len=p_len, n_keys=n_keys, tq=tq, tk=tk, top_k=top_k),
        out_shape=jax.ShapeDtypeStruct((b, w, t), BF),
        grid=(b, t // tq),
        in_specs=[_qt_spec(w, tq), _qt_spec(w, tq), _qt_spec(wit.shape[1], tq),
                  pl.BlockSpec((1, lp, LANES), lambda bi, qi: (bi, 0, 0)), _whole_spec(HEAD_DIM, lp),
                  pl.BlockSpec((1, lp, LANES), lambda bi, qi: (bi, 0, 1))],
        out_specs=_qt_spec(w, tq),
        scratch_shapes=[pltpu.VMEM((lp, tq), jnp.int32),
                        pltpu.VMEM((WORD_BITS, lp // WORD_BITS, tq), jnp.int32),
                        pltpu.VMEM((lp // WORD_BITS, tq), jnp.int32)],
        compiler_params=_params("parallel", "arbitrary"),
        name="dsa_attention",
    )(qt, qit, wit, kk, vt, kk)


def _ret_body(q_ref, k_ref, v_ref, g_ref, s0_ref, cos_ref, sin_ref, dec_ref, qd_ref, kd_ref, sd_ref,
              o_ref, so_ref, state_ref):
    c = pl.program_id(1)

    @pl.when(c == 0)
    def _():
        state_ref[...] = s0_ref[0]

    cos, sin = cos_ref[...], sin_ref[...]
    half = HEAD_DIM // 2

    def rotary(x):
        x1, x2 = x[:half], x[half:]
        return jnp.concatenate([x1 * cos - x2 * sin, x2 * cos + x1 * sin], axis=0)

    heads = range(N_HEADS)
    qb = [rotary(q_ref[0, _head_rows(h), :]).astype(BF) for h in heads]
    k = [rotary(k_ref[0, _head_rows(h), :]) for h in heads]
    vb = [v_ref[0, _head_rows(h), :].astype(BF) for h in heads]
    scores_t = [_dot_tn(k[h].astype(BF), qb[h]) * dec_ref[h] for h in heads]
    carried = [_dot(state_ref[h].astype(BF), qb[h]) * qd_ref[h] for h in heads]
    outs = [_dot(vb[h], scores_t[h].astype(BF)) + carried[h] for h in heads]
    for h in heads:
        state_ref[h] = sd_ref[h] * state_ref[h] + _dot_nt(vb[h], (k[h] * kd_ref[h]).astype(BF))
    for h in heads:
        o = outs[h]
        oc = o - jnp.mean(o, axis=0, keepdims=True)
        on = oc * lax.rsqrt(jnp.mean(oc * oc, axis=0, keepdims=True) + LN_EPS)
        g = g_ref[0, _head_rows(h), :]
        o_ref[0, _head_rows(h), :] = (on * (g * jax.nn.sigmoid(g))).astype(o_ref.dtype)

    @pl.when(c == pl.num_programs(1) - 1)
    def _():
        so_ref[0] = state_ref[...]


def _retention(qt, kt, vt, gt, state0_t, pos, c):
    b, w, t = qt.shape
    h = w // HEAD_DIM
    half = HEAD_DIM // 2
    inv_freq = ROPE_BASE ** (-jnp.arange(half, dtype=F32) / half)
    ang = inv_freq[:, None] * pos.astype(F32)[None, :]
    log_gamma = np.log(1.0 - 2.0 ** (-5.0 - np.arange(h, dtype=np.float64)))
    n = np.arange(c, dtype=np.float64)
    rel = n[None, :] - n[:, None]
    decay_t = np.where(rel >= 0, np.exp(np.maximum(rel, 0.0)[None] * log_gamma[:, None, None]), 0.0)
    q_decay = np.exp((n[None, :] + 1.0) * log_gamma[:, None])[:, None, :]
    k_decay = np.exp((c - 1.0 - n)[None, :] * log_gamma[:, None])[:, None, :]
    s_decay = np.exp(c * log_gamma)[:, None, None]
    tables = [jnp.asarray(a, F32) for a in (decay_t, q_decay, k_decay, s_decay)]
    assert h == N_HEADS
    x_spec = pl.BlockSpec((1, w, c), lambda bi, ci: (bi, 0, ci))
    s_spec = pl.BlockSpec((1, h, HEAD_DIM, HEAD_DIM), lambda bi, ci: (bi, 0, 0, 0))
    rope_spec = pl.BlockSpec((half, c), lambda bi, ci: (0, ci))
    return pl.pallas_call(
        _ret_body,
        out_shape=(jax.ShapeDtypeStruct((b, w, t), BF),
                   jax.ShapeDtypeStruct((b, h, HEAD_DIM, HEAD_DIM), F32)),
        grid=(b, t // c),
        in_specs=[x_spec, x_spec, x_spec, x_spec, s_spec, rope_spec, rope_spec] + [_const_spec(a.shape) for a in tables],
        out_specs=(x_spec, s_spec),
        scratch_shapes=[pltpu.VMEM((h, HEAD_DIM, HEAD_DIM), F32)],
        compiler_params=_params("parallel", "arbitrary"),
        name="retention",
    )(qt, kt, vt, gt, state0_t, jnp.cos(ang), jnp.sin(ang), *tables)


def _merge_body(h_ref, y0_ref, y1_ref, y2_ref, y3_ref, wg_ref, wb_ref, wo_ref, g_ref, b_ref, o_ref, *, alpha):
    nb, tm, d = h_ref.shape
    h = h_ref[...].reshape(nb * tm, d)
    hb = h.astype(BF)
    merged = jnp.zeros(h.shape, F32)
    for i, y_ref in enumerate((y0_ref, y1_ref, y2_ref, y3_ref)):
        gate = jax.nn.sigmoid(_dot(hb, wg_ref[:, i * d:(i + 1) * d]))
        branch = jnp.concatenate([_dot_tn(y_ref[j], wb_ref[i]) for j in range(nb)], axis=0)
        merged = merged + gate * branch
    r = alpha * h + _dot(merged.astype(BF), wo_ref[...])
    o_ref[...] = _layer_norm(r, g_ref[...], b_ref[...]).reshape(nb, tm, d)


def _merge(h, ys_t, w_gate, w_branch, w_out, ln_g, ln_b, alpha):
    b, t, d = h.shape
    tm = _row_tile(t, MERGE_ROWS)
    nb = math.gcd(b, max(1, MERGE_ROWS // tm))
    row = pl.BlockSpec((nb, tm, d), lambda bi, i: (bi, i, 0))
    col = pl.BlockSpec((nb, BRANCH_WIDTH, tm), lambda bi, i: (bi, 0, i))
    return pl.pallas_call(
        functools.partial(_merge_body, alpha=alpha),
        out_shape=jax.ShapeDtypeStruct((b, t, d), F32),
        grid=(b // nb, t // tm),
        in_specs=[row] + [col] * 4
                 + [_const_spec(w_gate.shape), _const_spec(w_branch.shape), _const_spec(w_out.shape),
                    _const_spec((1, d)), _const_spec((1, d))],
        out_specs=row,
        compiler_params=_params("parallel", "parallel"),
        name="merge",
    )(h, *ys_t, w_gate, w_branch, w_out, ln_g, ln_b)


def _ffn_body(h_ref, wi_ref, wo_ref, g_ref, b_ref, o_ref, *, alpha, f_chunk):
    h = h_ref[...]
    hb = h.astype(BF)
    f = wo_ref.shape[0]
    acc = jnp.zeros(h.shape, F32)
    for c in range(0, f, f_chunk):
        a = _dot(hb, wi_ref[:, c:c + f_chunk])
        u = _dot(hb, wi_ref[:, f + c:f + c + f_chunk])
        acc = acc + _dot((a * jax.nn.sigmoid(a) * u).astype(BF), wo_ref[c:c + f_chunk, :])
    o_ref[...] = _layer_norm(alpha * h + acc, g_ref[...], b_ref[...])


def _ffn(h, w_in, w_out, ln_g, ln_b, alpha):
    m, d = h.shape
    f = w_out.shape[0]
    tm = _row_tile(m, FFN_ROWS)
    f_chunk = f // 2 if (f // 2) % LANES == 0 else f
    row = pl.BlockSpec((tm, d), lambda i: (i, 0))
    return pl.pallas_call(
        functools.partial(_ffn_body, alpha=alpha, f_chunk=f_chunk),
        out_shape=jax.ShapeDtypeStruct((m, d), F32),
        grid=(m // tm,),
        in_specs=[row, _const_spec(w_in.shape), _const_spec(w_out.shape),
                  _const_spec((1, d)), _const_spec((1, d))],
        out_specs=row,
        compiler_params=_params("parallel"),
        name="ffn",
    )(h, w_in, w_out, ln_g, ln_b)


def _in_layout(d):
    w = BRANCH_WIDTH
    return (('sb_q', w), ('sb_k', w), ('sb_v', w), ('ret_q', w), ('ret_k', w), ('ret_v', w), ('ret_g', w),
            ('fox_q', w), ('fox_k', w), ('fox_v', w), ('fox_f', N_HEADS),
            ('dsa_q', w), ('dsa_k', HEAD_DIM), ('dsa_v', HEAD_DIM),
            ('idx_q', w), ('idx_k', HEAD_DIM), ('idx_w', N_HEADS), ('merge_gate', 4 * d))


_FOLDED_SCALE = dict(sb_q=QK_SCALE, fox_q=QK_SCALE, dsa_q=QK_SCALE, idx_q=QK_SCALE, ret_k=QK_SCALE,
                     idx_w=IDX_HEAD_SCALE)


def _swap(a):
    return jnp.swapaxes(a, -1, -2)


def _key_tiles(t, n_keys):
    tq = min(t, 512)
    tiles = dict(sb=256, fox=512, dsa=512)
    padded = {name: -(-n_keys // tk) * tk for name, tk in tiles.items()}
    return tq, tiles, padded


def _layer(h, b, t, past, ret_state, w, alpha, layer, depth, states):
    m, d = h.shape
    p_len = 0 if past is None else past[0].shape[1]
    n_keys = p_len + t
    tq, tk, lp = _key_tiles(t, n_keys)
    p = _inproj(h.reshape(b, t, d), w['w_row'], w['w_col'], layer, depth, states)
    states = {name: p[name] for name, _, _, _, _, is_state in _ROW_OUTS if is_state}

    old = (None,) * 8 if past is None else past
    sb_k0, sb_v0, fox_k0, fox_v0, fox_lf0, dsa_k0, dsa_v0, dsa_ki0 = old

    def rows_with_past(new_bf, olds, lp_):
        if past is not None:
            flat = [o.reshape(o.shape[0], o.shape[1], -1).astype(BF) for o in olds]
            new_bf = jnp.concatenate([jnp.concatenate(flat, axis=2), new_bf], axis=1)
        return jnp.pad(new_bf, ((0, 0), (0, lp_ - new_bf.shape[1]), (0, 0)))

    def cols_with_past(new_t, old, lp_):
        if old is not None:
            new_t = jnp.concatenate([_swap(old.reshape(old.shape[0], old.shape[1], -1).astype(BF)), new_t], axis=2)
        return jnp.pad(new_t, ((0, 0), (0, 0), (0, lp_ - new_t.shape[2])))

    y_sb = _sb_attention(p['sb_q_t'], rows_with_past(p['sb_k_bf'], [sb_k0], lp['sb']),
                         cols_with_past(p['sb_v_t'], sb_v0, lp['sb']), p_len, tq, tk['sb'])

    pos = p_len + jnp.arange(t, dtype=jnp.int32)
    y_ret, ret_state_t = _retention(p['ret_q_t'], p['ret_k_t'], p['ret_v_t'], p['ret_g_t'], _swap(ret_state),
                                    pos, min(t, RETENTION_CHUNK))

    fox_lf = jax.nn.log_sigmoid(states['fox_f'][layer] + w['b_forget'])
    lf_all = fox_lf if fox_lf0 is None else jnp.concatenate([fox_lf0, fox_lf], axis=1)
    cum = jnp.pad(jnp.cumsum(lf_all, axis=1), ((0, 0), (0, lp['fox'] - n_keys), (0, 0)))
    c_lanes = jnp.broadcast_to(_swap(cum)[..., None], (b, N_HEADS, lp['fox'], LANES))
    y_fox = _fox_attention(p['fox_q_t'], rows_with_past(p['fox_k_bf'], [fox_k0], lp['fox']),
                           cols_with_past(p['fox_v_t'], fox_v0, lp['fox']), c_lanes, p_len, tq, tk['fox'])

    top_k = min(DSA_TOP_K, n_keys // 4)
    kk_old = [dsa_k0, dsa_k0, dsa_ki0, dsa_ki0]
    y_dsa = _dsa_attention(p['dsa_q_t'], p['idx_q_t'], p['idx_w_t'], rows_with_past(p['dsa_kk_bf'], kk_old, lp['dsa']),
                           cols_with_past(p['dsa_v_t'], dsa_v0, lp['dsa']), p_len, n_keys, tq, tk['dsa'], top_k)

    h = _merge(h.reshape(b, t, d), (y_sb, y_ret, y_fox, y_dsa), w['w_gate'], w['w_branch'], w['w_out'],
               w['ln1_g'], w['ln1_b'], alpha)
    h = _ffn(h.reshape(m, d), w['w_ffn_in'], w['w_ffn_out'], w['ln2_g'], w['ln2_b'], alpha)
    return h, states, _swap(ret_state_t), fox_lf


def _group_outputs(states, ret_states, fox_lfs):
    def heads(a):
        return a.reshape(a.shape[:-1] + (N_HEADS, HEAD_DIM))

    return (heads(states['sb_k']), heads(states['sb_v']), jnp.stack(ret_states), heads(states['fox_k']),
            heads(states['fox_v']), jnp.stack(fox_lfs), states['dsa_k'], states['dsa_v'], states['idx_k'])


def kernel(x_prompt, x_sample, cache_sb_k, cache_sb_v, state_ret, cache_fox_k, cache_fox_v, cache_fox_logf,
           cache_dsa_k, cache_dsa_v, cache_dsa_kidx, w_in, b_forget, w_branch, w_out, ln1_g, ln1_b,
           w_ffn_in, w_ffn_out, ln2_g, ln2_b):
    depth = w_in.shape[0]
    alpha = float((2 * depth) ** 0.25)
    bp, tp, d = x_prompt.shape
    bs, ts, _ = x_sample.shape
    hp = x_prompt.reshape(bp * tp, d)
    hs = x_sample.reshape(bs * ts, d)
    ret_zero = jnp.zeros((bp, N_HEADS, HEAD_DIM, HEAD_DIM), F32)
    st_p, st_s, ret_p, ret_s, lf_p, lf_s = None, None, [], [], [], []
    for l in range(depth):
        w_row, w_col, w_gate = _inproj_weights(w_in[l])
        w = dict(w_row=w_row, w_col=w_col, w_gate=w_gate, b_forget=b_forget[l], w_branch=w_branch[l].astype(BF),
                 w_out=w_out[l].astype(BF), ln1_g=ln1_g[l][None], ln1_b=ln1_b[l][None],
                 w_ffn_in=w_ffn_in[l].astype(BF), w_ffn_out=w_ffn_out[l].astype(BF),
                 ln2_g=ln2_g[l][None], ln2_b=ln2_b[l][None])
        hp, st_p, ret, lf = _layer(hp, bp, tp, None, ret_zero, w, alpha, l, depth, st_p)
        ret_p.append(ret)
        lf_p.append(lf)
        past = (cache_sb_k[l], cache_sb_v[l], cache_fox_k[l], cache_fox_v[l], cache_fox_logf[l],
                cache_dsa_k[l], cache_dsa_v[l], cache_dsa_kidx[l])
        hs, st_s, ret, lf = _layer(hs, bs, ts, past, state_ret[l], w, alpha, l, depth, st_s)
        ret_s.append(ret)
        lf_s.append(lf)
    return ((hp.reshape(bp, tp, d), hs.reshape(bs, ts, d))
            + _group_outputs(st_p, ret_p, lf_p) + _group_outputs(st_s, ret_s, lf_s))
```

```python
import functools
import math

import numpy as np
import jax
import jax.numpy as jnp
from jax import lax
from jax.experimental import pallas as pl
from jax.experimental.pallas import tpu as pltpu

HEAD_DIM = 64
N_HEADS = 4
BRANCH_WIDTH = N_HEADS * HEAD_DIM
CHUNK_SHIFT = 6
DSA_TOP_K = 256
ROPE_BASE = 10000.0
LN_EPS = 1e-5
QK_SCALE = HEAD_DIM ** -0.5
IDX_HEAD_SCALE = N_HEADS ** -0.5
MASK_VALUE = -1e30
F32_EXP_UNDERFLOW = -105.0
INT_MIN = -2 ** 31

V7X_VMEM_LIMIT_BYTES = 56 * 1024 * 1024
LANES = 128
HEADS_PER_COL = LANES // HEAD_DIM
SUBLANES = 8
WORD_BITS = 32
GROUP_KEYS = SUBLANES * WORD_BITS
RETENTION_CHUNK = 512
MERGE_ROWS = 512
FFN_ROWS = 512

BF = jnp.bfloat16
F32 = jnp.float32


def _dot(a, b):
    return jnp.dot(a, b, preferred_element_type=F32)


def _dot_nt(a, b):
    return lax.dot_general(a, b, (((1,), (1,)), ((), ())), preferred_element_type=F32)


def _dot_tn(a, b):
    return lax.dot_general(a, b, (((0,), (0,)), ((), ())), preferred_element_type=F32)


def _params(*sem):
    return pltpu.CompilerParams(dimension_semantics=sem, vmem_limit_bytes=V7X_VMEM_LIMIT_BYTES)


def _const_spec(shape):
    nd = len(shape)
    return pl.BlockSpec(shape, lambda *_: (0,) * nd)


def _layer_norm(x, g, b):
    xc = x - jnp.mean(x, axis=-1, keepdims=True)
    var = jnp.mean(xc * xc, axis=-1, keepdims=True)
    return xc * lax.rsqrt(var + LN_EPS) * g + b


def _row_tile(m, want):
    t = min(m, want)
    assert m % t == 0
    return t


def _col(c):
    return slice(c * LANES, (c + 1) * LANES)


_NARROW_SRC = ('dsa_k', 'dsa_v', 'idx_k', 'fox_f')
_ROW_OUTS = (
    ('sb_k', ('sb_k',), F32, 0, BRANCH_WIDTH, True), ('sb_k_bf', ('sb_k',), BF, 0, BRANCH_WIDTH, False),
    ('sb_v', ('sb_v',), F32, 0, BRANCH_WIDTH, True),
    ('fox_k', ('fox_k',), F32, 0, BRANCH_WIDTH, True), ('fox_k_bf', ('fox_k',), BF, 0, BRANCH_WIDTH, False),
    ('fox_v', ('fox_v',), F32, 0, BRANCH_WIDTH, True),
    ('dsa_k', _NARROW_SRC, F32, 0, HEAD_DIM, True), ('dsa_v', _NARROW_SRC, F32, HEAD_DIM, HEAD_DIM, True),
    ('idx_k', _NARROW_SRC, F32, 2 * HEAD_DIM, HEAD_DIM, True), ('fox_f', _NARROW_SRC, F32, 3 * HEAD_DIM, N_HEADS, True),
    ('dsa_kk_bf', ('dsa_k', 'dsa_k', 'idx_k', 'idx_k'), BF, 0, 4 * HEAD_DIM, False),
)
_COL_OUTS = (
    ('sb_q_t', ('sb_q',), BF), ('fox_q_t', ('fox_q',), BF), ('dsa_q_t', ('dsa_q',), BF),
    ('idx_q_t', ('idx_q',), BF), ('sb_v_t', ('sb_v',), BF), ('fox_v_t', ('fox_v',), BF),
    ('ret_q_t', ('ret_q',), F32), ('ret_k_t', ('ret_k',), F32), ('ret_v_t', ('ret_v',), F32),
    ('ret_g_t', ('ret_g',), F32), ('dsa_v_t', ('dsa_v',), BF), ('idx_w_t', ('idx_w',), F32),
)
BF16_ROWS_PER_VREG = 16


def _inproj_plan(d):
    widths = dict(_in_layout(d))

    def spans(outs, multiple):
        span_of, off = {}, 0
        for out in outs:
            srcs = out[1]
            if srcs not in span_of:
                w = -(-sum(widths[s] for s in srcs) // multiple) * multiple
                span_of[srcs] = (off, w)
                off += w
        return span_of, off

    row_spans, _ = spans(_ROW_OUTS, LANES)
    col_spans, n_col = spans(_COL_OUTS, BF16_ROWS_PER_VREG)
    return row_spans, col_spans, -(-n_col // LANES) * LANES


def _inproj_weights(w_in):
    d = w_in.shape[0]
    w_bf = w_in.astype(BF)
    pieces, off = {}, 0
    for name, width in _in_layout(d):
        piece = w_bf[:, off:off + width]
        scale = _FOLDED_SCALE.get(name)
        pieces[name] = piece if scale is None else piece * jnp.asarray(scale, BF)
        off += width
    assert off == w_in.shape[1]
    row_spans, col_spans, n_col = _inproj_plan(d)

    def block(srcs, width):
        w = jnp.concatenate([pieces[s] for s in srcs], axis=1)
        return jnp.pad(w, ((0, 0), (0, width - w.shape[1])))

    w_row = jnp.concatenate([block(srcs, w) for srcs, (_, w) in row_spans.items()], axis=1)
    w_col = jnp.concatenate([block(srcs, w) for srcs, (_, w) in col_spans.items()], axis=1)
    w_col = jnp.pad(w_col, ((0, 0), (0, n_col - w_col.shape[1])))
    return w_row.astype(BF), w_col.T.astype(BF), pieces['merge_gate'].astype(BF)


def _inproj_body(x_ref, wr_ref, wc_ref, *refs, row_spans, col_spans, n_alias):
    o_refs = refs[n_alias:]
    xb = x_ref[0].astype(BF)
    done = {}
    for o_ref, (_, srcs, _, lane, width, _) in zip(o_refs, _ROW_OUTS):
        if srcs not in done:
            off, w = row_spans[srcs]
            done[srcs] = _dot(xb, wr_ref[:, off:off + w])
        o_ref[...] = done[srcs][:, lane:lane + width].astype(o_ref.dtype).reshape(o_ref.shape)
    for o_ref, (_, srcs, _) in zip(o_refs[len(_ROW_OUTS):], _COL_OUTS):
        off, w = col_spans[srcs]
        o_ref[0] = _dot_nt(wc_ref[off:off + w, :], xb).astype(o_ref.dtype)


def _inproj(x, w_row, w_col, layer, depth, states):
    b, t, d = x.shape
    tm = _row_tile(t, 512)
    row_spans, col_spans, _ = _inproj_plan(d)
    out_shape, out_specs, state_names = [], [], []
    for name, _, dt, _, w, is_state in _ROW_OUTS:
        if is_state:
            state_names.append(name)
            out_shape.append(jax.ShapeDtypeStruct((depth, b, t, w), dt))
            out_specs.append(pl.BlockSpec((1, 1, tm, w), lambda bi, i: (layer, bi, i, 0)))
        else:
            out_shape.append(jax.ShapeDtypeStruct((b, t, w), dt))
            out_specs.append(pl.BlockSpec((1, tm, w), lambda bi, i: (bi, i, 0)))
    for _, srcs, dt in _COL_OUTS:
        w = col_spans[srcs][1]
        out_shape.append(jax.ShapeDtypeStruct((b, w, t), dt))
        out_specs.append(pl.BlockSpec((1, w, tm), lambda bi, i: (bi, 0, i)))
    prev = [] if states is None else [states[n] for n in state_names]
    names = [o[0] for o in _ROW_OUTS] + [o[0] for o in _COL_OUTS]
    aliases = {3 + j: names.index(n) for j, n in enumerate(state_names)} if prev else {}
    outs = pl.pallas_call(
        functools.partial(_inproj_body, row_spans=row_spans, col_spans=col_spans, n_alias=len(prev)),
        out_shape=out_shape,
        grid=(b, t // tm),
        in_specs=[pl.BlockSpec((1, tm, d), lambda bi, i: (bi, i, 0)),
                  _const_spec(w_row.shape), _const_spec(w_col.shape)]
                 + [pl.BlockSpec(memory_space=pl.ANY)] * len(prev),
        out_specs=out_specs,
        input_output_aliases=aliases,
        compiler_params=_params("parallel", "parallel"),
        name="inproj",
    )(x, w_row, w_col, *prev)
    return dict(zip(names, outs))


def _block_counts(q0, tk, last_key):
    return lax.div(q0, tk), lax.div(last_key, tk) + 1


def _head_queries(qt_ref):
    low = lax.broadcasted_iota(jnp.int32, (LANES, 1), 0) < HEAD_DIM
    out = []
    for h in range(N_HEADS):
        qc = qt_ref[0, _col(h // HEADS_PER_COL), :]
        keep = low if h % HEADS_PER_COL == 0 else jnp.logical_not(low)
        out.append(jnp.where(keep, qc, jnp.zeros_like(qc)))
    return out


def _head_rows(h):
    return slice(h * HEAD_DIM, (h + 1) * HEAD_DIM)


def _key_minus_query(tk, tq):
    return (lax.broadcasted_iota(jnp.int32, (tk, tq), 0) - lax.broadcasted_iota(jnp.int32, (tk, tq), 1))


def _qt_spec(w, tq):
    return pl.BlockSpec((1, w, tq), lambda bi, qi: (bi, 0, qi))


def _whole_spec(rows, cols):
    return pl.BlockSpec((1, rows, cols), lambda bi, qi: (bi, 0, 0))


def _sb_body(qt_ref, k_ref, vt_ref, o_ref, *, p_len, tq, tk):
    q0 = p_len + pl.program_id(1) * tq
    qm = _head_queries(qt_ref)
    diff = _key_minus_query(tk, tq)
    later = (lax.broadcasted_iota(jnp.int32, (tk, 2 * tk), 1) & (tk - 1)) > lax.broadcasted_iota(
        jnp.int32, (tk, 2 * tk), 0)
    minus_later = jnp.where(later, -1.0, 0.0).astype(BF)
    n_full, n_all = _block_counts(q0, tk, jnp.maximum(q0 + tq - 2, 0))

    def step(kb, carry, masked):
        laters, accs = carry
        s0 = pl.multiple_of(kb * tk, tk)
        if masked:
            earlier = diff < (q0 - s0)
        zs = [_dot(k_ref[0, pl.ds(s0, tk), _col(h // HEADS_PER_COL)], qm[h]) for h in range(N_HEADS)]
        new_laters, log_bs, afters = [], [], []
        for h in range(N_HEADS):
            z = zs[h]
            minus_abs = pltpu.bitcast(pltpu.bitcast(z, jnp.int32) | jnp.int32(INT_MIN), F32)
            softplus = jnp.maximum(z, 0.0) + jnp.log(1.0 + jnp.exp(minus_abs))
            log_bs.append(z - softplus)
            if masked:
                softplus = jnp.where(earlier, softplus, 0.0)
            hi = softplus.astype(BF)
            lo = (softplus - hi.astype(F32)).astype(BF)
            after = _dot(minus_later, jnp.concatenate([hi, lo], axis=0)) + laters[h]
            afters.append(after)
            new_laters.append(after[0:1, :] - softplus[0:1, :])
        new_accs = []
        for h in range(N_HEADS):
            w = jnp.exp(log_bs[h] + afters[h])
            if masked:
                w = jnp.where(earlier, w, 0.0)
            new_accs.append(accs[h] + _dot(vt_ref[0, _head_rows(h), pl.ds(s0, tk)], w.astype(BF)))
        return tuple(new_laters), tuple(new_accs)

    carry = (tuple(jnp.zeros((1, tq), F32) for _ in range(N_HEADS)),
             tuple(jnp.zeros((HEAD_DIM, tq), F32) for _ in range(N_HEADS)))
    carry = lax.fori_loop(0, n_all - n_full, lambda i, c: step(n_all - 1 - i, c, True), carry)

    def reachable(laters):
        return jnp.max(functools.reduce(jnp.maximum, laters)) > F32_EXP_UNDERFLOW

    def earlier_block(state):
        i, _, c = state
        c = step(n_full - 1 - i, c, False)
        return i + 1, reachable(c[0]), c

    _, _, carry = lax.while_loop(lambda state: (state[0] < n_full) & state[1], earlier_block,
                                 (jnp.int32(0), reachable(carry[0]), carry))
    for h in range(N_HEADS):
        o_ref[0, _head_rows(h), :] = carry[1][h].astype(o_ref.dtype)


def _sb_attention(qt, k, vt, p_len, tq, tk):
    b, w, t = qt.shape
    lp = k.shape[1]
    assert tk & (tk - 1) == 0
    return pl.pallas_call(
        functools.partial(_sb_body, p_len=p_len, tq=tq, tk=tk),
        out_shape=jax.ShapeDtypeStruct((b, w, t), BF),
        grid=(b, t // tq),
        in_specs=[_qt_spec(w, tq), _whole_spec(lp, w), _whole_spec(w, lp)],
        out_specs=_qt_spec(w, tq),
        compiler_params=_params("parallel", "arbitrary"),
        name="sb_attention",
    )(qt, k, vt)


def _online_softmax_step(logits, m, l):
    m_new = jnp.maximum(m, jnp.max(logits, axis=0, keepdims=True))
    alpha = jnp.exp(m - m_new)
    p = jnp.exp(logits - m_new)
    return m_new, alpha, alpha * l + jnp.sum(p, axis=0, keepdims=True), p


def _fox_body(qt_ref, k_ref, vt_ref, c_ref, o_ref, *, p_len, tq, tk):
    q0 = p_len + pl.program_id(1) * tq
    qm = _head_queries(qt_ref)
    diff = _key_minus_query(tk, tq)
    n_full, n_all = _block_counts(q0, tk, q0 + tq - 1)

    def key_bias(h, s0):
        c = c_ref[0, h, pl.ds(s0, tk), :]
        return c[:, :tq] if tq <= LANES else jnp.concatenate([c] * (tq // LANES), axis=1)

    def step(kb, carry, masked):
        ms, ls, accs = carry
        s0 = pl.multiple_of(kb * tk, tk)
        if masked:
            visible = diff <= (q0 - s0)
        new_ms, new_ls, new_accs = [], [], []
        raw = [_dot(k_ref[0, pl.ds(s0, tk), _col(h // HEADS_PER_COL)], qm[h]) for h in range(N_HEADS)]
        for h in range(N_HEADS):
            logits = raw[h] - key_bias(h, s0)
            if masked:
                logits = jnp.where(visible, logits, MASK_VALUE)
            m_new, alpha, l_new, p = _online_softmax_step(logits, ms[h], ls[h])
            new_ms.append(m_new)
            new_ls.append(l_new)
            new_accs.append(alpha * accs[h] + _dot(vt_ref[0, _head_rows(h), pl.ds(s0, tk)], p.astype(BF)))
        return tuple(new_ms), tuple(new_ls), tuple(new_accs)

    carry = (tuple(jnp.full((1, tq), -jnp.inf, F32) for _ in range(N_HEADS)),
             tuple(jnp.zeros((1, tq), F32) for _ in range(N_HEADS)),
             tuple(jnp.zeros((HEAD_DIM, tq), F32) for _ in range(N_HEADS)))
    carry = lax.fori_loop(0, n_full, lambda i, c: step(i, c, False), carry)
    carry = lax.fori_loop(n_full, n_all, lambda i, c: step(i, c, True), carry)
    _, ls, accs = carry
    for h in range(N_HEADS):
        o_ref[0, _head_rows(h), :] = (accs[h] / ls[h]).astype(o_ref.dtype)


def _fox_attention(qt, k, vt, c_lanes, p_len, tq, tk):
    b, w, t = qt.shape
    lp = k.shape[1]
    assert tq <= LANES or tq % LANES == 0
    return pl.pallas_call(
        functools.partial(_fox_body, p_len=p_len, tq=tq, tk=tk),
        out_shape=jax.ShapeDtypeStruct((b, w, t), BF),
        grid=(b, t // tq),
        in_specs=[_qt_spec(w, tq), _whole_spec(lp, w), _whole_spec(w, lp),
                  pl.BlockSpec((1, N_HEADS, lp, LANES), lambda bi, qi: (bi, 0, 0, 0))],
        out_specs=_qt_spec(w, tq),
        compiler_params=_params("parallel", "arbitrary"),
        name="fox_attention",
    )(qt, k, vt, c_lanes)


def _float_key(bits):
    return jnp.where(bits < 0, jnp.int32(INT_MIN) - bits, bits)


_BUTTERFLY = ((16, 0x0000FFFF), (8, 0x00FF00FF), (4, 0x0F0F0F0F), (2, 0x33333333), (1, 0x55555555))


def _bit_planes(words):
    a = list(words)
    for shift, mask in _BUTTERFLY:
        for k in range(WORD_BITS):
            if k & shift == 0:
                t = (a[k] ^ lax.shift_right_logical(a[k + shift], shift)) & mask
                a[k] = a[k] ^ t
                a[k + shift] = a[k + shift] ^ lax.shift_left(t, shift)
    return a


def _dsa_body(qt_ref, qit_ref, wit_ref, k_ref, vt_ref, ki_ref, o_ref, keys_ref, planes_ref, active_ref, *,
              p_len, n_keys, tq, tk, top_k):
    q0 = p_len + pl.program_id(1) * tq
    qpos = q0 + lax.broadcasted_iota(jnp.int32, (1, tq), 1)
    limit = jnp.minimum(((qpos >> CHUNK_SHIFT) + 1) << CHUNK_SHIFT, n_keys)
    last_limit = jnp.minimum((((q0 + tq - 1) >> CHUNK_SHIFT) + 1) << CHUNK_SHIFT, n_keys)
    n_blk = lax.div(last_limit - 1, tk) + 1
    key_row = lax.broadcasted_iota(jnp.int32, (tk, tq), 0)

    wit = wit_ref[0]
    qim = _head_queries(qit_ref)

    def score_step(kb, _):
        s0 = pl.multiple_of(kb * tk, tk)
        ki = ki_ref[0, pl.ds(s0, tk), :]
        score = jnp.zeros((tk, tq), F32)
        for h in range(N_HEADS):
            score = score + wit[h:h + 1, :] * jnp.maximum(_dot(ki, qim[h]), 0.0)
        score = jnp.where(key_row < limit - s0, score, -jnp.inf)
        key = _float_key(pltpu.bitcast(score, jnp.int32))
        keys_ref[pl.ds(s0, tk), :] = key
        unsigned = key ^ jnp.int32(INT_MIN)
        for g in range(tk // GROUP_KEYS):
            words = [unsigned[g * GROUP_KEYS + SUBLANES * j:g * GROUP_KEYS + SUBLANES * (j + 1), :]
                     for j in range(WORD_BITS)]
            r0 = pl.multiple_of(kb * plane_rows + g * SUBLANES, SUBLANES)
            for i, plane in enumerate(_bit_planes(words)):
                planes_ref[i, pl.ds(r0, SUBLANES), :] = plane
        r0 = pl.multiple_of(kb * plane_rows, plane_rows)
        active_ref[pl.ds(r0, plane_rows), :] = jnp.full((plane_rows, tq), -1, jnp.int32)
        return 0

    plane_rows = tk // WORD_BITS
    lax.fori_loop(0, n_blk, score_step, 0)

    def ones_in(i, narrow):
        def body(kb, acc):
            r0 = pl.multiple_of(kb * plane_rows, plane_rows)
            live = active_ref[pl.ds(r0, plane_rows), :]
            if narrow is not None:
                live = live & (planes_ref[i - 1, pl.ds(r0, plane_rows), :] ^ narrow)
                active_ref[pl.ds(r0, plane_rows), :] = live
            return acc + lax.population_count(live & planes_ref[i, pl.ds(r0, plane_rows), :])
        acc = lax.fori_loop(0, n_blk, body, jnp.zeros((plane_rows, tq), jnp.int32))
        return jnp.sum(acc.astype(F32), axis=0, keepdims=True)

    def choose(i, n_ones, wanted, thr_bits):
        take = n_ones >= wanted
        bit = lax.shift_left(jnp.int32(1), WORD_BITS - 1 - i)
        return (jnp.where(take, wanted, wanted - n_ones), thr_bits | jnp.where(take, bit, 0),
                jnp.where(take, 0, -1))

    wanted, thr_bits, narrow = choose(0, ones_in(0, None), jnp.full((1, tq), top_k, F32),
                                      jnp.zeros((1, tq), jnp.int32))

    def bit_step(i, carry):
        wanted, thr_bits, narrow = carry
        return choose(i, ones_in(i, narrow), wanted, thr_bits)

    wanted, thr_bits, _ = lax.fori_loop(1, WORD_BITS, bit_step, (wanted, thr_bits, narrow))
    thr = thr_bits ^ jnp.int32(INT_MIN)
    n_tie_wanted = wanted

    earlier_keys = (lax.broadcasted_iota(jnp.int32, (tk, tk), 1)
                    < lax.broadcasted_iota(jnp.int32, (tk, tk), 0)).astype(BF)
    qm = _head_queries(qt_ref)

    def attend_step(kb, carry):
        ties_seen, ms, ls, accs = carry
        s0 = pl.multiple_of(kb * tk, tk)
        key = keys_ref[pl.ds(s0, tk), :]
        tie = jnp.where(key == thr, 1.0, 0.0)
        tie_rank = _dot(earlier_keys, tie.astype(BF)) + ties_seen
        take = jnp.where(key > thr, 1.0, jnp.where(tie_rank < n_tie_wanted, tie, 0.0))
        selected = jnp.where(key_row < limit - s0, take, 0.0) > 0.0
        k = k_ref[0, pl.ds(s0, tk), :]
        vt = vt_ref[0, :, pl.ds(s0, tk)]
        new_ms, new_ls, new_accs = [], [], []
        raw = [_dot(k, qm[h]) for h in range(N_HEADS)]
        for h in range(N_HEADS):
            logits = jnp.where(selected, raw[h], MASK_VALUE)
            m_new, alpha, l_new, p = _online_softmax_step(logits, ms[h], ls[h])
            new_ms.append(m_new)
            new_ls.append(l_new)
            new_accs.append(alpha * accs[h] + _dot(vt, p.astype(BF)))
        ties_seen = ties_seen + jnp.sum(tie, axis=0, keepdims=True)
        return ties_seen, tuple(new_ms), tuple(new_ls), tuple(new_accs)

    carry = (jnp.zeros((1, tq), F32),
             tuple(jnp.full((1, tq), -jnp.inf, F32) for _ in range(N_HEADS)),
             tuple(jnp.zeros((1, tq), F32) for _ in range(N_HEADS)),
             tuple(jnp.zeros((HEAD_DIM, tq), F32) for _ in range(N_HEADS)))
    _, _, ls, accs = lax.fori_loop(0, n_blk, attend_step, carry)
    for h in range(N_HEADS):
        o_ref[0, _head_rows(h), :] = (accs[h] / ls[h]).astype(o_ref.dtype)


def _dsa_attention(qt, qit, wit, kk, vt, p_len, n_keys, tq, tk, top_k):
    b, w, t = qt.shape
    lp = kk.shape[1]
    assert tk % GROUP_KEYS == 0
    return pl.pallas_call(
        functools.partial(_dsa_body, p_len=p_len, n_keys=n_keys, tq=tq, tk=tk, top_k=top_k),
        out_shape=jax.ShapeDtypeStruct((b, w, t), BF),
        grid=(b, t // tq),
        in_specs=[_qt_spec(w, tq), _qt_spec(w, tq), _qt_spec(wit.shape[1], tq),
                  pl.BlockSpec((1, lp, LANES), lambda bi, qi: (bi, 0, 0)), _whole_spec(HEAD_DIM, lp),
                  pl.BlockSpec((1, lp, LANES), lambda bi, qi: (bi, 0, 1))],
        out_specs=_qt_spec(w, tq),
        scratch_shapes=[pltpu.VMEM((lp, tq), jnp.int32),
                        pltpu.VMEM((WORD_BITS, lp // WORD_BITS, tq), jnp.int32),
                        pltpu.VMEM((lp // WORD_BITS, tq), jnp.int32)],
        compiler_params=_params("parallel", "arbitrary"),
        name="dsa_attention",
    )(qt, qit, wit, kk, vt, kk)


def _ret_body(q_ref, k_ref, v_ref, g_ref, s0_ref, cos_ref, sin_ref, dec_ref, qd_ref, kd_ref, sd_ref,
              o_ref, so_ref, state_ref):
    c = pl.program_id(1)

    @pl.when(c == 0)
    def _():
        state_ref[...] = s0_ref[0]

    cos, sin = cos_ref[...], sin_ref[...]
    half = HEAD_DIM // 2

    def rotary(x):
        x1, x2 = x[:half], x[half:]
        return jnp.concatenate([x1 * cos - x2 * sin, x2 * cos + x1 * sin], axis=0)

    heads = range(N_HEADS)
    qb = [rotary(q_ref[0, _head_rows(h), :]).astype(BF) for h in heads]
    k = [rotary(k_ref[0, _head_rows(h), :]) for h in heads]
    vb = [v_ref[0, _head_rows(h), :].astype(BF) for h in heads]
    scores_t = [_dot_tn(k[h].astype(BF), qb[h]) * dec_ref[h] for h in heads]
    carried = [_dot(state_ref[h].astype(BF), qb[h]) * qd_ref[h] for h in heads]
    outs = [_dot(vb[h], scores_t[h].astype(BF)) + carried[h] for h in heads]
    for h in heads:
        state_ref[h] = sd_ref[h] * state_ref[h] + _dot_nt(vb[h], (k[h] * kd_ref[h]).astype(BF))
    for h in heads:
        o = outs[h]
        oc = o - jnp.mean(o, axis=0, keepdims=True)
        on = oc * lax.rsqrt(jnp.mean(oc * oc, axis=0, keepdims=True) + LN_EPS)
        g = g_ref[0, _head_rows(h), :]
        o_ref[0, _head_rows(h), :] = (on * (g * jax.nn.sigmoid(g))).astype(o_ref.dtype)

    @pl.when(c == pl.num_programs(1) - 1)
    def _():
        so_ref[0] = state_ref[...]


def _retention(qt, kt, vt, gt, state0_t, pos, c):
    b, w, t = qt.shape
    h = w // HEAD_DIM
    half = HEAD_DIM // 2
    inv_freq = ROPE_BASE ** (-jnp.arange(half, dtype=F32) / half)
    ang = inv_freq[:, None] * pos.astype(F32)[None, :]
    log_gamma = np.log(1.0 - 2.0 ** (-5.0 - np.arange(h, dtype=np.float64)))
    n = np.arange(c, dtype=np.float64)
    rel = n[None, :] - n[:, None]
    decay_t = np.where(rel >= 0, np.exp(np.maximum(rel, 0.0)[None] * log_gamma[:, None, None]), 0.0)
    q_decay = np.exp((n[None, :] + 1.0) * log_gamma[:, None])[:, None, :]
    k_decay = np.exp((c - 1.0 - n)[None, :] * log_gamma[:, None])[:, None, :]
    s_decay = np.exp(c * log_gamma)[:, None, None]
    tables = [jnp.asarray(a, F32) for a in (decay_t, q_decay, k_decay, s_decay)]
    assert h == N_HEADS
    x_spec = pl.BlockSpec((1, w, c), lambda bi, ci: (bi, 0, ci))
    s_spec = pl.BlockSpec((1, h, HEAD_DIM, HEAD_DIM), lambda bi, ci: (bi, 0, 0, 0))
    rope_spec = pl.BlockSpec((half, c), lambda bi, ci: (0, ci))
    return pl.pallas_call(
        _ret_body,
        out_shape=(jax.ShapeDtypeStruct((b, w, t), BF),
                   jax.ShapeDtypeStruct((b, h, HEAD_DIM, HEAD_DIM), F32)),
        grid=(b, t // c),
        in_specs=[x_spec, x_spec, x_spec, x_spec, s_spec, rope_spec, rope_spec] + [_const_spec(a.shape) for a in tables],
        out_specs=(x_spec, s_spec),
        scratch_shapes=[pltpu.VMEM((h, HEAD_DIM, HEAD_DIM), F32)],
        compiler_params=_params("parallel", "arbitrary"),
        name="retention",
    )(qt, kt, vt, gt, state0_t, jnp.cos(ang), jnp.sin(ang), *tables)


def _merge_body(h_ref, y0_ref, y1_ref, y2_ref, y3_ref, wg_ref, wb_ref, wo_ref, g_ref, b_ref, o_ref, *, alpha):
    nb, tm, d = h_ref.shape
    h = h_ref[...].reshape(nb * tm, d)
    hb = h.astype(BF)
    merged = jnp.zeros(h.shape, F32)
    for i, y_ref in enumerate((y0_ref, y1_ref, y2_ref, y3_ref)):
        gate = jax.nn.sigmoid(_dot(hb, wg_ref[:, i * d:(i + 1) * d]))
        branch = jnp.concatenate([_dot_tn(y_ref[j], wb_ref[i]) for j in range(nb)], axis=0)
        merged = merged + gate * branch
    r = alpha * h + _dot(merged.astype(BF), wo_ref[...])
    o_ref[...] = _layer_norm(r, g_ref[...], b_ref[...]).reshape(nb, tm, d)


def _merge(h, ys_t, w_gate, w_branch, w_out, ln_g, ln_b, alpha):
    b, t, d = h.shape
    tm = _row_tile(t, MERGE_ROWS)
    nb = math.gcd(b, max(1, MERGE_ROWS // tm))
    row = pl.BlockSpec((nb, tm, d), lambda bi, i: (bi, i, 0))
    col = pl.BlockSpec((nb, BRANCH_WIDTH, tm), lambda bi, i: (bi, 0, i))
    return pl.pallas_call(
        functools.partial(_merge_body, alpha=alpha),
        out_shape=jax.ShapeDtypeStruct((b, t, d), F32),
        grid=(b // nb, t // tm),
        in_specs=[row] + [col] * 4
                 + [_const_spec(w_gate.shape), _const_spec(w_branch.shape), _const_spec(w_out.shape),
                    _const_spec((1, d)), _const_spec((1, d))],
        out_specs=row,
        compiler_params=_params("parallel", "parallel"),
        name="merge",
    )(h, *ys_t, w_gate, w_branch, w_out, ln_g, ln_b)


def _ffn_body(h_ref, wi_ref, wo_ref, g_ref, b_ref, o_ref, *, alpha, f_chunk):
    h = h_ref[...]
    hb = h.astype(BF)
    f = wo_ref.shape[0]
    acc = jnp.zeros(h.shape, F32)
    for c in range(0, f, f_chunk):
        a = _dot(hb, wi_ref[:, c:c + f_chunk])
        u = _dot(hb, wi_ref[:, f + c:f + c + f_chunk])
        acc = acc + _dot((a * jax.nn.sigmoid(a) * u).astype(BF), wo_ref[c:c + f_chunk, :])
    o_ref[...] = _layer_norm(alpha * h + acc, g_ref[...], b_ref[...])


def _ffn(h, w_in, w_out, ln_g, ln_b, alpha):
    m, d = h.shape
    f = w_out.shape[0]
    tm = _row_tile(m, FFN_ROWS)
    f_chunk = f // 2 if (f // 2) % LANES == 0 else f
    row = pl.BlockSpec((tm, d), lambda i: (i, 0))
    return pl.pallas_call(
        functools.partial(_ffn_body, alpha=alpha, f_chunk=f_chunk),
        out_shape=jax.ShapeDtypeStruct((m, d), F32),
        grid=(m // tm,),
        in_specs=[row, _const_spec(w_in.shape), _const_spec(w_out.shape),
                  _const_spec((1, d)), _const_spec((1, d))],
        out_specs=row,
        compiler_params=_params("parallel"),
        name="ffn",
    )(h, w_in, w_out, ln_g, ln_b)


def _in_layout(d):
    w = BRANCH_WIDTH
    return (('sb_q', w), ('sb_k', w), ('sb_v', w), ('ret_q', w), ('ret_k', w), ('ret_v', w), ('ret_g', w),
            ('fox_q', w), ('fox_k', w), ('fox_v', w), ('fox_f', N_HEADS),
            ('dsa_q', w), ('dsa_k', HEAD_DIM), ('dsa_v', HEAD_DIM),
            ('idx_q', w), ('idx_k', HEAD_DIM), ('idx_w', N_HEADS), ('merge_gate', 4 * d))


_FOLDED_SCALE = dict(sb_q=QK_SCALE, fox_q=QK_SCALE, dsa_q=QK_SCALE, idx_q=QK_SCALE, ret_k=QK_SCALE,
                     idx_w=IDX_HEAD_SCALE)


def _swap(a):
    return jnp.swapaxes(a, -1, -2)


def _key_tiles(t, n_keys):
    tq = min(t, 512)
    tiles = dict(sb=256, fox=512, dsa=512)
    padded = {name: -(-n_keys // tk) * tk for name, tk in tiles.items()}
    return tq, tiles, padded


def _layer(h, b, t, past, ret_state, w, alpha, layer, depth, states):
    m, d = h.shape
    p_len = 0 if past is None else past[0].shape[1]
    n_keys = p_len + t
    tq, tk, lp = _key_tiles(t, n_keys)
    p = _inproj(h.reshape(b, t, d), w['w_row'], w['w_col'], layer, depth, states)
    states = {name: p[name] for name, _, _, _, _, is_state in _ROW_OUTS if is_state}

    old = (None,) * 8 if past is None else past
    sb_k0, sb_v0, fox_k0, fox_v0, fox_lf0, dsa_k0, dsa_v0, dsa_ki0 = old

    def rows_with_past(new_bf, olds, lp_):
        if past is not None:
            flat = [o.reshape(o.shape[0], o.shape[1], -1).astype(BF) for o in olds]
            new_bf = jnp.concatenate([jnp.concatenate(flat, axis=2), new_bf], axis=1)
        return jnp.pad(new_bf, ((0, 0), (0, lp_ - new_bf.shape[1]), (0, 0)))

    def cols_with_past(new_t, old, lp_):
        if old is not None:
            new_t = jnp.concatenate([_swap(old.reshape(old.shape[0], old.shape[1], -1).astype(BF)), new_t], axis=2)
        return jnp.pad(new_t, ((0, 0), (0, 0), (0, lp_ - new_t.shape[2])))

    y_sb = _sb_attention(p['sb_q_t'], rows_with_past(p['sb_k_bf'], [sb_k0], lp['sb']),
                         cols_with_past(p['sb_v_t'], sb_v0, lp['sb']), p_len, tq, tk['sb'])

    pos = p_len + jnp.arange(t, dtype=jnp.int32)
    y_ret, ret_state_t = _retention(p['ret_q_t'], p['ret_k_t'], p['ret_v_t'], p['ret_g_t'], _swap(ret_state),
                                    pos, min(t, RETENTION_CHUNK))

    fox_lf = jax.nn.log_sigmoid(states['fox_f'][layer] + w['b_forget'])
    lf_all = fox_lf if fox_lf0 is None else jnp.concatenate([fox_lf0, fox_lf], axis=1)
    cum = jnp.pad(jnp.cumsum(lf_all, axis=1), ((0, 0), (0, lp['fox'] - n_keys), (0, 0)))
    c_lanes = jnp.broadcast_to(_swap(cum)[..., None], (b, N_HEADS, lp['fox'], LANES))
    y_fox = _fox_attention(p['fox_q_t'], rows_with_past(p['fox_k_bf'], [fox_k0], lp['fox']),
                           cols_with_past(p['fox_v_t'], fox_v0, lp['fox']), c_lanes, p_len, tq, tk['fox'])

    top_k = min(DSA_TOP_K, n_keys // 4)
    kk_old = [dsa_k0, dsa_k0, dsa_ki0, dsa_ki0]
    y_dsa = _dsa_attention(p['dsa_q_t'], p['idx_q_t'], p['idx_w_t'], rows_with_past(p['dsa_kk_bf'], kk_old, lp['dsa']),
                           cols_with_past(p['dsa_v_t'], dsa_v0, lp['dsa']), p_len, n_keys, tq, tk['dsa'], top_k)

    h = _merge(h.reshape(b, t, d), (y_sb, y_ret, y_fox, y_dsa), w['w_gate'], w['w_branch'], w['w_out'],
               w['ln1_g'], w['ln1_b'], alpha)
    h = _ffn(h.reshape(m, d), w['w_ffn_in'], w['w_ffn_out'], w['ln2_g'], w['ln2_b'], alpha)
    return h, states, _swap(ret_state_t), fox_lf


def _group_outputs(states, ret_states, fox_lfs):
    def heads(a):
        return a.reshape(a.shape[:-1] + (N_HEADS, HEAD_DIM))

    return (heads(states['sb_k']), heads(states['sb_v']), jnp.stack(ret_states), heads(states['fox_k']),
            heads(states['fox_v']), jnp.stack(fox_lfs), states['dsa_k'], states['dsa_v'], states['idx_k'])


def kernel(x_prompt, x_sample, cache_sb_k, cache_sb_v, state_ret, cache_fox_k, cache_fox_v, cache_fox_logf,
           cache_dsa_k, cache_dsa_v, cache_dsa_kidx, w_in, b_forget, w_branch, w_out, ln1_g, ln1_b,
           w_ffn_in, w_ffn_out, ln2_g, ln2_b):
    depth = w_in.shape[0]
    alpha = float((2 * depth) ** 0.25)
    bp, tp, d = x_prompt.shape
    bs, ts, _ = x_sample.shape
    hp = x_prompt.reshape(bp * tp, d)
    hs = x_sample.reshape(bs * ts, d)
    ret_zero = jnp.zeros((bp, N_HEADS, HEAD_DIM, HEAD_DIM), F32)
    st_p, st_s, ret_p, ret_s, lf_p, lf_s = None, None, [], [], [], []
    for l in range(depth):
        w_row, w_col, w_gate = _inproj_weights(w_in[l])
        w = dict(w_row=w_row, w_col=w_col, w_gate=w_gate, b_forget=b_forget[l], w_branch=w_branch[l].astype(BF),
                 w_out=w_out[l].astype(BF), ln1_g=ln1_g[l][None], ln1_b=ln1_b[l][None],
                 w_ffn_in=w_ffn_in[l].astype(BF), w_ffn_out=w_ffn_out[l].astype(BF),
                 ln2_g=ln2_g[l][None], ln2_b=ln2_b[l][None])
        hp, st_p, ret, lf = _layer(hp, bp, tp, None, ret_zero, w, alpha, l, depth, st_p)
        ret_p.append(ret)
        lf_p.append(lf)
        past = (cache_sb_k[l], cache_sb_v[l], cache_fox_k[l], cache_fox_v[l], cache_fox_logf[l],
                cache_dsa_k[l], cache_dsa_v[l], cache_dsa_kidx[l])
        hs, st_s, ret, lf = _layer(hs, bs, ts, past, state_ret[l], w, alpha, l, depth, st_s)
        ret_s.append(ret)
        lf_s.append(lf)
    return ((hp.reshape(bp, tp, d), hs.reshape(bs, ts, d))
            + _group_outputs(st_p, ret_p, lf_p) + _group_outputs(st_s, ret_s, lf_s))
```

```python
import functools
import math

import numpy as np
import jax
import jax.numpy as jnp
from jax import lax
from jax.experimental import pallas as pl
from jax.experimental.pallas import tpu as pltpu

HEAD_DIM = 64
N_HEADS = 4
BRANCH_WIDTH = N_HEADS * HEAD_DIM
CHUNK_SHIFT = 6
DSA_TOP_K = 256
ROPE_BASE = 10000.0
LN_EPS = 1e-5
QK_SCALE = HEAD_DIM ** -0.5
IDX_HEAD_SCALE = N_HEADS ** -0.5
MASK_VALUE = -1e30
F32_EXP_UNDERFLOW = -105.0
INT_MIN = -2 ** 31

V7X_VMEM_LIMIT_BYTES = 56 * 1024 * 1024
LANES = 128
HEADS_PER_COL = LANES // HEAD_DIM
SUBLANES = 8
WORD_BITS = 32
GROUP_KEYS = SUBLANES * WORD_BITS
RETENTION_CHUNK = 512
MERGE_ROWS = 512
FFN_ROWS = 512

BF = jnp.bfloat16
F32 = jnp.float32


def _dot(a, b):
    return jnp.dot(a, b, preferred_element_type=F32)


def _dot_nt(a, b):
    return lax.dot_general(a, b, (((1,), (1,)), ((), ())), preferred_element_type=F32)


def _dot_tn(a, b):
    return lax.dot_general(a, b, (((0,), (0,)), ((), ())), preferred_element_type=F32)


def _params(*sem):
    return pltpu.CompilerParams(dimension_semantics=sem, vmem_limit_bytes=V7X_VMEM_LIMIT_BYTES)


def _const_spec(shape):
    nd = len(shape)
    return pl.BlockSpec(shape, lambda *_: (0,) * nd)


def _layer_norm(x, g, b):
    xc = x - jnp.mean(x, axis=-1, keepdims=True)
    var = jnp.mean(xc * xc, axis=-1, keepdims=True)
    return xc * lax.rsqrt(var + LN_EPS) * g + b


def _row_tile(m, want):
    t = min(m, want)
    assert m % t == 0
    return t


def _col(c):
    return slice(c * LANES, (c + 1) * LANES)


_NARROW_SRC = ('dsa_k', 'dsa_v', 'idx_k', 'fox_f')
_ROW_OUTS = (
    ('sb_k', ('sb_k',), F32, 0, BRANCH_WIDTH, True), ('sb_k_bf', ('sb_k',), BF, 0, BRANCH_WIDTH, False),
    ('sb_v', ('sb_v',), F32, 0, BRANCH_WIDTH, True),
    ('fox_k', ('fox_k',), F32, 0, BRANCH_WIDTH, True), ('fox_k_bf', ('fox_k',), BF, 0, BRANCH_WIDTH, False),
    ('fox_v', ('fox_v',), F32, 0, BRANCH_WIDTH, True),
    ('dsa_k', _NARROW_SRC, F32, 0, HEAD_DIM, True), ('dsa_v', _NARROW_SRC, F32, HEAD_DIM, HEAD_DIM, True),
    ('idx_k', _NARROW_SRC, F32, 2 * HEAD_DIM, HEAD_DIM, True), ('fox_f', _NARROW_SRC, F32, 3 * HEAD_DIM, N_HEADS, True),
    ('dsa_kk_bf', ('dsa_k', 'dsa_k', 'idx_k', 'idx_k'), BF, 0, 4 * HEAD_DIM, False),
)
_COL_OUTS = (
    ('sb_q_t', ('sb_q',), BF), ('fox_q_t', ('fox_q',), BF), ('dsa_q_t', ('dsa_q',), BF),
    ('idx_q_t', ('idx_q',), BF), ('sb_v_t', ('sb_v',), BF), ('fox_v_t', ('fox_v',), BF),
    ('ret_q_t', ('ret_q',), F32), ('ret_k_t', ('ret_k',), F32), ('ret_v_t', ('ret_v',), F32),
    ('ret_g_t', ('ret_g',), F32), ('dsa_v_t', ('dsa_v',), BF), ('idx_w_t', ('idx_w',), F32),
)
BF16_ROWS_PER_VREG = 16


def _inproj_plan(d):
    widths = dict(_in_layout(d))

    def spans(outs, multiple):
        span_of, off = {}, 0
        for out in outs:
            srcs = out[1]
            if srcs not in span_of:
                w = -(-sum(widths[s] for s in srcs) // multiple) * multiple
                span_of[srcs] = (off, w)
                off += w
        return span_of, off

    row_spans, _ = spans(_ROW_OUTS, LANES)
    col_spans, n_col = spans(_COL_OUTS, BF16_ROWS_PER_VREG)
    return row_spans, col_spans, -(-n_col // LANES) * LANES


def _inproj_weights(w_in):
    d = w_in.shape[0]
    w_bf = w_in.astype(BF)
    pieces, off = {}, 0
    for name, width in _in_layout(d):
        piece = w_bf[:, off:off + width]
        scale = _FOLDED_SCALE.get(name)
        pieces[name] = piece if scale is None else piece * jnp.asarray(scale, BF)
        off += width
    assert off == w_in.shape[1]
    row_spans, col_spans, n_col = _inproj_plan(d)

    def block(srcs, width):
        w = jnp.concatenate([pieces[s] for s in srcs], axis=1)
        return jnp.pad(w, ((0, 0), (0, width - w.shape[1])))

    w_row = jnp.concatenate([block(srcs, w) for srcs, (_, w) in row_spans.items()], axis=1)
    w_col = jnp.concatenate([block(srcs, w) for srcs, (_, w) in col_spans.items()], axis=1)
    w_col = jnp.pad(w_col, ((0, 0), (0, n_col - w_col.shape[1])))
    return w_row.astype(BF), w_col.T.astype(BF), pieces['merge_gate'].astype(BF)


def _inproj_body(x_ref, wr_ref, wc_ref, *refs, row_spans, col_spans, n_alias):
    o_refs = refs[n_alias:]
    xb = x_ref[0].astype(BF)
    done = {}
    for o_ref, (_, srcs, _, lane, width, _) in zip(o_refs, _ROW_OUTS):
        if srcs not in done:
            off, w = row_spans[srcs]
            done[srcs] = _dot(xb, wr_ref[:, off:off + w])
        o_ref[...] = done[srcs][:, lane:lane + width].astype(o_ref.dtype).reshape(o_ref.shape)
    for o_ref, (_, srcs, _) in zip(o_refs[len(_ROW_OUTS):], _COL_OUTS):
        off, w = col_spans[srcs]
        o_ref[0] = _dot_nt(wc_ref[off:off + w, :], xb).astype(o_ref.dtype)


def _inproj(x, w_row, w_col, layer, depth, states):
    b, t, d = x.shape
    tm = _row_tile(t, 512)
    row_spans, col_spans, _ = _inproj_plan(d)
    out_shape, out_specs, state_names = [], [], []
    for name, _, dt, _, w, is_state in _ROW_OUTS:
        if is_state:
            state_names.append(name)
            out_shape.append(jax.ShapeDtypeStruct((depth, b, t, w), dt))
            out_specs.append(pl.BlockSpec((1, 1, tm, w), lambda bi, i: (layer, bi, i, 0)))
        else:
            out_shape.append(jax.ShapeDtypeStruct((b, t, w), dt))
            out_specs.append(pl.BlockSpec((1, tm, w), lambda bi, i: (bi, i, 0)))
    for _, srcs, dt in _COL_OUTS:
        w = col_spans[srcs][1]
        out_shape.append(jax.ShapeDtypeStruct((b, w, t), dt))
        out_specs.append(pl.BlockSpec((1, w, tm), lambda bi, i: (bi, 0, i)))
    prev = [] if states is None else [states[n] for n in state_names]
    names = [o[0] for o in _ROW_OUTS] + [o[0] for o in _COL_OUTS]
    aliases = {3 + j: names.index(n) for j, n in enumerate(state_names)} if prev else {}
    outs = pl.pallas_call(
        functools.partial(_inproj_body, row_spans=row_spans, col_spans=col_spans, n_alias=len(prev)),
        out_shape=out_shape,
        grid=(b, t // tm),
        in_specs=[pl.BlockSpec((1, tm, d), lambda bi, i: (bi, i, 0)),
                  _const_spec(w_row.shape), _const_spec(w_col.shape)]
                 + [pl.BlockSpec(memory_space=pl.ANY)] * len(prev),
        out_specs=out_specs,
        input_output_aliases=aliases,
        compiler_params=_params("parallel", "parallel"),
        name="inproj",
    )(x, w_row, w_col, *prev)
    return dict(zip(names, outs))


def _block_counts(q0, tk, last_key):
    return lax.div(q0, tk), lax.div(last_key, tk) + 1


def _head_queries(qt_ref):
    low = lax.broadcasted_iota(jnp.int32, (LANES, 1), 0) < HEAD_DIM
    out = []
    for h in range(N_HEADS):
        qc = qt_ref[0, _col(h // HEADS_PER_COL), :]
        keep = low if h % HEADS_PER_COL == 0 else jnp.logical_not(low)
        out.append(jnp.where(keep, qc, jnp.zeros_like(qc)))
    return out


def _head_rows(h):
    return slice(h * HEAD_DIM, (h + 1) * HEAD_DIM)


def _key_minus_query(tk, tq):
    return (lax.broadcasted_iota(jnp.int32, (tk, tq), 0) - lax.broadcasted_iota(jnp.int32, (tk, tq), 1))


def _qt_spec(w, tq):
    return pl.BlockSpec((1, w, tq), lambda bi, qi: (bi, 0, qi))


def _whole_spec(rows, cols):
    return pl.BlockSpec((1, rows, cols), lambda bi, qi: (bi, 0, 0))


def _sb_body(qt_ref, k_ref, vt_ref, o_ref, *, p_len, tq, tk):
    q0 = p_len + pl.program_id(1) * tq
    qm = _head_queries(qt_ref)
    diff = _key_minus_query(tk, tq)
    later = (lax.broadcasted_iota(jnp.int32, (tk, 2 * tk), 1) & (tk - 1)) > lax.broadcasted_iota(
        jnp.int32, (tk, 2 * tk), 0)
    minus_later = jnp.where(later, -1.0, 0.0).astype(BF)
    n_full, n_all = _block_counts(q0, tk, jnp.maximum(q0 + tq - 2, 0))

    def step(kb, carry, masked):
        laters, accs = carry
        s0 = pl.multiple_of(kb * tk, tk)
        if masked:
            earlier = diff < (q0 - s0)
        zs = [_dot(k_ref[0, pl.ds(s0, tk), _col(h // HEADS_PER_COL)], qm[h]) for h in range(N_HEADS)]
        new_laters, log_bs, afters = [], [], []
        for h in range(N_HEADS):
            z = zs[h]
            minus_abs = pltpu.bitcast(pltpu.bitcast(z, jnp.int32) | jnp.int32(INT_MIN), F32)
            softplus = jnp.maximum(z, 0.0) + jnp.log(1.0 + jnp.exp(minus_abs))
            log_bs.append(z - softplus)
            if masked:
                softplus = jnp.where(earlier, softplus, 0.0)
            hi = softplus.astype(BF)
            lo = (softplus - hi.astype(F32)).astype(BF)
            after = _dot(minus_later, jnp.concatenate([hi, lo], axis=0)) + laters[h]
            afters.append(after)
            new_laters.append(after[0:1, :] - softplus[0:1, :])
        new_accs = []
        for h in range(N_HEADS):
            w = jnp.exp(log_bs[h] + afters[h])
            if masked:
                w = jnp.where(earlier, w, 0.0)
            new_accs.append(accs[h] + _dot(vt_ref[0, _head_rows(h), pl.ds(s0, tk)], w.astype(BF)))
        return tuple(new_laters), tuple(new_accs)

    carry = (tuple(jnp.zeros((1, tq), F32) for _ in range(N_HEADS)),
             tuple(jnp.zeros((HEAD_DIM, tq), F32) for _ in range(N_HEADS)))
    carry = lax.fori_loop(0, n_all - n_full, lambda i, c: step(n_all - 1 - i, c, True), carry)

    def reachable(laters):
        return jnp.max(functools.reduce(jnp.maximum, laters)) > F32_EXP_UNDERFLOW

    def earlier_block(state):
        i, _, c = state
        c = step(n_full - 1 - i, c, False)
        return i + 1, reachable(c[0]), c

    _, _, carry = lax.while_loop(lambda state: (state[0] < n_full) & state[1], earlier_block,
                                 (jnp.int32(0), reachable(carry[0]), carry))
    for h in range(N_HEADS):
        o_ref[0, _head_rows(h), :] = carry[1][h].astype(o_ref.dtype)


def _sb_attention(qt, k, vt, p_len, tq, tk):
    b, w, t = qt.shape
    lp = k.shape[1]
    assert tk & (tk - 1) == 0
    return pl.pallas_call(
        functools.partial(_sb_body, p_len=p_len, tq=tq, tk=tk),
        out_shape=jax.ShapeDtypeStruct((b, w, t), BF),
        grid=(b, t // tq),
        in_specs=[_qt_spec(w, tq), _whole_spec(lp, w), _whole_spec(w, lp)],
        out_specs=_qt_spec(w, tq),
        compiler_params=_params("parallel", "arbitrary"),
        name="sb_attention",
    )(qt, k, vt)


def _online_softmax_step(logits, m, l):
    m_new = jnp.maximum(m, jnp.max(logits, axis=0, keepdims=True))
    alpha = jnp.exp(m - m_new)
    p = jnp.exp(logits - m_new)
    return m_new, alpha, alpha * l + jnp.sum(p, axis=0, keepdims=True), p


def _fox_body(qt_ref, k_ref, vt_ref, c_ref, kmax_ref, ctop_ref, o_ref, *, p_len, tq, tk, n_blocks):
    bi = pl.program_id(0)
    q0 = p_len + pl.program_id(1) * tq
    qm = _head_queries(qt_ref)
    diff = _key_minus_query(tk, tq)
    n_full, n_all = _block_counts(q0, tk, q0 + tq - 1)
    dot_bound = [_dot(kmax_ref[0, :, _col(h // HEADS_PER_COL)], jnp.abs(qm[h]))[0:1, :] for h in range(N_HEADS)]

    def key_bias(h, s0):
        c = c_ref[0, h, pl.ds(s0, tk), :]
        return c[:, :tq] if tq <= LANES else jnp.concatenate([c] * (tq // LANES), axis=1)

    def step(kb, carry, masked):
        ms, ls, accs = carry
        s0 = pl.multiple_of(kb * tk, tk)
        if masked:
            visible = diff <= (q0 - s0)
        new_ms, new_ls, new_accs = [], [], []
        raw = [_dot(k_ref[0, pl.ds(s0, tk), _col(h // HEADS_PER_COL)], qm[h]) for h in range(N_HEADS)]
        for h in range(N_HEADS):
            logits = raw[h] - key_bias(h, s0)
            if masked:
                logits = jnp.where(visible, logits, MASK_VALUE)
            m_new, alpha, l_new, p = _online_softmax_step(logits, ms[h], ls[h])
            new_ms.append(m_new)
            new_ls.append(l_new)
            new_accs.append(alpha * accs[h] + _dot(vt_ref[0, _head_rows(h), pl.ds(s0, tk)], p.astype(BF)))
        return tuple(new_ms), tuple(new_ls), tuple(new_accs)

    carry = (tuple(jnp.full((1, tq), -jnp.inf, F32) for _ in range(N_HEADS)),
             tuple(jnp.zeros((1, tq), F32) for _ in range(N_HEADS)),
             tuple(jnp.zeros((HEAD_DIM, tq), F32) for _ in range(N_HEADS)))
    carry = lax.fori_loop(0, n_all - n_full, lambda i, c: step(n_all - 1 - i, c, True), carry)

    def reachable(kb, ms):
        j = jnp.maximum(kb, 0)
        gaps = [dot_bound[h] + ctop_ref[bi, h * n_blocks + j] - ms[h] for h in range(N_HEADS)]
        return jnp.max(functools.reduce(jnp.maximum, gaps)) > F32_EXP_UNDERFLOW - 1.0

    def older_block(state):
        i, _, c = state
        c = step(n_full - 1 - i, c, False)
        return i + 1, reachable(n_full - 2 - i, c[0]), c

    _, _, carry = lax.while_loop(lambda state: (state[0] < n_full) & state[1], older_block,
                                 (jnp.int32(0), reachable(n_full - 1, carry[0]), carry))
    _, ls, accs = carry
    for h in range(N_HEADS):
        o_ref[0, _head_rows(h), :] = (accs[h] / ls[h]).astype(o_ref.dtype)


def _fox_attention(qt, k, vt, c_lanes, kmax, ctop, p_len, tq, tk):
    b, w, t = qt.shape
    lp = k.shape[1]
    assert tq <= LANES or tq % LANES == 0
    return pl.pallas_call(
        functools.partial(_fox_body, p_len=p_len, tq=tq, tk=tk, n_blocks=lp // tk),
        out_shape=jax.ShapeDtypeStruct((b, w, t), BF),
        grid=(b, t // tq),
        in_specs=[_qt_spec(w, tq), _whole_spec(lp, w), _whole_spec(w, lp),
                  pl.BlockSpec((1, N_HEADS, lp, LANES), lambda bi, qi: (bi, 0, 0, 0)),
                  _whole_spec(SUBLANES, w), pl.BlockSpec(memory_space=pltpu.SMEM)],
        out_specs=_qt_spec(w, tq),
        compiler_params=_params("parallel", "arbitrary"),
        name="fox_attention",
    )(qt, k, vt, c_lanes, kmax, ctop)


def _float_key(bits):
    return jnp.where(bits < 0, jnp.int32(INT_MIN) - bits, bits)


_BUTTERFLY = ((16, 0x0000FFFF), (8, 0x00FF00FF), (4, 0x0F0F0F0F), (2, 0x33333333), (1, 0x55555555))


def _bit_planes(words):
    a = list(words)
    for shift, mask in _BUTTERFLY:
        for k in range(WORD_BITS):
            if k & shift == 0:
                t = (a[k] ^ lax.shift_right_logical(a[k + shift], shift)) & mask
                a[k] = a[k] ^ t
                a[k + shift] = a[k + shift] ^ lax.shift_left(t, shift)
    return a


def _dsa_body(qt_ref, qit_ref, wit_ref, k_ref, vt_ref, ki_ref, o_ref, keys_ref, planes_ref, active_ref, *,
              p_len, n_keys, tq, tk, top_k):
    q0 = p_len + pl.program_id(1) * tq
    qpos = q0 + lax.broadcasted_iota(jnp.int32, (1, tq), 1)
    limit = jnp.minimum(((qpos >> CHUNK_SHIFT) + 1) << CHUNK_SHIFT, n_keys)
    last_limit = jnp.minimum((((q0 + tq - 1) >> CHUNK_SHIFT) + 1) << CHUNK_SHIFT, n_keys)
    n_blk = lax.div(last_limit - 1, tk) + 1
    key_row = lax.broadcasted_iota(jnp.int32, (tk, tq), 0)

    wit = wit_ref[0]
    qim = _head_queries(qit_ref)

    def score_step(kb, _):
        s0 = pl.multiple_of(kb * tk, tk)
        ki = ki_ref[0, pl.ds(s0, tk), :]
        score = jnp.zeros((tk, tq), F32)
        for h in range(N_HEADS):
            score = score + wit[h:h + 1, :] * jnp.maximum(_dot(ki, qim[h]), 0.0)
        score = jnp.where(key_row < limit - s0, score, -jnp.inf)
        key = _float_key(pltpu.bitcast(score, jnp.int32))
        keys_ref[pl.ds(s0, tk), :] = key
        unsigned = key ^ jnp.int32(INT_MIN)
        for g in range(tk // GROUP_KEYS):
            words = [unsigned[g * GROUP_KEYS + SUBLANES * j:g * GROUP_KEYS + SUBLANES * (j + 1), :]
                     for j in range(WORD_BITS)]
            r0 = pl.multiple_of(kb * plane_rows + g * SUBLANES, SUBLANES)
            for i, plane in enumerate(_bit_planes(words)):
                planes_ref[i, pl.ds(r0, SUBLANES), :] = plane
        r0 = pl.multiple_of(kb * plane_rows, plane_rows)
        active_ref[pl.ds(r0, plane_rows), :] = jnp.full((plane_rows, tq), -1, jnp.int32)
        return 0

    plane_rows = tk // WORD_BITS
    lax.fori_loop(0, n_blk, score_step, 0)

    def ones_in(i, narrow):
        def body(kb, acc):
            r0 = pl.multiple_of(kb * plane_rows, plane_rows)
            live = active_ref[pl.ds(r0, plane_rows), :]
            if narrow is not None:
                live = live & (planes_ref[i - 1, pl.ds(r0, plane_rows), :] ^ narrow)
                active_ref[pl.ds(r0, plane_rows), :] = live
            return acc + lax.population_count(live & planes_ref[i, pl.ds(r0, plane_rows), :])
        acc = lax.fori_loop(0, n_blk, body, jnp.zeros((plane_rows, tq), jnp.int32))
        return jnp.sum(acc.astype(F32), axis=0, keepdims=True)

    def choose(i, n_ones, wanted, thr_bits):
        take = n_ones >= wanted
        bit = lax.shift_left(jnp.int32(1), WORD_BITS - 1 - i)
        return (jnp.where(take, wanted, wanted - n_ones), thr_bits | jnp.where(take, bit, 0),
                jnp.where(take, 0, -1))

    wanted, thr_bits, narrow = choose(0, ones_in(0, None), jnp.full((1, tq), top_k, F32),
                                      jnp.zeros((1, tq), jnp.int32))

    def bit_step(i, carry):
        wanted, thr_bits, narrow = carry
        return choose(i, ones_in(i, narrow), wanted, thr_bits)

    wanted, thr_bits, _ = lax.fori_loop(1, WORD_BITS, bit_step, (wanted, thr_bits, narrow))
    thr = thr_bits ^ jnp.int32(INT_MIN)
    n_tie_wanted = wanted

    earlier_keys = (lax.broadcasted_iota(jnp.int32, (tk, tk), 1)
                    < lax.broadcasted_iota(jnp.int32, (tk, tk), 0)).astype(BF)
    qm = _head_queries(qt_ref)

    def attend_step(kb, carry):
        ties_seen, ms, ls, accs = carry
        s0 = pl.multiple_of(kb * tk, tk)
        key = keys_ref[pl.ds(s0, tk), :]
        tie = jnp.where(key == thr, 1.0, 0.0)
        tie_rank = _dot(earlier_keys, tie.astype(BF)) + ties_seen
        take = jnp.where(key > thr, 1.0, jnp.where(tie_rank < n_tie_wanted, tie, 0.0))
        selected = jnp.where(key_row < limit - s0, take, 0.0) > 0.0
        k = k_ref[0, pl.ds(s0, tk), :]
        vt = vt_ref[0, :, pl.ds(s0, tk)]
        new_ms, new_ls, new_accs = [], [], []
        raw = [_dot(k, qm[h]) for h in range(N_HEADS)]
        for h in range(N_HEADS):
            logits = jnp.where(selected, raw[h], MASK_VALUE)
            m_new, alpha, l_new, p = _online_softmax_step(logits, ms[h], ls[h])
            new_ms.append(m_new)
            new_ls.append(l_new)
            new_accs.append(alpha * accs[h] + _dot(vt, p.astype(BF)))
        ties_seen = ties_seen + jnp.sum(tie, axis=0, keepdims=True)
        return ties_seen, tuple(new_ms), tuple(new_ls), tuple(new_accs)

    carry = (jnp.zeros((1, tq), F32),
             tuple(jnp.full((1, tq), -jnp.inf, F32) for _ in range(N_HEADS)),
             tuple(jnp.zeros((1, tq), F32) for _ in range(N_HEADS)),
             tuple(jnp.zeros((HEAD_DIM, tq), F32) for _ in range(N_HEADS)))
    _, _, ls, accs = lax.fori_loop(0, n_blk, attend_step, carry)
    for h in range(N_HEADS):
        o_ref[0, _head_rows(h), :] = (accs[h] / ls[h]).astype(o_ref.dtype)


def _dsa_attention(qt, qit, wit, kk, vt, p_len, n_keys, tq, tk, top_k):
    b, w, t = qt.shape
    lp = kk.shape[1]
    assert tk % GROUP_KEYS == 0
    return pl.pallas_call(
        functools.partial(_dsa_body, p_len=p_len, n_keys=n_keys, tq=tq, tk=tk, top_k=top_k),
        out_shape=jax.ShapeDtypeStruct((b, w, t), BF),
        grid=(b, t // tq),
        in_specs=[_qt_spec(w, tq), _qt_spec(w, tq), _qt_spec(wit.shape[1], tq),
                  pl.BlockSpec((1, lp, LANES), lambda bi, qi: (bi, 0, 0)), _whole_spec(HEAD_DIM, lp),
                  pl.BlockSpec((1, lp, LANES), lambda bi, qi: (bi, 0, 1))],
        out_specs=_qt_spec(w, tq),
        scratch_shapes=[pltpu.VMEM((lp, tq), jnp.int32),
                        pltpu.VMEM((WORD_BITS, lp // WORD_BITS, tq), jnp.int32),
                        pltpu.VMEM((lp // WORD_BITS, tq), jnp.int32)],
        compiler_params=_params("parallel", "arbitrary"),
        name="dsa_attention",
    )(qt, qit, wit, kk, vt, kk)


def _ret_body(q_ref, k_ref, v_ref, g_ref, s0_ref, cos_ref, sin_ref, dec_ref, qd_ref, kd_ref, sd_ref,
              o_ref, so_ref, state_ref):
    c = pl.program_id(1)

    @pl.when(c == 0)
    def _():
        state_ref[...] = s0_ref[0]

    cos, sin = cos_ref[...], sin_ref[...]
    half = HEAD_DIM // 2

    def rotary(x):
        x1, x2 = x[:half], x[half:]
        return jnp.concatenate([x1 * cos - x2 * sin, x2 * cos + x1 * sin], axis=0)

    heads = range(N_HEADS)
    qb = [rotary(q_ref[0, _head_rows(h), :]).astype(BF) for h in heads]
    k = [rotary(k_ref[0, _head_rows(h), :]) for h in heads]
    vb = [v_ref[0, _head_rows(h), :].astype(BF) for h in heads]
    scores_t = [_dot_tn(k[h].astype(BF), qb[h]) * dec_ref[h] for h in heads]
    carried = [_dot(state_ref[h].astype(BF), qb[h]) * qd_ref[h] for h in heads]
    outs = [_dot(vb[h], scores_t[h].astype(BF)) + carried[h] for h in heads]
    for h in heads:
        state_ref[h] = sd_ref[h] * state_ref[h] + _dot_nt(vb[h], (k[h] * kd_ref[h]).astype(BF))
    for h in heads:
        o = outs[h]
        oc = o - jnp.mean(o, axis=0, keepdims=True)
        on = oc * lax.rsqrt(jnp.mean(oc * oc, axis=0, keepdims=True) + LN_EPS)
        g = g_ref[0, _head_rows(h), :]
        o_ref[0, _head_rows(h), :] = (on * (g * jax.nn.sigmoid(g))).astype(o_ref.dtype)

    @pl.when(c == pl.num_programs(1) - 1)
    def _():
        so_ref[0] = state_ref[...]


def _retention(qt, kt, vt, gt, state0_t, pos, c):
    b, w, t = qt.shape
    h = w // HEAD_DIM
    half = HEAD_DIM // 2
    inv_freq = ROPE_BASE ** (-jnp.arange(half, dtype=F32) / half)
    ang = inv_freq[:, None] * pos.astype(F32)[None, :]
    log_gamma = np.log(1.0 - 2.0 ** (-5.0 - np.arange(h, dtype=np.float64)))
    n = np.arange(c, dtype=np.float64)
    rel = n[None, :] - n[:, None]
    decay_t = np.where(rel >= 0, np.exp(np.maximum(rel, 0.0)[None] * log_gamma[:, None, None]), 0.0)
    q_decay = np.exp((n[None, :] + 1.0) * log_gamma[:, None])[:, None, :]
    k_decay = np.exp((c - 1.0 - n)[None, :] * log_gamma[:, None])[:, None, :]
    s_decay = np.exp(c * log_gamma)[:, None, None]
    tables = [jnp.asarray(a, F32) for a in (decay_t, q_decay, k_decay, s_decay)]
    assert h == N_HEADS
    x_spec = pl.BlockSpec((1, w, c), lambda bi, ci: (bi, 0, ci))
    s_spec = pl.BlockSpec((1, h, HEAD_DIM, HEAD_DIM), lambda bi, ci: (bi, 0, 0, 0))
    rope_spec = pl.BlockSpec((half, c), lambda bi, ci: (0, ci))
    return pl.pallas_call(
        _ret_body,
        out_shape=(jax.ShapeDtypeStruct((b, w, t), BF),
                   jax.ShapeDtypeStruct((b, h, HEAD_DIM, HEAD_DIM), F32)),
        grid=(b, t // c),
        in_specs=[x_spec, x_spec, x_spec, x_spec, s_spec, rope_spec, rope_spec] + [_const_spec(a.shape) for a in tables],
        out_specs=(x_spec, s_spec),
        scratch_shapes=[pltpu.VMEM((h, HEAD_DIM, HEAD_DIM), F32)],
        compiler_params=_params("parallel", "arbitrary"),
        name="retention",
    )(qt, kt, vt, gt, state0_t, jnp.cos(ang), jnp.sin(ang), *tables)


def _merge_body(h_ref, y0_ref, y1_ref, y2_ref, y3_ref, wg_ref, wb_ref, wo_ref, g_ref, b_ref, o_ref, *, alpha):
    nb, tm, d = h_ref.shape
    h = h_ref[...].reshape(nb * tm, d)
    hb = h.astype(BF)
    merged = jnp.zeros(h.shape, F32)
    for i, y_ref in enumerate((y0_ref, y1_ref, y2_ref, y3_ref)):
        gate = jax.nn.sigmoid(_dot(hb, wg_ref[:, i * d:(i + 1) * d]))
        branch = jnp.concatenate([_dot_tn(y_ref[j], wb_ref[i]) for j in range(nb)], axis=0)
        merged = merged + gate * branch
    r = alpha * h + _dot(merged.astype(BF), wo_ref[...])
    o_ref[...] = _layer_norm(r, g_ref[...], b_ref[...]).reshape(nb, tm, d)


def _merge(h, ys_t, w_gate, w_branch, w_out, ln_g, ln_b, alpha):
    b, t, d = h.shape
    tm = _row_tile(t, MERGE_ROWS)
    nb = math.gcd(b, max(1, MERGE_ROWS // tm))
    row = pl.BlockSpec((nb, tm, d), lambda bi, i: (bi, i, 0))
    col = pl.BlockSpec((nb, BRANCH_WIDTH, tm), lambda bi, i: (bi, 0, i))
    return pl.pallas_call(
        functools.partial(_merge_body, alpha=alpha),
        out_shape=jax.ShapeDtypeStruct((b, t, d), F32),
        grid=(b // nb, t // tm),
        in_specs=[row] + [col] * 4
                 + [_const_spec(w_gate.shape), _const_spec(w_branch.shape), _const_spec(w_out.shape),
                    _const_spec((1, d)), _const_spec((1, d))],
        out_specs=row,
        compiler_params=_params("parallel", "parallel"),
        name="merge",
    )(h, *ys_t, w_gate, w_branch, w_out, ln_g, ln_b)


def _ffn_body(h_ref, wi_ref, wo_ref, g_ref, b_ref, o_ref, *, alpha, f_chunk):
    h = h_ref[...]
    hb = h.astype(BF)
    f = wo_ref.shape[0]
    acc = jnp.zeros(h.shape, F32)
    for c in range(0, f, f_chunk):
        a = _dot(hb, wi_ref[:, c:c + f_chunk])
        u = _dot(hb, wi_ref[:, f + c:f + c + f_chunk])
        acc = acc + _dot((a * jax.nn.sigmoid(a) * u).astype(BF), wo_ref[c:c + f_chunk, :])
    o_ref[...] = _layer_norm(alpha * h + acc, g_ref[...], b_ref[...])


def _ffn(h, w_in, w_out, ln_g, ln_b, alpha):
    m, d = h.shape
    f = w_out.shape[0]
    tm = _row_tile(m, FFN_ROWS)
    f_chunk = f // 2 if (f // 2) % LANES == 0 else f
    row = pl.BlockSpec((tm, d), lambda i: (i, 0))
    return pl.pallas_call(
        functools.partial(_ffn_body, alpha=alpha, f_chunk=f_chunk),
        out_shape=jax.ShapeDtypeStruct((m, d), F32),
        grid=(m // tm,),
        in_specs=[row, _const_spec(w_in.shape), _const_spec(w_out.shape),
                  _const_spec((1, d)), _const_spec((1, d))],
        out_specs=row,
        compiler_params=_params("parallel"),
        name="ffn",
    )(h, w_in, w_out, ln_g, ln_b)


def _in_layout(d):
    w = BRANCH_WIDTH
    return (('sb_q', w), ('sb_k', w), ('sb_v', w), ('ret_q', w), ('ret_k', w), ('ret_v', w), ('ret_g', w),
            ('fox_q', w), ('fox_k', w), ('fox_v', w), ('fox_f', N_HEADS),
            ('dsa_q', w), ('dsa_k', HEAD_DIM), ('dsa_v', HEAD_DIM),
            ('idx_q', w), ('idx_k', HEAD_DIM), ('idx_w', N_HEADS), ('merge_gate', 4 * d))


_FOLDED_SCALE = dict(sb_q=QK_SCALE, fox_q=QK_SCALE, dsa_q=QK_SCALE, idx_q=QK_SCALE, ret_k=QK_SCALE,
                     idx_w=IDX_HEAD_SCALE)


def _swap(a):
    return jnp.swapaxes(a, -1, -2)


def _key_tiles(t, n_keys):
    tq = min(t, 512)
    tiles = dict(sb=256, fox=512, dsa=512)
    padded = {name: -(-n_keys // tk) * tk for name, tk in tiles.items()}
    return tq, tiles, padded


def _layer(h, b, t, past, ret_state, w, alpha, layer, depth, states):
    m, d = h.shape
    p_len = 0 if past is None else past[0].shape[1]
    n_keys = p_len + t
    tq, tk, lp = _key_tiles(t, n_keys)
    p = _inproj(h.reshape(b, t, d), w['w_row'], w['w_col'], layer, depth, states)
    states = {name: p[name] for name, _, _, _, _, is_state in _ROW_OUTS if is_state}

    old = (None,) * 8 if past is None else past
    sb_k0, sb_v0, fox_k0, fox_v0, fox_lf0, dsa_k0, dsa_v0, dsa_ki0 = old

    def rows_with_past(new_bf, olds, lp_):
        if past is not None:
            flat = [o.reshape(o.shape[0], o.shape[1], -1).astype(BF) for o in olds]
            new_bf = jnp.concatenate([jnp.concatenate(flat, axis=2), new_bf], axis=1)
        return jnp.pad(new_bf, ((0, 0), (0, lp_ - new_bf.shape[1]), (0, 0)))

    def cols_with_past(new_t, old, lp_):
        if old is not None:
            new_t = jnp.concatenate([_swap(old.reshape(old.shape[0], old.shape[1], -1).astype(BF)), new_t], axis=2)
        return jnp.pad(new_t, ((0, 0), (0, 0), (0, lp_ - new_t.shape[2])))

    y_sb = _sb_attention(p['sb_q_t'], rows_with_past(p['sb_k_bf'], [sb_k0], lp['sb']),
                         cols_with_past(p['sb_v_t'], sb_v0, lp['sb']), p_len, tq, tk['sb'])

    pos = p_len + jnp.arange(t, dtype=jnp.int32)
    y_ret, ret_state_t = _retention(p['ret_q_t'], p['ret_k_t'], p['ret_v_t'], p['ret_g_t'], _swap(ret_state),
                                    pos, min(t, RETENTION_CHUNK))

    fox_lf = jax.nn.log_sigmoid(states['fox_f'][layer] + w['b_forget'])
    lf_all = fox_lf if fox_lf0 is None else jnp.concatenate([fox_lf0, fox_lf], axis=1)
    cum = jnp.pad(jnp.cumsum(lf_all, axis=1), ((0, 0), (0, lp['fox'] - n_keys), (0, 0)))
    c_lanes = jnp.broadcast_to(_swap(cum)[..., None], (b, N_HEADS, lp['fox'], LANES))
    ctop = _swap(lax.cummax(-cum, axis=1)[:, tk['fox'] - 1::tk['fox']]).reshape(b, -1)
    fox_keys = rows_with_past(p['fox_k_bf'], [fox_k0], lp['fox'])
    kmax = jnp.broadcast_to(jnp.max(jnp.abs(fox_keys), axis=1, keepdims=True), (b, SUBLANES, BRANCH_WIDTH))
    y_fox = _fox_attention(p['fox_q_t'], fox_keys, cols_with_past(p['fox_v_t'], fox_v0, lp['fox']), c_lanes,
                           kmax, ctop, p_len, tq, tk['fox'])

    top_k = min(DSA_TOP_K, n_keys // 4)
    kk_old = [dsa_k0, dsa_k0, dsa_ki0, dsa_ki0]
    y_dsa = _dsa_attention(p['dsa_q_t'], p['idx_q_t'], p['idx_w_t'], rows_with_past(p['dsa_kk_bf'], kk_old, lp['dsa']),
                           cols_with_past(p['dsa_v_t'], dsa_v0, lp['dsa']), p_len, n_keys, tq, tk['dsa'], top_k)

    h = _merge(h.reshape(b, t, d), (y_sb, y_ret, y_fox, y_dsa), w['w_gate'], w['w_branch'], w['w_out'],
               w['ln1_g'], w['ln1_b'], alpha)
    h = _ffn(h.reshape(m, d), w['w_ffn_in'], w['w_ffn_out'], w['ln2_g'], w['ln2_b'], alpha)
    return h, states, _swap(ret_state_t), fox_lf


def _group_outputs(states, ret_states, fox_lfs):
    def heads(a):
        return a.reshape(a.shape[:-1] + (N_HEADS, HEAD_DIM))

    return (heads(states['sb_k']), heads(states['sb_v']), jnp.stack(ret_states), heads(states['fox_k']),
            heads(states['fox_v']), jnp.stack(fox_lfs), states['dsa_k'], states['dsa_v'], states['idx_k'])


def kernel(x_prompt, x_sample, cache_sb_k, cache_sb_v, state_ret, cache_fox_k, cache_fox_v, cache_fox_logf,
           cache_dsa_k, cache_dsa_v, cache_dsa_kidx, w_in, b_forget, w_branch, w_out, ln1_g, ln1_b,
           w_ffn_in, w_ffn_out, ln2_g, ln2_b):
    depth = w_in.shape[0]
    alpha = float((2 * depth) ** 0.25)
    bp, tp, d = x_prompt.shape
    bs, ts, _ = x_sample.shape
    hp = x_prompt.reshape(bp * tp, d)
    hs = x_sample.reshape(bs * ts, d)
    ret_zero = jnp.zeros((bp, N_HEADS, HEAD_DIM, HEAD_DIM), F32)
    st_p, st_s, ret_p, ret_s, lf_p, lf_s = None, None, [], [], [], []
    for l in range(depth):
        w_row, w_col, w_gate = _inproj_weights(w_in[l])
        w = dict(w_row=w_row, w_col=w_col, w_gate=w_gate, b_forget=b_forget[l], w_branch=w_branch[l].astype(BF),
                 w_out=w_out[l].astype(BF), ln1_g=ln1_g[l][None], ln1_b=ln1_b[l][None],
                 w_ffn_in=w_ffn_in[l].astype(BF), w_ffn_out=w_ffn_out[l].astype(BF),
                 ln2_g=ln2_g[l][None], ln2_b=ln2_b[l][None])
        hp, st_p, ret, lf = _layer(hp, bp, tp, None, ret_zero, w, alpha, l, depth, st_p)
        ret_p.append(ret)
        lf_p.append(lf)
        past = (cache_sb_k[l], cache_sb_v[l], cache_fox_k[l], cache_fox_v[l], cache_fox_logf[l],
                cache_dsa_k[l], cache_dsa_v[l], cache_dsa_kidx[l])
        hs, st_s, ret, lf = _layer(hs, bs, ts, past, state_ret[l], w, alpha, l, depth, st_s)
        ret_s.append(ret)
        lf_s.append(lf)
    return ((hp.reshape(bp, tp, d), hs.reshape(bs, ts, d))
            + _group_outputs(st_p, ret_p, lf_p) + _group_outputs(st_s, ret_s, lf_s))
```

```python
import functools
import math

import numpy as np
import jax
import jax.numpy as jnp
from jax import lax
from jax.experimental import pallas as pl
from jax.experimental.pallas import tpu as pltpu

HEAD_DIM = 64
N_HEADS = 4
BRANCH_WIDTH = N_HEADS * HEAD_DIM
CHUNK_SHIFT = 6
DSA_TOP_K = 256
ROPE_BASE = 10000.0
LN_EPS = 1e-5
QK_SCALE = HEAD_DIM ** -0.5
IDX_HEAD_SCALE = N_HEADS ** -0.5
MASK_VALUE = -1e30
F32_EXP_UNDERFLOW = -105.0
BOUND_SLACK = 1.0
INT_MIN = -2 ** 31

V7X_VMEM_LIMIT_BYTES = 56 * 1024 * 1024
LANES = 128
HEADS_PER_COL = LANES // HEAD_DIM
SUBLANES = 8
WORD_BITS = 32
GROUP_KEYS = SUBLANES * WORD_BITS
INPROJ_ROWS = 512
QUERY_TILE = 512
RETENTION_CHUNK = 512
MERGE_ROWS = 512
FFN_ROWS = 512

BF = jnp.bfloat16
F32 = jnp.float32


def _dot(a, b):
    return jnp.dot(a, b, preferred_element_type=F32)


def _dot_nt(a, b):
    return lax.dot_general(a, b, (((1,), (1,)), ((), ())), preferred_element_type=F32)


def _dot_tn(a, b):
    return lax.dot_general(a, b, (((0,), (0,)), ((), ())), preferred_element_type=F32)


def _params(*sem):
    return pltpu.CompilerParams(dimension_semantics=sem, vmem_limit_bytes=V7X_VMEM_LIMIT_BYTES)


def _const_spec(shape):
    nd = len(shape)
    return pl.BlockSpec(shape, lambda *_: (0,) * nd)


def _layer_norm(x, g, b):
    xc = x - jnp.mean(x, axis=-1, keepdims=True)
    var = jnp.mean(xc * xc, axis=-1, keepdims=True)
    return xc * lax.rsqrt(var + LN_EPS) * g + b


def _row_tile(m, want):
    t = min(m, want)
    assert m % t == 0
    return t


def _col(c):
    return slice(c * LANES, (c + 1) * LANES)


_NARROW_SRC = ('dsa_k', 'dsa_v', 'idx_k', 'fox_f')
_ROW_OUTS = (
    ('sb_k', ('sb_k',), F32, 0, BRANCH_WIDTH, True), ('sb_k_bf', ('sb_k',), BF, 0, BRANCH_WIDTH, False),
    ('sb_v', ('sb_v',), F32, 0, BRANCH_WIDTH, True),
    ('fox_k', ('fox_k',), F32, 0, BRANCH_WIDTH, True), ('fox_k_bf', ('fox_k',), BF, 0, BRANCH_WIDTH, False),
    ('fox_v', ('fox_v',), F32, 0, BRANCH_WIDTH, True),
    ('dsa_k', _NARROW_SRC, F32, 0, HEAD_DIM, True), ('dsa_v', _NARROW_SRC, F32, HEAD_DIM, HEAD_DIM, True),
    ('idx_k', _NARROW_SRC, F32, 2 * HEAD_DIM, HEAD_DIM, True), ('fox_f', _NARROW_SRC, F32, 3 * HEAD_DIM, N_HEADS, True),
    ('dsa_kk_bf', ('dsa_k', 'dsa_k', 'idx_k', 'idx_k'), BF, 0, 4 * HEAD_DIM, False),
)
_COL_OUTS = (
    ('sb_q_t', ('sb_q',), BF), ('fox_q_t', ('fox_q',), BF), ('dsa_q_t', ('dsa_q',), BF),
    ('idx_q_t', ('idx_q',), BF), ('sb_v_t', ('sb_v',), BF), ('fox_v_t', ('fox_v',), BF),
    ('ret_q_t', ('ret_q',), F32), ('ret_k_t', ('ret_k',), F32), ('ret_v_t', ('ret_v',), F32),
    ('ret_g_t', ('ret_g',), F32), ('dsa_v_t', ('dsa_v',), BF), ('idx_w_t', ('idx_w',), F32),
)
BF16_ROWS_PER_VREG = 16


def _inproj_plan(d):
    widths = dict(_in_layout(d))

    def spans(outs, multiple):
        span_of, off = {}, 0
        for out in outs:
            srcs = out[1]
            if srcs not in span_of:
                w = -(-sum(widths[s] for s in srcs) // multiple) * multiple
                span_of[srcs] = (off, w)
                off += w
        return span_of, off

    row_spans, _ = spans(_ROW_OUTS, LANES)
    col_spans, n_col = spans(_COL_OUTS, BF16_ROWS_PER_VREG)
    return row_spans, col_spans, -(-n_col // LANES) * LANES


def _inproj_weights(w_in):
    d = w_in.shape[0]
    w_bf = w_in.astype(BF)
    pieces, off = {}, 0
    for name, width in _in_layout(d):
        piece = w_bf[:, off:off + width]
        scale = _FOLDED_SCALE.get(name)
        pieces[name] = piece if scale is None else piece * jnp.asarray(scale, BF)
        off += width
    assert off == w_in.shape[1]
    row_spans, col_spans, n_col = _inproj_plan(d)

    def block(srcs, width):
        w = jnp.concatenate([pieces[s] for s in srcs], axis=1)
        return jnp.pad(w, ((0, 0), (0, width - w.shape[1])))

    w_row = jnp.concatenate([block(srcs, w) for srcs, (_, w) in row_spans.items()], axis=1)
    w_col = jnp.concatenate([block(srcs, w) for srcs, (_, w) in col_spans.items()], axis=1)
    w_col = jnp.pad(w_col, ((0, 0), (0, n_col - w_col.shape[1])))
    return w_row.astype(BF), w_col.T.astype(BF), pieces['merge_gate'].astype(BF)


def _inproj_body(x_ref, wr_ref, wc_ref, *refs, row_spans, col_spans, n_alias):
    o_refs = refs[n_alias:]
    xb = x_ref[0].astype(BF)
    done = {}
    for o_ref, (_, srcs, _, lane, width, _) in zip(o_refs, _ROW_OUTS):
        if srcs not in done:
            off, w = row_spans[srcs]
            done[srcs] = _dot(xb, wr_ref[:, off:off + w])
        o_ref[...] = done[srcs][:, lane:lane + width].astype(o_ref.dtype).reshape(o_ref.shape)
    for o_ref, (_, srcs, _) in zip(o_refs[len(_ROW_OUTS):], _COL_OUTS):
        off, w = col_spans[srcs]
        o_ref[0] = _dot_nt(wc_ref[off:off + w, :], xb).astype(o_ref.dtype)


def _inproj(x, w_row, w_col, layer, depth, states):
    b, t, d = x.shape
    tm = _row_tile(t, INPROJ_ROWS)
    row_spans, col_spans, _ = _inproj_plan(d)
    out_shape, out_specs, state_names = [], [], []
    for name, _, dt, _, w, is_state in _ROW_OUTS:
        if is_state:
            state_names.append(name)
            out_shape.append(jax.ShapeDtypeStruct((depth, b, t, w), dt))
            out_specs.append(pl.BlockSpec((1, 1, tm, w), lambda bi, i: (layer, bi, i, 0)))
        else:
            out_shape.append(jax.ShapeDtypeStruct((b, t, w), dt))
            out_specs.append(pl.BlockSpec((1, tm, w), lambda bi, i: (bi, i, 0)))
    for _, srcs, dt in _COL_OUTS:
        w = col_spans[srcs][1]
        out_shape.append(jax.ShapeDtypeStruct((b, w, t), dt))
        out_specs.append(pl.BlockSpec((1, w, tm), lambda bi, i: (bi, 0, i)))
    prev = [] if states is None else [states[n] for n in state_names]
    names = [o[0] for o in _ROW_OUTS] + [o[0] for o in _COL_OUTS]
    aliases = {3 + j: names.index(n) for j, n in enumerate(state_names)} if prev else {}
    outs = pl.pallas_call(
        functools.partial(_inproj_body, row_spans=row_spans, col_spans=col_spans, n_alias=len(prev)),
        out_shape=out_shape,
        grid=(b, t // tm),
        in_specs=[pl.BlockSpec((1, tm, d), lambda bi, i: (bi, i, 0)),
                  _const_spec(w_row.shape), _const_spec(w_col.shape)]
                 + [pl.BlockSpec(memory_space=pl.ANY)] * len(prev),
        out_specs=out_specs,
        input_output_aliases=aliases,
        compiler_params=_params("parallel", "parallel"),
        name="inproj",
    )(x, w_row, w_col, *prev)
    return dict(zip(names, outs))


def _block_counts(q0, tk, last_key):
    return lax.div(q0, tk), lax.div(last_key, tk) + 1


def _head_queries(qt_ref):
    low = lax.broadcasted_iota(jnp.int32, (LANES, 1), 0) < HEAD_DIM
    out = []
    for h in range(N_HEADS):
        qc = qt_ref[0, _col(h // HEADS_PER_COL), :]
        keep = low if h % HEADS_PER_COL == 0 else jnp.logical_not(low)
        out.append(jnp.where(keep, qc, jnp.zeros_like(qc)))
    return out


def _head_rows(h):
    return slice(h * HEAD_DIM, (h + 1) * HEAD_DIM)


def _key_minus_query(tk, tq):
    return (lax.broadcasted_iota(jnp.int32, (tk, tq), 0) - lax.broadcasted_iota(jnp.int32, (tk, tq), 1))


def _qt_spec(w, tq):
    return pl.BlockSpec((1, w, tq), lambda bi, qi: (bi, 0, qi))


def _whole_spec(rows, cols):
    return pl.BlockSpec((1, rows, cols), lambda bi, qi: (bi, 0, 0))


def _sb_body(qt_ref, k_ref, vt_ref, o_ref, *, p_len, tq, tk):
    q0 = p_len + pl.program_id(1) * tq
    qm = _head_queries(qt_ref)
    diff = _key_minus_query(tk, tq)
    later = (lax.broadcasted_iota(jnp.int32, (tk, 2 * tk), 1) & (tk - 1)) > lax.broadcasted_iota(
        jnp.int32, (tk, 2 * tk), 0)
    minus_later = jnp.where(later, -1.0, 0.0).astype(BF)
    n_full, n_all = _block_counts(q0, tk, jnp.maximum(q0 + tq - 2, 0))

    def step(kb, carry, masked):
        laters, accs = carry
        s0 = pl.multiple_of(kb * tk, tk)
        if masked:
            earlier = diff < (q0 - s0)
        zs = [_dot(k_ref[0, pl.ds(s0, tk), _col(h // HEADS_PER_COL)], qm[h]) for h in range(N_HEADS)]
        new_laters, log_bs, afters = [], [], []
        for h in range(N_HEADS):
            z = zs[h]
            minus_abs = pltpu.bitcast(pltpu.bitcast(z, jnp.int32) | jnp.int32(INT_MIN), F32)
            softplus = jnp.maximum(z, 0.0) + jnp.log(1.0 + jnp.exp(minus_abs))
            log_bs.append(z - softplus)
            if masked:
                softplus = jnp.where(earlier, softplus, 0.0)
            hi = softplus.astype(BF)
            lo = (softplus - hi.astype(F32)).astype(BF)
            after = _dot(minus_later, jnp.concatenate([hi, lo], axis=0)) + laters[h]
            afters.append(after)
            new_laters.append(after[0:1, :] - softplus[0:1, :])
        new_accs = []
        for h in range(N_HEADS):
            w = jnp.exp(log_bs[h] + afters[h])
            if masked:
                w = jnp.where(earlier, w, 0.0)
            new_accs.append(accs[h] + _dot(vt_ref[0, _head_rows(h), pl.ds(s0, tk)], w.astype(BF)))
        return tuple(new_laters), tuple(new_accs)

    carry = (tuple(jnp.zeros((1, tq), F32) for _ in range(N_HEADS)),
             tuple(jnp.zeros((HEAD_DIM, tq), F32) for _ in range(N_HEADS)))
    carry = lax.fori_loop(0, n_all - n_full, lambda i, c: step(n_all - 1 - i, c, True), carry)

    def reachable(laters):
        return jnp.max(functools.reduce(jnp.maximum, laters)) > F32_EXP_UNDERFLOW

    def earlier_block(state):
        i, _, c = state
        c = step(n_full - 1 - i, c, False)
        return i + 1, reachable(c[0]), c

    _, _, carry = lax.while_loop(lambda state: (state[0] < n_full) & state[1], earlier_block,
                                 (jnp.int32(0), reachable(carry[0]), carry))
    for h in range(N_HEADS):
        o_ref[0, _head_rows(h), :] = carry[1][h].astype(o_ref.dtype)


def _sb_attention(qt, k, vt, p_len, tq, tk):
    b, w, t = qt.shape
    lp = k.shape[1]
    assert tk & (tk - 1) == 0
    return pl.pallas_call(
        functools.partial(_sb_body, p_len=p_len, tq=tq, tk=tk),
        out_shape=jax.ShapeDtypeStruct((b, w, t), BF),
        grid=(b, t // tq),
        in_specs=[_qt_spec(w, tq), _whole_spec(lp, w), _whole_spec(w, lp)],
        out_specs=_qt_spec(w, tq),
        compiler_params=_params("parallel", "arbitrary"),
        name="sb_attention",
    )(qt, k, vt)


def _online_softmax_step(logits, m, l):
    m_new = jnp.maximum(m, jnp.max(logits, axis=0, keepdims=True))
    alpha = jnp.exp(m - m_new)
    p = jnp.exp(logits - m_new)
    return m_new, alpha, alpha * l + jnp.sum(p, axis=0, keepdims=True), p


def _fox_body(qt_ref, k_ref, vt_ref, c_ref, kmax_ref, ctop_ref, o_ref, *, p_len, tq, tk, n_blocks):
    bi = pl.program_id(0)
    q0 = p_len + pl.program_id(1) * tq
    qm = _head_queries(qt_ref)
    diff = _key_minus_query(tk, tq)
    n_full, n_all = _block_counts(q0, tk, q0 + tq - 1)
    dot_bound = [_dot(kmax_ref[0, :, _col(h // HEADS_PER_COL)], jnp.abs(qm[h]))[0:1, :] for h in range(N_HEADS)]

    def key_bias(h, s0):
        c = c_ref[0, h, pl.ds(s0, tk), :]
        return c[:, :tq] if tq <= LANES else jnp.concatenate([c] * (tq // LANES), axis=1)

    def step(kb, carry, masked):
        ms, ls, accs = carry
        s0 = pl.multiple_of(kb * tk, tk)
        if masked:
            visible = diff <= (q0 - s0)
        new_ms, new_ls, new_accs = [], [], []
        raw = [_dot(k_ref[0, pl.ds(s0, tk), _col(h // HEADS_PER_COL)], qm[h]) for h in range(N_HEADS)]
        for h in range(N_HEADS):
            logits = raw[h] - key_bias(h, s0)
            if masked:
                logits = jnp.where(visible, logits, MASK_VALUE)
            m_new, alpha, l_new, p = _online_softmax_step(logits, ms[h], ls[h])
            new_ms.append(m_new)
            new_ls.append(l_new)
            new_accs.append(alpha * accs[h] + _dot(vt_ref[0, _head_rows(h), pl.ds(s0, tk)], p.astype(BF)))
        return tuple(new_ms), tuple(new_ls), tuple(new_accs)

    carry = (tuple(jnp.full((1, tq), -jnp.inf, F32) for _ in range(N_HEADS)),
             tuple(jnp.zeros((1, tq), F32) for _ in range(N_HEADS)),
             tuple(jnp.zeros((HEAD_DIM, tq), F32) for _ in range(N_HEADS)))
    carry = lax.fori_loop(0, n_all - n_full, lambda i, c: step(n_all - 1 - i, c, True), carry)

    def reachable(kb, ms):
        j = jnp.maximum(kb, 0)
        gaps = [dot_bound[h] + ctop_ref[bi, h * n_blocks + j] - ms[h] for h in range(N_HEADS)]
        return jnp.max(functools.reduce(jnp.maximum, gaps)) > F32_EXP_UNDERFLOW - BOUND_SLACK

    def older_block(state):
        i, _, c = state
        c = step(n_full - 1 - i, c, False)
        return i + 1, reachable(n_full - 2 - i, c[0]), c

    _, _, carry = lax.while_loop(lambda state: (state[0] < n_full) & state[1], older_block,
                                 (jnp.int32(0), reachable(n_full - 1, carry[0]), carry))
    _, ls, accs = carry
    for h in range(N_HEADS):
        o_ref[0, _head_rows(h), :] = (accs[h] / ls[h]).astype(o_ref.dtype)


def _fox_attention(qt, k, vt, c_lanes, kmax, ctop, p_len, tq, tk):
    b, w, t = qt.shape
    lp = k.shape[1]
    assert tq <= LANES or tq % LANES == 0
    return pl.pallas_call(
        functools.partial(_fox_body, p_len=p_len, tq=tq, tk=tk, n_blocks=lp // tk),
        out_shape=jax.ShapeDtypeStruct((b, w, t), BF),
        grid=(b, t // tq),
        in_specs=[_qt_spec(w, tq), _whole_spec(lp, w), _whole_spec(w, lp),
                  pl.BlockSpec((1, N_HEADS, lp, LANES), lambda bi, qi: (bi, 0, 0, 0)),
                  _whole_spec(SUBLANES, w), pl.BlockSpec(memory_space=pltpu.SMEM)],
        out_specs=_qt_spec(w, tq),
        compiler_params=_params("parallel", "arbitrary"),
        name="fox_attention",
    )(qt, k, vt, c_lanes, kmax, ctop)


def _float_key(bits):
    return jnp.where(bits < 0, jnp.int32(INT_MIN) - bits, bits)


_BUTTERFLY = ((16, 0x0000FFFF), (8, 0x00FF00FF), (4, 0x0F0F0F0F), (2, 0x33333333), (1, 0x55555555))


def _bit_planes(words):
    a = list(words)
    for shift, mask in _BUTTERFLY:
        for k in range(WORD_BITS):
            if k & shift == 0:
                t = (a[k] ^ lax.shift_right_logical(a[k + shift], shift)) & mask
                a[k] = a[k] ^ t
                a[k + shift] = a[k + shift] ^ lax.shift_left(t, shift)
    return a


def _dsa_body(qt_ref, qit_ref, wit_ref, k_ref, vt_ref, ki_ref, o_ref, keys_ref, planes_ref, active_ref, *,
              p_len, n_keys, tq, tk, top_k):
    q0 = p_len + pl.program_id(1) * tq
    qpos = q0 + lax.broadcasted_iota(jnp.int32, (1, tq), 1)
    limit = jnp.minimum(((qpos >> CHUNK_SHIFT) + 1) << CHUNK_SHIFT, n_keys)
    last_limit = jnp.minimum((((q0 + tq - 1) >> CHUNK_SHIFT) + 1) << CHUNK_SHIFT, n_keys)
    n_blk = lax.div(last_limit - 1, tk) + 1
    key_row = lax.broadcasted_iota(jnp.int32, (tk, tq), 0)

    wit = wit_ref[0]
    qim = _head_queries(qit_ref)

    def score_step(kb, _):
        s0 = pl.multiple_of(kb * tk, tk)
        ki = ki_ref[0, pl.ds(s0, tk), :]
        score = jnp.zeros((tk, tq), F32)
        for h in range(N_HEADS):
            score = score + wit[h:h + 1, :] * jnp.maximum(_dot(ki, qim[h]), 0.0)
        score = jnp.where(key_row < limit - s0, score, -jnp.inf)
        key = _float_key(pltpu.bitcast(score, jnp.int32))
        keys_ref[pl.ds(s0, tk), :] = key
        unsigned = key ^ jnp.int32(INT_MIN)
        for g in range(tk // GROUP_KEYS):
            words = [unsigned[g * GROUP_KEYS + SUBLANES * j:g * GROUP_KEYS + SUBLANES * (j + 1), :]
                     for j in range(WORD_BITS)]
            r0 = pl.multiple_of(kb * plane_rows + g * SUBLANES, SUBLANES)
            for i, plane in enumerate(_bit_planes(words)):
                planes_ref[i, pl.ds(r0, SUBLANES), :] = plane
        r0 = pl.multiple_of(kb * plane_rows, plane_rows)
        active_ref[pl.ds(r0, plane_rows), :] = jnp.full((plane_rows, tq), -1, jnp.int32)
        return 0

    plane_rows = tk // WORD_BITS
    lax.fori_loop(0, n_blk, score_step, 0)

    def ones_in(i, narrow):
        def body(kb, acc):
            r0 = pl.multiple_of(kb * plane_rows, plane_rows)
            live = active_ref[pl.ds(r0, plane_rows), :]
            if narrow is not None:
                live = live & (planes_ref[i - 1, pl.ds(r0, plane_rows), :] ^ narrow)
                active_ref[pl.ds(r0, plane_rows), :] = live
            return acc + lax.population_count(live & planes_ref[i, pl.ds(r0, plane_rows), :])
        acc = lax.fori_loop(0, n_blk, body, jnp.zeros((plane_rows, tq), jnp.int32))
        return jnp.sum(acc.astype(F32), axis=0, keepdims=True)

    def choose(i, n_ones, wanted, thr_bits):
        take = n_ones >= wanted
        bit = lax.shift_left(jnp.int32(1), WORD_BITS - 1 - i)
        return (jnp.where(take, wanted, wanted - n_ones), thr_bits | jnp.where(take, bit, 0),
                jnp.where(take, 0, -1))

    wanted, thr_bits, narrow = choose(0, ones_in(0, None), jnp.full((1, tq), top_k, F32),
                                      jnp.zeros((1, tq), jnp.int32))

    def bit_step(i, carry):
        wanted, thr_bits, narrow = carry
        return choose(i, ones_in(i, narrow), wanted, thr_bits)

    wanted, thr_bits, _ = lax.fori_loop(1, WORD_BITS, bit_step, (wanted, thr_bits, narrow))
    thr = thr_bits ^ jnp.int32(INT_MIN)
    n_tie_wanted = wanted

    earlier_keys = (lax.broadcasted_iota(jnp.int32, (tk, tk), 1)
                    < lax.broadcasted_iota(jnp.int32, (tk, tk), 0)).astype(BF)
    qm = _head_queries(qt_ref)

    def attend_step(kb, carry):
        ties_seen, ms, ls, accs = carry
        s0 = pl.multiple_of(kb * tk, tk)
        key = keys_ref[pl.ds(s0, tk), :]
        tie = jnp.where(key == thr, 1.0, 0.0)
        tie_rank = _dot(earlier_keys, tie.astype(BF)) + ties_seen
        take = jnp.where(key > thr, 1.0, jnp.where(tie_rank < n_tie_wanted, tie, 0.0))
        selected = jnp.where(key_row < limit - s0, take, 0.0) > 0.0
        k = k_ref[0, pl.ds(s0, tk), :]
        vt = vt_ref[0, :, pl.ds(s0, tk)]
        new_ms, new_ls, new_accs = [], [], []
        raw = [_dot(k, qm[h]) for h in range(N_HEADS)]
        for h in range(N_HEADS):
            logits = jnp.where(selected, raw[h], MASK_VALUE)
            m_new, alpha, l_new, p = _online_softmax_step(logits, ms[h], ls[h])
            new_ms.append(m_new)
            new_ls.append(l_new)
            new_accs.append(alpha * accs[h] + _dot(vt, p.astype(BF)))
        ties_seen = ties_seen + jnp.sum(tie, axis=0, keepdims=True)
        return ties_seen, tuple(new_ms), tuple(new_ls), tuple(new_accs)

    carry = (jnp.zeros((1, tq), F32),
             tuple(jnp.full((1, tq), -jnp.inf, F32) for _ in range(N_HEADS)),
             tuple(jnp.zeros((1, tq), F32) for _ in range(N_HEADS)),
             tuple(jnp.zeros((HEAD_DIM, tq), F32) for _ in range(N_HEADS)))
    _, _, ls, accs = lax.fori_loop(0, n_blk, attend_step, carry)
    for h in range(N_HEADS):
        o_ref[0, _head_rows(h), :] = (accs[h] / ls[h]).astype(o_ref.dtype)


def _dsa_attention(qt, qit, wit, kk, vt, p_len, n_keys, tq, tk, top_k):
    b, w, t = qt.shape
    lp = kk.shape[1]
    assert tk % GROUP_KEYS == 0
    return pl.pallas_call(
        functools.partial(_dsa_body, p_len=p_len, n_keys=n_keys, tq=tq, tk=tk, top_k=top_k),
        out_shape=jax.ShapeDtypeStruct((b, w, t), BF),
        grid=(b, t // tq),
        in_specs=[_qt_spec(w, tq), _qt_spec(w, tq), _qt_spec(wit.shape[1], tq),
                  pl.BlockSpec((1, lp, LANES), lambda bi, qi: (bi, 0, 0)), _whole_spec(HEAD_DIM, lp),
                  pl.BlockSpec((1, lp, LANES), lambda bi, qi: (bi, 0, 1))],
        out_specs=_qt_spec(w, tq),
        scratch_shapes=[pltpu.VMEM((lp, tq), jnp.int32),
                        pltpu.VMEM((WORD_BITS, lp // WORD_BITS, tq), jnp.int32),
                        pltpu.VMEM((lp // WORD_BITS, tq), jnp.int32)],
        compiler_params=_params("parallel", "arbitrary"),
        name="dsa_attention",
    )(qt, qit, wit, kk, vt, kk)


def _ret_body(q_ref, k_ref, v_ref, g_ref, s0_ref, cos_ref, sin_ref, dec_ref, qd_ref, kd_ref, sd_ref,
              o_ref, so_ref, state_ref):
    c = pl.program_id(1)

    @pl.when(c == 0)
    def _():
        state_ref[...] = s0_ref[0]

    cos, sin = cos_ref[...], sin_ref[...]
    half = HEAD_DIM // 2

    def rotary(x):
        x1, x2 = x[:half], x[half:]
        return jnp.concatenate([x1 * cos - x2 * sin, x2 * cos + x1 * sin], axis=0)

    heads = range(N_HEADS)
    qb = [rotary(q_ref[0, _head_rows(h), :]).astype(BF) for h in heads]
    k = [rotary(k_ref[0, _head_rows(h), :]) for h in heads]
    vb = [v_ref[0, _head_rows(h), :].astype(BF) for h in heads]
    scores_t = [_dot_tn(k[h].astype(BF), qb[h]) * dec_ref[h] for h in heads]
    carried = [_dot(state_ref[h].astype(BF), qb[h]) * qd_ref[h] for h in heads]
    outs = [_dot(vb[h], scores_t[h].astype(BF)) + carried[h] for h in heads]
    for h in heads:
        state_ref[h] = sd_ref[h] * state_ref[h] + _dot_nt(vb[h], (k[h] * kd_ref[h]).astype(BF))
    for h in heads:
        o = outs[h]
        oc = o - jnp.mean(o, axis=0, keepdims=True)
        on = oc * lax.rsqrt(jnp.mean(oc * oc, axis=0, keepdims=True) + LN_EPS)
        g = g_ref[0, _head_rows(h), :]
        o_ref[0, _head_rows(h), :] = (on * (g * jax.nn.sigmoid(g))).astype(o_ref.dtype)

    @pl.when(c == pl.num_programs(1) - 1)
    def _():
        so_ref[0] = state_ref[...]


def _retention(qt, kt, vt, gt, state0_t, pos, c):
    b, w, t = qt.shape
    h = w // HEAD_DIM
    half = HEAD_DIM // 2
    inv_freq = ROPE_BASE ** (-jnp.arange(half, dtype=F32) / half)
    ang = inv_freq[:, None] * pos.astype(F32)[None, :]
    log_gamma = np.log(1.0 - 2.0 ** (-5.0 - np.arange(h, dtype=np.float64)))
    n = np.arange(c, dtype=np.float64)
    rel = n[None, :] - n[:, None]
    decay_t = np.where(rel >= 0, np.exp(np.maximum(rel, 0.0)[None] * log_gamma[:, None, None]), 0.0)
    q_decay = np.exp((n[None, :] + 1.0) * log_gamma[:, None])[:, None, :]
    k_decay = np.exp((c - 1.0 - n)[None, :] * log_gamma[:, None])[:, None, :]
    s_decay = np.exp(c * log_gamma)[:, None, None]
    tables = [jnp.asarray(a, F32) for a in (decay_t, q_decay, k_decay, s_decay)]
    assert h == N_HEADS
    x_spec = pl.BlockSpec((1, w, c), lambda bi, ci: (bi, 0, ci))
    s_spec = pl.BlockSpec((1, h, HEAD_DIM, HEAD_DIM), lambda bi, ci: (bi, 0, 0, 0))
    rope_spec = pl.BlockSpec((half, c), lambda bi, ci: (0, ci))
    return pl.pallas_call(
        _ret_body,
        out_shape=(jax.ShapeDtypeStruct((b, w, t), BF),
                   jax.ShapeDtypeStruct((b, h, HEAD_DIM, HEAD_DIM), F32)),
        grid=(b, t // c),
        in_specs=[x_spec, x_spec, x_spec, x_spec, s_spec, rope_spec, rope_spec] + [_const_spec(a.shape) for a in tables],
        out_specs=(x_spec, s_spec),
        scratch_shapes=[pltpu.VMEM((h, HEAD_DIM, HEAD_DIM), F32)],
        compiler_params=_params("parallel", "arbitrary"),
        name="retention",
    )(qt, kt, vt, gt, state0_t, jnp.cos(ang), jnp.sin(ang), *tables)


def _merge_body(h_ref, y0_ref, y1_ref, y2_ref, y3_ref, wg_ref, wb_ref, wo_ref, g_ref, b_ref, o_ref, *, alpha):
    nb, tm, d = h_ref.shape
    h = h_ref[...].reshape(nb * tm, d)
    hb = h.astype(BF)
    merged = jnp.zeros(h.shape, F32)
    for i, y_ref in enumerate((y0_ref, y1_ref, y2_ref, y3_ref)):
        gate = jax.nn.sigmoid(_dot(hb, wg_ref[:, i * d:(i + 1) * d]))
        branch = jnp.concatenate([_dot_tn(y_ref[j], wb_ref[i]) for j in range(nb)], axis=0)
        merged = merged + gate * branch
    r = alpha * h + _dot(merged.astype(BF), wo_ref[...])
    o_ref[...] = _layer_norm(r, g_ref[...], b_ref[...]).reshape(nb, tm, d)


def _merge(h, ys_t, w_gate, w_branch, w_out, ln_g, ln_b, alpha):
    b, t, d = h.shape
    tm = _row_tile(t, MERGE_ROWS)
    nb = math.gcd(b, max(1, MERGE_ROWS // tm))
    row = pl.BlockSpec((nb, tm, d), lambda bi, i: (bi, i, 0))
    col = pl.BlockSpec((nb, BRANCH_WIDTH, tm), lambda bi, i: (bi, 0, i))
    return pl.pallas_call(
        functools.partial(_merge_body, alpha=alpha),
        out_shape=jax.ShapeDtypeStruct((b, t, d), F32),
        grid=(b // nb, t // tm),
        in_specs=[row] + [col] * 4
                 + [_const_spec(w_gate.shape), _const_spec(w_branch.shape), _const_spec(w_out.shape),
                    _const_spec((1, d)), _const_spec((1, d))],
        out_specs=row,
        compiler_params=_params("parallel", "parallel"),
        name="merge",
    )(h, *ys_t, w_gate, w_branch, w_out, ln_g, ln_b)


def _ffn_body(h_ref, wi_ref, wo_ref, g_ref, b_ref, o_ref, *, alpha, f_chunk):
    h = h_ref[...]
    hb = h.astype(BF)
    f = wo_ref.shape[0]
    acc = jnp.zeros(h.shape, F32)
    for c in range(0, f, f_chunk):
        a = _dot(hb, wi_ref[:, c:c + f_chunk])
        u = _dot(hb, wi_ref[:, f + c:f + c + f_chunk])
        acc = acc + _dot((a * jax.nn.sigmoid(a) * u).astype(BF), wo_ref[c:c + f_chunk, :])
    o_ref[...] = _layer_norm(alpha * h + acc, g_ref[...], b_ref[...])


def _ffn(h, w_in, w_out, ln_g, ln_b, alpha):
    m, d = h.shape
    f = w_out.shape[0]
    tm = _row_tile(m, FFN_ROWS)
    f_chunk = f // 2 if (f // 2) % LANES == 0 else f
    row = pl.BlockSpec((tm, d), lambda i: (i, 0))
    return pl.pallas_call(
        functools.partial(_ffn_body, alpha=alpha, f_chunk=f_chunk),
        out_shape=jax.ShapeDtypeStruct((m, d), F32),
        grid=(m // tm,),
        in_specs=[row, _const_spec(w_in.shape), _const_spec(w_out.shape),
                  _const_spec((1, d)), _const_spec((1, d))],
        out_specs=row,
        compiler_params=_params("parallel"),
        name="ffn",
    )(h, w_in, w_out, ln_g, ln_b)


def _in_layout(d):
    w = BRANCH_WIDTH
    return (('sb_q', w), ('sb_k', w), ('sb_v', w), ('ret_q', w), ('ret_k', w), ('ret_v', w), ('ret_g', w),
            ('fox_q', w), ('fox_k', w), ('fox_v', w), ('fox_f', N_HEADS),
            ('dsa_q', w), ('dsa_k', HEAD_DIM), ('dsa_v', HEAD_DIM),
            ('idx_q', w), ('idx_k', HEAD_DIM), ('idx_w', N_HEADS), ('merge_gate', 4 * d))


_FOLDED_SCALE = dict(sb_q=QK_SCALE, fox_q=QK_SCALE, dsa_q=QK_SCALE, idx_q=QK_SCALE, ret_k=QK_SCALE,
                     idx_w=IDX_HEAD_SCALE)


def _swap(a):
    return jnp.swapaxes(a, -1, -2)


def _key_tiles(t, n_keys):
    tq = min(t, QUERY_TILE)
    tiles = dict(sb=256, fox=512, dsa=512)
    padded = {name: -(-n_keys // tk) * tk for name, tk in tiles.items()}
    return tq, tiles, padded


def _layer(h, b, t, past, ret_state, w, alpha, layer, depth, states):
    m, d = h.shape
    p_len = 0 if past is None else past[0].shape[1]
    n_keys = p_len + t
    tq, tk, lp = _key_tiles(t, n_keys)
    p = _inproj(h.reshape(b, t, d), w['w_row'], w['w_col'], layer, depth, states)
    states = {name: p[name] for name, _, _, _, _, is_state in _ROW_OUTS if is_state}

    old = (None,) * 8 if past is None else past
    sb_k0, sb_v0, fox_k0, fox_v0, fox_lf0, dsa_k0, dsa_v0, dsa_ki0 = old

    def rows_with_past(new_bf, olds, lp_):
        if past is not None:
            flat = [o.reshape(o.shape[0], o.shape[1], -1).astype(BF) for o in olds]
            new_bf = jnp.concatenate([jnp.concatenate(flat, axis=2), new_bf], axis=1)
        return jnp.pad(new_bf, ((0, 0), (0, lp_ - new_bf.shape[1]), (0, 0)))

    def cols_with_past(new_t, old, lp_):
        if old is not None:
            new_t = jnp.concatenate([_swap(old.reshape(old.shape[0], old.shape[1], -1).astype(BF)), new_t], axis=2)
        return jnp.pad(new_t, ((0, 0), (0, 0), (0, lp_ - new_t.shape[2])))

    y_sb = _sb_attention(p['sb_q_t'], rows_with_past(p['sb_k_bf'], [sb_k0], lp['sb']),
                         cols_with_past(p['sb_v_t'], sb_v0, lp['sb']), p_len, tq, tk['sb'])

    pos = p_len + jnp.arange(t, dtype=jnp.int32)
    y_ret, ret_state_t = _retention(p['ret_q_t'], p['ret_k_t'], p['ret_v_t'], p['ret_g_t'], _swap(ret_state),
                                    pos, min(t, RETENTION_CHUNK))

    fox_lf = jax.nn.log_sigmoid(states['fox_f'][layer] + w['b_forget'])
    lf_all = fox_lf if fox_lf0 is None else jnp.concatenate([fox_lf0, fox_lf], axis=1)
    cum = jnp.pad(jnp.cumsum(lf_all, axis=1), ((0, 0), (0, lp['fox'] - n_keys), (0, 0)))
    c_lanes = jnp.broadcast_to(_swap(cum)[..., None], (b, N_HEADS, lp['fox'], LANES))
    ctop = _swap(lax.cummax(-cum, axis=1)[:, tk['fox'] - 1::tk['fox']]).reshape(b, -1)
    fox_keys = rows_with_past(p['fox_k_bf'], [fox_k0], lp['fox'])
    kmax = jnp.broadcast_to(jnp.max(jnp.abs(fox_keys), axis=1, keepdims=True), (b, SUBLANES, BRANCH_WIDTH))
    y_fox = _fox_attention(p['fox_q_t'], fox_keys, cols_with_past(p['fox_v_t'], fox_v0, lp['fox']), c_lanes,
                           kmax, ctop, p_len, tq, tk['fox'])

    top_k = min(DSA_TOP_K, n_keys // 4)
    kk_old = [dsa_k0, dsa_k0, dsa_ki0, dsa_ki0]
    y_dsa = _dsa_attention(p['dsa_q_t'], p['idx_q_t'], p['idx_w_t'], rows_with_past(p['dsa_kk_bf'], kk_old, lp['dsa']),
                           cols_with_past(p['dsa_v_t'], dsa_v0, lp['dsa']), p_len, n_keys, tq, tk['dsa'], top_k)

    h = _merge(h.reshape(b, t, d), (y_sb, y_ret, y_fox, y_dsa), w['w_gate'], w['w_branch'], w['w_out'],
               w['ln1_g'], w['ln1_b'], alpha)
    h = _ffn(h.reshape(m, d), w['w_ffn_in'], w['w_ffn_out'], w['ln2_g'], w['ln2_b'], alpha)
    return h, states, _swap(ret_state_t), fox_lf


def _group_outputs(states, ret_states, fox_lfs):
    def heads(a):
        return a.reshape(a.shape[:-1] + (N_HEADS, HEAD_DIM))

    return (heads(states['sb_k']), heads(states['sb_v']), jnp.stack(ret_states), heads(states['fox_k']),
            heads(states['fox_v']), jnp.stack(fox_lfs), states['dsa_k'], states['dsa_v'], states['idx_k'])


def kernel(x_prompt, x_sample, cache_sb_k, cache_sb_v, state_ret, cache_fox_k, cache_fox_v, cache_fox_logf,
           cache_dsa_k, cache_dsa_v, cache_dsa_kidx, w_in, b_forget, w_branch, w_out, ln1_g, ln1_b,
           w_ffn_in, w_ffn_out, ln2_g, ln2_b):
    depth = w_in.shape[0]
    alpha = float((2 * depth) ** 0.25)
    bp, tp, d = x_prompt.shape
    bs, ts, _ = x_sample.shape
    hp = x_prompt.reshape(bp * tp, d)
    hs = x_sample.reshape(bs * ts, d)
    ret_zero = jnp.zeros((bp, N_HEADS, HEAD_DIM, HEAD_DIM), F32)
    st_p, st_s, ret_p, ret_s, lf_p, lf_s = None, None, [], [], [], []
    for l in range(depth):
        w_row, w_col, w_gate = _inproj_weights(w_in[l])
        w = dict(w_row=w_row, w_col=w_col, w_gate=w_gate, b_forget=b_forget[l], w_branch=w_branch[l].astype(BF),
                 w_out=w_out[l].astype(BF), ln1_g=ln1_g[l][None], ln1_b=ln1_b[l][None],
                 w_ffn_in=w_ffn_in[l].astype(BF), w_ffn_out=w_ffn_out[l].astype(BF),
                 ln2_g=ln2_g[l][None], ln2_b=ln2_b[l][None])
        hp, st_p, ret, lf = _layer(hp, bp, tp, None, ret_zero, w, alpha, l, depth, st_p)
        ret_p.append(ret)
        lf_p.append(lf)
        past = (cache_sb_k[l], cache_sb_v[l], cache_fox_k[l], cache_fox_v[l], cache_fox_logf[l],
                cache_dsa_k[l], cache_dsa_v[l], cache_dsa_kidx[l])
        hs, st_s, ret, lf = _layer(hs, bs, ts, past, state_ret[l], w, alpha, l, depth, st_s)
        ret_s.append(ret)
        lf_s.append(lf)
    return ((hp.reshape(bp, tp, d), hs.reshape(bs, ts, d))
            + _group_outputs(st_p, ret_p, lf_p) + _group_outputs(st_s, ret_s, lf_s))
```

```python
import functools
import math

import numpy as np
import jax
import jax.numpy as jnp
from jax import lax
from jax.experimental import pallas as pl
from jax.experimental.pallas import tpu as pltpu

HEAD_DIM = 64
N_HEADS = 4
BRANCH_WIDTH = N_HEADS * HEAD_DIM
CHUNK_SHIFT = 6
DSA_TOP_K = 256
ROPE_BASE = 10000.0
LN_EPS = 1e-5
QK_SCALE = HEAD_DIM ** -0.5
IDX_HEAD_SCALE = N_HEADS ** -0.5
MASK_VALUE = -1e30
F32_EXP_UNDERFLOW = -105.0
BOUND_SLACK = 1.0
INT_MIN = -2 ** 31

V7X_VMEM_LIMIT_BYTES = 56 * 1024 * 1024
LANES = 128
HEADS_PER_COL = LANES // HEAD_DIM
SUBLANES = 8
WORD_BITS = 32
GROUP_KEYS = SUBLANES * WORD_BITS
INPROJ_ROWS = 512
QUERY_TILE = dict(sb=256, fox=512, dsa=512)
RETENTION_CHUNK = 512
MERGE_ROWS = 512
FFN_ROWS = 512

BF = jnp.bfloat16
F32 = jnp.float32


def _dot(a, b):
    return jnp.dot(a, b, preferred_element_type=F32)


def _dot_nt(a, b):
    return lax.dot_general(a, b, (((1,), (1,)), ((), ())), preferred_element_type=F32)


def _dot_tn(a, b):
    return lax.dot_general(a, b, (((0,), (0,)), ((), ())), preferred_element_type=F32)


def _params(*sem):
    return pltpu.CompilerParams(dimension_semantics=sem, vmem_limit_bytes=V7X_VMEM_LIMIT_BYTES)


def _const_spec(shape):
    nd = len(shape)
    return pl.BlockSpec(shape, lambda *_: (0,) * nd)


def _layer_norm(x, g, b):
    xc = x - jnp.mean(x, axis=-1, keepdims=True)
    var = jnp.mean(xc * xc, axis=-1, keepdims=True)
    return xc * lax.rsqrt(var + LN_EPS) * g + b


def _row_tile(m, want):
    t = min(m, want)
    assert m % t == 0
    return t


def _col(c):
    return slice(c * LANES, (c + 1) * LANES)


_NARROW_SRC = ('dsa_k', 'dsa_v', 'idx_k', 'fox_f')
_ROW_OUTS = (
    ('sb_k', ('sb_k',), F32, 0, BRANCH_WIDTH, True), ('sb_k_bf', ('sb_k',), BF, 0, BRANCH_WIDTH, False),
    ('sb_v', ('sb_v',), F32, 0, BRANCH_WIDTH, True),
    ('fox_k', ('fox_k',), F32, 0, BRANCH_WIDTH, True), ('fox_k_bf', ('fox_k',), BF, 0, BRANCH_WIDTH, False),
    ('fox_v', ('fox_v',), F32, 0, BRANCH_WIDTH, True),
    ('dsa_k', _NARROW_SRC, F32, 0, HEAD_DIM, True), ('dsa_v', _NARROW_SRC, F32, HEAD_DIM, HEAD_DIM, True),
    ('idx_k', _NARROW_SRC, F32, 2 * HEAD_DIM, HEAD_DIM, True), ('fox_f', _NARROW_SRC, F32, 3 * HEAD_DIM, N_HEADS, True),
    ('dsa_kk_bf', ('dsa_k', 'dsa_k', 'idx_k', 'idx_k'), BF, 0, 4 * HEAD_DIM, False),
)
_COL_OUTS = (
    ('sb_q_t', ('sb_q',), BF), ('fox_q_t', ('fox_q',), BF), ('dsa_q_t', ('dsa_q',), BF),
    ('idx_q_t', ('idx_q',), BF), ('sb_v_t', ('sb_v',), BF), ('fox_v_t', ('fox_v',), BF),
    ('ret_q_t', ('ret_q',), F32), ('ret_k_t', ('ret_k',), F32), ('ret_v_t', ('ret_v',), F32),
    ('ret_g_t', ('ret_g',), F32), ('dsa_v_t', ('dsa_v',), BF), ('idx_w_t', ('idx_w',), F32),
)
BF16_ROWS_PER_VREG = 16


def _inproj_plan(d):
    widths = dict(_in_layout(d))

    def spans(outs, multiple):
        span_of, off = {}, 0
        for out in outs:
            srcs = out[1]
            if srcs not in span_of:
                w = -(-sum(widths[s] for s in srcs) // multiple) * multiple
                span_of[srcs] = (off, w)
                off += w
        return span_of, off

    row_spans, _ = spans(_ROW_OUTS, LANES)
    col_spans, n_col = spans(_COL_OUTS, BF16_ROWS_PER_VREG)
    return row_spans, col_spans, -(-n_col // LANES) * LANES


def _inproj_weights(w_in):
    d = w_in.shape[0]
    w_bf = w_in.astype(BF)
    pieces, off = {}, 0
    for name, width in _in_layout(d):
        piece = w_bf[:, off:off + width]
        scale = _FOLDED_SCALE.get(name)
        pieces[name] = piece if scale is None else piece * jnp.asarray(scale, BF)
        off += width
    assert off == w_in.shape[1]
    row_spans, col_spans, n_col = _inproj_plan(d)

    def block(srcs, width):
        w = jnp.concatenate([pieces[s] for s in srcs], axis=1)
        return jnp.pad(w, ((0, 0), (0, width - w.shape[1])))

    w_row = jnp.concatenate([block(srcs, w) for srcs, (_, w) in row_spans.items()], axis=1)
    w_col = jnp.concatenate([block(srcs, w) for srcs, (_, w) in col_spans.items()], axis=1)
    w_col = jnp.pad(w_col, ((0, 0), (0, n_col - w_col.shape[1])))
    return w_row.astype(BF), w_col.T.astype(BF), pieces['merge_gate'].astype(BF)


def _inproj_body(x_ref, wr_ref, wc_ref, *refs, row_spans, col_spans, n_alias):
    o_refs = refs[n_alias:]
    xb = x_ref[0].astype(BF)
    done = {}
    for o_ref, (_, srcs, _, lane, width, _) in zip(o_refs, _ROW_OUTS):
        if srcs not in done:
            off, w = row_spans[srcs]
            done[srcs] = _dot(xb, wr_ref[:, off:off + w])
        o_ref[...] = done[srcs][:, lane:lane + width].astype(o_ref.dtype).reshape(o_ref.shape)
    for o_ref, (_, srcs, _) in zip(o_refs[len(_ROW_OUTS):], _COL_OUTS):
        off, w = col_spans[srcs]
        o_ref[0] = _dot_nt(wc_ref[off:off + w, :], xb).astype(o_ref.dtype)


def _inproj(x, w_row, w_col, layer, depth, states):
    b, t, d = x.shape
    tm = _row_tile(t, INPROJ_ROWS)
    row_spans, col_spans, _ = _inproj_plan(d)
    out_shape, out_specs, state_names = [], [], []
    for name, _, dt, _, w, is_state in _ROW_OUTS:
        if is_state:
            state_names.append(name)
            out_shape.append(jax.ShapeDtypeStruct((depth, b, t, w), dt))
            out_specs.append(pl.BlockSpec((1, 1, tm, w), lambda bi, i: (layer, bi, i, 0)))
        else:
            out_shape.append(jax.ShapeDtypeStruct((b, t, w), dt))
            out_specs.append(pl.BlockSpec((1, tm, w), lambda bi, i: (bi, i, 0)))
    for _, srcs, dt in _COL_OUTS:
        w = col_spans[srcs][1]
        out_shape.append(jax.ShapeDtypeStruct((b, w, t), dt))
        out_specs.append(pl.BlockSpec((1, w, tm), lambda bi, i: (bi, 0, i)))
    prev = [] if states is None else [states[n] for n in state_names]
    names = [o[0] for o in _ROW_OUTS] + [o[0] for o in _COL_OUTS]
    aliases = {3 + j: names.index(n) for j, n in enumerate(state_names)} if prev else {}
    outs = pl.pallas_call(
        functools.partial(_inproj_body, row_spans=row_spans, col_spans=col_spans, n_alias=len(prev)),
        out_shape=out_shape,
        grid=(b, t // tm),
        in_specs=[pl.BlockSpec((1, tm, d), lambda bi, i: (bi, i, 0)),
                  _const_spec(w_row.shape), _const_spec(w_col.shape)]
                 + [pl.BlockSpec(memory_space=pl.ANY)] * len(prev),
        out_specs=out_specs,
        input_output_aliases=aliases,
        compiler_params=_params("parallel", "parallel"),
        name="inproj",
    )(x, w_row, w_col, *prev)
    return dict(zip(names, outs))


def _block_counts(q0, tk, last_key):
    return lax.div(q0, tk), lax.div(last_key, tk) + 1


def _head_queries(qt_ref):
    low = lax.broadcasted_iota(jnp.int32, (LANES, 1), 0) < HEAD_DIM
    out = []
    for h in range(N_HEADS):
        qc = qt_ref[0, _col(h // HEADS_PER_COL), :]
        keep = low if h % HEADS_PER_COL == 0 else jnp.logical_not(low)
        out.append(jnp.where(keep, qc, jnp.zeros_like(qc)))
    return out


def _head_rows(h):
    return slice(h * HEAD_DIM, (h + 1) * HEAD_DIM)


def _key_minus_query(tk, tq):
    return (lax.broadcasted_iota(jnp.int32, (tk, tq), 0) - lax.broadcasted_iota(jnp.int32, (tk, tq), 1))


def _qt_spec(w, tq):
    return pl.BlockSpec((1, w, tq), lambda bi, qi: (bi, 0, qi))


def _whole_spec(rows, cols):
    return pl.BlockSpec((1, rows, cols), lambda bi, qi: (bi, 0, 0))


def _sb_body(qt_ref, k_ref, vt_ref, o_ref, *, p_len, tq, tk):
    q0 = p_len + pl.program_id(1) * tq
    qm = _head_queries(qt_ref)
    diff = _key_minus_query(tk, tq)
    later = (lax.broadcasted_iota(jnp.int32, (tk, 2 * tk), 1) & (tk - 1)) > lax.broadcasted_iota(
        jnp.int32, (tk, 2 * tk), 0)
    minus_later = jnp.where(later, -1.0, 0.0).astype(BF)
    n_full, n_all = _block_counts(q0, tk, jnp.maximum(q0 + tq - 2, 0))

    def step(kb, carry, masked):
        laters, accs = carry
        s0 = pl.multiple_of(kb * tk, tk)
        if masked:
            earlier = diff < (q0 - s0)
        zs = [_dot(k_ref[0, pl.ds(s0, tk), _col(h // HEADS_PER_COL)], qm[h]) for h in range(N_HEADS)]
        new_laters, log_bs, afters = [], [], []
        for h in range(N_HEADS):
            z = zs[h]
            minus_abs = pltpu.bitcast(pltpu.bitcast(z, jnp.int32) | jnp.int32(INT_MIN), F32)
            softplus = jnp.maximum(z, 0.0) + jnp.log(1.0 + jnp.exp(minus_abs))
            log_bs.append(z - softplus)
            if masked:
                softplus = jnp.where(earlier, softplus, 0.0)
            hi = softplus.astype(BF)
            lo = (softplus - hi.astype(F32)).astype(BF)
            after = _dot(minus_later, jnp.concatenate([hi, lo], axis=0)) + laters[h]
            afters.append(after)
            new_laters.append(after[0:1, :] - softplus[0:1, :])
        new_accs = []
        for h in range(N_HEADS):
            w = jnp.exp(log_bs[h] + afters[h])
            if masked:
                w = jnp.where(earlier, w, 0.0)
            new_accs.append(accs[h] + _dot(vt_ref[0, _head_rows(h), pl.ds(s0, tk)], w.astype(BF)))
        return tuple(new_laters), tuple(new_accs)

    carry = (tuple(jnp.zeros((1, tq), F32) for _ in range(N_HEADS)),
             tuple(jnp.zeros((HEAD_DIM, tq), F32) for _ in range(N_HEADS)))
    carry = lax.fori_loop(0, n_all - n_full, lambda i, c: step(n_all - 1 - i, c, True), carry)

    def reachable(laters):
        return jnp.max(functools.reduce(jnp.maximum, laters)) > F32_EXP_UNDERFLOW

    def earlier_block(state):
        i, _, c = state
        c = step(n_full - 1 - i, c, False)
        return i + 1, reachable(c[0]), c

    _, _, carry = lax.while_loop(lambda state: (state[0] < n_full) & state[1], earlier_block,
                                 (jnp.int32(0), reachable(carry[0]), carry))
    for h in range(N_HEADS):
        o_ref[0, _head_rows(h), :] = carry[1][h].astype(o_ref.dtype)


def _sb_attention(qt, k, vt, p_len, tq, tk):
    b, w, t = qt.shape
    lp = k.shape[1]
    assert tk & (tk - 1) == 0
    return pl.pallas_call(
        functools.partial(_sb_body, p_len=p_len, tq=tq, tk=tk),
        out_shape=jax.ShapeDtypeStruct((b, w, t), BF),
        grid=(b, t // tq),
        in_specs=[_qt_spec(w, tq), _whole_spec(lp, w), _whole_spec(w, lp)],
        out_specs=_qt_spec(w, tq),
        compiler_params=_params("parallel", "arbitrary"),
        name="sb_attention",
    )(qt, k, vt)


def _online_softmax_step(logits, m, l):
    m_new = jnp.maximum(m, jnp.max(logits, axis=0, keepdims=True))
    alpha = jnp.exp(m - m_new)
    p = jnp.exp(logits - m_new)
    return m_new, alpha, alpha * l + jnp.sum(p, axis=0, keepdims=True), p


def _fox_body(qt_ref, k_ref, vt_ref, c_ref, kmax_ref, ctop_ref, o_ref, *, p_len, tq, tk, n_blocks):
    bi = pl.program_id(0)
    q0 = p_len + pl.program_id(1) * tq
    qm = _head_queries(qt_ref)
    diff = _key_minus_query(tk, tq)
    n_full, n_all = _block_counts(q0, tk, q0 + tq - 1)
    dot_bound = [_dot(kmax_ref[0, :, _col(h // HEADS_PER_COL)], jnp.abs(qm[h]))[0:1, :] for h in range(N_HEADS)]

    def key_bias(h, s0):
        c = c_ref[0, h, pl.ds(s0, tk), :]
        return c[:, :tq] if tq <= LANES else jnp.concatenate([c] * (tq // LANES), axis=1)

    def step(kb, carry, masked):
        ms, ls, accs = carry
        s0 = pl.multiple_of(kb * tk, tk)
        if masked:
            visible = diff <= (q0 - s0)
        new_ms, new_ls, new_accs = [], [], []
        raw = [_dot(k_ref[0, pl.ds(s0, tk), _col(h // HEADS_PER_COL)], qm[h]) for h in range(N_HEADS)]
        for h in range(N_HEADS):
            logits = raw[h] - key_bias(h, s0)
            if masked:
                logits = jnp.where(visible, logits, MASK_VALUE)
            m_new, alpha, l_new, p = _online_softmax_step(logits, ms[h], ls[h])
            new_ms.append(m_new)
            new_ls.append(l_new)
            new_accs.append(alpha * accs[h] + _dot(vt_ref[0, _head_rows(h), pl.ds(s0, tk)], p.astype(BF)))
        return tuple(new_ms), tuple(new_ls), tuple(new_accs)

    carry = (tuple(jnp.full((1, tq), -jnp.inf, F32) for _ in range(N_HEADS)),
             tuple(jnp.zeros((1, tq), F32) for _ in range(N_HEADS)),
             tuple(jnp.zeros((HEAD_DIM, tq), F32) for _ in range(N_HEADS)))
    carry = lax.fori_loop(0, n_all - n_full, lambda i, c: step(n_all - 1 - i, c, True), carry)

    def reachable(kb, ms):
        j = jnp.maximum(kb, 0)
        gaps = [dot_bound[h] + ctop_ref[bi, h * n_blocks + j] - ms[h] for h in range(N_HEADS)]
        return jnp.max(functools.reduce(jnp.maximum, gaps)) > F32_EXP_UNDERFLOW - BOUND_SLACK

    def older_block(state):
        i, _, c = state
        c = step(n_full - 1 - i, c, False)
        return i + 1, reachable(n_full - 2 - i, c[0]), c

    _, _, carry = lax.while_loop(lambda state: (state[0] < n_full) & state[1], older_block,
                                 (jnp.int32(0), reachable(n_full - 1, carry[0]), carry))
    _, ls, accs = carry
    for h in range(N_HEADS):
        o_ref[0, _head_rows(h), :] = (accs[h] / ls[h]).astype(o_ref.dtype)


def _fox_attention(qt, k, vt, c_lanes, kmax, ctop, p_len, tq, tk):
    b, w, t = qt.shape
    lp = k.shape[1]
    assert tq <= LANES or tq % LANES == 0
    return pl.pallas_call(
        functools.partial(_fox_body, p_len=p_len, tq=tq, tk=tk, n_blocks=lp // tk),
        out_shape=jax.ShapeDtypeStruct((b, w, t), BF),
        grid=(b, t // tq),
        in_specs=[_qt_spec(w, tq), _whole_spec(lp, w), _whole_spec(w, lp),
                  pl.BlockSpec((1, N_HEADS, lp, LANES), lambda bi, qi: (bi, 0, 0, 0)),
                  _whole_spec(SUBLANES, w), pl.BlockSpec(memory_space=pltpu.SMEM)],
        out_specs=_qt_spec(w, tq),
        compiler_params=_params("parallel", "arbitrary"),
        name="fox_attention",
    )(qt, k, vt, c_lanes, kmax, ctop)


def _float_key(bits):
    return jnp.where(bits < 0, jnp.int32(INT_MIN) - bits, bits)


_BUTTERFLY = ((16, 0x0000FFFF), (8, 0x00FF00FF), (4, 0x0F0F0F0F), (2, 0x33333333), (1, 0x55555555))


def _bit_planes(words):
    a = list(words)
    for shift, mask in _BUTTERFLY:
        for k in range(WORD_BITS):
            if k & shift == 0:
                t = (a[k] ^ lax.shift_right_logical(a[k + shift], shift)) & mask
                a[k] = a[k] ^ t
                a[k + shift] = a[k + shift] ^ lax.shift_left(t, shift)
    return a


def _dsa_body(qt_ref, qit_ref, wit_ref, k_ref, vt_ref, ki_ref, o_ref, keys_ref, planes_ref, active_ref, *,
              p_len, n_keys, tq, tk, top_k):
    q0 = p_len + pl.program_id(1) * tq
    qpos = q0 + lax.broadcasted_iota(jnp.int32, (1, tq), 1)
    limit = jnp.minimum(((qpos >> CHUNK_SHIFT) + 1) << CHUNK_SHIFT, n_keys)
    last_limit = jnp.minimum((((q0 + tq - 1) >> CHUNK_SHIFT) + 1) << CHUNK_SHIFT, n_keys)
    n_blk = lax.div(last_limit - 1, tk) + 1
    key_row = lax.broadcasted_iota(jnp.int32, (tk, tq), 0)

    wit = wit_ref[0]
    qim = _head_queries(qit_ref)

    def score_step(kb, _):
        s0 = pl.multiple_of(kb * tk, tk)
        ki = ki_ref[0, pl.ds(s0, tk), :]
        score = jnp.zeros((tk, tq), F32)
        for h in range(N_HEADS):
            score = score + wit[h:h + 1, :] * jnp.maximum(_dot(ki, qim[h]), 0.0)
        score = jnp.where(key_row < limit - s0, score, -jnp.inf)
        key = _float_key(pltpu.bitcast(score, jnp.int32))
        keys_ref[pl.ds(s0, tk), :] = key
        unsigned = key ^ jnp.int32(INT_MIN)
        for g in range(tk // GROUP_KEYS):
            words = [unsigned[g * GROUP_KEYS + SUBLANES * j:g * GROUP_KEYS + SUBLANES * (j + 1), :]
                     for j in range(WORD_BITS)]
            r0 = pl.multiple_of(kb * plane_rows + g * SUBLANES, SUBLANES)
            for i, plane in enumerate(_bit_planes(words)):
                planes_ref[i, pl.ds(r0, SUBLANES), :] = plane
        r0 = pl.multiple_of(kb * plane_rows, plane_rows)
        active_ref[pl.ds(r0, plane_rows), :] = jnp.full((plane_rows, tq), -1, jnp.int32)
        return 0

    plane_rows = tk // WORD_BITS
    lax.fori_loop(0, n_blk, score_step, 0)

    def ones_in(i, narrow):
        def body(kb, acc):
            r0 = pl.multiple_of(kb * plane_rows, plane_rows)
            live = active_ref[pl.ds(r0, plane_rows), :]
            if narrow is not None:
                live = live & (planes_ref[i - 1, pl.ds(r0, plane_rows), :] ^ narrow)
                active_ref[pl.ds(r0, plane_rows), :] = live
            return acc + lax.population_count(live & planes_ref[i, pl.ds(r0, plane_rows), :])
        acc = lax.fori_loop(0, n_blk, body, jnp.zeros((plane_rows, tq), jnp.int32))
        return jnp.sum(acc.astype(F32), axis=0, keepdims=True)

    def choose(i, n_ones, wanted, thr_bits):
        take = n_ones >= wanted
        bit = lax.shift_left(jnp.int32(1), WORD_BITS - 1 - i)
        return (jnp.where(take, wanted, wanted - n_ones), thr_bits | jnp.where(take, bit, 0),
                jnp.where(take, 0, -1))

    wanted, thr_bits, narrow = choose(0, ones_in(0, None), jnp.full((1, tq), top_k, F32),
                                      jnp.zeros((1, tq), jnp.int32))

    def bit_step(i, carry):
        wanted, thr_bits, narrow = carry
        return choose(i, ones_in(i, narrow), wanted, thr_bits)

    wanted, thr_bits, _ = lax.fori_loop(1, WORD_BITS, bit_step, (wanted, thr_bits, narrow))
    thr = thr_bits ^ jnp.int32(INT_MIN)
    n_tie_wanted = wanted

    earlier_keys = (lax.broadcasted_iota(jnp.int32, (tk, tk), 1)
                    < lax.broadcasted_iota(jnp.int32, (tk, tk), 0)).astype(BF)
    qm = _head_queries(qt_ref)

    def attend_step(kb, carry):
        ties_seen, ms, ls, accs = carry
        s0 = pl.multiple_of(kb * tk, tk)
        key = keys_ref[pl.ds(s0, tk), :]
        tie = jnp.where(key == thr, 1.0, 0.0)
        tie_rank = _dot(earlier_keys, tie.astype(BF)) + ties_seen
        take = jnp.where(key > thr, 1.0, jnp.where(tie_rank < n_tie_wanted, tie, 0.0))
        selected = jnp.where(key_row < limit - s0, take, 0.0) > 0.0
        k = k_ref[0, pl.ds(s0, tk), :]
        vt = vt_ref[0, :, pl.ds(s0, tk)]
        new_ms, new_ls, new_accs = [], [], []
        raw = [_dot(k, qm[h]) for h in range(N_HEADS)]
        for h in range(N_HEADS):
            logits = jnp.where(selected, raw[h], MASK_VALUE)
            m_new, alpha, l_new, p = _online_softmax_step(logits, ms[h], ls[h])
            new_ms.append(m_new)
            new_ls.append(l_new)
            new_accs.append(alpha * accs[h] + _dot(vt, p.astype(BF)))
        ties_seen = ties_seen + jnp.sum(tie, axis=0, keepdims=True)
        return ties_seen, tuple(new_ms), tuple(new_ls), tuple(new_accs)

    carry = (jnp.zeros((1, tq), F32),
             tuple(jnp.full((1, tq), -jnp.inf, F32) for _ in range(N_HEADS)),
             tuple(jnp.zeros((1, tq), F32) for _ in range(N_HEADS)),
             tuple(jnp.zeros((HEAD_DIM, tq), F32) for _ in range(N_HEADS)))
    _, _, ls, accs = lax.fori_loop(0, n_blk, attend_step, carry)
    for h in range(N_HEADS):
        o_ref[0, _head_rows(h), :] = (accs[h] / ls[h]).astype(o_ref.dtype)


def _dsa_attention(qt, qit, wit, kk, vt, p_len, n_keys, tq, tk, top_k):
    b, w, t = qt.shape
    lp = kk.shape[1]
    assert tk % GROUP_KEYS == 0
    return pl.pallas_call(
        functools.partial(_dsa_body, p_len=p_len, n_keys=n_keys, tq=tq, tk=tk, top_k=top_k),
        out_shape=jax.ShapeDtypeStruct((b, w, t), BF),
        grid=(b, t // tq),
        in_specs=[_qt_spec(w, tq), _qt_spec(w, tq), _qt_spec(wit.shape[1], tq),
                  pl.BlockSpec((1, lp, LANES), lambda bi, qi: (bi, 0, 0)), _whole_spec(HEAD_DIM, lp),
                  pl.BlockSpec((1, lp, LANES), lambda bi, qi: (bi, 0, 1))],
        out_specs=_qt_spec(w, tq),
        scratch_shapes=[pltpu.VMEM((lp, tq), jnp.int32),
                        pltpu.VMEM((WORD_BITS, lp // WORD_BITS, tq), jnp.int32),
                        pltpu.VMEM((lp // WORD_BITS, tq), jnp.int32)],
        compiler_params=_params("parallel", "arbitrary"),
        name="dsa_attention",
    )(qt, qit, wit, kk, vt, kk)


def _ret_body(q_ref, k_ref, v_ref, g_ref, s0_ref, cos_ref, sin_ref, dec_ref, qd_ref, kd_ref, sd_ref,
              o_ref, so_ref, state_ref):
    c = pl.program_id(1)

    @pl.when(c == 0)
    def _():
        state_ref[...] = s0_ref[0]

    cos, sin = cos_ref[...], sin_ref[...]
    half = HEAD_DIM // 2

    def rotary(x):
        x1, x2 = x[:half], x[half:]
        return jnp.concatenate([x1 * cos - x2 * sin, x2 * cos + x1 * sin], axis=0)

    heads = range(N_HEADS)
    qb = [rotary(q_ref[0, _head_rows(h), :]).astype(BF) for h in heads]
    k = [rotary(k_ref[0, _head_rows(h), :]) for h in heads]
    vb = [v_ref[0, _head_rows(h), :].astype(BF) for h in heads]
    scores_t = [_dot_tn(k[h].astype(BF), qb[h]) * dec_ref[h] for h in heads]
    carried = [_dot(state_ref[h].astype(BF), qb[h]) * qd_ref[h] for h in heads]
    outs = [_dot(vb[h], scores_t[h].astype(BF)) + carried[h] for h in heads]
    for h in heads:
        state_ref[h] = sd_ref[h] * state_ref[h] + _dot_nt(vb[h], (k[h] * kd_ref[h]).astype(BF))
    for h in heads:
        o = outs[h]
        oc = o - jnp.mean(o, axis=0, keepdims=True)
        on = oc * lax.rsqrt(jnp.mean(oc * oc, axis=0, keepdims=True) + LN_EPS)
        g = g_ref[0, _head_rows(h), :]
        o_ref[0, _head_rows(h), :] = (on * (g * jax.nn.sigmoid(g))).astype(o_ref.dtype)

    @pl.when(c == pl.num_programs(1) - 1)
    def _():
        so_ref[0] = state_ref[...]


def _retention(qt, kt, vt, gt, state0_t, pos, c):
    b, w, t = qt.shape
    h = w // HEAD_DIM
    half = HEAD_DIM // 2
    inv_freq = ROPE_BASE ** (-jnp.arange(half, dtype=F32) / half)
    ang = inv_freq[:, None] * pos.astype(F32)[None, :]
    log_gamma = np.log(1.0 - 2.0 ** (-5.0 - np.arange(h, dtype=np.float64)))
    n = np.arange(c, dtype=np.float64)
    rel = n[None, :] - n[:, None]
    decay_t = np.where(rel >= 0, np.exp(np.maximum(rel, 0.0)[None] * log_gamma[:, None, None]), 0.0)
    q_decay = np.exp((n[None, :] + 1.0) * log_gamma[:, None])[:, None, :]
    k_decay = np.exp((c - 1.0 - n)[None, :] * log_gamma[:, None])[:, None, :]
    s_decay = np.exp(c * log_gamma)[:, None, None]
    tables = [jnp.asarray(a, F32) for a in (decay_t, q_decay, k_decay, s_decay)]
    assert h == N_HEADS
    x_spec = pl.BlockSpec((1, w, c), lambda bi, ci: (bi, 0, ci))
    s_spec = pl.BlockSpec((1, h, HEAD_DIM, HEAD_DIM), lambda bi, ci: (bi, 0, 0, 0))
    rope_spec = pl.BlockSpec((half, c), lambda bi, ci: (0, ci))
    return pl.pallas_call(
        _ret_body,
        out_shape=(jax.ShapeDtypeStruct((b, w, t), BF),
                   jax.ShapeDtypeStruct((b, h, HEAD_DIM, HEAD_DIM), F32)),
        grid=(b, t // c),
        in_specs=[x_spec, x_spec, x_spec, x_spec, s_spec, rope_spec, rope_spec] + [_const_spec(a.shape) for a in tables],
        out_specs=(x_spec, s_spec),
        scratch_shapes=[pltpu.VMEM((h, HEAD_DIM, HEAD_DIM), F32)],
        compiler_params=_params("parallel", "arbitrary"),
        name="retention",
    )(qt, kt, vt, gt, state0_t, jnp.cos(ang), jnp.sin(ang), *tables)


def _merge_body(h_ref, y0_ref, y1_ref, y2_ref, y3_ref, wg_ref, wb_ref, wo_ref, g_ref, b_ref, o_ref, *, alpha):
    nb, tm, d = h_ref.shape
    h = h_ref[...].reshape(nb * tm, d)
    hb = h.astype(BF)
    merged = jnp.zeros(h.shape, F32)
    for i, y_ref in enumerate((y0_ref, y1_ref, y2_ref, y3_ref)):
        gate = jax.nn.sigmoid(_dot(hb, wg_ref[:, i * d:(i + 1) * d]))
        branch = jnp.concatenate([_dot_tn(y_ref[j], wb_ref[i]) for j in range(nb)], axis=0)
        merged = merged + gate * branch
    r = alpha * h + _dot(merged.astype(BF), wo_ref[...])
    o_ref[...] = _layer_norm(r, g_ref[...], b_ref[...]).reshape(nb, tm, d)


def _merge(h, ys_t, w_gate, w_branch, w_out, ln_g, ln_b, alpha):
    b, t, d = h.shape
    tm = _row_tile(t, MERGE_ROWS)
    nb = math.gcd(b, max(1, MERGE_ROWS // tm))
    row = pl.BlockSpec((nb, tm, d), lambda bi, i: (bi, i, 0))
    col = pl.BlockSpec((nb, BRANCH_WIDTH, tm), lambda bi, i: (bi, 0, i))
    return pl.pallas_call(
        functools.partial(_merge_body, alpha=alpha),
        out_shape=jax.ShapeDtypeStruct((b, t, d), F32),
        grid=(b // nb, t // tm),
        in_specs=[row] + [col] * 4
                 + [_const_spec(w_gate.shape), _const_spec(w_branch.shape), _const_spec(w_out.shape),
                    _const_spec((1, d)), _const_spec((1, d))],
        out_specs=row,
        compiler_params=_params("parallel", "parallel"),
        name="merge",
    )(h, *ys_t, w_gate, w_branch, w_out, ln_g, ln_b)


def _ffn_body(h_ref, wi_ref, wo_ref, g_ref, b_ref, o_ref, *, alpha, f_chunk):
    h = h_ref[...]
    hb = h.astype(BF)
    f = wo_ref.shape[0]
    acc = jnp.zeros(h.shape, F32)
    for c in range(0, f, f_chunk):
        a = _dot(hb, wi_ref[:, c:c + f_chunk])
        u = _dot(hb, wi_ref[:, f + c:f + c + f_chunk])
        acc = acc + _dot((a * jax.nn.sigmoid(a) * u).astype(BF), wo_ref[c:c + f_chunk, :])
    o_ref[...] = _layer_norm(alpha * h + acc, g_ref[...], b_ref[...])


def _ffn(h, w_in, w_out, ln_g, ln_b, alpha):
    m, d = h.shape
    f = w_out.shape[0]
    tm = _row_tile(m, FFN_ROWS)
    f_chunk = f // 2 if (f // 2) % LANES == 0 else f
    row = pl.BlockSpec((tm, d), lambda i: (i, 0))
    return pl.pallas_call(
        functools.partial(_ffn_body, alpha=alpha, f_chunk=f_chunk),
        out_shape=jax.ShapeDtypeStruct((m, d), F32),
        grid=(m // tm,),
        in_specs=[row, _const_spec(w_in.shape), _const_spec(w_out.shape),
                  _const_spec((1, d)), _const_spec((1, d))],
        out_specs=row,
        compiler_params=_params("parallel"),
        name="ffn",
    )(h, w_in, w_out, ln_g, ln_b)


def _in_layout(d):
    w = BRANCH_WIDTH
    return (('sb_q', w), ('sb_k', w), ('sb_v', w), ('ret_q', w), ('ret_k', w), ('ret_v', w), ('ret_g', w),
            ('fox_q', w), ('fox_k', w), ('fox_v', w), ('fox_f', N_HEADS),
            ('dsa_q', w), ('dsa_k', HEAD_DIM), ('dsa_v', HEAD_DIM),
            ('idx_q', w), ('idx_k', HEAD_DIM), ('idx_w', N_HEADS), ('merge_gate', 4 * d))


_FOLDED_SCALE = dict(sb_q=QK_SCALE, fox_q=QK_SCALE, dsa_q=QK_SCALE, idx_q=QK_SCALE, ret_k=QK_SCALE,
                     idx_w=IDX_HEAD_SCALE)


def _swap(a):
    return jnp.swapaxes(a, -1, -2)


def _key_tiles(t, n_keys):
    tq = {name: min(t, q) for name, q in QUERY_TILE.items()}
    tiles = dict(sb=256, fox=512, dsa=512)
    padded = {name: -(-n_keys // tk) * tk for name, tk in tiles.items()}
    return tq, tiles, padded


def _layer(h, b, t, past, ret_state, w, alpha, layer, depth, states):
    m, d = h.shape
    p_len = 0 if past is None else past[0].shape[1]
    n_keys = p_len + t
    tq, tk, lp = _key_tiles(t, n_keys)
    p = _inproj(h.reshape(b, t, d), w['w_row'], w['w_col'], layer, depth, states)
    states = {name: p[name] for name, _, _, _, _, is_state in _ROW_OUTS if is_state}

    old = (None,) * 8 if past is None else past
    sb_k0, sb_v0, fox_k0, fox_v0, fox_lf0, dsa_k0, dsa_v0, dsa_ki0 = old

    def rows_with_past(new_bf, olds, lp_):
        if past is not None:
            flat = [o.reshape(o.shape[0], o.shape[1], -1).astype(BF) for o in olds]
            new_bf = jnp.concatenate([jnp.concatenate(flat, axis=2), new_bf], axis=1)
        return jnp.pad(new_bf, ((0, 0), (0, lp_ - new_bf.shape[1]), (0, 0)))

    def cols_with_past(new_t, old, lp_):
        if old is not None:
            new_t = jnp.concatenate([_swap(old.reshape(old.shape[0], old.shape[1], -1).astype(BF)), new_t], axis=2)
        return jnp.pad(new_t, ((0, 0), (0, 0), (0, lp_ - new_t.shape[2])))

    y_sb = _sb_attention(p['sb_q_t'], rows_with_past(p['sb_k_bf'], [sb_k0], lp['sb']),
                         cols_with_past(p['sb_v_t'], sb_v0, lp['sb']), p_len, tq['sb'], tk['sb'])

    pos = p_len + jnp.arange(t, dtype=jnp.int32)
    y_ret, ret_state_t = _retention(p['ret_q_t'], p['ret_k_t'], p['ret_v_t'], p['ret_g_t'], _swap(ret_state),
                                    pos, min(t, RETENTION_CHUNK))

    fox_lf = jax.nn.log_sigmoid(states['fox_f'][layer] + w['b_forget'])
    lf_all = fox_lf if fox_lf0 is None else jnp.concatenate([fox_lf0, fox_lf], axis=1)
    cum = jnp.pad(jnp.cumsum(lf_all, axis=1), ((0, 0), (0, lp['fox'] - n_keys), (0, 0)))
    c_lanes = jnp.broadcast_to(_swap(cum)[..., None], (b, N_HEADS, lp['fox'], LANES))
    ctop = _swap(lax.cummax(-cum, axis=1)[:, tk['fox'] - 1::tk['fox']]).reshape(b, -1)
    fox_keys = rows_with_past(p['fox_k_bf'], [fox_k0], lp['fox'])
    kmax = jnp.broadcast_to(jnp.max(jnp.abs(fox_keys), axis=1, keepdims=True), (b, SUBLANES, BRANCH_WIDTH))
    y_fox = _fox_attention(p['fox_q_t'], fox_keys, cols_with_past(p['fox_v_t'], fox_v0, lp['fox']), c_lanes,
                           kmax, ctop, p_len, tq['fox'], tk['fox'])

    top_k = min(DSA_TOP_K, n_keys // 4)
    kk_old = [dsa_k0, dsa_k0, dsa_ki0, dsa_ki0]
    y_dsa = _dsa_attention(p['dsa_q_t'], p['idx_q_t'], p['idx_w_t'], rows_with_past(p['dsa_kk_bf'], kk_old, lp['dsa']),
                           cols_with_past(p['dsa_v_t'], dsa_v0, lp['dsa']), p_len, n_keys, tq['dsa'], tk['dsa'], top_k)

    h = _merge(h.reshape(b, t, d), (y_sb, y_ret, y_fox, y_dsa), w['w_gate'], w['w_branch'], w['w_out'],
               w['ln1_g'], w['ln1_b'], alpha)
    h = _ffn(h.reshape(m, d), w['w_ffn_in'], w['w_ffn_out'], w['ln2_g'], w['ln2_b'], alpha)
    return h, states, _swap(ret_state_t), fox_lf


def _group_outputs(states, ret_states, fox_lfs):
    def heads(a):
        return a.reshape(a.shape[:-1] + (N_HEADS, HEAD_DIM))

    return (heads(states['sb_k']), heads(states['sb_v']), jnp.stack(ret_states), heads(states['fox_k']),
            heads(states['fox_v']), jnp.stack(fox_lfs), states['dsa_k'], states['dsa_v'], states['idx_k'])


def kernel(x_prompt, x_sample, cache_sb_k, cache_sb_v, state_ret, cache_fox_k, cache_fox_v, cache_fox_logf,
           cache_dsa_k, cache_dsa_v, cache_dsa_kidx, w_in, b_forget, w_branch, w_out, ln1_g, ln1_b,
           w_ffn_in, w_ffn_out, ln2_g, ln2_b):
    depth = w_in.shape[0]
    alpha = float((2 * depth) ** 0.25)
    bp, tp, d = x_prompt.shape
    bs, ts, _ = x_sample.shape
    hp = x_prompt.reshape(bp * tp, d)
    hs = x_sample.reshape(bs * ts, d)
    ret_zero = jnp.zeros((bp, N_HEADS, HEAD_DIM, HEAD_DIM), F32)
    st_p, st_s, ret_p, ret_s, lf_p, lf_s = None, None, [], [], [], []
    for l in range(depth):
        w_row, w_col, w_gate = _inproj_weights(w_in[l])
        w = dict(w_row=w_row, w_col=w_col, w_gate=w_gate, b_forget=b_forget[l], w_branch=w_branch[l].astype(BF),
                 w_out=w_out[l].astype(BF), ln1_g=ln1_g[l][None], ln1_b=ln1_b[l][None],
                 w_ffn_in=w_ffn_in[l].astype(BF), w_ffn_out=w_ffn_out[l].astype(BF),
                 ln2_g=ln2_g[l][None], ln2_b=ln2_b[l][None])
        hp, st_p, ret, lf = _layer(hp, bp, tp, None, ret_zero, w, alpha, l, depth, st_p)
        ret_p.append(ret)
        lf_p.append(lf)
        past = (cache_sb_k[l], cache_sb_v[l], cache_fox_k[l], cache_fox_v[l], cache_fox_logf[l],
                cache_dsa_k[l], cache_dsa_v[l], cache_dsa_kidx[l])
        hs, st_s, ret, lf = _layer(hs, bs, ts, past, state_ret[l], w, alpha, l, depth, st_s)
        ret_s.append(ret)
        lf_s.append(lf)
    return ((hp.reshape(bp, tp, d), hs.reshape(bs, ts, d))
            + _group_outputs(st_p, ret_p, lf_p) + _group_outputs(st_s, ret_s, lf_s))
```

```python
import functools
import math

import numpy as np
import jax
import jax.numpy as jnp
from jax import lax
from jax.experimental import pallas as pl
from jax.experimental.pallas import tpu as pltpu

HEAD_DIM = 64
N_HEADS = 4
BRANCH_WIDTH = N_HEADS * HEAD_DIM
CHUNK_SHIFT = 6
DSA_TOP_K = 256
ROPE_BASE = 10000.0
LN_EPS = 1e-5
QK_SCALE = HEAD_DIM ** -0.5
IDX_HEAD_SCALE = N_HEADS ** -0.5
MASK_VALUE = -1e30
F32_EXP_UNDERFLOW = -105.0
BOUND_SLACK = 1.0
INT_MIN = -2 ** 31

V7X_VMEM_LIMIT_BYTES = 56 * 1024 * 1024
LANES = 128
HEADS_PER_COL = LANES // HEAD_DIM
SUBLANES = 8
WORD_BITS = 32
GROUP_KEYS = SUBLANES * WORD_BITS
INPROJ_ROWS = 512
QUERY_TILE = dict(sb=256, fox=512, dsa=512)
RETENTION_CHUNK = 512
MERGE_ROWS = 512
MERGE_COLS = 256
FFN_ROWS = 512

BF = jnp.bfloat16
F32 = jnp.float32


def _dot(a, b):
    return jnp.dot(a, b, preferred_element_type=F32)


def _dot_nt(a, b):
    return lax.dot_general(a, b, (((1,), (1,)), ((), ())), preferred_element_type=F32)


def _dot_tn(a, b):
    return lax.dot_general(a, b, (((0,), (0,)), ((), ())), preferred_element_type=F32)


def _params(*sem):
    return pltpu.CompilerParams(dimension_semantics=sem, vmem_limit_bytes=V7X_VMEM_LIMIT_BYTES)


def _const_spec(shape):
    nd = len(shape)
    return pl.BlockSpec(shape, lambda *_: (0,) * nd)


def _layer_norm(x, g, b):
    xc = x - jnp.mean(x, axis=-1, keepdims=True)
    var = jnp.mean(xc * xc, axis=-1, keepdims=True)
    return xc * lax.rsqrt(var + LN_EPS) * g + b


def _row_tile(m, want):
    t = min(m, want)
    assert m % t == 0
    return t


def _col(c):
    return slice(c * LANES, (c + 1) * LANES)


_NARROW_SRC = ('dsa_k', 'dsa_v', 'idx_k', 'fox_f')
_ROW_OUTS = (
    ('sb_k', ('sb_k',), F32, 0, BRANCH_WIDTH, True), ('sb_k_bf', ('sb_k',), BF, 0, BRANCH_WIDTH, False),
    ('sb_v', ('sb_v',), F32, 0, BRANCH_WIDTH, True),
    ('fox_k', ('fox_k',), F32, 0, BRANCH_WIDTH, True), ('fox_k_bf', ('fox_k',), BF, 0, BRANCH_WIDTH, False),
    ('fox_v', ('fox_v',), F32, 0, BRANCH_WIDTH, True),
    ('dsa_k', _NARROW_SRC, F32, 0, HEAD_DIM, True), ('dsa_v', _NARROW_SRC, F32, HEAD_DIM, HEAD_DIM, True),
    ('idx_k', _NARROW_SRC, F32, 2 * HEAD_DIM, HEAD_DIM, True), ('fox_f', _NARROW_SRC, F32, 3 * HEAD_DIM, N_HEADS, True),
    ('dsa_kk_bf', ('dsa_k', 'dsa_k', 'idx_k', 'idx_k'), BF, 0, 4 * HEAD_DIM, False),
)
_COL_OUTS = (
    ('sb_q_t', ('sb_q',), BF), ('fox_q_t', ('fox_q',), BF), ('dsa_q_t', ('dsa_q',), BF),
    ('idx_q_t', ('idx_q',), BF), ('sb_v_t', ('sb_v',), BF), ('fox_v_t', ('fox_v',), BF),
    ('ret_q_t', ('ret_q',), F32), ('ret_k_t', ('ret_k',), F32), ('ret_v_t', ('ret_v',), F32),
    ('ret_g_t', ('ret_g',), F32), ('dsa_v_t', ('dsa_v',), BF), ('idx_w_t', ('idx_w',), F32),
)
BF16_ROWS_PER_VREG = 16


def _inproj_plan(d):
    widths = dict(_in_layout(d))

    def spans(outs, multiple):
        span_of, off = {}, 0
        for out in outs:
            srcs = out[1]
            if srcs not in span_of:
                w = -(-sum(widths[s] for s in srcs) // multiple) * multiple
                span_of[srcs] = (off, w)
                off += w
        return span_of, off

    row_spans, _ = spans(_ROW_OUTS, LANES)
    col_spans, n_col = spans(_COL_OUTS, BF16_ROWS_PER_VREG)
    return row_spans, col_spans, -(-n_col // LANES) * LANES


def _inproj_weights(w_in):
    d = w_in.shape[0]
    w_bf = w_in.astype(BF)
    pieces, off = {}, 0
    for name, width in _in_layout(d):
        piece = w_bf[:, off:off + width]
        scale = _FOLDED_SCALE.get(name)
        pieces[name] = piece if scale is None else piece * jnp.asarray(scale, BF)
        off += width
    assert off == w_in.shape[1]
    row_spans, col_spans, n_col = _inproj_plan(d)

    def block(srcs, width):
        w = jnp.concatenate([pieces[s] for s in srcs], axis=1)
        return jnp.pad(w, ((0, 0), (0, width - w.shape[1])))

    w_row = jnp.concatenate([block(srcs, w) for srcs, (_, w) in row_spans.items()], axis=1)
    w_col = jnp.concatenate([block(srcs, w) for srcs, (_, w) in col_spans.items()], axis=1)
    w_col = jnp.pad(w_col, ((0, 0), (0, n_col - w_col.shape[1])))
    return w_row.astype(BF), w_col.T.astype(BF), pieces['merge_gate'].astype(BF)


def _inproj_body(x_ref, wr_ref, wc_ref, *refs, row_spans, col_spans, n_alias):
    o_refs = refs[n_alias:]
    xb = x_ref[0].astype(BF)
    done = {}
    for o_ref, (_, srcs, _, lane, width, _) in zip(o_refs, _ROW_OUTS):
        if srcs not in done:
            off, w = row_spans[srcs]
            done[srcs] = _dot(xb, wr_ref[:, off:off + w])
        o_ref[...] = done[srcs][:, lane:lane + width].astype(o_ref.dtype).reshape(o_ref.shape)
    for o_ref, (_, srcs, _) in zip(o_refs[len(_ROW_OUTS):], _COL_OUTS):
        off, w = col_spans[srcs]
        o_ref[0] = _dot_nt(wc_ref[off:off + w, :], xb).astype(o_ref.dtype)


def _inproj(x, w_row, w_col, layer, depth, states):
    b, t, d = x.shape
    tm = _row_tile(t, INPROJ_ROWS)
    row_spans, col_spans, _ = _inproj_plan(d)
    out_shape, out_specs, state_names = [], [], []
    for name, _, dt, _, w, is_state in _ROW_OUTS:
        if is_state:
            state_names.append(name)
            out_shape.append(jax.ShapeDtypeStruct((depth, b, t, w), dt))
            out_specs.append(pl.BlockSpec((1, 1, tm, w), lambda bi, i: (layer, bi, i, 0)))
        else:
            out_shape.append(jax.ShapeDtypeStruct((b, t, w), dt))
            out_specs.append(pl.BlockSpec((1, tm, w), lambda bi, i: (bi, i, 0)))
    for _, srcs, dt in _COL_OUTS:
        w = col_spans[srcs][1]
        out_shape.append(jax.ShapeDtypeStruct((b, w, t), dt))
        out_specs.append(pl.BlockSpec((1, w, tm), lambda bi, i: (bi, 0, i)))
    prev = [] if states is None else [states[n] for n in state_names]
    names = [o[0] for o in _ROW_OUTS] + [o[0] for o in _COL_OUTS]
    aliases = {3 + j: names.index(n) for j, n in enumerate(state_names)} if prev else {}
    outs = pl.pallas_call(
        functools.partial(_inproj_body, row_spans=row_spans, col_spans=col_spans, n_alias=len(prev)),
        out_shape=out_shape,
        grid=(b, t // tm),
        in_specs=[pl.BlockSpec((1, tm, d), lambda bi, i: (bi, i, 0)),
                  _const_spec(w_row.shape), _const_spec(w_col.shape)]
                 + [pl.BlockSpec(memory_space=pl.ANY)] * len(prev),
        out_specs=out_specs,
        input_output_aliases=aliases,
        compiler_params=_params("parallel", "parallel"),
        name="inproj",
    )(x, w_row, w_col, *prev)
    return dict(zip(names, outs))


def _block_counts(q0, tk, last_key):
    return lax.div(q0, tk), lax.div(last_key, tk) + 1


def _head_queries(qt_ref):
    low = lax.broadcasted_iota(jnp.int32, (LANES, 1), 0) < HEAD_DIM
    out = []
    for h in range(N_HEADS):
        qc = qt_ref[0, _col(h // HEADS_PER_COL), :]
        keep = low if h % HEADS_PER_COL == 0 else jnp.logical_not(low)
        out.append(jnp.where(keep, qc, jnp.zeros_like(qc)))
    return out


def _head_rows(h):
    return slice(h * HEAD_DIM, (h + 1) * HEAD_DIM)


def _key_minus_query(tk, tq):
    return (lax.broadcasted_iota(jnp.int32, (tk, tq), 0) - lax.broadcasted_iota(jnp.int32, (tk, tq), 1))


def _qt_spec(w, tq):
    return pl.BlockSpec((1, w, tq), lambda bi, qi: (bi, 0, qi))


def _whole_spec(rows, cols):
    return pl.BlockSpec((1, rows, cols), lambda bi, qi: (bi, 0, 0))


def _sb_body(qt_ref, k_ref, vt_ref, o_ref, *, p_len, tq, tk):
    q0 = p_len + pl.program_id(1) * tq
    qm = _head_queries(qt_ref)
    diff = _key_minus_query(tk, tq)
    later = (lax.broadcasted_iota(jnp.int32, (tk, 2 * tk), 1) & (tk - 1)) > lax.broadcasted_iota(
        jnp.int32, (tk, 2 * tk), 0)
    minus_later = jnp.where(later, -1.0, 0.0).astype(BF)
    n_full, n_all = _block_counts(q0, tk, jnp.maximum(q0 + tq - 2, 0))

    def step(kb, carry, masked):
        laters, accs = carry
        s0 = pl.multiple_of(kb * tk, tk)
        if masked:
            earlier = diff < (q0 - s0)
        zs = [_dot(k_ref[0, pl.ds(s0, tk), _col(h // HEADS_PER_COL)], qm[h]) for h in range(N_HEADS)]
        new_laters, log_bs, afters = [], [], []
        for h in range(N_HEADS):
            z = zs[h]
            minus_abs = pltpu.bitcast(pltpu.bitcast(z, jnp.int32) | jnp.int32(INT_MIN), F32)
            softplus = jnp.maximum(z, 0.0) + jnp.log(1.0 + jnp.exp(minus_abs))
            log_bs.append(z - softplus)
            if masked:
                softplus = jnp.where(earlier, softplus, 0.0)
            hi = softplus.astype(BF)
            lo = (softplus - hi.astype(F32)).astype(BF)
            after = _dot(minus_later, jnp.concatenate([hi, lo], axis=0)) + laters[h]
            afters.append(after)
            new_laters.append(after[0:1, :] - softplus[0:1, :])
        new_accs = []
        for h in range(N_HEADS):
            w = jnp.exp(log_bs[h] + afters[h])
            if masked:
                w = jnp.where(earlier, w, 0.0)
            new_accs.append(accs[h] + _dot(vt_ref[0, _head_rows(h), pl.ds(s0, tk)], w.astype(BF)))
        return tuple(new_laters), tuple(new_accs)

    carry = (tuple(jnp.zeros((1, tq), F32) for _ in range(N_HEADS)),
             tuple(jnp.zeros((HEAD_DIM, tq), F32) for _ in range(N_HEADS)))
    carry = lax.fori_loop(0, n_all - n_full, lambda i, c: step(n_all - 1 - i, c, True), carry)

    def reachable(laters):
        return jnp.max(functools.reduce(jnp.maximum, laters)) > F32_EXP_UNDERFLOW

    def earlier_block(state):
        i, _, c = state
        c = step(n_full - 1 - i, c, False)
        return i + 1, reachable(c[0]), c

    _, _, carry = lax.while_loop(lambda state: (state[0] < n_full) & state[1], earlier_block,
                                 (jnp.int32(0), reachable(carry[0]), carry))
    for h in range(N_HEADS):
        o_ref[0, _head_rows(h), :] = carry[1][h].astype(o_ref.dtype)


def _sb_attention(qt, k, vt, p_len, tq, tk):
    b, w, t = qt.shape
    lp = k.shape[1]
    assert tk & (tk - 1) == 0
    return pl.pallas_call(
        functools.partial(_sb_body, p_len=p_len, tq=tq, tk=tk),
        out_shape=jax.ShapeDtypeStruct((b, w, t), BF),
        grid=(b, t // tq),
        in_specs=[_qt_spec(w, tq), _whole_spec(lp, w), _whole_spec(w, lp)],
        out_specs=_qt_spec(w, tq),
        compiler_params=_params("parallel", "arbitrary"),
        name="sb_attention",
    )(qt, k, vt)


def _online_softmax_step(logits, m, l):
    m_new = jnp.maximum(m, jnp.max(logits, axis=0, keepdims=True))
    alpha = jnp.exp(m - m_new)
    p = jnp.exp(logits - m_new)
    return m_new, alpha, alpha * l + jnp.sum(p, axis=0, keepdims=True), p


def _fox_body(qt_ref, k_ref, vt_ref, c_ref, kmax_ref, ctop_ref, o_ref, *, p_len, tq, tk, n_blocks):
    bi = pl.program_id(0)
    q0 = p_len + pl.program_id(1) * tq
    qm = _head_queries(qt_ref)
    diff = _key_minus_query(tk, tq)
    n_full, n_all = _block_counts(q0, tk, q0 + tq - 1)
    dot_bound = [_dot(kmax_ref[0, :, _col(h // HEADS_PER_COL)], jnp.abs(qm[h]))[0:1, :] for h in range(N_HEADS)]

    def key_bias(h, s0):
        c = c_ref[0, h, pl.ds(s0, tk), :]
        return c[:, :tq] if tq <= LANES else jnp.concatenate([c] * (tq // LANES), axis=1)

    def step(kb, carry, masked):
        ms, ls, accs = carry
        s0 = pl.multiple_of(kb * tk, tk)
        if masked:
            visible = diff <= (q0 - s0)
        new_ms, new_ls, new_accs = [], [], []
        raw = [_dot(k_ref[0, pl.ds(s0, tk), _col(h // HEADS_PER_COL)], qm[h]) for h in range(N_HEADS)]
        for h in range(N_HEADS):
            logits = raw[h] - key_bias(h, s0)
            if masked:
                logits = jnp.where(visible, logits, MASK_VALUE)
            m_new, alpha, l_new, p = _online_softmax_step(logits, ms[h], ls[h])
            new_ms.append(m_new)
            new_ls.append(l_new)
            new_accs.append(alpha * accs[h] + _dot(vt_ref[0, _head_rows(h), pl.ds(s0, tk)], p.astype(BF)))
        return tuple(new_ms), tuple(new_ls), tuple(new_accs)

    carry = (tuple(jnp.full((1, tq), -jnp.inf, F32) for _ in range(N_HEADS)),
             tuple(jnp.zeros((1, tq), F32) for _ in range(N_HEADS)),
             tuple(jnp.zeros((HEAD_DIM, tq), F32) for _ in range(N_HEADS)))
    carry = lax.fori_loop(0, n_all - n_full, lambda i, c: step(n_all - 1 - i, c, True), carry)

    def reachable(kb, ms):
        j = jnp.maximum(kb, 0)
        gaps = [dot_bound[h] + ctop_ref[bi, h * n_blocks + j] - ms[h] for h in range(N_HEADS)]
        return jnp.max(functools.reduce(jnp.maximum, gaps)) > F32_EXP_UNDERFLOW - BOUND_SLACK

    def older_block(state):
        i, _, c = state
        c = step(n_full - 1 - i, c, False)
        return i + 1, reachable(n_full - 2 - i, c[0]), c

    _, _, carry = lax.while_loop(lambda state: (state[0] < n_full) & state[1], older_block,
                                 (jnp.int32(0), reachable(n_full - 1, carry[0]), carry))
    _, ls, accs = carry
    for h in range(N_HEADS):
        o_ref[0, _head_rows(h), :] = (accs[h] / ls[h]).astype(o_ref.dtype)


def _fox_attention(qt, k, vt, c_lanes, kmax, ctop, p_len, tq, tk):
    b, w, t = qt.shape
    lp = k.shape[1]
    assert tq <= LANES or tq % LANES == 0
    return pl.pallas_call(
        functools.partial(_fox_body, p_len=p_len, tq=tq, tk=tk, n_blocks=lp // tk),
        out_shape=jax.ShapeDtypeStruct((b, w, t), BF),
        grid=(b, t // tq),
        in_specs=[_qt_spec(w, tq), _whole_spec(lp, w), _whole_spec(w, lp),
                  pl.BlockSpec((1, N_HEADS, lp, LANES), lambda bi, qi: (bi, 0, 0, 0)),
                  _whole_spec(SUBLANES, w), pl.BlockSpec(memory_space=pltpu.SMEM)],
        out_specs=_qt_spec(w, tq),
        compiler_params=_params("parallel", "arbitrary"),
        name="fox_attention",
    )(qt, k, vt, c_lanes, kmax, ctop)


def _float_key(bits):
    return jnp.where(bits < 0, jnp.int32(INT_MIN) - bits, bits)


_BUTTERFLY = ((16, 0x0000FFFF), (8, 0x00FF00FF), (4, 0x0F0F0F0F), (2, 0x33333333), (1, 0x55555555))


def _bit_planes(words):
    a = list(words)
    for shift, mask in _BUTTERFLY:
        for k in range(WORD_BITS):
            if k & shift == 0:
                t = (a[k] ^ lax.shift_right_logical(a[k + shift], shift)) & mask
                a[k] = a[k] ^ t
                a[k + shift] = a[k + shift] ^ lax.shift_left(t, shift)
    return a


def _dsa_body(qt_ref, qit_ref, wit_ref, k_ref, vt_ref, ki_ref, o_ref, keys_ref, planes_ref, active_ref, *,
              p_len, n_keys, tq, tk, top_k):
    q0 = p_len + pl.program_id(1) * tq
    qpos = q0 + lax.broadcasted_iota(jnp.int32, (1, tq), 1)
    limit = jnp.minimum(((qpos >> CHUNK_SHIFT) + 1) << CHUNK_SHIFT, n_keys)
    last_limit = jnp.minimum((((q0 + tq - 1) >> CHUNK_SHIFT) + 1) << CHUNK_SHIFT, n_keys)
    n_blk = lax.div(last_limit - 1, tk) + 1
    key_row = lax.broadcasted_iota(jnp.int32, (tk, tq), 0)

    wit = wit_ref[0]
    qim = _head_queries(qit_ref)

    def score_step(kb, _):
        s0 = pl.multiple_of(kb * tk, tk)
        ki = ki_ref[0, pl.ds(s0, tk), :]
        score = jnp.zeros((tk, tq), F32)
        for h in range(N_HEADS):
            score = score + wit[h:h + 1, :] * jnp.maximum(_dot(ki, qim[h]), 0.0)
        score = jnp.where(key_row < limit - s0, score, -jnp.inf)
        key = _float_key(pltpu.bitcast(score, jnp.int32))
        keys_ref[pl.ds(s0, tk), :] = key
        unsigned = key ^ jnp.int32(INT_MIN)
        for g in range(tk // GROUP_KEYS):
            words = [unsigned[g * GROUP_KEYS + SUBLANES * j:g * GROUP_KEYS + SUBLANES * (j + 1), :]
                     for j in range(WORD_BITS)]
            r0 = pl.multiple_of(kb * plane_rows + g * SUBLANES, SUBLANES)
            for i, plane in enumerate(_bit_planes(words)):
                planes_ref[i, pl.ds(r0, SUBLANES), :] = plane
        r0 = pl.multiple_of(kb * plane_rows, plane_rows)
        active_ref[pl.ds(r0, plane_rows), :] = jnp.full((plane_rows, tq), -1, jnp.int32)
        return 0

    plane_rows = tk // WORD_BITS
    lax.fori_loop(0, n_blk, score_step, 0)

    def ones_in(i, narrow):
        def body(kb, acc):
            r0 = pl.multiple_of(kb * plane_rows, plane_rows)
            live = active_ref[pl.ds(r0, plane_rows), :]
            if narrow is not None:
                live = live & (planes_ref[i - 1, pl.ds(r0, plane_rows), :] ^ narrow)
                active_ref[pl.ds(r0, plane_rows), :] = live
            return acc + lax.population_count(live & planes_ref[i, pl.ds(r0, plane_rows), :])
        acc = lax.fori_loop(0, n_blk, body, jnp.zeros((plane_rows, tq), jnp.int32))
        return jnp.sum(acc.astype(F32), axis=0, keepdims=True)

    def choose(i, n_ones, wanted, thr_bits):
        take = n_ones >= wanted
        bit = lax.shift_left(jnp.int32(1), WORD_BITS - 1 - i)
        return (jnp.where(take, wanted, wanted - n_ones), thr_bits | jnp.where(take, bit, 0),
                jnp.where(take, 0, -1))

    wanted, thr_bits, narrow = choose(0, ones_in(0, None), jnp.full((1, tq), top_k, F32),
                                      jnp.zeros((1, tq), jnp.int32))

    def bit_step(i, carry):
        wanted, thr_bits, narrow = carry
        return choose(i, ones_in(i, narrow), wanted, thr_bits)

    wanted, thr_bits, _ = lax.fori_loop(1, WORD_BITS, bit_step, (wanted, thr_bits, narrow))
    thr = thr_bits ^ jnp.int32(INT_MIN)
    n_tie_wanted = wanted

    earlier_keys = (lax.broadcasted_iota(jnp.int32, (tk, tk), 1)
                    < lax.broadcasted_iota(jnp.int32, (tk, tk), 0)).astype(BF)
    qm = _head_queries(qt_ref)

    def attend_step(kb, carry):
        ties_seen, ms, ls, accs = carry
        s0 = pl.multiple_of(kb * tk, tk)
        key = keys_ref[pl.ds(s0, tk), :]
        tie = jnp.where(key == thr, 1.0, 0.0)
        tie_rank = _dot(earlier_keys, tie.astype(BF)) + ties_seen
        take = jnp.where(key > thr, 1.0, jnp.where(tie_rank < n_tie_wanted, tie, 0.0))
        selected = jnp.where(key_row < limit - s0, take, 0.0) > 0.0
        k = k_ref[0, pl.ds(s0, tk), :]
        vt = vt_ref[0, :, pl.ds(s0, tk)]
        new_ms, new_ls, new_accs = [], [], []
        raw = [_dot(k, qm[h]) for h in range(N_HEADS)]
        for h in range(N_HEADS):
            logits = jnp.where(selected, raw[h], MASK_VALUE)
            m_new, alpha, l_new, p = _online_softmax_step(logits, ms[h], ls[h])
            new_ms.append(m_new)
            new_ls.append(l_new)
            new_accs.append(alpha * accs[h] + _dot(vt, p.astype(BF)))
        ties_seen = ties_seen + jnp.sum(tie, axis=0, keepdims=True)
        return ties_seen, tuple(new_ms), tuple(new_ls), tuple(new_accs)

    carry = (jnp.zeros((1, tq), F32),
             tuple(jnp.full((1, tq), -jnp.inf, F32) for _ in range(N_HEADS)),
             tuple(jnp.zeros((1, tq), F32) for _ in range(N_HEADS)),
             tuple(jnp.zeros((HEAD_DIM, tq), F32) for _ in range(N_HEADS)))
    _, _, ls, accs = lax.fori_loop(0, n_blk, attend_step, carry)
    for h in range(N_HEADS):
        o_ref[0, _head_rows(h), :] = (accs[h] / ls[h]).astype(o_ref.dtype)


def _dsa_attention(qt, qit, wit, kk, vt, p_len, n_keys, tq, tk, top_k):
    b, w, t = qt.shape
    lp = kk.shape[1]
    assert tk % GROUP_KEYS == 0
    return pl.pallas_call(
        functools.partial(_dsa_body, p_len=p_len, n_keys=n_keys, tq=tq, tk=tk, top_k=top_k),
        out_shape=jax.ShapeDtypeStruct((b, w, t), BF),
        grid=(b, t // tq),
        in_specs=[_qt_spec(w, tq), _qt_spec(w, tq), _qt_spec(wit.shape[1], tq),
                  pl.BlockSpec((1, lp, LANES), lambda bi, qi: (bi, 0, 0)), _whole_spec(HEAD_DIM, lp),
                  pl.BlockSpec((1, lp, LANES), lambda bi, qi: (bi, 0, 1))],
        out_specs=_qt_spec(w, tq),
        scratch_shapes=[pltpu.VMEM((lp, tq), jnp.int32),
                        pltpu.VMEM((WORD_BITS, lp // WORD_BITS, tq), jnp.int32),
                        pltpu.VMEM((lp // WORD_BITS, tq), jnp.int32)],
        compiler_params=_params("parallel", "arbitrary"),
        name="dsa_attention",
    )(qt, qit, wit, kk, vt, kk)


def _ret_body(q_ref, k_ref, v_ref, g_ref, s0_ref, cos_ref, sin_ref, dec_ref, qd_ref, kd_ref, sd_ref,
              o_ref, so_ref, state_ref):
    c = pl.program_id(1)

    @pl.when(c == 0)
    def _():
        state_ref[...] = s0_ref[0]

    cos, sin = cos_ref[...], sin_ref[...]
    half = HEAD_DIM // 2

    def rotary(x):
        x1, x2 = x[:half], x[half:]
        return jnp.concatenate([x1 * cos - x2 * sin, x2 * cos + x1 * sin], axis=0)

    heads = range(N_HEADS)
    qb = [rotary(q_ref[0, _head_rows(h), :]).astype(BF) for h in heads]
    k = [rotary(k_ref[0, _head_rows(h), :]) for h in heads]
    vb = [v_ref[0, _head_rows(h), :].astype(BF) for h in heads]
    scores_t = [_dot_tn(k[h].astype(BF), qb[h]) * dec_ref[h] for h in heads]
    carried = [_dot(state_ref[h].astype(BF), qb[h]) * qd_ref[h] for h in heads]
    outs = [_dot(vb[h], scores_t[h].astype(BF)) + carried[h] for h in heads]
    for h in heads:
        state_ref[h] = sd_ref[h] * state_ref[h] + _dot_nt(vb[h], (k[h] * kd_ref[h]).astype(BF))
    for h in heads:
        o = outs[h]
        oc = o - jnp.mean(o, axis=0, keepdims=True)
        on = oc * lax.rsqrt(jnp.mean(oc * oc, axis=0, keepdims=True) + LN_EPS)
        g = g_ref[0, _head_rows(h), :]
        o_ref[0, _head_rows(h), :] = (on * (g * jax.nn.sigmoid(g))).astype(o_ref.dtype)

    @pl.when(c == pl.num_programs(1) - 1)
    def _():
        so_ref[0] = state_ref[...]


def _retention(qt, kt, vt, gt, state0_t, pos, c):
    b, w, t = qt.shape
    h = w // HEAD_DIM
    half = HEAD_DIM // 2
    inv_freq = ROPE_BASE ** (-jnp.arange(half, dtype=F32) / half)
    ang = inv_freq[:, None] * pos.astype(F32)[None, :]
    log_gamma = np.log(1.0 - 2.0 ** (-5.0 - np.arange(h, dtype=np.float64)))
    n = np.arange(c, dtype=np.float64)
    rel = n[None, :] - n[:, None]
    decay_t = np.where(rel >= 0, np.exp(np.maximum(rel, 0.0)[None] * log_gamma[:, None, None]), 0.0)
    q_decay = np.exp((n[None, :] + 1.0) * log_gamma[:, None])[:, None, :]
    k_decay = np.exp((c - 1.0 - n)[None, :] * log_gamma[:, None])[:, None, :]
    s_decay = np.exp(c * log_gamma)[:, None, None]
    tables = [jnp.asarray(a, F32) for a in (decay_t, q_decay, k_decay, s_decay)]
    assert h == N_HEADS
    x_spec = pl.BlockSpec((1, w, c), lambda bi, ci: (bi, 0, ci))
    s_spec = pl.BlockSpec((1, h, HEAD_DIM, HEAD_DIM), lambda bi, ci: (bi, 0, 0, 0))
    rope_spec = pl.BlockSpec((half, c), lambda bi, ci: (0, ci))
    return pl.pallas_call(
        _ret_body,
        out_shape=(jax.ShapeDtypeStruct((b, w, t), BF),
                   jax.ShapeDtypeStruct((b, h, HEAD_DIM, HEAD_DIM), F32)),
        grid=(b, t // c),
        in_specs=[x_spec, x_spec, x_spec, x_spec, s_spec, rope_spec, rope_spec] + [_const_spec(a.shape) for a in tables],
        out_specs=(x_spec, s_spec),
        scratch_shapes=[pltpu.VMEM((h, HEAD_DIM, HEAD_DIM), F32)],
        compiler_params=_params("parallel", "arbitrary"),
        name="retention",
    )(qt, kt, vt, gt, state0_t, jnp.cos(ang), jnp.sin(ang), *tables)


def _merge_body(h_ref, y0_ref, y1_ref, y2_ref, y3_ref, wg_ref, wb_ref, wo_ref, g_ref, b_ref, o_ref, *, alpha):
    nb, tm, d = h_ref.shape
    h = h_ref[...].reshape(nb * tm, d)
    hb = h.astype(BF)
    n_chunk = MERGE_COLS if d % MERGE_COLS == 0 else d
    slabs = []
    for c in range(0, d, n_chunk):
        merged = jnp.zeros((nb * tm, n_chunk), F32)
        for i, y_ref in enumerate((y0_ref, y1_ref, y2_ref, y3_ref)):
            gate = jax.nn.sigmoid(_dot(hb, wg_ref[:, i * d + c:i * d + c + n_chunk]))
            branch = jnp.concatenate([_dot_tn(y_ref[j], wb_ref[i, :, c:c + n_chunk]) for j in range(nb)], axis=0)
            merged = merged + gate * branch
        slabs.append(merged.astype(BF))
    r = alpha * h + _dot(jnp.concatenate(slabs, axis=1), wo_ref[...])
    o_ref[...] = _layer_norm(r, g_ref[...], b_ref[...]).reshape(nb, tm, d)


def _merge(h, ys_t, w_gate, w_branch, w_out, ln_g, ln_b, alpha):
    b, t, d = h.shape
    tm = _row_tile(t, MERGE_ROWS)
    nb = math.gcd(b, max(1, MERGE_ROWS // tm))
    row = pl.BlockSpec((nb, tm, d), lambda bi, i: (bi, i, 0))
    col = pl.BlockSpec((nb, BRANCH_WIDTH, tm), lambda bi, i: (bi, 0, i))
    return pl.pallas_call(
        functools.partial(_merge_body, alpha=alpha),
        out_shape=jax.ShapeDtypeStruct((b, t, d), F32),
        grid=(b // nb, t // tm),
        in_specs=[row] + [col] * 4
                 + [_const_spec(w_gate.shape), _const_spec(w_branch.shape), _const_spec(w_out.shape),
                    _const_spec((1, d)), _const_spec((1, d))],
        out_specs=row,
        compiler_params=_params("parallel", "parallel"),
        name="merge",
    )(h, *ys_t, w_gate, w_branch, w_out, ln_g, ln_b)


def _ffn_body(h_ref, wi_ref, wo_ref, g_ref, b_ref, o_ref, *, alpha, f_chunk):
    h = h_ref[...]
    hb = h.astype(BF)
    f = wo_ref.shape[0]
    acc = jnp.zeros(h.shape, F32)
    for c in range(0, f, f_chunk):
        a = _dot(hb, wi_ref[:, c:c + f_chunk])
        u = _dot(hb, wi_ref[:, f + c:f + c + f_chunk])
        acc = acc + _dot((a * jax.nn.sigmoid(a) * u).astype(BF), wo_ref[c:c + f_chunk, :])
    o_ref[...] = _layer_norm(alpha * h + acc, g_ref[...], b_ref[...])


def _ffn(h, w_in, w_out, ln_g, ln_b, alpha):
    m, d = h.shape
    f = w_out.shape[0]
    tm = _row_tile(m, FFN_ROWS)
    f_chunk = next((c for c in (512, 256) if f % c == 0), f)
    row = pl.BlockSpec((tm, d), lambda i: (i, 0))
    return pl.pallas_call(
        functools.partial(_ffn_body, alpha=alpha, f_chunk=f_chunk),
        out_shape=jax.ShapeDtypeStruct((m, d), F32),
        grid=(m // tm,),
        in_specs=[row, _const_spec(w_in.shape), _const_spec(w_out.shape),
                  _const_spec((1, d)), _const_spec((1, d))],
        out_specs=row,
        compiler_params=_params("parallel"),
        name="ffn",
    )(h, w_in, w_out, ln_g, ln_b)


def _in_layout(d):
    w = BRANCH_WIDTH
    return (('sb_q', w), ('sb_k', w), ('sb_v', w), ('ret_q', w), ('ret_k', w), ('ret_v', w), ('ret_g', w),
            ('fox_q', w), ('fox_k', w), ('fox_v', w), ('fox_f', N_HEADS),
            ('dsa_q', w), ('dsa_k', HEAD_DIM), ('dsa_v', HEAD_DIM),
            ('idx_q', w), ('idx_k', HEAD_DIM), ('idx_w', N_HEADS), ('merge_gate', 4 * d))


_FOLDED_SCALE = dict(sb_q=QK_SCALE, fox_q=QK_SCALE, dsa_q=QK_SCALE, idx_q=QK_SCALE, ret_k=QK_SCALE,
                     idx_w=IDX_HEAD_SCALE)


def _swap(a):
    return jnp.swapaxes(a, -1, -2)


def _key_tiles(t, n_keys):
    tq = {name: min(t, q) for name, q in QUERY_TILE.items()}
    tiles = dict(sb=256, fox=512, dsa=512)
    padded = {name: -(-n_keys // tk) * tk for name, tk in tiles.items()}
    return tq, tiles, padded


def _layer(h, b, t, past, ret_state, w, alpha, layer, depth, states):
    m, d = h.shape
    p_len = 0 if past is None else past[0].shape[1]
    n_keys = p_len + t
    tq, tk, lp = _key_tiles(t, n_keys)
    p = _inproj(h.reshape(b, t, d), w['w_row'], w['w_col'], layer, depth, states)
    states = {name: p[name] for name, _, _, _, _, is_state in _ROW_OUTS if is_state}

    old = (None,) * 8 if past is None else past
    sb_k0, sb_v0, fox_k0, fox_v0, fox_lf0, dsa_k0, dsa_v0, dsa_ki0 = old

    def rows_with_past(new_bf, olds, lp_):
        if past is not None:
            flat = [o.reshape(o.shape[0], o.shape[1], -1).astype(BF) for o in olds]
            new_bf = jnp.concatenate([jnp.concatenate(flat, axis=2), new_bf], axis=1)
        return jnp.pad(new_bf, ((0, 0), (0, lp_ - new_bf.shape[1]), (0, 0)))

    def cols_with_past(new_t, old, lp_):
        if old is not None:
            new_t = jnp.concatenate([_swap(old.reshape(old.shape[0], old.shape[1], -1).astype(BF)), new_t], axis=2)
        return jnp.pad(new_t, ((0, 0), (0, 0), (0, lp_ - new_t.shape[2])))

    y_sb = _sb_attention(p['sb_q_t'], rows_with_past(p['sb_k_bf'], [sb_k0], lp['sb']),
                         cols_with_past(p['sb_v_t'], sb_v0, lp['sb']), p_len, tq['sb'], tk['sb'])

    pos = p_len + jnp.arange(t, dtype=jnp.int32)
    y_ret, ret_state_t = _retention(p['ret_q_t'], p['ret_k_t'], p['ret_v_t'], p['ret_g_t'], _swap(ret_state),
                                    pos, min(t, RETENTION_CHUNK))

    fox_lf = jax.nn.log_sigmoid(states['fox_f'][layer] + w['b_forget'])
    lf_all = fox_lf if fox_lf0 is None else jnp.concatenate([fox_lf0, fox_lf], axis=1)
    cum = jnp.pad(jnp.cumsum(lf_all, axis=1), ((0, 0), (0, lp['fox'] - n_keys), (0, 0)))
    c_lanes = jnp.broadcast_to(_swap(cum)[..., None], (b, N_HEADS, lp['fox'], LANES))
    ctop = _swap(lax.cummax(-cum, axis=1)[:, tk['fox'] - 1::tk['fox']]).reshape(b, -1)
    fox_keys = rows_with_past(p['fox_k_bf'], [fox_k0], lp['fox'])
    kmax = jnp.broadcast_to(jnp.max(jnp.abs(fox_keys), axis=1, keepdims=True), (b, SUBLANES, BRANCH_WIDTH))
    y_fox = _fox_attention(p['fox_q_t'], fox_keys, cols_with_past(p['fox_v_t'], fox_v0, lp['fox']), c_lanes,
                           kmax, ctop, p_len, tq['fox'], tk['fox'])

    top_k = min(DSA_TOP_K, n_keys // 4)
    kk_old = [dsa_k0, dsa_k0, dsa_ki0, dsa_ki0]
    y_dsa = _dsa_attention(p['dsa_q_t'], p['idx_q_t'], p['idx_w_t'], rows_with_past(p['dsa_kk_bf'], kk_old, lp['dsa']),
                           cols_with_past(p['dsa_v_t'], dsa_v0, lp['dsa']), p_len, n_keys, tq['dsa'], tk['dsa'], top_k)

    h = _merge(h.reshape(b, t, d), (y_sb, y_ret, y_fox, y_dsa), w['w_gate'], w['w_branch'], w['w_out'],
               w['ln1_g'], w['ln1_b'], alpha)
    h = _ffn(h.reshape(m, d), w['w_ffn_in'], w['w_ffn_out'], w['ln2_g'], w['ln2_b'], alpha)
    return h, states, _swap(ret_state_t), fox_lf


def _group_outputs(states, ret_states, fox_lfs):
    def heads(a):
        return a.reshape(a.shape[:-1] + (N_HEADS, HEAD_DIM))

    return (heads(states['sb_k']), heads(states['sb_v']), jnp.stack(ret_states), heads(states['fox_k']),
            heads(states['fox_v']), jnp.stack(fox_lfs), states['dsa_k'], states['dsa_v'], states['idx_k'])


def kernel(x_prompt, x_sample, cache_sb_k, cache_sb_v, state_ret, cache_fox_k, cache_fox_v, cache_fox_logf,
           cache_dsa_k, cache_dsa_v, cache_dsa_kidx, w_in, b_forget, w_branch, w_out, ln1_g, ln1_b,
           w_ffn_in, w_ffn_out, ln2_g, ln2_b):
    depth = w_in.shape[0]
    alpha = float((2 * depth) ** 0.25)
    bp, tp, d = x_prompt.shape
    bs, ts, _ = x_sample.shape
    hp = x_prompt.reshape(bp * tp, d)
    hs = x_sample.reshape(bs * ts, d)
    ret_zero = jnp.zeros((bp, N_HEADS, HEAD_DIM, HEAD_DIM), F32)
    st_p, st_s, ret_p, ret_s, lf_p, lf_s = None, None, [], [], [], []
    for l in range(depth):
        w_row, w_col, w_gate = _inproj_weights(w_in[l])
        w = dict(w_row=w_row, w_col=w_col, w_gate=w_gate, b_forget=b_forget[l], w_branch=w_branch[l].astype(BF),
                 w_out=w_out[l].astype(BF), ln1_g=ln1_g[l][None], ln1_b=ln1_b[l][None],
                 w_ffn_in=w_ffn_in[l].astype(BF), w_ffn_out=w_ffn_out[l].astype(BF),
                 ln2_g=ln2_g[l][None], ln2_b=ln2_b[l][None])
        hp, st_p, ret, lf = _layer(hp, bp, tp, None, ret_zero, w, alpha, l, depth, st_p)
        ret_p.append(ret)
        lf_p.append(lf)
        past = (cache_sb_k[l], cache_sb_v[l], cache_fox_k[l], cache_fox_v[l], cache_fox_logf[l],
                cache_dsa_k[l], cache_dsa_v[l], cache_dsa_kidx[l])
        hs, st_s, ret, lf = _layer(hs, bs, ts, past, state_ret[l], w, alpha, l, depth, st_s)
        ret_s.append(ret)
        lf_s.append(lf)
    return ((hp.reshape(bp, tp, d), hs.reshape(bs, ts, d))
            + _group_outputs(st_p, ret_p, lf_p) + _group_outputs(st_s, ret_s, lf_s))
```

```python
import functools
import math

import numpy as np
import jax
import jax.numpy as jnp
from jax import lax
from jax.experimental import pallas as pl
from jax.experimental.pallas import tpu as pltpu

HEAD_DIM = 64
N_HEADS = 4
BRANCH_WIDTH = N_HEADS * HEAD_DIM
CHUNK_SHIFT = 6
DSA_TOP_K = 256
ROPE_BASE = 10000.0
LN_EPS = 1e-5
QK_SCALE = HEAD_DIM ** -0.5
IDX_HEAD_SCALE = N_HEADS ** -0.5
MASK_VALUE = -1e30
F32_EXP_UNDERFLOW = -105.0
BOUND_SLACK = 1.0
INT_MIN = -2 ** 31

V7X_VMEM_LIMIT_BYTES = 56 * 1024 * 1024
LANES = 128
HEADS_PER_COL = LANES // HEAD_DIM
SUBLANES = 8
WORD_BITS = 32
GROUP_KEYS = SUBLANES * WORD_BITS
INPROJ_ROWS = 512
QUERY_TILE = dict(sb=256, fox=512, dsa=512)
RETENTION_CHUNK = 512
MERGE_ROWS = 512
MERGE_COLS = 256
FFN_ROWS = 512

BF = jnp.bfloat16
F32 = jnp.float32


def _dot(a, b):
    return jnp.dot(a, b, preferred_element_type=F32)


def _dot_nt(a, b):
    return lax.dot_general(a, b, (((1,), (1,)), ((), ())), preferred_element_type=F32)


def _dot_tn(a, b):
    return lax.dot_general(a, b, (((0,), (0,)), ((), ())), preferred_element_type=F32)


def _params(*sem):
    return pltpu.CompilerParams(dimension_semantics=sem, vmem_limit_bytes=V7X_VMEM_LIMIT_BYTES)


def _const_spec(shape):
    nd = len(shape)
    return pl.BlockSpec(shape, lambda *_: (0,) * nd, pipeline_mode=pl.Buffered(1))


def _layer_norm(x, g, b):
    xc = x - jnp.mean(x, axis=-1, keepdims=True)
    var = jnp.mean(xc * xc, axis=-1, keepdims=True)
    return xc * lax.rsqrt(var + LN_EPS) * g + b


def _row_tile(m, want):
    t = min(m, want)
    assert m % t == 0
    return t


def _col(c):
    return slice(c * LANES, (c + 1) * LANES)


_NARROW_SRC = ('dsa_k', 'dsa_v', 'idx_k', 'fox_f')
_ROW_OUTS = (
    ('sb_k', ('sb_k',), F32, 0, BRANCH_WIDTH, True), ('sb_k_bf', ('sb_k',), BF, 0, BRANCH_WIDTH, False),
    ('sb_v', ('sb_v',), F32, 0, BRANCH_WIDTH, True),
    ('fox_k', ('fox_k',), F32, 0, BRANCH_WIDTH, True), ('fox_k_bf', ('fox_k',), BF, 0, BRANCH_WIDTH, False),
    ('fox_v', ('fox_v',), F32, 0, BRANCH_WIDTH, True),
    ('dsa_k', _NARROW_SRC, F32, 0, HEAD_DIM, True), ('dsa_v', _NARROW_SRC, F32, HEAD_DIM, HEAD_DIM, True),
    ('idx_k', _NARROW_SRC, F32, 2 * HEAD_DIM, HEAD_DIM, True), ('fox_f', _NARROW_SRC, F32, 3 * HEAD_DIM, N_HEADS, True),
    ('dsa_kk_bf', ('dsa_k', 'dsa_k', 'idx_k', 'idx_k'), BF, 0, 4 * HEAD_DIM, False),
)
_COL_OUTS = (
    ('sb_q_t', ('sb_q',), BF), ('fox_q_t', ('fox_q',), BF), ('dsa_q_t', ('dsa_q',), BF),
    ('idx_q_t', ('idx_q',), BF), ('sb_v_t', ('sb_v',), BF), ('fox_v_t', ('fox_v',), BF),
    ('ret_q_t', ('ret_q',), F32), ('ret_k_t', ('ret_k',), F32), ('ret_v_t', ('ret_v',), F32),
    ('ret_g_t', ('ret_g',), F32), ('dsa_v_t', ('dsa_v',), BF), ('idx_w_t', ('idx_w',), F32),
)
BF16_ROWS_PER_VREG = 16


def _inproj_plan(d):
    widths = dict(_in_layout(d))

    def spans(outs, multiple):
        span_of, off = {}, 0
        for out in outs:
            srcs = out[1]
            if srcs not in span_of:
                w = -(-sum(widths[s] for s in srcs) // multiple) * multiple
                span_of[srcs] = (off, w)
                off += w
        return span_of, off

    row_spans, _ = spans(_ROW_OUTS, LANES)
    col_spans, n_col = spans(_COL_OUTS, BF16_ROWS_PER_VREG)
    return row_spans, col_spans, -(-n_col // LANES) * LANES


def _inproj_weights(w_in):
    d = w_in.shape[0]
    w_bf = w_in.astype(BF)
    pieces, off = {}, 0
    for name, width in _in_layout(d):
        piece = w_bf[:, off:off + width]
        scale = _FOLDED_SCALE.get(name)
        pieces[name] = piece if scale is None else piece * jnp.asarray(scale, BF)
        off += width
    assert off == w_in.shape[1]
    row_spans, col_spans, n_col = _inproj_plan(d)

    def block(srcs, width):
        w = jnp.concatenate([pieces[s] for s in srcs], axis=1)
        return jnp.pad(w, ((0, 0), (0, width - w.shape[1])))

    w_row = jnp.concatenate([block(srcs, w) for srcs, (_, w) in row_spans.items()], axis=1)
    w_col = jnp.concatenate([block(srcs, w) for srcs, (_, w) in col_spans.items()], axis=1)
    w_col = jnp.pad(w_col, ((0, 0), (0, n_col - w_col.shape[1])))
    return w_row.astype(BF), w_col.T.astype(BF), pieces['merge_gate'].astype(BF)


def _inproj_body(x_ref, wr_ref, wc_ref, *refs, row_spans, col_spans, n_alias):
    o_refs = refs[n_alias:]
    xb = x_ref[0].astype(BF)
    done = {}
    for o_ref, (_, srcs, _, lane, width, _) in zip(o_refs, _ROW_OUTS):
        if srcs not in done:
            off, w = row_spans[srcs]
            done[srcs] = _dot(xb, wr_ref[:, off:off + w])
        o_ref[...] = done[srcs][:, lane:lane + width].astype(o_ref.dtype).reshape(o_ref.shape)
    for o_ref, (_, srcs, _) in zip(o_refs[len(_ROW_OUTS):], _COL_OUTS):
        off, w = col_spans[srcs]
        o_ref[0] = _dot_nt(wc_ref[off:off + w, :], xb).astype(o_ref.dtype)


def _inproj(x, w_row, w_col, layer, depth, states):
    b, t, d = x.shape
    tm = _row_tile(t, INPROJ_ROWS)
    row_spans, col_spans, _ = _inproj_plan(d)
    out_shape, out_specs, state_names = [], [], []
    for name, _, dt, _, w, is_state in _ROW_OUTS:
        if is_state:
            state_names.append(name)
            out_shape.append(jax.ShapeDtypeStruct((depth, b, t, w), dt))
            out_specs.append(pl.BlockSpec((1, 1, tm, w), lambda bi, i: (layer, bi, i, 0)))
        else:
            out_shape.append(jax.ShapeDtypeStruct((b, t, w), dt))
            out_specs.append(pl.BlockSpec((1, tm, w), lambda bi, i: (bi, i, 0)))
    for _, srcs, dt in _COL_OUTS:
        w = col_spans[srcs][1]
        out_shape.append(jax.ShapeDtypeStruct((b, w, t), dt))
        out_specs.append(pl.BlockSpec((1, w, tm), lambda bi, i: (bi, 0, i)))
    prev = [] if states is None else [states[n] for n in state_names]
    names = [o[0] for o in _ROW_OUTS] + [o[0] for o in _COL_OUTS]
    aliases = {3 + j: names.index(n) for j, n in enumerate(state_names)} if prev else {}
    outs = pl.pallas_call(
        functools.partial(_inproj_body, row_spans=row_spans, col_spans=col_spans, n_alias=len(prev)),
        out_shape=out_shape,
        grid=(b, t // tm),
        in_specs=[pl.BlockSpec((1, tm, d), lambda bi, i: (bi, i, 0)),
                  _const_spec(w_row.shape), _const_spec(w_col.shape)]
                 + [pl.BlockSpec(memory_space=pl.ANY)] * len(prev),
        out_specs=out_specs,
        input_output_aliases=aliases,
        compiler_params=_params("parallel", "parallel"),
        name="inproj",
    )(x, w_row, w_col, *prev)
    return dict(zip(names, outs))


def _block_counts(q0, tk, last_key):
    return lax.div(q0, tk), lax.div(last_key, tk) + 1


def _head_queries(qt_ref):
    low = lax.broadcasted_iota(jnp.int32, (LANES, 1), 0) < HEAD_DIM
    out = []
    for h in range(N_HEADS):
        qc = qt_ref[0, _col(h // HEADS_PER_COL), :]
        keep = low if h % HEADS_PER_COL == 0 else jnp.logical_not(low)
        out.append(jnp.where(keep, qc, jnp.zeros_like(qc)))
    return out


def _head_rows(h):
    return slice(h * HEAD_DIM, (h + 1) * HEAD_DIM)


def _key_minus_query(tk, tq):
    return (lax.broadcasted_iota(jnp.int32, (tk, tq), 0) - lax.broadcasted_iota(jnp.int32, (tk, tq), 1))


def _qt_spec(w, tq):
    return pl.BlockSpec((1, w, tq), lambda bi, qi: (bi, 0, qi))


def _whole_spec(rows, cols):
    return pl.BlockSpec((1, rows, cols), lambda bi, qi: (bi, 0, 0))


def _sb_body(qt_ref, k_ref, vt_ref, o_ref, *, p_len, tq, tk):
    q0 = p_len + pl.program_id(1) * tq
    qm = _head_queries(qt_ref)
    diff = _key_minus_query(tk, tq)
    later = (lax.broadcasted_iota(jnp.int32, (tk, 2 * tk), 1) & (tk - 1)) > lax.broadcasted_iota(
        jnp.int32, (tk, 2 * tk), 0)
    minus_later = jnp.where(later, -1.0, 0.0).astype(BF)
    n_full, n_all = _block_counts(q0, tk, jnp.maximum(q0 + tq - 2, 0))

    def step(kb, carry, masked):
        laters, accs = carry
        s0 = pl.multiple_of(kb * tk, tk)
        if masked:
            earlier = diff < (q0 - s0)
        zs = [_dot(k_ref[0, pl.ds(s0, tk), _col(h // HEADS_PER_COL)], qm[h]) for h in range(N_HEADS)]
        new_laters, log_bs, afters = [], [], []
        for h in range(N_HEADS):
            z = zs[h]
            minus_abs = pltpu.bitcast(pltpu.bitcast(z, jnp.int32) | jnp.int32(INT_MIN), F32)
            softplus = jnp.maximum(z, 0.0) + jnp.log(1.0 + jnp.exp(minus_abs))
            log_bs.append(z - softplus)
            if masked:
                softplus = jnp.where(earlier, softplus, 0.0)
            hi = softplus.astype(BF)
            lo = (softplus - hi.astype(F32)).astype(BF)
            after = _dot(minus_later, jnp.concatenate([hi, lo], axis=0)) + laters[h]
            afters.append(after)
            new_laters.append(after[0:1, :] - softplus[0:1, :])
        new_accs = []
        for h in range(N_HEADS):
            w = jnp.exp(log_bs[h] + afters[h])
            if masked:
                w = jnp.where(earlier, w, 0.0)
            new_accs.append(accs[h] + _dot(vt_ref[0, _head_rows(h), pl.ds(s0, tk)], w.astype(BF)))
        return tuple(new_laters), tuple(new_accs)

    carry = (tuple(jnp.zeros((1, tq), F32) for _ in range(N_HEADS)),
             tuple(jnp.zeros((HEAD_DIM, tq), F32) for _ in range(N_HEADS)))
    carry = lax.fori_loop(0, n_all - n_full, lambda i, c: step(n_all - 1 - i, c, True), carry)

    def reachable(laters):
        return jnp.max(functools.reduce(jnp.maximum, laters)) > F32_EXP_UNDERFLOW

    def earlier_block(state):
        i, _, c = state
        c = step(n_full - 1 - i, c, False)
        return i + 1, reachable(c[0]), c

    _, _, carry = lax.while_loop(lambda state: (state[0] < n_full) & state[1], earlier_block,
                                 (jnp.int32(0), reachable(carry[0]), carry))
    for h in range(N_HEADS):
        o_ref[0, _head_rows(h), :] = carry[1][h].astype(o_ref.dtype)


def _sb_attention(qt, k, vt, p_len, tq, tk):
    b, w, t = qt.shape
    lp = k.shape[1]
    assert tk & (tk - 1) == 0
    return pl.pallas_call(
        functools.partial(_sb_body, p_len=p_len, tq=tq, tk=tk),
        out_shape=jax.ShapeDtypeStruct((b, w, t), BF),
        grid=(b, t // tq),
        in_specs=[_qt_spec(w, tq), _whole_spec(lp, w), _whole_spec(w, lp)],
        out_specs=_qt_spec(w, tq),
        compiler_params=_params("parallel", "arbitrary"),
        name="sb_attention",
    )(qt, k, vt)


def _online_softmax_step(logits, m, l):
    m_new = jnp.maximum(m, jnp.max(logits, axis=0, keepdims=True))
    alpha = jnp.exp(m - m_new)
    p = jnp.exp(logits - m_new)
    return m_new, alpha, alpha * l + jnp.sum(p, axis=0, keepdims=True), p


def _fox_body(qt_ref, k_ref, vt_ref, c_ref, kmax_ref, ctop_ref, o_ref, *, p_len, tq, tk, n_blocks):
    bi = pl.program_id(0)
    q0 = p_len + pl.program_id(1) * tq
    qm = _head_queries(qt_ref)
    diff = _key_minus_query(tk, tq)
    n_full, n_all = _block_counts(q0, tk, q0 + tq - 1)
    dot_bound = [_dot(kmax_ref[0, :, _col(h // HEADS_PER_COL)], jnp.abs(qm[h]))[0:1, :] for h in range(N_HEADS)]

    def key_bias(h, s0):
        c = c_ref[0, h, pl.ds(s0, tk), :]
        return c[:, :tq] if tq <= LANES else jnp.concatenate([c] * (tq // LANES), axis=1)

    def step(kb, carry, masked):
        ms, ls, accs = carry
        s0 = pl.multiple_of(kb * tk, tk)
        if masked:
            visible = diff <= (q0 - s0)
        new_ms, new_ls, new_accs = [], [], []
        raw = [_dot(k_ref[0, pl.ds(s0, tk), _col(h // HEADS_PER_COL)], qm[h]) for h in range(N_HEADS)]
        for h in range(N_HEADS):
            logits = raw[h] - key_bias(h, s0)
            if masked:
                logits = jnp.where(visible, logits, MASK_VALUE)
            m_new, alpha, l_new, p = _online_softmax_step(logits, ms[h], ls[h])
            new_ms.append(m_new)
            new_ls.append(l_new)
            new_accs.append(alpha * accs[h] + _dot(vt_ref[0, _head_rows(h), pl.ds(s0, tk)], p.astype(BF)))
        return tuple(new_ms), tuple(new_ls), tuple(new_accs)

    carry = (tuple(jnp.full((1, tq), -jnp.inf, F32) for _ in range(N_HEADS)),
             tuple(jnp.zeros((1, tq), F32) for _ in range(N_HEADS)),
             tuple(jnp.zeros((HEAD_DIM, tq), F32) for _ in range(N_HEADS)))
    carry = lax.fori_loop(0, n_all - n_full, lambda i, c: step(n_all - 1 - i, c, True), carry)

    def reachable(kb, ms):
        j = jnp.maximum(kb, 0)
        gaps = [dot_bound[h] + ctop_ref[bi, h * n_blocks + j] - ms[h] for h in range(N_HEADS)]
        return jnp.max(functools.reduce(jnp.maximum, gaps)) > F32_EXP_UNDERFLOW - BOUND_SLACK

    def older_block(state):
        i, _, c = state
        c = step(n_full - 1 - i, c, False)
        return i + 1, reachable(n_full - 2 - i, c[0]), c

    _, _, carry = lax.while_loop(lambda state: (state[0] < n_full) & state[1], older_block,
                                 (jnp.int32(0), reachable(n_full - 1, carry[0]), carry))
    _, ls, accs = carry
    for h in range(N_HEADS):
        o_ref[0, _head_rows(h), :] = (accs[h] / ls[h]).astype(o_ref.dtype)


def _fox_attention(qt, k, vt, c_lanes, kmax, ctop, p_len, tq, tk):
    b, w, t = qt.shape
    lp = k.shape[1]
    assert tq <= LANES or tq % LANES == 0
    return pl.pallas_call(
        functools.partial(_fox_body, p_len=p_len, tq=tq, tk=tk, n_blocks=lp // tk),
        out_shape=jax.ShapeDtypeStruct((b, w, t), BF),
        grid=(b, t // tq),
        in_specs=[_qt_spec(w, tq), _whole_spec(lp, w), _whole_spec(w, lp),
                  pl.BlockSpec((1, N_HEADS, lp, LANES), lambda bi, qi: (bi, 0, 0, 0)),
                  _whole_spec(SUBLANES, w), pl.BlockSpec(memory_space=pltpu.SMEM)],
        out_specs=_qt_spec(w, tq),
        compiler_params=_params("parallel", "arbitrary"),
        name="fox_attention",
    )(qt, k, vt, c_lanes, kmax, ctop)


def _float_key(bits):
    return jnp.where(bits < 0, jnp.int32(INT_MIN) - bits, bits)


_BUTTERFLY = ((16, 0x0000FFFF), (8, 0x00FF00FF), (4, 0x0F0F0F0F), (2, 0x33333333), (1, 0x55555555))


def _bit_planes(words):
    a = list(words)
    for shift, mask in _BUTTERFLY:
        for k in range(WORD_BITS):
            if k & shift == 0:
                t = (a[k] ^ lax.shift_right_logical(a[k + shift], shift)) & mask
                a[k] = a[k] ^ t
                a[k + shift] = a[k + shift] ^ lax.shift_left(t, shift)
    return a


def _dsa_body(qt_ref, qit_ref, wit_ref, k_ref, vt_ref, ki_ref, o_ref, keys_ref, planes_ref, active_ref, *,
              p_len, n_keys, tq, tk, top_k):
    q0 = p_len + pl.program_id(1) * tq
    qpos = q0 + lax.broadcasted_iota(jnp.int32, (1, tq), 1)
    limit = jnp.minimum(((qpos >> CHUNK_SHIFT) + 1) << CHUNK_SHIFT, n_keys)
    last_limit = jnp.minimum((((q0 + tq - 1) >> CHUNK_SHIFT) + 1) << CHUNK_SHIFT, n_keys)
    n_blk = lax.div(last_limit - 1, tk) + 1
    key_row = lax.broadcasted_iota(jnp.int32, (tk, tq), 0)

    wit = wit_ref[0]
    qim = _head_queries(qit_ref)

    def score_step(kb, _):
        s0 = pl.multiple_of(kb * tk, tk)
        ki = ki_ref[0, pl.ds(s0, tk), :]
        score = jnp.zeros((tk, tq), F32)
        for h in range(N_HEADS):
            score = score + wit[h:h + 1, :] * jnp.maximum(_dot(ki, qim[h]), 0.0)
        score = jnp.where(key_row < limit - s0, score, -jnp.inf)
        key = _float_key(pltpu.bitcast(score, jnp.int32))
        keys_ref[pl.ds(s0, tk), :] = key
        unsigned = key ^ jnp.int32(INT_MIN)
        for g in range(tk // GROUP_KEYS):
            words = [unsigned[g * GROUP_KEYS + SUBLANES * j:g * GROUP_KEYS + SUBLANES * (j + 1), :]
                     for j in range(WORD_BITS)]
            r0 = pl.multiple_of(kb * plane_rows + g * SUBLANES, SUBLANES)
            for i, plane in enumerate(_bit_planes(words)):
                planes_ref[i, pl.ds(r0, SUBLANES), :] = plane
        r0 = pl.multiple_of(kb * plane_rows, plane_rows)
        active_ref[pl.ds(r0, plane_rows), :] = jnp.full((plane_rows, tq), -1, jnp.int32)
        return 0

    plane_rows = tk // WORD_BITS
    lax.fori_loop(0, n_blk, score_step, 0)

    def ones_in(i, narrow):
        def body(kb, acc):
            r0 = pl.multiple_of(kb * plane_rows, plane_rows)
            live = active_ref[pl.ds(r0, plane_rows), :]
            if narrow is not None:
                live = live & (planes_ref[i - 1, pl.ds(r0, plane_rows), :] ^ narrow)
                active_ref[pl.ds(r0, plane_rows), :] = live
            return acc + lax.population_count(live & planes_ref[i, pl.ds(r0, plane_rows), :])
        acc = lax.fori_loop(0, n_blk, body, jnp.zeros((plane_rows, tq), jnp.int32))
        return jnp.sum(acc.astype(F32), axis=0, keepdims=True)

    def choose(i, n_ones, wanted, thr_bits):
        take = n_ones >= wanted
        bit = lax.shift_left(jnp.int32(1), WORD_BITS - 1 - i)
        return (jnp.where(take, wanted, wanted - n_ones), thr_bits | jnp.where(take, bit, 0),
                jnp.where(take, 0, -1))

    wanted, thr_bits, narrow = choose(0, ones_in(0, None), jnp.full((1, tq), top_k, F32),
                                      jnp.zeros((1, tq), jnp.int32))

    def bit_step(i, carry):
        wanted, thr_bits, narrow = carry
        return choose(i, ones_in(i, narrow), wanted, thr_bits)

    wanted, thr_bits, _ = lax.fori_loop(1, WORD_BITS, bit_step, (wanted, thr_bits, narrow))
    thr = thr_bits ^ jnp.int32(INT_MIN)
    n_tie_wanted = wanted

    earlier_keys = (lax.broadcasted_iota(jnp.int32, (tk, tk), 1)
                    < lax.broadcasted_iota(jnp.int32, (tk, tk), 0)).astype(BF)
    qm = _head_queries(qt_ref)

    def attend_step(kb, carry):
        ties_seen, ms, ls, accs = carry
        s0 = pl.multiple_of(kb * tk, tk)
        key = keys_ref[pl.ds(s0, tk), :]
        tie = jnp.where(key == thr, 1.0, 0.0)
        tie_rank = _dot(earlier_keys, tie.astype(BF)) + ties_seen
        take = jnp.where(key > thr, 1.0, jnp.where(tie_rank < n_tie_wanted, tie, 0.0))
        selected = jnp.where(key_row < limit - s0, take, 0.0) > 0.0
        k = k_ref[0, pl.ds(s0, tk), :]
        vt = vt_ref[0, :, pl.ds(s0, tk)]
        new_ms, new_ls, new_accs = [], [], []
        raw = [_dot(k, qm[h]) for h in range(N_HEADS)]
        for h in range(N_HEADS):
            logits = jnp.where(selected, raw[h], MASK_VALUE)
            m_new, alpha, l_new, p = _online_softmax_step(logits, ms[h], ls[h])
            new_ms.append(m_new)
            new_ls.append(l_new)
            new_accs.append(alpha * accs[h] + _dot(vt, p.astype(BF)))
        ties_seen = ties_seen + jnp.sum(tie, axis=0, keepdims=True)
        return ties_seen, tuple(new_ms), tuple(new_ls), tuple(new_accs)

    carry = (jnp.zeros((1, tq), F32),
             tuple(jnp.full((1, tq), -jnp.inf, F32) for _ in range(N_HEADS)),
             tuple(jnp.zeros((1, tq), F32) for _ in range(N_HEADS)),
             tuple(jnp.zeros((HEAD_DIM, tq), F32) for _ in range(N_HEADS)))
    _, _, ls, accs = lax.fori_loop(0, n_blk, attend_step, carry)
    for h in range(N_HEADS):
        o_ref[0, _head_rows(h), :] = (accs[h] / ls[h]).astype(o_ref.dtype)


def _dsa_attention(qt, qit, wit, kk, vt, p_len, n_keys, tq, tk, top_k):
    b, w, t = qt.shape
    lp = kk.shape[1]
    assert tk % GROUP_KEYS == 0
    return pl.pallas_call(
        functools.partial(_dsa_body, p_len=p_len, n_keys=n_keys, tq=tq, tk=tk, top_k=top_k),
        out_shape=jax.ShapeDtypeStruct((b, w, t), BF),
        grid=(b, t // tq),
        in_specs=[_qt_spec(w, tq), _qt_spec(w, tq), _qt_spec(wit.shape[1], tq),
                  pl.BlockSpec((1, lp, LANES), lambda bi, qi: (bi, 0, 0)), _whole_spec(HEAD_DIM, lp),
                  pl.BlockSpec((1, lp, LANES), lambda bi, qi: (bi, 0, 1))],
        out_specs=_qt_spec(w, tq),
        scratch_shapes=[pltpu.VMEM((lp, tq), jnp.int32),
                        pltpu.VMEM((WORD_BITS, lp // WORD_BITS, tq), jnp.int32),
                        pltpu.VMEM((lp // WORD_BITS, tq), jnp.int32)],
        compiler_params=_params("parallel", "arbitrary"),
        name="dsa_attention",
    )(qt, qit, wit, kk, vt, kk)


def _ret_body(q_ref, k_ref, v_ref, g_ref, s0_ref, cos_ref, sin_ref, dec_ref, qd_ref, kd_ref, sd_ref,
              o_ref, so_ref, state_ref):
    c = pl.program_id(1)

    @pl.when(c == 0)
    def _():
        state_ref[...] = s0_ref[0]

    cos, sin = cos_ref[...], sin_ref[...]
    half = HEAD_DIM // 2

    def rotary(x):
        x1, x2 = x[:half], x[half:]
        return jnp.concatenate([x1 * cos - x2 * sin, x2 * cos + x1 * sin], axis=0)

    heads = range(N_HEADS)
    qb = [rotary(q_ref[0, _head_rows(h), :]).astype(BF) for h in heads]
    k = [rotary(k_ref[0, _head_rows(h), :]) for h in heads]
    vb = [v_ref[0, _head_rows(h), :].astype(BF) for h in heads]
    scores_t = [_dot_tn(k[h].astype(BF), qb[h]) * dec_ref[h] for h in heads]
    carried = [_dot(state_ref[h].astype(BF), qb[h]) * qd_ref[h] for h in heads]
    outs = [_dot(vb[h], scores_t[h].astype(BF)) + carried[h] for h in heads]
    for h in heads:
        state_ref[h] = sd_ref[h] * state_ref[h] + _dot_nt(vb[h], (k[h] * kd_ref[h]).astype(BF))
    for h in heads:
        o = outs[h]
        oc = o - jnp.mean(o, axis=0, keepdims=True)
        on = oc * lax.rsqrt(jnp.mean(oc * oc, axis=0, keepdims=True) + LN_EPS)
        g = g_ref[0, _head_rows(h), :]
        o_ref[0, _head_rows(h), :] = (on * (g * jax.nn.sigmoid(g))).astype(o_ref.dtype)

    @pl.when(c == pl.num_programs(1) - 1)
    def _():
        so_ref[0] = state_ref[...]


def _retention(qt, kt, vt, gt, state0_t, pos, c):
    b, w, t = qt.shape
    h = w // HEAD_DIM
    half = HEAD_DIM // 2
    inv_freq = ROPE_BASE ** (-jnp.arange(half, dtype=F32) / half)
    ang = inv_freq[:, None] * pos.astype(F32)[None, :]
    log_gamma = np.log(1.0 - 2.0 ** (-5.0 - np.arange(h, dtype=np.float64)))
    n = np.arange(c, dtype=np.float64)
    rel = n[None, :] - n[:, None]
    decay_t = np.where(rel >= 0, np.exp(np.maximum(rel, 0.0)[None] * log_gamma[:, None, None]), 0.0)
    q_decay = np.exp((n[None, :] + 1.0) * log_gamma[:, None])[:, None, :]
    k_decay = np.exp((c - 1.0 - n)[None, :] * log_gamma[:, None])[:, None, :]
    s_decay = np.exp(c * log_gamma)[:, None, None]
    tables = [jnp.asarray(a, F32) for a in (decay_t, q_decay, k_decay, s_decay)]
    assert h == N_HEADS
    x_spec = pl.BlockSpec((1, w, c), lambda bi, ci: (bi, 0, ci))
    s_spec = pl.BlockSpec((1, h, HEAD_DIM, HEAD_DIM), lambda bi, ci: (bi, 0, 0, 0))
    rope_spec = pl.BlockSpec((half, c), lambda bi, ci: (0, ci))
    return pl.pallas_call(
        _ret_body,
        out_shape=(jax.ShapeDtypeStruct((b, w, t), BF),
                   jax.ShapeDtypeStruct((b, h, HEAD_DIM, HEAD_DIM), F32)),
        grid=(b, t // c),
        in_specs=[x_spec, x_spec, x_spec, x_spec, s_spec, rope_spec, rope_spec] + [_const_spec(a.shape) for a in tables],
        out_specs=(x_spec, s_spec),
        scratch_shapes=[pltpu.VMEM((h, HEAD_DIM, HEAD_DIM), F32)],
        compiler_params=_params("parallel", "arbitrary"),
        name="retention",
    )(qt, kt, vt, gt, state0_t, jnp.cos(ang), jnp.sin(ang), *tables)


def _merge_body(h_ref, y0_ref, y1_ref, y2_ref, y3_ref, wg_ref, wb_ref, wo_ref, g_ref, b_ref, o_ref, *, alpha):
    nb, tm, d = h_ref.shape
    h = h_ref[...].reshape(nb * tm, d)
    hb = h.astype(BF)
    n_chunk = MERGE_COLS if d % MERGE_COLS == 0 else d
    slabs = []
    for c in range(0, d, n_chunk):
        merged = jnp.zeros((nb * tm, n_chunk), F32)
        for i, y_ref in enumerate((y0_ref, y1_ref, y2_ref, y3_ref)):
            gate = jax.nn.sigmoid(_dot(hb, wg_ref[:, i * d + c:i * d + c + n_chunk]))
            branch = jnp.concatenate([_dot_tn(y_ref[j], wb_ref[i, :, c:c + n_chunk]) for j in range(nb)], axis=0)
            merged = merged + gate * branch
        slabs.append(merged.astype(BF))
    r = alpha * h + _dot(jnp.concatenate(slabs, axis=1), wo_ref[...])
    o_ref[...] = _layer_norm(r, g_ref[...], b_ref[...]).reshape(nb, tm, d)


def _merge(h, ys_t, w_gate, w_branch, w_out, ln_g, ln_b, alpha):
    b, t, d = h.shape
    tm = _row_tile(t, MERGE_ROWS)
    nb = math.gcd(b, max(1, MERGE_ROWS // tm))
    row = pl.BlockSpec((nb, tm, d), lambda bi, i: (bi, i, 0))
    col = pl.BlockSpec((nb, BRANCH_WIDTH, tm), lambda bi, i: (bi, 0, i))
    return pl.pallas_call(
        functools.partial(_merge_body, alpha=alpha),
        out_shape=jax.ShapeDtypeStruct((b, t, d), F32),
        grid=(b // nb, t // tm),
        in_specs=[row] + [col] * 4
                 + [_const_spec(w_gate.shape), _const_spec(w_branch.shape), _const_spec(w_out.shape),
                    _const_spec((1, d)), _const_spec((1, d))],
        out_specs=row,
        compiler_params=_params("parallel", "parallel"),
        name="merge",
    )(h, *ys_t, w_gate, w_branch, w_out, ln_g, ln_b)


def _ffn_body(h_ref, wi_ref, wo_ref, g_ref, b_ref, o_ref, *, alpha, f_chunk):
    h = h_ref[...]
    hb = h.astype(BF)
    f = wo_ref.shape[0]
    acc = jnp.zeros(h.shape, F32)
    for c in range(0, f, f_chunk):
        a = _dot(hb, wi_ref[:, c:c + f_chunk])
        u = _dot(hb, wi_ref[:, f + c:f + c + f_chunk])
        acc = acc + _dot((a * jax.nn.sigmoid(a) * u).astype(BF), wo_ref[c:c + f_chunk, :])
    o_ref[...] = _layer_norm(alpha * h + acc, g_ref[...], b_ref[...])


def _ffn(h, w_in, w_out, ln_g, ln_b, alpha):
    m, d = h.shape
    f = w_out.shape[0]
    tm = _row_tile(m, FFN_ROWS)
    f_chunk = next((c for c in (512, 256) if f % c == 0), f)
    row = pl.BlockSpec((tm, d), lambda i: (i, 0))
    return pl.pallas_call(
        functools.partial(_ffn_body, alpha=alpha, f_chunk=f_chunk),
        out_shape=jax.ShapeDtypeStruct((m, d), F32),
        grid=(m // tm,),
        in_specs=[row, _const_spec(w_in.shape), _const_spec(w_out.shape),
                  _const_spec((1, d)), _const_spec((1, d))],
        out_specs=row,
        compiler_params=_params("parallel"),
        name="ffn",
    )(h, w_in, w_out, ln_g, ln_b)


def _in_layout(d):
    w = BRANCH_WIDTH
    return (('sb_q', w), ('sb_k', w), ('sb_v', w), ('ret_q', w), ('ret_k', w), ('ret_v', w), ('ret_g', w),
            ('fox_q', w), ('fox_k', w), ('fox_v', w), ('fox_f', N_HEADS),
            ('dsa_q', w), ('dsa_k', HEAD_DIM), ('dsa_v', HEAD_DIM),
            ('idx_q', w), ('idx_k', HEAD_DIM), ('idx_w', N_HEADS), ('merge_gate', 4 * d))


_FOLDED_SCALE = dict(sb_q=QK_SCALE, fox_q=QK_SCALE, dsa_q=QK_SCALE, idx_q=QK_SCALE, ret_k=QK_SCALE,
                     idx_w=IDX_HEAD_SCALE)


def _swap(a):
    return jnp.swapaxes(a, -1, -2)


def _key_tiles(t, n_keys):
    tq = {name: min(t, q) for name, q in QUERY_TILE.items()}
    tiles = dict(sb=256, fox=512, dsa=512)
    padded = {name: -(-n_keys // tk) * tk for name, tk in tiles.items()}
    return tq, tiles, padded


def _layer(h, b, t, past, ret_state, w, alpha, layer, depth, states):
    m, d = h.shape
    p_len = 0 if past is None else past[0].shape[1]
    n_keys = p_len + t
    tq, tk, lp = _key_tiles(t, n_keys)
    p = _inproj(h.reshape(b, t, d), w['w_row'], w['w_col'], layer, depth, states)
    states = {name: p[name] for name, _, _, _, _, is_state in _ROW_OUTS if is_state}

    old = (None,) * 8 if past is None else past
    sb_k0, sb_v0, fox_k0, fox_v0, fox_lf0, dsa_k0, dsa_v0, dsa_ki0 = old

    def rows_with_past(new_bf, olds, lp_):
        if past is not None:
            flat = [o.reshape(o.shape[0], o.shape[1], -1).astype(BF) for o in olds]
            new_bf = jnp.concatenate([jnp.concatenate(flat, axis=2), new_bf], axis=1)
        return jnp.pad(new_bf, ((0, 0), (0, lp_ - new_bf.shape[1]), (0, 0)))

    def cols_with_past(new_t, old, lp_):
        if old is not None:
            new_t = jnp.concatenate([_swap(old.reshape(old.shape[0], old.shape[1], -1).astype(BF)), new_t], axis=2)
        return jnp.pad(new_t, ((0, 0), (0, 0), (0, lp_ - new_t.shape[2])))

    y_sb = _sb_attention(p['sb_q_t'], rows_with_past(p['sb_k_bf'], [sb_k0], lp['sb']),
                         cols_with_past(p['sb_v_t'], sb_v0, lp['sb']), p_len, tq['sb'], tk['sb'])

    pos = p_len + jnp.arange(t, dtype=jnp.int32)
    y_ret, ret_state_t = _retention(p['ret_q_t'], p['ret_k_t'], p['ret_v_t'], p['ret_g_t'], _swap(ret_state),
                                    pos, min(t, RETENTION_CHUNK))

    fox_lf = jax.nn.log_sigmoid(states['fox_f'][layer] + w['b_forget'])
    lf_all = fox_lf if fox_lf0 is None else jnp.concatenate([fox_lf0, fox_lf], axis=1)
    cum = jnp.pad(jnp.cumsum(lf_all, axis=1), ((0, 0), (0, lp['fox'] - n_keys), (0, 0)))
    c_lanes = jnp.broadcast_to(_swap(cum)[..., None], (b, N_HEADS, lp['fox'], LANES))
    ctop = _swap(lax.cummax(-cum, axis=1)[:, tk['fox'] - 1::tk['fox']]).reshape(b, -1)
    fox_keys = rows_with_past(p['fox_k_bf'], [fox_k0], lp['fox'])
    kmax = jnp.broadcast_to(jnp.max(jnp.abs(fox_keys), axis=1, keepdims=True), (b, SUBLANES, BRANCH_WIDTH))
    y_fox = _fox_attention(p['fox_q_t'], fox_keys, cols_with_past(p['fox_v_t'], fox_v0, lp['fox']), c_lanes,
                           kmax, ctop, p_len, tq['fox'], tk['fox'])

    top_k = min(DSA_TOP_K, n_keys // 4)
    kk_old = [dsa_k0, dsa_k0, dsa_ki0, dsa_ki0]
    y_dsa = _dsa_attention(p['dsa_q_t'], p['idx_q_t'], p['idx_w_t'], rows_with_past(p['dsa_kk_bf'], kk_old, lp['dsa']),
                           cols_with_past(p['dsa_v_t'], dsa_v0, lp['dsa']), p_len, n_keys, tq['dsa'], tk['dsa'], top_k)

    h = _merge(h.reshape(b, t, d), (y_sb, y_ret, y_fox, y_dsa), w['w_gate'], w['w_branch'], w['w_out'],
               w['ln1_g'], w['ln1_b'], alpha)
    h = _ffn(h.reshape(m, d), w['w_ffn_in'], w['w_ffn_out'], w['ln2_g'], w['ln2_b'], alpha)
    return h, states, _swap(ret_state_t), fox_lf


def _group_outputs(states, ret_states, fox_lfs):
    def heads(a):
        return a.reshape(a.shape[:-1] + (N_HEADS, HEAD_DIM))

    return (heads(states['sb_k']), heads(states['sb_v']), jnp.stack(ret_states), heads(states['fox_k']),
            heads(states['fox_v']), jnp.stack(fox_lfs), states['dsa_k'], states['dsa_v'], states['idx_k'])


def kernel(x_prompt, x_sample, cache_sb_k, cache_sb_v, state_ret, cache_fox_k, cache_fox_v, cache_fox_logf,
           cache_dsa_k, cache_dsa_v, cache_dsa_kidx, w_in, b_forget, w_branch, w_out, ln1_g, ln1_b,
           w_ffn_in, w_ffn_out, ln2_g, ln2_b):
    depth = w_in.shape[0]
    alpha = float((2 * depth) ** 0.25)
    bp, tp, d = x_prompt.shape
    bs, ts, _ = x_sample.shape
    hp = x_prompt.reshape(bp * tp, d)
    hs = x_sample.reshape(bs * ts, d)
    ret_zero = jnp.zeros((bp, N_HEADS, HEAD_DIM, HEAD_DIM), F32)
    st_p, st_s, ret_p, ret_s, lf_p, lf_s = None, None, [], [], [], []
    for l in range(depth):
        w_row, w_col, w_gate = _inproj_weights(w_in[l])
        w = dict(w_row=w_row, w_col=w_col, w_gate=w_gate, b_forget=b_forget[l], w_branch=w_branch[l].astype(BF),
                 w_out=w_out[l].astype(BF), ln1_g=ln1_g[l][None], ln1_b=ln1_b[l][None],
                 w_ffn_in=w_ffn_in[l].astype(BF), w_ffn_out=w_ffn_out[l].astype(BF),
                 ln2_g=ln2_g[l][None], ln2_b=ln2_b[l][None])
        hp, st_p, ret, lf = _layer(hp, bp, tp, None, ret_zero, w, alpha, l, depth, st_p)
        ret_p.append(ret)
        lf_p.append(lf)
        past = (cache_sb_k[l], cache_sb_v[l], cache_fox_k[l], cache_fox_v[l], cache_fox_logf[l],
                cache_dsa_k[l], cache_dsa_v[l], cache_dsa_kidx[l])
        hs, st_s, ret, lf = _layer(hs, bs, ts, past, state_ret[l], w, alpha, l, depth, st_s)
        ret_s.append(ret)
        lf_s.append(lf)
    return ((hp.reshape(bp, tp, d), hs.reshape(bs, ts, d))
            + _group_outputs(st_p, ret_p, lf_p) + _group_outputs(st_s, ret_s, lf_s))
```
